```python
import math
import jax
import jax.numpy as jnp
from jax import lax
import numpy as np

D_MODEL = 2048
BATCH = 1
SEQ = 16384
DEPTH = 2

CONV_DIM = D_MODEL
CONV_WIDTH = 31
SSM_INNER = D_MODEL
SSM_HEAD_DIM = 64
SSM_HEADS = SSM_INNER // SSM_HEAD_DIM
SSM_GROUPS = 8
SSM_STATE = 128
SSM_CONV_WIDTH = 4
SSM_CHUNK = 128
SSM_BC = SSM_GROUPS * SSM_STATE
SSM_XBC = SSM_INNER + 2 * SSM_BC
IN_PROJ_DIM = 2 * CONV_DIM + SSM_INNER + SSM_XBC + SSM_HEADS
MIX_DIM = CONV_DIM + SSM_INNER
ATTN_HEAD_DIM = 64
ATTN_Q_HEADS = D_MODEL // ATTN_HEAD_DIM
ATTN_KV_HEADS = ATTN_Q_HEADS // 8
ATTN_REP = ATTN_Q_HEADS // ATTN_KV_HEADS
ATTN_Q_DIM = ATTN_Q_HEADS * ATTN_HEAD_DIM
ATTN_KV_DIM = ATTN_KV_HEADS * ATTN_HEAD_DIM
QKV_DIM = ATTN_Q_DIM + 2 * ATTN_KV_DIM
WINDOW = 128
ATTN_BLOCK = WINDOW
REL_BUCKETS = 32
REL_MAX_DIST = 128
FFN_DIM = 5632
N_EXPERTS = 8
TOP_K = 2
EXPERT_DIM = 7168
MOE_BLOCK = 256
NORM_EPS = 1e-5
N_EVEN = (DEPTH + 1) // 2
N_ODD = DEPTH // 2

kernel_name = 'hybrid_conv_ssd_swa_moe_block'


def rms_norm(x, g):
    xf = x.astype(jnp.float32)
    y = xf * lax.rsqrt(jnp.mean(xf * xf, axis=-1, keepdims=True) + NORM_EPS)
    return (y * g.astype(jnp.float32)).astype(x.dtype)


def layer_norm(x, g, b):
    xf = x.astype(jnp.float32)
    mu = jnp.mean(xf, axis=-1, keepdims=True)
    var = jnp.mean(jnp.square(xf - mu), axis=-1, keepdims=True)
    y = (xf - mu) * lax.rsqrt(var + NORM_EPS)
    return (y * g.astype(jnp.float32) + b.astype(jnp.float32)).astype(x.dtype)


def causal_depthwise_conv(u, w, b):
    width, ch = w.shape
    out = lax.conv_general_dilated(
        u, w[:, None, :].astype(u.dtype), window_strides=(1,), padding=[(width - 1, 0)],
        dimension_numbers=('NWC', 'WIO', 'NWC'), feature_group_count=ch)
    return out + b.astype(u.dtype)


def ssd_chunked_scan(xs, dt, a, bm, cm):
    f32 = jnp.float32
    bsz, seq, n_heads, p = xs.shape
    n_groups, n_state = bm.shape[2], bm.shape[3]
    rep = n_heads // n_groups
    nc = seq // SSM_CHUNK
    x = xs.astype(f32).reshape(bsz, nc, SSM_CHUNK, n_groups, rep, p)
    dtc = dt.astype(f32).reshape(bsz, nc, SSM_CHUNK, n_groups, rep)
    bc = bm.astype(f32).reshape(bsz, nc, SSM_CHUNK, n_groups, n_state)
    cc = cm.astype(f32).reshape(bsz, nc, SSM_CHUNK, n_groups, n_state)
    xdt = x * dtc[..., None]
    a_cs = jnp.cumsum(dtc * a.astype(f32).reshape(n_groups, rep), axis=2)
    seg = a_cs[:, :, :, None] - a_cs[:, :, None, :]
    causal = jnp.tril(jnp.ones((SSM_CHUNK, SSM_CHUNK), dtype=bool))[:, :, None, None]
    decay = jnp.exp(jnp.where(causal, seg, -jnp.inf))
    cb = jnp.einsum('bclgn,bcsgn->bclsg', cc, bc)
    y_diag = jnp.einsum('bclsgr,bcsgrp->bclgrp', cb[..., None] * decay, xdt)
    decay_to_end = jnp.exp(a_cs[:, :, -1:] - a_cs)
    states = jnp.einsum('bcsgn,bcsgrp->bcgrpn', bc, xdt * decay_to_end[..., None])
    chunk_decay = jnp.exp(a_cs[:, :, -1])

    def step(h, inp):
        st, dec = inp
        return h * dec[..., None, None] + st, h

    h0 = jnp.zeros((bsz, n_groups, rep, p, n_state), f32)
    _, prev = lax.scan(step, h0, (jnp.moveaxis(states, 1, 0), jnp.moveaxis(chunk_decay, 1, 0)))
    prev = jnp.moveaxis(prev, 0, 1)
    y_off = jnp.einsum('bclgn,bcgrpn->bclgrp', cc, prev) * jnp.exp(a_cs)[..., None]
    return (y_diag + y_off).reshape(bsz, seq, n_heads, p)


def conv_ssd_mixer(h, w_in, conv_w, conv_b, conv_ln_g, conv_ln_b, ssm_conv_w, ssm_conv_b,
                   dt_bias, a_log, d_skip, ssm_norm_g, w_out):
    f32 = jnp.float32
    bsz, seq, _ = h.shape
    proj = h @ w_in
    o1 = CONV_DIM
    o2 = o1 + CONV_DIM
    o3 = o2 + SSM_INNER
    o4 = o3 + SSM_XBC
    c_val, c_gate, z, xbc, dt_raw = proj[..., :o1], proj[..., o1:o2], proj[..., o2:o3], proj[..., o3:o4], proj[..., o4:]
    u = c_val * jax.nn.sigmoid(c_gate)
    u = causal_depthwise_conv(u, conv_w, conv_b)
    conv_out = jax.nn.silu(layer_norm(u, conv_ln_g, conv_ln_b))
    xbc = jax.nn.silu(causal_depthwise_conv(xbc, ssm_conv_w, ssm_conv_b))
    xs = xbc[..., :SSM_INNER].reshape(bsz, seq, SSM_HEADS, SSM_HEAD_DIM)
    bm = xbc[..., SSM_INNER:SSM_INNER + SSM_BC].reshape(bsz, seq, SSM_GROUPS, SSM_STATE)
    cm = xbc[..., SSM_INNER + SSM_BC:].reshape(bsz, seq, SSM_GROUPS, SSM_STATE)
    dt = jax.nn.softplus(dt_raw.astype(f32) + dt_bias.astype(f32))
    a = -jnp.exp(a_log.astype(f32))
    y = ssd_chunked_scan(xs, dt, a, bm, cm) + xs.astype(f32) * d_skip.astype(f32)[:, None]
    y = y.reshape(bsz, seq, SSM_INNER) * jax.nn.silu(z.astype(f32))
    yg = y.reshape(bsz, seq, SSM_GROUPS, SSM_INNER // SSM_GROUPS)
    yg = yg * lax.rsqrt(jnp.mean(yg * yg, axis=-1, keepdims=True) + NORM_EPS)
    ssm_out = (yg.reshape(bsz, seq, SSM_INNER) * ssm_norm_g.astype(f32)).astype(h.dtype)
    return jnp.concatenate([conv_out, ssm_out], axis=-1) @ w_out


def t5_causal_bucket(dist):
    n = jnp.maximum(dist, 0)
    max_exact = REL_BUCKETS // 2
    nf = jnp.maximum(n, 1).astype(jnp.float32)
    large = max_exact + (jnp.log(nf / max_exact) / math.log(REL_MAX_DIST / max_exact)
                         * (REL_BUCKETS - max_exact)).astype(jnp.int32)
    large = jnp.minimum(large, REL_BUCKETS - 1)
    return jnp.where(n < max_exact, n, large)


def sliding_window_sink_attention(h, w_qkv, b_qkv, w_o, b_o, sinks, rel_bias):
    f32 = jnp.float32
    bsz, seq, _ = h.shape
    nb = seq // ATTN_BLOCK
    qkv = h @ w_qkv + b_qkv
    q = qkv[..., :ATTN_Q_DIM].reshape(bsz, nb, ATTN_BLOCK, ATTN_KV_HEADS, ATTN_REP, ATTN_HEAD_DIM)
    k = qkv[..., ATTN_Q_DIM:ATTN_Q_DIM + ATTN_KV_DIM].reshape(bsz, nb, ATTN_BLOCK, ATTN_KV_HEADS, ATTN_HEAD_DIM)
    v = qkv[..., ATTN_Q_DIM + ATTN_KV_DIM:].reshape(bsz, nb, ATTN_BLOCK, ATTN_KV_HEADS, ATTN_HEAD_DIM)

    def band(t):
        prev = jnp.pad(t[:, :-1], ((0, 0), (1, 0), (0, 0), (0, 0), (0, 0)))
        return jnp.concatenate([prev, t], axis=2)

    kb, vb = band(k), band(v)
    scores = jnp.einsum('bnqgrd,bnkgd->bngrqk', q.astype(f32), kb.astype(f32)) * (ATTN_HEAD_DIM ** -0.5)
    dist = jnp.arange(ATTN_BLOCK)[:, None] + ATTN_BLOCK - jnp.arange(2 * ATTN_BLOCK)[None, :]
    bias = rel_bias[t5_causal_bucket(dist)].astype(f32)
    bias = bias.transpose(2, 0, 1).reshape(ATTN_KV_HEADS, ATTN_REP, ATTN_BLOCK, 2 * ATTN_BLOCK)
    key_pos = (jnp.arange(nb)[:, None] - 1) * ATTN_BLOCK + jnp.arange(2 * ATTN_BLOCK)[None, :]
    mask = ((dist >= 0) & (dist < WINDOW))[None] & (key_pos >= 0)[:, None, :]
    logits = jnp.where(mask[None, :, None, None], scores + bias, -jnp.inf)
    sink = sinks.astype(f32).reshape(ATTN_KV_HEADS, ATTN_REP)[None, None, :, :, None, None]
    m = jnp.maximum(jnp.max(logits, axis=-1, keepdims=True), sink)
    p = jnp.exp(logits - m)
    probs = p / (jnp.sum(p, axis=-1, keepdims=True) + jnp.exp(sink - m))
    out = jnp.einsum('bngrqk,bnkgd->bnqgrd', probs, vb.astype(f32)).astype(h.dtype)
    return out.reshape(bsz, seq, ATTN_Q_DIM) @ w_o + b_o


def swiglu(h, w_gate, w_up, w_down):
    return (jax.nn.silu(h @ w_gate) * (h @ w_up)) @ w_down


def moe_swiglu(h, router_w, router_b, w_gate, w_up, w_down):
    f32 = jnp.float32
    bsz, seq, d = h.shape
    tok = h.reshape(-1, d)
    n_tok = tok.shape[0]
    n_assign = n_tok * TOP_K
    logits = tok.astype(f32) @ router_w.astype(f32) + router_b.astype(f32)
    top_logit, top_e = lax.top_k(logits, TOP_K)
    gate = jax.nn.softmax(top_logit, axis=-1)
    flat_e = top_e.reshape(-1)
    flat_tok = jnp.repeat(jnp.arange(n_tok), TOP_K)
    flat_gate = gate.reshape(-1)
    order = jnp.argsort(flat_e)
    s_e, s_tok, s_gate = flat_e[order], flat_tok[order], flat_gate[order]
    counts = jnp.bincount(flat_e, length=N_EXPERTS)
    start = jnp.cumsum(counts) - counts
    padded = (counts + MOE_BLOCK - 1) // MOE_BLOCK * MOE_BLOCK
    padded_end = jnp.cumsum(padded)
    dest = padded_end[s_e] - padded[s_e] + jnp.arange(n_assign) - start[s_e]
    n_rows = n_assign + N_EXPERTS * MOE_BLOCK
    n_blocks = n_rows // MOE_BLOCK
    rows = jnp.zeros((n_rows, d), tok.dtype).at[dest].set(tok[s_tok])
    block_e = jnp.minimum(jnp.searchsorted(padded_end, jnp.arange(n_blocks) * MOE_BLOCK, side='right'),
                          N_EXPERTS - 1)

    def expert_block(args):
        xb, e = args
        return (jax.nn.silu(xb @ w_gate[e]) * (xb @ w_up[e])) @ w_down[e]

    out_rows = lax.map(expert_block, (rows.reshape(n_blocks, MOE_BLOCK, d), block_e)).reshape(n_rows, d)
    contrib = out_rows[dest].astype(f32) * s_gate[:, None]
    y = jnp.zeros((n_tok, d), f32).at[s_tok].add(contrib)
    return y.astype(h.dtype).reshape(bsz, seq, d)


def setup_inputs(seed: int = 0) -> dict:
    key = jax.random.key(seed)
    ks = jax.random.split(key, 32)
    f32 = jnp.float32

    def nrm(k, shape, scale):
        return jax.random.normal(k, shape, f32) * scale

    dt = jnp.exp(jax.random.uniform(ks[11], (N_EVEN, SSM_HEADS), f32) * (math.log(0.1) - math.log(1e-3))
                 + math.log(1e-3))
    return {
        'x': nrm(ks[0], (BATCH, SEQ, D_MODEL), 1.0),
        'mix_norm_g': 1.0 + nrm(ks[1], (DEPTH, D_MODEL), 0.02),
        'ffn_norm_g': 1.0 + nrm(ks[2], (DEPTH, D_MODEL), 0.02),
        'final_norm_g': 1.0 + nrm(ks[3], (D_MODEL,), 0.02),
        'w_in': nrm(ks[4], (N_EVEN, D_MODEL, IN_PROJ_DIM), D_MODEL ** -0.5),
        'conv_w': nrm(ks[5], (N_EVEN, CONV_WIDTH, CONV_DIM), CONV_WIDTH ** -0.5),
        'conv_b': nrm(ks[6], (N_EVEN, CONV_DIM), 0.02),
        'conv_ln_g': 1.0 + nrm(ks[7], (N_EVEN, CONV_DIM), 0.02),
        'conv_ln_b': nrm(ks[8], (N_EVEN, CONV_DIM), 0.02),
        'ssm_conv_w': nrm(ks[9], (N_EVEN, SSM_CONV_WIDTH, SSM_XBC), SSM_CONV_WIDTH ** -0.5),
        'ssm_conv_b': nrm(ks[10], (N_EVEN, SSM_XBC), 0.02),
        'dt_bias': dt + jnp.log(-jnp.expm1(-dt)),
        'a_log': jnp.log(jax.random.uniform(ks[12], (N_EVEN, SSM_HEADS), f32, 1.0, 16.0)),
        'd_skip': 1.0 + nrm(ks[13], (N_EVEN, SSM_HEADS), 0.1),
        'ssm_norm_g': 1.0 + nrm(ks[14], (N_EVEN, SSM_INNER), 0.02),
        'w_out': nrm(ks[15], (N_EVEN, MIX_DIM, D_MODEL), MIX_DIM ** -0.5),
        'ffn_w_gate': nrm(ks[16], (N_EVEN, D_MODEL, FFN_DIM), D_MODEL ** -0.5),
        'ffn_w_up': nrm(ks[17], (N_EVEN, D_MODEL, FFN_DIM), D_MODEL ** -0.5),
        'ffn_w_down': nrm(ks[18], (N_EVEN, FFN_DIM, D_MODEL), FFN_DIM ** -0.5),
        'w_qkv': nrm(ks[19], (N_ODD, D_MODEL, QKV_DIM), D_MODEL ** -0.5),
        'b_qkv': nrm(ks[20], (N_ODD, QKV_DIM), 0.02),
        'w_o': nrm(ks[21], (N_ODD, ATTN_Q_DIM, D_MODEL), ATTN_Q_DIM ** -0.5),
        'b_o': nrm(ks[22], (N_ODD, D_MODEL), 0.02),
        'sinks': nrm(ks[23], (N_ODD, ATTN_Q_HEADS), 0.5),
        'rel_bias': nrm(ks[24], (REL_BUCKETS, ATTN_Q_HEADS), 0.5),
        'router_w': nrm(ks[25], (N_ODD, D_MODEL, N_EXPERTS), D_MODEL ** -0.5),
        'router_b': nrm(ks[26], (N_ODD, N_EXPERTS), 0.01),
        'moe_w_gate': nrm(ks[27], (N_ODD, N_EXPERTS, D_MODEL, EXPERT_DIM), D_MODEL ** -0.5),
        'moe_w_up': nrm(ks[28], (N_ODD, N_EXPERTS, D_MODEL, EXPERT_DIM), D_MODEL ** -0.5),
        'moe_w_down': nrm(ks[29], (N_ODD, N_EXPERTS, EXPERT_DIM, D_MODEL), EXPERT_DIM ** -0.5),
    }


def reference(x, mix_norm_g, ffn_norm_g, final_norm_g, w_in, conv_w, conv_b, conv_ln_g, conv_ln_b,
              ssm_conv_w, ssm_conv_b, dt_bias, a_log, d_skip, ssm_norm_g, w_out,
              ffn_w_gate, ffn_w_up, ffn_w_down, w_qkv, b_qkv, w_o, b_o, sinks, rel_bias,
              router_w, router_b, moe_w_gate, moe_w_up, moe_w_down):
    h = x
    for layer in range(DEPTH):
        i = layer // 2
        u = rms_norm(h, mix_norm_g[layer])
        if layer % 2 == 0:
            h = h + conv_ssd_mixer(u, w_in[i], conv_w[i], conv_b[i], conv_ln_g[i], conv_ln_b[i],
                                   ssm_conv_w[i], ssm_conv_b[i], dt_bias[i], a_log[i], d_skip[i],
                                   ssm_norm_g[i], w_out[i])
            u = rms_norm(h, ffn_norm_g[layer])
            h = h + swiglu(u, ffn_w_gate[i], ffn_w_up[i], ffn_w_down[i])
        else:
            h = h + sliding_window_sink_attention(u, w_qkv[i], b_qkv[i], w_o[i], b_o[i], sinks[i], rel_bias)
            u = rms_norm(h, ffn_norm_g[layer])
            h = h + moe_swiglu(u, router_w[i], router_b[i], moe_w_gate[i], moe_w_up[i], moe_w_down[i])
    return rms_norm(h, final_norm_g)
```

```python
import functools
import math

import numpy as np
import jax
import jax.numpy as jnp
from jax import lax
from jax.experimental import pallas as pl
from jax.experimental.pallas import tpu as pltpu

F32 = jnp.float32
BF16 = jnp.bfloat16
NORM_EPS = 1e-5
NEG_INF = float("-inf")

LANES = 128
VMEM_LIMIT = 56 << 20

D_MODEL = 2048
CONV_WIDTH = 31
SSM_HEADS = 32
SSM_HEAD_DIM = 64
SSM_GROUPS = 8
SSM_STATE = 128
SSM_CONV_WIDTH = 4
SSM_CHUNK = 128
ATTN_Q_HEADS = 32
ATTN_KV_HEADS = 4
ATTN_HEAD_DIM = 64
ATTN_BLOCK = 128
REL_BUCKETS = 32
REL_MAX_DIST = 128
N_EXPERTS = 8
TOP_K = 2
MOE_SUB = 256
MOE_SUBS_PER_GROUP = 4
MOE_GROUP = MOE_SUB * MOE_SUBS_PER_GROUP


def _params(sem):
    return pltpu.CompilerParams(dimension_semantics=sem, vmem_limit_bytes=VMEM_LIMIT)


def _sigmoid(x):
    return 1.0 / (1.0 + jnp.exp(-x))


def _rms_rows(x, g):
    ms = jnp.mean(x * x, axis=-1, keepdims=True)
    return x * lax.rsqrt(ms + NORM_EPS) * g


def _norm_matmul_kernel(x_ref, g_ref, w_ref, b_ref, o_ref, xn_ref):
    @pl.when(pl.program_id(1) == 0)
    def _():
        xn_ref[...] = _rms_rows(x_ref[...], g_ref[...]).astype(BF16)

    acc = jnp.dot(xn_ref[...], w_ref[...], preferred_element_type=F32)
    o_ref[...] = (acc + b_ref[...]).astype(o_ref.dtype)


def _norm_matmul(x, g, w, b, *, tm, tn, out_dtype, name):
    m, k = x.shape
    n = w.shape[1]
    return pl.pallas_call(
        _norm_matmul_kernel,
        grid=(m // tm, n // tn),
        in_specs=[
            pl.BlockSpec((tm, k), lambda i, j: (i, 0)),
            pl.BlockSpec((1, k), lambda i, j: (0, 0)),
            pl.BlockSpec((k, tn), lambda i, j: (0, j)),
            pl.BlockSpec((1, tn), lambda i, j: (0, j)),
        ],
        out_specs=pl.BlockSpec((tm, tn), lambda i, j: (i, j)),
        out_shape=jax.ShapeDtypeStruct((m, n), out_dtype),
        scratch_shapes=[pltpu.VMEM((tm, k), BF16)],
        compiler_params=_params(("parallel", "arbitrary")),
        name=name,
    )(x, g, w, b)


def _matmul_residual_kernel(*refs, n_in):
    a_refs, w_refs = refs[:n_in], refs[n_in:2 * n_in]
    res_ref, b_ref, o_ref = refs[2 * n_in:]
    acc = res_ref[...] + b_ref[...]
    for a_ref, w_ref in zip(a_refs, w_refs):
        acc = acc + jnp.dot(a_ref[...], w_ref[...], preferred_element_type=F32)
    o_ref[...] = acc


def _matmul_residual(a_list, w_list, res, b, *, tm, tn, name):
    m, n = res.shape
    n_in = len(a_list)
    in_specs = [pl.BlockSpec((tm, a.shape[1]), lambda j, i: (i, 0)) for a in a_list]
    in_specs += [pl.BlockSpec((w.shape[0], tn), lambda j, i: (0, j)) for w in w_list]
    in_specs += [pl.BlockSpec((tm, tn), lambda j, i: (i, j)), pl.BlockSpec((1, tn), lambda j, i: (0, j))]
    return pl.pallas_call(
        functools.partial(_matmul_residual_kernel, n_in=n_in),
        grid=(n // tn, m // tm),
        in_specs=in_specs,
        out_specs=pl.BlockSpec((tm, tn), lambda j, i: (i, j)),
        out_shape=jax.ShapeDtypeStruct((m, n), F32),
        compiler_params=_params(("parallel", "parallel")),
        name=name,
    )(*a_list, *w_list, res, b)


def _swiglu_kernel(x_ref, g_ref, wg_ref, wu_ref, wd_ref, o_ref, xn_ref):
    @pl.when(pl.program_id(1) == 0)
    def _():
        x = x_ref[...]
        xn_ref[...] = _rms_rows(x, g_ref[...]).astype(BF16)
        o_ref[...] = x

    xn = xn_ref[...]
    gate = jnp.dot(xn, wg_ref[...], preferred_element_type=F32)
    up = jnp.dot(xn, wu_ref[...], preferred_element_type=F32)
    hid = (gate * _sigmoid(gate) * up).astype(BF16)
    o_ref[...] += jnp.dot(hid, wd_ref[...], preferred_element_type=F32)


def _swiglu(x, g, wg, wu, wd, *, tm, tf):
    m, d = x.shape
    f = wg.shape[1]
    return pl.pallas_call(
        _swiglu_kernel,
        grid=(m // tm, f // tf),
        in_specs=[
            pl.BlockSpec((tm, d), lambda i, j: (i, 0)),
            pl.BlockSpec((1, d), lambda i, j: (0, 0)),
            pl.BlockSpec((d, tf), lambda i, j: (0, j)),
            pl.BlockSpec((d, tf), lambda i, j: (0, j)),
            pl.BlockSpec((tf, d), lambda i, j: (j, 0)),
        ],
        out_specs=pl.BlockSpec((tm, d), lambda i, j: (i, 0)),
        out_shape=jax.ShapeDtypeStruct((m, d), F32),
        scratch_shapes=[pltpu.VMEM((tm, d), BF16)],
        compiler_params=_params(("parallel", "arbitrary")),
        name="swiglu",
    )(x, g, wg, wu, wd)


CONV_HALO = 32
CONV_ROWS = 64


def _conv_module_kernel(val_ref, gate_ref, pval_ref, pgate_ref, w_ref, b_ref, lg_ref, lb_ref, o_ref,
                        ubuf, cbuf, *, tt):
    i = pl.program_id(0)

    def glu(v, g):
        return v.astype(F32) * _sigmoid(g.astype(F32))

    ubuf[0:CONV_HALO, :] = jnp.where(i > 0, glu(pval_ref[...], pgate_ref[...]), 0.0)
    ubuf[CONV_HALO:CONV_HALO + tt, :] = glu(val_ref[...], gate_ref[...])

    off = CONV_HALO - (CONV_WIDTH - 1)
    n_ch = ubuf.shape[1]
    for cb in range(n_ch // LANES):
        ls = slice(cb * LANES, (cb + 1) * LANES)
        for r0 in range(0, tt, CONV_ROWS):
            acc = jnp.zeros((CONV_ROWS, LANES), F32)
            for j in range(CONV_WIDTH):
                acc = acc + w_ref[j:j + 1, ls] * ubuf[off + r0 + j:off + r0 + j + CONV_ROWS, ls]
            cbuf[r0:r0 + CONV_ROWS, ls] = acc

    c = cbuf[...] + b_ref[...]
    mu = jnp.mean(c, axis=-1, keepdims=True)
    d = c - mu
    var = jnp.mean(d * d, axis=-1, keepdims=True)
    y = d * lax.rsqrt(var + NORM_EPS) * lg_ref[...] + lb_ref[...]
    o_ref[...] = (y * _sigmoid(y)).astype(o_ref.dtype)


def _conv_module(proj, conv_w, conv_b, ln_g, ln_b, *, tt):
    m = proj.shape[0]
    c = D_MODEL
    hb = tt // CONV_HALO
    return pl.pallas_call(
        functools.partial(_conv_module_kernel, tt=tt),
        grid=(m // tt,),
        in_specs=[
            pl.BlockSpec((tt, c), lambda i: (i, 0)),
            pl.BlockSpec((tt, c), lambda i: (i, 1)),
            pl.BlockSpec((CONV_HALO, c), lambda i: (jnp.maximum(i * hb - 1, 0), 0)),
            pl.BlockSpec((CONV_HALO, c), lambda i: (jnp.maximum(i * hb - 1, 0), 1)),
            pl.BlockSpec((CONV_WIDTH, c), lambda i: (0, 0)),
            pl.BlockSpec((1, c), lambda i: (0, 0)),
            pl.BlockSpec((1, c), lambda i: (0, 0)),
            pl.BlockSpec((1, c), lambda i: (0, 0)),
        ],
        out_specs=pl.BlockSpec((tt, c), lambda i: (i, 0)),
        out_shape=jax.ShapeDtypeStruct((m, c), BF16),
        scratch_shapes=[pltpu.VMEM((CONV_HALO + tt, c), F32), pltpu.VMEM((tt, c), F32)],
        compiler_params=_params(("arbitrary",)),
        name="conv_module",
    )(proj, proj, proj, proj, conv_w, conv_b, ln_g, ln_b)


SSM_HALO = 16


def _conv4_silu(cur_ref, prev_ref, w_ref, b_ref, buf_ref, first):
    q = cur_ref.shape[0]
    buf_ref[0:SSM_HALO, :] = jnp.where(first, 0.0, prev_ref[...].astype(F32))
    buf_ref[SSM_HALO:SSM_HALO + q, :] = cur_ref[...].astype(F32)
    off = SSM_HALO - (SSM_CONV_WIDTH - 1)
    acc = b_ref[...] + w_ref[0:1, :] * buf_ref[off:off + q, :]
    for j in range(1, SSM_CONV_WIDTH):
        acc = acc + w_ref[j:j + 1, :] * buf_ref[off + j:off + j + q, :]
    return acc * _sigmoid(acc)


def _ssd_kernel(z_ref, x_ref, b_ref, c_ref, px_ref, pb_ref, pc_ref, dtr_ref,
                wx_ref, wb_ref, wc_ref, bx_ref, bb_ref, bc_ref,
                dtb_ref, a_ref, dskip_ref, ng_ref, e_ref, o_ref,
                state, xbuf, bbuf, cbuf, ybuf):
    i = pl.program_id(0)
    q = SSM_CHUNK
    first = i == 0

    @pl.when(first)
    def _():
        state[...] = jnp.zeros_like(state)

    xs = _conv4_silu(x_ref, px_ref, wx_ref, bx_ref, xbuf, first)
    bm = _conv4_silu(b_ref, pb_ref, wb_ref, bb_ref, bbuf, first)
    cm = _conv4_silu(c_ref, pc_ref, wc_ref, bc_ref, cbuf, first)

    pre = dtr_ref[...] + dtb_ref[...]
    dt = jnp.maximum(pre, 0.0) + jnp.log(1.0 + jnp.exp(-jnp.abs(pre)))
    dta = dt * a_ref[...]
    row = lax.broadcasted_iota(jnp.int32, (q, q), 0)
    col = lax.broadcasted_iota(jnp.int32, (q, q), 1)
    causal = row >= col
    acs = jnp.dot(causal.astype(F32), dta, precision=lax.Precision.HIGHEST, preferred_element_type=F32)
    acs_t = acs.T
    dt_t = dt.T
    eacs = jnp.exp(acs)
    wdec = dt * jnp.exp(acs[q - 1:q, :] - acs)

    def expand(v):
        hi = v.astype(BF16)
        lo = (v - hi.astype(F32)).astype(BF16)
        return (jnp.dot(hi, e_ref[...], preferred_element_type=F32)
                + jnp.dot(lo, e_ref[...], preferred_element_type=F32))

    eacs_full = expand(eacs)
    xd = (xs * expand(wdec)).astype(BF16)

    xs_b = xs.astype(BF16)
    lane = lax.broadcasted_iota(jnp.int32, (1, xs.shape[1]), 1)
    low_head = (lane % LANES) < SSM_HEAD_DIM
    xs_lo = jnp.where(low_head, xs_b, jnp.zeros_like(xs_b))
    xs_hi = jnp.where(low_head, jnp.zeros_like(xs_b), xs_b)

    gw = SSM_HEAD_DIM * (SSM_HEADS // SSM_GROUPS)
    for g in range(SSM_GROUPS):
        ns = slice(g * SSM_STATE, (g + 1) * SSM_STATE)
        gs = slice(g * gw, (g + 1) * gw)
        bg_f = bm[:, ns]
        bg = bg_f.astype(BF16)
        cg = cm[:, ns].astype(BF16)
        cb = lax.dot_general(cg, bg, (((1,), (1,)), ((), ())), preferred_element_type=F32)
        prev = state[g]
        y_off = jnp.dot(cg, prev.astype(BF16), preferred_element_type=F32)
        for c in range(gw // LANES):
            cs = slice(g * gw + c * LANES, g * gw + (c + 1) * LANES)
            acc = y_off[:, c * LANES:(c + 1) * LANES] * eacs_full[:, cs]
            for par, xpart in ((0, xs_lo), (1, xs_hi)):
                h = g * (SSM_HEADS // SSM_GROUPS) + 2 * c + par
                seg = acs[:, h:h + 1] - acs_t[h:h + 1, :]
                mmat = cb * jnp.exp(jnp.where(causal, seg, NEG_INF)) * dt_t[h:h + 1, :]
                acc = acc + jnp.dot(mmat.astype(BF16), xpart[:, cs], preferred_element_type=F32)
            ybuf[:, cs] = acc
        new_states = jnp.dot(bg_f.T.astype(BF16), xd[:, gs], preferred_element_type=F32)
        state[g] = prev * eacs_full[q - 1:q, gs] + new_states

    y = ybuf[...] + xs * dskip_ref[...]
    z = z_ref[...].astype(F32)
    y = y * (z * _sigmoid(z))
    for g in range(SSM_GROUPS):
        gs = slice(g * gw, (g + 1) * gw)
        yg = y[:, gs]
        ms = jnp.mean(yg * yg, axis=-1, keepdims=True)
        o_ref[:, gs] = (yg * lax.rsqrt(ms + NORM_EPS) * ng_ref[:, gs]).astype(o_ref.dtype)


def _ssd(proj, dt_raw, conv_w, conv_b, dt_bias, a_neg, d_skip, norm_g):
    m = proj.shape[0]
    q = SSM_CHUNK
    inner = D_MODEL
    bc = SSM_GROUPS * SSM_STATE
    hb = q // SSM_HALO
    wx, wb, wc = conv_w[:, :inner], conv_w[:, inner:inner + bc], conv_w[:, inner + bc:]
    bx, bb, bcc = conv_b[:, :inner], conv_b[:, inner:inner + bc], conv_b[:, inner + bc:]
    expand = jnp.asarray(np.arange(LANES)[:, None] == (np.arange(inner) // SSM_HEAD_DIM)[None, :], BF16)
    prev = lambda i: jnp.maximum(i * hb - 1, 0)
    full = lambda shape: pl.BlockSpec(shape, lambda i: (0,) * len(shape))
    return pl.pallas_call(
        _ssd_kernel,
        grid=(m // q,),
        in_specs=[
            pl.BlockSpec((q, inner), lambda i: (i, 2)),
            pl.BlockSpec((q, inner), lambda i: (i, 3)),
            pl.BlockSpec((q, bc), lambda i: (i, 8)),
            pl.BlockSpec((q, bc), lambda i: (i, 9)),
            pl.BlockSpec((SSM_HALO, inner), lambda i: (prev(i), 3)),
            pl.BlockSpec((SSM_HALO, bc), lambda i: (prev(i), 8)),
            pl.BlockSpec((SSM_HALO, bc), lambda i: (prev(i), 9)),
            pl.BlockSpec((q, LANES), lambda i: (i, 0)),
            full(wx.shape), full(wb.shape), full(wc.shape),
            full(bx.shape), full(bb.shape), full(bcc.shape),
            full((1, LANES)), full((1, LANES)), full((1, inner)), full((1, inner)),
            full((LANES, inner)),
        ],
        out_specs=pl.BlockSpec((q, inner), lambda i: (i, 0)),
        out_shape=jax.ShapeDtypeStruct((m, inner), BF16),
        scratch_shapes=[
            pltpu.VMEM((SSM_GROUPS, SSM_STATE, inner // SSM_GROUPS), F32),
            pltpu.VMEM((SSM_HALO + q, inner), F32),
            pltpu.VMEM((SSM_HALO + q, bc), F32),
            pltpu.VMEM((SSM_HALO + q, bc), F32),
            pltpu.VMEM((q, inner), F32),
        ],
        compiler_params=_params(("arbitrary",)),
        name="ssd",
    )(proj, proj, proj, proj, proj, proj, proj, dt_raw, wx, wb, wc, bx, bb, bcc,
      dt_bias, a_neg, d_skip, norm_g, expand)


def _attn_kernel(sinks_ref, q_ref, kvc_ref, kvp_ref, bias_ref, o_ref):
    blk = ATTN_BLOCK
    kvw = ATTN_KV_HEADS * ATTN_HEAD_DIM
    rep = ATTN_Q_HEADS // ATTN_KV_HEADS
    kv = jnp.concatenate([kvp_ref[...], kvc_ref[...]], axis=0).astype(F32)
    lane = lax.broadcasted_iota(jnp.int32, (1, LANES), 1)
    low = lane < ATTN_HEAD_DIM
    scale = ATTN_HEAD_DIM ** -0.5

    for g in range(ATTN_KV_HEADS):
        pc = g // 2
        kcol = kv[:, pc * LANES:(pc + 1) * LANES] * scale
        vcol = kv[:, kvw + pc * LANES:kvw + (pc + 1) * LANES]
        kroll = pltpu.roll(kcol, ATTN_HEAD_DIM, axis=1)
        vroll = pltpu.roll(vcol, ATTN_HEAD_DIM, axis=1)
        if g % 2 == 0:
            k_lo, k_hi = jnp.where(low, kcol, 0.0), jnp.where(low, 0.0, kroll)
            v_lo, v_hi = jnp.where(low, vcol, 0.0), jnp.where(low, 0.0, vroll)
        else:
            k_lo, k_hi = jnp.where(low, kroll, 0.0), jnp.where(low, 0.0, kcol)
            v_lo, v_hi = jnp.where(low, vroll, 0.0), jnp.where(low, 0.0, vcol)
        k_lo, k_hi, v_lo, v_hi = (t.astype(BF16) for t in (k_lo, k_hi, v_lo, v_hi))

        ncol = rep // 2
        qs = jnp.concatenate([q_ref[:, (g * ncol + c) * LANES:(g * ncol + c + 1) * LANES] for c in range(ncol)], axis=0)
        nt = (((1,), (1,)), ((), ()))
        s_even = lax.dot_general(qs, k_lo, nt, preferred_element_type=F32)
        s_odd = lax.dot_general(qs, k_hi, nt, preferred_element_type=F32)
        for c in range(ncol):
            out = None
            for par, s_all, vv in ((0, s_even, v_lo), (1, s_odd, v_hi)):
                h = g * rep + 2 * c + par
                logits = s_all[c * blk:(c + 1) * blk, :] + bias_ref[0, h]
                sink = sinks_ref[h]
                mx = jnp.maximum(jnp.max(logits, axis=-1, keepdims=True), sink)
                p = jnp.exp(logits - mx)
                denom = jnp.sum(p, axis=-1, keepdims=True) + jnp.exp(sink - mx)
                o = jnp.dot(p.astype(BF16), vv, preferred_element_type=F32) * (1.0 / denom)
                out = o if out is None else out + o
            col = g * ncol + c
            o_ref[:, col * LANES:(col + 1) * LANES] = out.astype(o_ref.dtype)


def _t5_bucket_table():
    dist = np.arange(ATTN_BLOCK)[:, None] + ATTN_BLOCK - np.arange(2 * ATTN_BLOCK)[None, :]
    n = np.maximum(dist, 0)
    max_exact = REL_BUCKETS // 2
    nf = np.maximum(n, 1).astype(np.float32)
    large = max_exact + (np.log(nf / max_exact) / math.log(REL_MAX_DIST / max_exact)
                         * (REL_BUCKETS - max_exact)).astype(np.int32)
    large = np.minimum(large, REL_BUCKETS - 1)
    return dist, np.where(n < max_exact, n, large)


def _attention(qkv, sinks, rel_bias):
    m = qkv.shape[0]
    blk = ATTN_BLOCK
    qd = ATTN_Q_HEADS * ATTN_HEAD_DIM
    kvd = 2 * ATTN_KV_HEADS * ATTN_HEAD_DIM
    dist, bucket = _t5_bucket_table()
    visible = (dist >= 0) & (dist < blk)
    bias = jnp.transpose(rel_bias.astype(F32)[bucket], (2, 0, 1))
    general = jnp.where(visible[None], bias, NEG_INF)
    first = jnp.where((visible & (np.arange(2 * blk) >= blk)[None, :])[None], bias, NEG_INF)
    table = jnp.stack([first, general])
    return pl.pallas_call(
        _attn_kernel,
        grid=(m // blk,),
        in_specs=[
            pl.BlockSpec(memory_space=pltpu.SMEM),
            pl.BlockSpec((blk, qd), lambda n: (n, 0)),
            pl.BlockSpec((blk, kvd), lambda n: (n, qd // kvd)),
            pl.BlockSpec((blk, kvd), lambda n: (jnp.maximum(n - 1, 0), qd // kvd)),
            pl.BlockSpec((1,) + table.shape[1:], lambda n: (jnp.minimum(n, 1), 0, 0, 0)),
        ],
        out_specs=pl.BlockSpec((blk, qd), lambda n: (n, 0)),
        out_shape=jax.ShapeDtypeStruct((m, qd), BF16),
        compiler_params=_params(("arbitrary",)),
        name="swa_attention",
    )(sinks.astype(F32), qkv, qkv, qkv, table)


def _router_kernel(x_ref, g_ref, rw_ref, rb_ref, o_ref):
    xn = _rms_rows(x_ref[...], g_ref[...])
    logits = jnp.dot(xn, rw_ref[...], precision=lax.Precision.HIGHEST, preferred_element_type=F32) + rb_ref[...]
    lane = lax.broadcasted_iota(jnp.int32, logits.shape, 1)
    m1 = jnp.max(logits, axis=-1, keepdims=True)
    i1 = jnp.min(jnp.where(logits == m1, lane, LANES), axis=-1, keepdims=True)
    rest = jnp.where(lane == i1, NEG_INF, logits)
    m2 = jnp.max(rest, axis=-1, keepdims=True)
    i2 = jnp.min(jnp.where(rest == m2, lane, LANES), axis=-1, keepdims=True)
    e = jnp.exp(m2 - m1)
    g1 = 1.0 / (1.0 + e)
    g2 = e * g1
    out = jnp.where(lane == 0, i1.astype(F32),
                    jnp.where(lane == 1, i2.astype(F32),
                              jnp.where(lane == 2, g1, jnp.where(lane == 3, g2, 0.0))))
    o_ref[...] = out


def _router(h, g, rw, rb, *, tm):
    m, d = h.shape
    rw_pad = jnp.zeros((d, LANES), F32).at[:, :N_EXPERTS].set(rw.astype(F32))
    rb_pad = jnp.full((1, LANES), NEG_INF, F32).at[0, :N_EXPERTS].set(rb.astype(F32))
    return pl.pallas_call(
        _router_kernel,
        grid=(m // tm,),
        in_specs=[
            pl.BlockSpec((tm, d), lambda i: (i, 0)),
            pl.BlockSpec((1, d), lambda i: (0, 0)),
            pl.BlockSpec((d, LANES), lambda i: (0, 0)),
            pl.BlockSpec((1, LANES), lambda i: (0, 0)),
        ],
        out_specs=pl.BlockSpec((tm, LANES), lambda i: (i, 0)),
        out_shape=jax.ShapeDtypeStruct((m, LANES), F32),
        compiler_params=_params(("parallel",)),
        name="router",
    )(h, g, rw_pad, rb_pad)


def _moe_kernel(ge_ref, ns_ref, tok_ref, h_hbm, g_ref, wg_ref, wu_ref, wd_ref, o_ref, xbuf, xn_ref, sem):
    grp = pl.program_id(0)
    j = pl.program_id(1)
    ns = ns_ref[grp]

    def row_copy(r):
        tok = tok_ref[grp * MOE_GROUP + r]
        return pltpu.make_async_copy(h_hbm.at[pl.ds(tok, 1)], xbuf.at[pl.ds(r, 1)], sem)

    @pl.when(j == 0)
    def _():
        nrows = ns * MOE_SUB

        def issue(r, carry):
            row_copy(r).start()
            return carry

        def wait(r, carry):
            row_copy(r).wait()
            return carry

        lax.fori_loop(0, nrows, issue, 0)
        lax.fori_loop(0, nrows, wait, 0)
        for s in range(MOE_SUBS_PER_GROUP):
            rs = slice(s * MOE_SUB, (s + 1) * MOE_SUB)
            o_ref[rs, :] = jnp.zeros((MOE_SUB, o_ref.shape[1]), F32)

            @pl.when(s < ns)
            def _():
                xn_ref[rs, :] = _rms_rows(xbuf[rs, :], g_ref[...]).astype(BF16)

    for s in range(MOE_SUBS_PER_GROUP):
        rs = slice(s * MOE_SUB, (s + 1) * MOE_SUB)

        @pl.when(s < ns)
        def _():
            xn = xn_ref[rs, :]
            gate = jnp.dot(xn, wg_ref[...], preferred_element_type=F32)
            up = jnp.dot(xn, wu_ref[...], preferred_element_type=F32)
            hid = (gate * _sigmoid(gate) * up).astype(BF16)
            o_ref[rs, :] += jnp.dot(hid, wd_ref[...], preferred_element_type=F32)


def _moe_ffn(h, g, wg, wu, wd, group_e, group_ns, src_tok, *, tf):
    t, d = h.shape
    n_groups = group_e.shape[0]
    f = wg.shape[2]
    nj = f // tf

    def jj(grp, j, ns):
        return jnp.where(ns[grp] > 0, j, nj - 1)

    return pl.pallas_call(
        _moe_kernel,
        grid_spec=pltpu.PrefetchScalarGridSpec(
            num_scalar_prefetch=3,
            grid=(n_groups, nj),
            in_specs=[
                pl.BlockSpec(memory_space=pl.ANY),
                pl.BlockSpec((1, d), lambda grp, j, ge, ns, tok: (0, 0)),
                pl.BlockSpec((None, d, tf), lambda grp, j, ge, ns, tok: (ge[grp], 0, jj(grp, j, ns))),
                pl.BlockSpec((None, d, tf), lambda grp, j, ge, ns, tok: (ge[grp], 0, jj(grp, j, ns))),
                pl.BlockSpec((None, tf, d), lambda grp, j, ge, ns, tok: (ge[grp], jj(grp, j, ns), 0)),
            ],
            out_specs=pl.BlockSpec((MOE_GROUP, d), lambda grp, j, ge, ns, tok: (grp, 0)),
            scratch_shapes=[
                pltpu.VMEM((MOE_GROUP, d), F32),
                pltpu.VMEM((MOE_GROUP, d), BF16),
                pltpu.SemaphoreType.DMA(()),
            ],
        ),
        out_shape=jax.ShapeDtypeStruct((n_groups * MOE_GROUP, d), F32),
        compiler_params=_params(("arbitrary", "arbitrary")),
        name="moe_ffn",
    )(group_e, group_ns, src_tok, h, g, wg, wu, wd)


def _combine_kernel(pos_ref, h_ref, route_ref, fg_ref, rows_hbm, o_ref, buf_a, buf_b, sem, *, tc):
    base = pl.program_id(0) * tc

    def copies(r):
        pa = pos_ref[TOP_K * (base + r)]
        pb = pos_ref[TOP_K * (base + r) + 1]
        return (pltpu.make_async_copy(rows_hbm.at[pl.ds(pa, 1)], buf_a.at[pl.ds(r, 1)], sem),
                pltpu.make_async_copy(rows_hbm.at[pl.ds(pb, 1)], buf_b.at[pl.ds(r, 1)], sem))

    def issue(r, carry):
        ca, cb = copies(r)
        ca.start()
        cb.start()
        return carry

    def wait(r, carry):
        ca, cb = copies(r)
        ca.wait()
        cb.wait()
        return carry

    lax.fori_loop(0, tc, issue, 0)
    lax.fori_loop(0, tc, wait, 0)
    route = route_ref[...]
    moe = route[:, 2:3] * buf_a[...] + route[:, 3:4] * buf_b[...]
    o_ref[...] = _rms_rows(h_ref[...] + moe, fg_ref[...])


def _combine(h, route, final_g, rows, pos, *, tc):
    t, d = h.shape
    return pl.pallas_call(
        functools.partial(_combine_kernel, tc=tc),
        grid_spec=pltpu.PrefetchScalarGridSpec(
            num_scalar_prefetch=1,
            grid=(t // tc,),
            in_specs=[
                pl.BlockSpec((tc, d), lambda i, pos: (i, 0)),
                pl.BlockSpec((tc, LANES), lambda i, pos: (i, 0)),
                pl.BlockSpec((1, d), lambda i, pos: (0, 0)),
                pl.BlockSpec(memory_space=pl.ANY),
            ],
            out_specs=pl.BlockSpec((tc, d), lambda i, pos: (i, 0)),
            scratch_shapes=[pltpu.VMEM((tc, d), F32), pltpu.VMEM((tc, d), F32), pltpu.SemaphoreType.DMA(())],
        ),
        out_shape=jax.ShapeDtypeStruct((t, d), F32),
        compiler_params=_params(("arbitrary",)),
        name="moe_combine",
    )(pos, h, route, final_g, rows)


def _moe_plan(route, n_tok):
    flat_e = route[:, :TOP_K].astype(jnp.int32).reshape(-1)
    onehot = (flat_e[:, None] == jnp.arange(N_EXPERTS, dtype=jnp.int32)[None, :]).astype(jnp.int32)
    csum = jnp.cumsum(onehot, axis=0)
    rank = jnp.sum(onehot * (csum - onehot), axis=1)
    counts = csum[-1]
    n_groups = (n_tok * TOP_K) // MOE_GROUP + N_EXPERTS
    groups_e = (counts + MOE_GROUP - 1) // MOE_GROUP
    group_end = jnp.cumsum(groups_e)
    group_start = group_end - groups_e
    pos = (group_start * MOE_GROUP)[flat_e] + rank
    flat_tok = jnp.arange(n_tok * TOP_K, dtype=jnp.int32) // TOP_K
    src_tok = jnp.zeros((n_groups * MOE_GROUP,), jnp.int32).at[pos].set(flat_tok)
    gidx = jnp.arange(n_groups, dtype=jnp.int32)
    used = gidx < group_end[-1]
    ge = jnp.minimum(jnp.searchsorted(group_end, gidx, side="right"), N_EXPERTS - 1).astype(jnp.int32)
    last_e = ge[jnp.maximum(group_end[-1] - 1, 0)]
    ge = jnp.where(used, ge, last_e)
    subs_e = (counts + MOE_SUB - 1) // MOE_SUB
    ns = jnp.clip(subs_e[ge] - MOE_SUBS_PER_GROUP * (gidx - group_start[ge]), 0, MOE_SUBS_PER_GROUP)
    ns = jnp.where(used, ns, 0).astype(jnp.int32)
    return ge, ns, src_tok, pos.astype(jnp.int32)


def kernel(x, mix_norm_g, ffn_norm_g, final_norm_g, w_in, conv_w, conv_b, conv_ln_g, conv_ln_b, ssm_conv_w, ssm_conv_b, dt_bias, a_log, d_skip, ssm_norm_g, w_out, ffn_w_gate, ffn_w_up, ffn_w_down, w_qkv, b_qkv, w_o, b_o, sinks, rel_bias, router_w, router_b, moe_w_gate, moe_w_up, moe_w_down):
    bsz, seq, d = x.shape
    assert bsz == 1 and d == D_MODEL and seq % 512 == 0
    m = seq
    h = x.reshape(m, d)
    row = lambda v: v.reshape(1, -1).astype(F32)
    tm = min(1024, m)

    main_w = 2 * D_MODEL + D_MODEL + (D_MODEL + 2 * SSM_GROUPS * SSM_STATE)
    w_main = w_in[0][:, :main_w].astype(BF16)
    w_dt = jnp.zeros((d, LANES), F32).at[:, :SSM_HEADS].set(w_in[0][:, main_w:]).astype(BF16)
    g0 = row(mix_norm_g[0])
    proj = _norm_matmul(h, g0, w_main, jnp.zeros((1, main_w), F32), tm=tm, tn=1024, out_dtype=BF16, name="in_proj")
    dt_raw = _norm_matmul(h, g0, w_dt, jnp.zeros((1, LANES), F32), tm=tm, tn=LANES, out_dtype=F32, name="dt_proj")

    conv_out = _conv_module(proj, conv_w[0].astype(F32), row(conv_b[0]), row(conv_ln_g[0]), row(conv_ln_b[0]), tt=128)

    pad_heads = lambda v: jnp.zeros((1, LANES), F32).at[0, :SSM_HEADS].set(v.astype(F32))
    ssm_out = _ssd(proj, dt_raw, ssm_conv_w[0].astype(F32), row(ssm_conv_b[0]),
                   pad_heads(dt_bias[0]), pad_heads(-jnp.exp(a_log[0].astype(F32))),
                   row(jnp.repeat(d_skip[0].astype(F32), SSM_HEAD_DIM)), row(ssm_norm_g[0]))

    wo = w_out[0].astype(BF16)
    h = _matmul_residual([conv_out, ssm_out], [wo[:D_MODEL], wo[D_MODEL:]], h, jnp.zeros((1, d), F32),
                         tm=512, tn=1024, name="out_proj")
    h = _swiglu(h, row(ffn_norm_g[0]), ffn_w_gate[0].astype(BF16), ffn_w_up[0].astype(BF16),
                ffn_w_down[0].astype(BF16), tm=512, tf=512)

    qkv = _norm_matmul(h, row(mix_norm_g[1]), w_qkv[0].astype(BF16), row(b_qkv[0]), tm=tm, tn=1280,
                       out_dtype=BF16, name="qkv_proj")
    attn = _attention(qkv, sinks[0], rel_bias)
    h = _matmul_residual([attn], [w_o[0].astype(BF16)], h, row(b_o[0]), tm=512, tn=1024, name="attn_out_proj")

    g1 = row(ffn_norm_g[1])
    route = _router(h, g1, router_w[0], router_b[0], tm=512)
    ge, ns, src_tok, pos = _moe_plan(route, m)
    rows = _moe_ffn(h, g1, moe_w_gate[0].astype(BF16), moe_w_up[0].astype(BF16), moe_w_down[0].astype(BF16),
                    ge, ns, src_tok, tf=512)
    out = _combine(h, route, row(final_norm_g), rows, pos, tc=256)
    return out.reshape(bsz, seq, d)
```

```python
import functools
import math

import numpy as np
import jax
import jax.numpy as jnp
from jax import lax
from jax.experimental import pallas as pl
from jax.experimental.pallas import tpu as pltpu

F32 = jnp.float32
BF16 = jnp.bfloat16
NORM_EPS = 1e-5
NEG_INF = float("-inf")

LANES = 128
SUBLANES = 8
VMEM_LIMIT = 56 << 20

D_MODEL = 2048
CONV_WIDTH = 31
SSM_HEADS = 32
SSM_HEAD_DIM = 64
SSM_GROUPS = 8
SSM_STATE = 128
SSM_CONV_WIDTH = 4
SSM_CHUNK = 128
ATTN_Q_HEADS = 32
ATTN_KV_HEADS = 4
ATTN_HEAD_DIM = 64
ATTN_BLOCK = 128
REL_BUCKETS = 32
REL_MAX_DIST = 128
N_EXPERTS = 8
TOP_K = 2
MOE_SUB = 256
MOE_SUBS_PER_GROUP = 4
MOE_GROUP = MOE_SUB * MOE_SUBS_PER_GROUP
MOE_UNROLL = 8


def _params(sem):
    return pltpu.CompilerParams(dimension_semantics=sem, vmem_limit_bytes=VMEM_LIMIT)


def _sigmoid(x):
    return 1.0 / (1.0 + jnp.exp(-x))


def _rms_rows(x, g):
    ms = jnp.mean(x * x, axis=-1, keepdims=True)
    return x * lax.rsqrt(ms + NORM_EPS) * g


def _norm_matmul_kernel(x_ref, g_ref, w_ref, b_ref, o_ref, xn_ref):
    @pl.when(pl.program_id(1) == 0)
    def _():
        xn_ref[...] = _rms_rows(x_ref[...], g_ref[...]).astype(BF16)

    acc = jnp.dot(xn_ref[...], w_ref[...], preferred_element_type=F32)
    o_ref[...] = (acc + b_ref[...]).astype(o_ref.dtype)


def _norm_matmul(x, g, w, b, *, tm, tn, out_dtype, name):
    m, k = x.shape
    n = w.shape[1]
    return pl.pallas_call(
        _norm_matmul_kernel,
        grid=(m // tm, n // tn),
        in_specs=[
            pl.BlockSpec((tm, k), lambda i, j: (i, 0)),
            pl.BlockSpec((1, k), lambda i, j: (0, 0)),
            pl.BlockSpec((k, tn), lambda i, j: (0, j)),
            pl.BlockSpec((1, tn), lambda i, j: (0, j)),
        ],
        out_specs=pl.BlockSpec((tm, tn), lambda i, j: (i, j)),
        out_shape=jax.ShapeDtypeStruct((m, n), out_dtype),
        scratch_shapes=[pltpu.VMEM((tm, k), BF16)],
        compiler_params=_params(("parallel", "arbitrary")),
        name=name,
    )(x, g, w, b)


def _in_proj_kernel(x_ref, g_ref, w_ref, wdt_ref, o_ref, odt_ref, xn_ref):
    @pl.when(pl.program_id(1) == 0)
    def _():
        xn_ref[...] = _rms_rows(x_ref[...], g_ref[...]).astype(BF16)
        odt_ref[...] = jnp.dot(xn_ref[...], wdt_ref[...], preferred_element_type=F32)

    o_ref[...] = jnp.dot(xn_ref[...], w_ref[...], preferred_element_type=F32).astype(o_ref.dtype)


def _in_proj(x, g, w, wdt, *, tm, tn):
    m, k = x.shape
    n = w.shape[1]
    ndt = wdt.shape[1]
    return pl.pallas_call(
        _in_proj_kernel,
        grid=(m // tm, n // tn),
        in_specs=[
            pl.BlockSpec((tm, k), lambda i, j: (i, 0)),
            pl.BlockSpec((1, k), lambda i, j: (0, 0)),
            pl.BlockSpec((k, tn), lambda i, j: (0, j)),
            pl.BlockSpec((k, ndt), lambda i, j: (0, 0)),
        ],
        out_specs=[pl.BlockSpec((tm, tn), lambda i, j: (i, j)), pl.BlockSpec((tm, ndt), lambda i, j: (i, 0))],
        out_shape=[jax.ShapeDtypeStruct((m, n), BF16), jax.ShapeDtypeStruct((m, ndt), F32)],
        scratch_shapes=[pltpu.VMEM((tm, k), BF16)],
        compiler_params=_params(("parallel", "arbitrary")),
        name="in_proj",
    )(x, g, w, wdt)


def _matmul_residual_kernel(*refs, n_in):
    a_refs, w_refs = refs[:n_in], refs[n_in:2 * n_in]
    res_ref, b_ref, o_ref = refs[2 * n_in:]
    acc = res_ref[...] + b_ref[...]
    for a_ref, w_ref in zip(a_refs, w_refs):
        acc = acc + jnp.dot(a_ref[...], w_ref[...], preferred_element_type=F32)
    o_ref[...] = acc


def _matmul_residual(a_list, w_list, res, b, *, tm, tn, name):
    m, n = res.shape
    n_in = len(a_list)
    in_specs = [pl.BlockSpec((tm, a.shape[1]), lambda j, i: (i, 0)) for a in a_list]
    in_specs += [pl.BlockSpec((w.shape[0], tn), lambda j, i: (0, j)) for w in w_list]
    in_specs += [pl.BlockSpec((tm, tn), lambda j, i: (i, j)), pl.BlockSpec((1, tn), lambda j, i: (0, j))]
    return pl.pallas_call(
        functools.partial(_matmul_residual_kernel, n_in=n_in),
        grid=(n // tn, m // tm),
        in_specs=in_specs,
        out_specs=pl.BlockSpec((tm, tn), lambda j, i: (i, j)),
        out_shape=jax.ShapeDtypeStruct((m, n), F32),
        compiler_params=_params(("parallel", "parallel")),
        name=name,
    )(*a_list, *w_list, res, b)


def _swiglu_kernel(x_ref, g_ref, wg_ref, wu_ref, wd_ref, o_ref, xn_ref):
    @pl.when(pl.program_id(1) == 0)
    def _():
        x = x_ref[...]
        xn_ref[...] = _rms_rows(x, g_ref[...]).astype(BF16)
        o_ref[...] = x

    xn = xn_ref[...]
    gate = jnp.dot(xn, wg_ref[...], preferred_element_type=F32)
    up = jnp.dot(xn, wu_ref[...], preferred_element_type=F32)
    hid = (gate * _sigmoid(gate) * up).astype(BF16)
    o_ref[...] += jnp.dot(hid, wd_ref[...], preferred_element_type=F32)


def _swiglu(x, g, wg, wu, wd, *, tm, tf):
    m, d = x.shape
    f = wg.shape[1]
    return pl.pallas_call(
        _swiglu_kernel,
        grid=(m // tm, f // tf),
        in_specs=[
            pl.BlockSpec((tm, d), lambda i, j: (i, 0)),
            pl.BlockSpec((1, d), lambda i, j: (0, 0)),
            pl.BlockSpec((d, tf), lambda i, j: (0, j)),
            pl.BlockSpec((d, tf), lambda i, j: (0, j)),
            pl.BlockSpec((tf, d), lambda i, j: (j, 0)),
        ],
        out_specs=pl.BlockSpec((tm, d), lambda i, j: (i, 0)),
        out_shape=jax.ShapeDtypeStruct((m, d), F32),
        scratch_shapes=[pltpu.VMEM((tm, d), BF16)],
        compiler_params=_params(("parallel", "arbitrary")),
        name="swiglu",
    )(x, g, wg, wu, wd)


CONV_HALO = 32
CONV_ROWS = 64


def _conv_module_kernel(val_ref, gate_ref, pval_ref, pgate_ref, w_ref, b_ref, lg_ref, lb_ref, o_ref,
                        ubuf, cbuf, *, tt):
    i = pl.program_id(0)

    def glu(v, g):
        return v.astype(F32) * _sigmoid(g.astype(F32))

    ubuf[0:CONV_HALO, :] = jnp.where(i > 0, glu(pval_ref[...], pgate_ref[...]), 0.0)
    ubuf[CONV_HALO:CONV_HALO + tt, :] = glu(val_ref[...], gate_ref[...])

    off = CONV_HALO - (CONV_WIDTH - 1)
    taps = [[] for _ in range(SUBLANES)]
    for j in range(CONV_WIDTH):
        a, r = divmod(off + j, SUBLANES)
        taps[r].append((a, j))
    n_ch = ubuf.shape[1]
    for cb in range(n_ch // LANES):
        ls = slice(cb * LANES, (cb + 1) * LANES)
        for r0 in range(0, tt, CONV_ROWS):
            big = ubuf[r0:r0 + CONV_ROWS + CONV_HALO, ls]
            out = None
            for r in range(SUBLANES):
                rows = CONV_ROWS if r == 0 else CONV_ROWS + SUBLANES
                q = None
                for a, j in taps[r]:
                    term = w_ref[j:j + 1, ls] * big[a * SUBLANES:a * SUBLANES + rows, :]
                    q = term if q is None else q + term
                if r:
                    q = pltpu.roll(q, rows - r, axis=0)[:CONV_ROWS, :]
                out = q if out is None else out + q
            cbuf[r0:r0 + CONV_ROWS, ls] = out

    c = cbuf[...] + b_ref[...]
    mu = jnp.mean(c, axis=-1, keepdims=True)
    d = c - mu
    var = jnp.mean(d * d, axis=-1, keepdims=True)
    y = d * lax.rsqrt(var + NORM_EPS) * lg_ref[...] + lb_ref[...]
    o_ref[...] = (y * _sigmoid(y)).astype(o_ref.dtype)


def _conv_module(proj, conv_w, conv_b, ln_g, ln_b, *, tt):
    m = proj.shape[0]
    c = D_MODEL
    hb = tt // CONV_HALO
    return pl.pallas_call(
        functools.partial(_conv_module_kernel, tt=tt),
        grid=(m // tt,),
        in_specs=[
            pl.BlockSpec((tt, c), lambda i: (i, 0)),
            pl.BlockSpec((tt, c), lambda i: (i, 1)),
            pl.BlockSpec((CONV_HALO, c), lambda i: (jnp.maximum(i * hb - 1, 0), 0)),
            pl.BlockSpec((CONV_HALO, c), lambda i: (jnp.maximum(i * hb - 1, 0), 1)),
            pl.BlockSpec((CONV_WIDTH, c), lambda i: (0, 0)),
            pl.BlockSpec((1, c), lambda i: (0, 0)),
            pl.BlockSpec((1, c), lambda i: (0, 0)),
            pl.BlockSpec((1, c), lambda i: (0, 0)),
        ],
        out_specs=pl.BlockSpec((tt, c), lambda i: (i, 0)),
        out_shape=jax.ShapeDtypeStruct((m, c), BF16),
        scratch_shapes=[pltpu.VMEM((CONV_HALO + tt, c), F32), pltpu.VMEM((tt, c), F32)],
        compiler_params=_params(("arbitrary",)),
        name="conv_module",
    )(proj, proj, proj, proj, conv_w, conv_b, ln_g, ln_b)


SSM_HALO = 16


def _conv4_silu(cur_ref, prev_ref, w_ref, b_ref, buf_ref, first):
    q = cur_ref.shape[0]
    buf_ref[0:SSM_HALO, :] = jnp.where(first, 0.0, prev_ref[...].astype(F32))
    buf_ref[SSM_HALO:SSM_HALO + q, :] = cur_ref[...].astype(F32)
    off = SSM_HALO - (SSM_CONV_WIDTH - 1)
    acc = b_ref[...] + w_ref[0:1, :] * buf_ref[off:off + q, :]
    for j in range(1, SSM_CONV_WIDTH):
        acc = acc + w_ref[j:j + 1, :] * buf_ref[off + j:off + j + q, :]
    return acc * _sigmoid(acc)


def _ssd_kernel(z_ref, x_ref, b_ref, c_ref, px_ref, pb_ref, pc_ref, dtr_ref,
                wx_ref, wb_ref, wc_ref, bx_ref, bb_ref, bc_ref,
                dtb_ref, a_ref, dskip_ref, ng_ref, e_ref, o_ref,
                state, xbuf, bbuf, cbuf, ybuf):
    i = pl.program_id(0)
    q = SSM_CHUNK
    first = i == 0

    @pl.when(first)
    def _():
        state[...] = jnp.zeros_like(state)

    xs = _conv4_silu(x_ref, px_ref, wx_ref, bx_ref, xbuf, first)
    bm = _conv4_silu(b_ref, pb_ref, wb_ref, bb_ref, bbuf, first)
    cm = _conv4_silu(c_ref, pc_ref, wc_ref, bc_ref, cbuf, first)

    pre = dtr_ref[...] + dtb_ref[...]
    dt = jnp.maximum(pre, 0.0) + jnp.log(1.0 + jnp.exp(-jnp.abs(pre)))
    dta = dt * a_ref[...]
    row = lax.broadcasted_iota(jnp.int32, (q, q), 0)
    col = lax.broadcasted_iota(jnp.int32, (q, q), 1)
    causal = row >= col
    acs = jnp.dot(causal.astype(F32), dta, precision=lax.Precision.HIGHEST, preferred_element_type=F32)
    acs_t = acs.T
    dt_t = dt.T
    eacs = jnp.exp(acs)
    wdec = dt * jnp.exp(acs[q - 1:q, :] - acs)

    def expand(v):
        hi = v.astype(BF16)
        lo = (v - hi.astype(F32)).astype(BF16)
        return (jnp.dot(hi, e_ref[...], preferred_element_type=F32)
                + jnp.dot(lo, e_ref[...], preferred_element_type=F32))

    eacs_full = expand(eacs)
    xd = (xs * expand(wdec)).astype(BF16)

    xs_b = xs.astype(BF16)
    lane = lax.broadcasted_iota(jnp.int32, (1, xs.shape[1]), 1)
    low_head = (lane % LANES) < SSM_HEAD_DIM
    xs_lo = jnp.where(low_head, xs_b, jnp.zeros_like(xs_b))
    xs_hi = jnp.where(low_head, jnp.zeros_like(xs_b), xs_b)

    gw = SSM_HEAD_DIM * (SSM_HEADS // SSM_GROUPS)
    for g in range(SSM_GROUPS):
        ns = slice(g * SSM_STATE, (g + 1) * SSM_STATE)
        gs = slice(g * gw, (g + 1) * gw)
        bg_f = bm[:, ns]
        bg = bg_f.astype(BF16)
        cg = cm[:, ns].astype(BF16)
        cb = lax.dot_general(cg, bg, (((1,), (1,)), ((), ())), preferred_element_type=F32)
        prev = state[g]
        y_off = jnp.dot(cg, prev.astype(BF16), preferred_element_type=F32)
        for c in range(gw // LANES):
            cs = slice(g * gw + c * LANES, g * gw + (c + 1) * LANES)
            acc = y_off[:, c * LANES:(c + 1) * LANES] * eacs_full[:, cs]
            for par, xpart in ((0, xs_lo), (1, xs_hi)):
                h = g * (SSM_HEADS // SSM_GROUPS) + 2 * c + par
                seg = acs[:, h:h + 1] - acs_t[h:h + 1, :]
                mmat = cb * jnp.exp(jnp.where(causal, seg, NEG_INF)) * dt_t[h:h + 1, :]
                acc = acc + jnp.dot(mmat.astype(BF16), xpart[:, cs], preferred_element_type=F32)
            ybuf[:, cs] = acc
        new_states = jnp.dot(bg_f.T.astype(BF16), xd[:, gs], preferred_element_type=F32)
        state[g] = prev * eacs_full[q - 1:q, gs] + new_states

    y = ybuf[...] + xs * dskip_ref[...]
    z = z_ref[...].astype(F32)
    y = y * (z * _sigmoid(z))
    for g in range(SSM_GROUPS):
        gs = slice(g * gw, (g + 1) * gw)
        yg = y[:, gs]
        ms = jnp.mean(yg * yg, axis=-1, keepdims=True)
        o_ref[:, gs] = (yg * lax.rsqrt(ms + NORM_EPS) * ng_ref[:, gs]).astype(o_ref.dtype)


def _ssd(proj, dt_raw, conv_w, conv_b, dt_bias, a_neg, d_skip, norm_g):
    m = proj.shape[0]
    q = SSM_CHUNK
    inner = D_MODEL
    bc = SSM_GROUPS * SSM_STATE
    hb = q // SSM_HALO
    wx, wb, wc = conv_w[:, :inner], conv_w[:, inner:inner + bc], conv_w[:, inner + bc:]
    bx, bb, bcc = conv_b[:, :inner], conv_b[:, inner:inner + bc], conv_b[:, inner + bc:]
    expand = jnp.asarray(np.arange(LANES)[:, None] == (np.arange(inner) // SSM_HEAD_DIM)[None, :], BF16)
    prev = lambda i: jnp.maximum(i * hb - 1, 0)
    full = lambda shape: pl.BlockSpec(shape, lambda i: (0,) * len(shape))
    return pl.pallas_call(
        _ssd_kernel,
        grid=(m // q,),
        in_specs=[
            pl.BlockSpec((q, inner), lambda i: (i, 2)),
            pl.BlockSpec((q, inner), lambda i: (i, 3)),
            pl.BlockSpec((q, bc), lambda i: (i, 8)),
            pl.BlockSpec((q, bc), lambda i: (i, 9)),
            pl.BlockSpec((SSM_HALO, inner), lambda i: (prev(i), 3)),
            pl.BlockSpec((SSM_HALO, bc), lambda i: (prev(i), 8)),
            pl.BlockSpec((SSM_HALO, bc), lambda i: (prev(i), 9)),
            pl.BlockSpec((q, LANES), lambda i: (i, 0)),
            full(wx.shape), full(wb.shape), full(wc.shape),
            full(bx.shape), full(bb.shape), full(bcc.shape),
            full((1, LANES)), full((1, LANES)), full((1, inner)), full((1, inner)),
            full((LANES, inner)),
        ],
        out_specs=pl.BlockSpec((q, inner), lambda i: (i, 0)),
        out_shape=jax.ShapeDtypeStruct((m, inner), BF16),
        scratch_shapes=[
            pltpu.VMEM((SSM_GROUPS, SSM_STATE, inner // SSM_GROUPS), F32),
            pltpu.VMEM((SSM_HALO + q, inner), F32),
            pltpu.VMEM((SSM_HALO + q, bc), F32),
            pltpu.VMEM((SSM_HALO + q, bc), F32),
            pltpu.VMEM((q, inner), F32),
        ],
        compiler_params=_params(("arbitrary",)),
        name="ssd",
    )(proj, proj, proj, proj, proj, proj, proj, dt_raw, wx, wb, wc, bx, bb, bcc,
      dt_bias, a_neg, d_skip, norm_g, expand)


def _attn_kernel(sinks_ref, q_ref, kvc_ref, kvp_ref, bias_ref, o_ref):
    blk = ATTN_BLOCK
    kvw = ATTN_KV_HEADS * ATTN_HEAD_DIM
    rep = ATTN_Q_HEADS // ATTN_KV_HEADS
    kv = jnp.concatenate([kvp_ref[...], kvc_ref[...]], axis=0).astype(F32)
    lane = lax.broadcasted_iota(jnp.int32, (1, LANES), 1)
    low = lane < ATTN_HEAD_DIM
    scale = ATTN_HEAD_DIM ** -0.5

    for g in range(ATTN_KV_HEADS):
        pc = g // 2
        kcol = kv[:, pc * LANES:(pc + 1) * LANES] * scale
        vcol = kv[:, kvw + pc * LANES:kvw + (pc + 1) * LANES]
        kroll = pltpu.roll(kcol, ATTN_HEAD_DIM, axis=1)
        vroll = pltpu.roll(vcol, ATTN_HEAD_DIM, axis=1)
        if g % 2 == 0:
            k_lo, k_hi = jnp.where(low, kcol, 0.0), jnp.where(low, 0.0, kroll)
            v_lo, v_hi = jnp.where(low, vcol, 0.0), jnp.where(low, 0.0, vroll)
        else:
            k_lo, k_hi = jnp.where(low, kroll, 0.0), jnp.where(low, 0.0, kcol)
            v_lo, v_hi = jnp.where(low, vroll, 0.0), jnp.where(low, 0.0, vcol)
        k_lo, k_hi, v_lo, v_hi = (t.astype(BF16) for t in (k_lo, k_hi, v_lo, v_hi))

        ncol = rep // 2
        qs = jnp.concatenate([q_ref[:, (g * ncol + c) * LANES:(g * ncol + c + 1) * LANES] for c in range(ncol)], axis=0)
        nt = (((1,), (1,)), ((), ()))
        s_even = lax.dot_general(qs, k_lo, nt, preferred_element_type=F32)
        s_odd = lax.dot_general(qs, k_hi, nt, preferred_element_type=F32)
        for c in range(ncol):
            out = None
            for par, s_all, vv in ((0, s_even, v_lo), (1, s_odd, v_hi)):
                h = g * rep + 2 * c + par
                logits = s_all[c * blk:(c + 1) * blk, :] + bias_ref[0, h]
                sink = sinks_ref[h]
                mx = jnp.maximum(jnp.max(logits, axis=-1, keepdims=True), sink)
                p = jnp.exp(logits - mx)
                denom = jnp.sum(p, axis=-1, keepdims=True) + jnp.exp(sink - mx)
                o = jnp.dot(p.astype(BF16), vv, preferred_element_type=F32) * (1.0 / denom)
                out = o if out is None else out + o
            col = g * ncol + c
            o_ref[:, col * LANES:(col + 1) * LANES] = out.astype(o_ref.dtype)


def _t5_bucket_table():
    dist = np.arange(ATTN_BLOCK)[:, None] + ATTN_BLOCK - np.arange(2 * ATTN_BLOCK)[None, :]
    n = np.maximum(dist, 0)
    max_exact = REL_BUCKETS // 2
    nf = np.maximum(n, 1).astype(np.float32)
    large = max_exact + (np.log(nf / max_exact) / math.log(REL_MAX_DIST / max_exact)
                         * (REL_BUCKETS - max_exact)).astype(np.int32)
    large = np.minimum(large, REL_BUCKETS - 1)
    return dist, np.where(n < max_exact, n, large)


def _attention(qkv, sinks, rel_bias):
    m = qkv.shape[0]
    blk = ATTN_BLOCK
    qd = ATTN_Q_HEADS * ATTN_HEAD_DIM
    kvd = 2 * ATTN_KV_HEADS * ATTN_HEAD_DIM
    dist, bucket = _t5_bucket_table()
    visible = (dist >= 0) & (dist < blk)
    onehot = jnp.asarray(np.arange(REL_BUCKETS)[:, None] == bucket.reshape(1, -1), F32)
    bias = jnp.dot(rel_bias.astype(F32).T, onehot, precision=lax.Precision.HIGHEST).reshape(-1, blk, 2 * blk)
    general = jnp.where(visible[None], bias, NEG_INF)
    first = jnp.where((visible & (np.arange(2 * blk) >= blk)[None, :])[None], bias, NEG_INF)
    table = jnp.stack([first, general])
    return pl.pallas_call(
        _attn_kernel,
        grid=(m // blk,),
        in_specs=[
            pl.BlockSpec(memory_space=pltpu.SMEM),
            pl.BlockSpec((blk, qd), lambda n: (n, 0)),
            pl.BlockSpec((blk, kvd), lambda n: (n, qd // kvd)),
            pl.BlockSpec((blk, kvd), lambda n: (jnp.maximum(n - 1, 0), qd // kvd)),
            pl.BlockSpec((1,) + table.shape[1:], lambda n: (jnp.minimum(n, 1), 0, 0, 0)),
        ],
        out_specs=pl.BlockSpec((blk, qd), lambda n: (n, 0)),
        out_shape=jax.ShapeDtypeStruct((m, qd), BF16),
        compiler_params=_params(("arbitrary",)),
        name="swa_attention",
    )(sinks.astype(F32), qkv, qkv, qkv, table)


def _router_kernel(x_ref, g_ref, rw_ref, rb_ref, o_ref):
    xn = _rms_rows(x_ref[...], g_ref[...])
    logits = jnp.dot(xn, rw_ref[...], precision=lax.Precision.HIGHEST, preferred_element_type=F32) + rb_ref[...]
    lane = lax.broadcasted_iota(jnp.int32, logits.shape, 1)
    m1 = jnp.max(logits, axis=-1, keepdims=True)
    i1 = jnp.min(jnp.where(logits == m1, lane, LANES), axis=-1, keepdims=True)
    rest = jnp.where(lane == i1, NEG_INF, logits)
    m2 = jnp.max(rest, axis=-1, keepdims=True)
    i2 = jnp.min(jnp.where(rest == m2, lane, LANES), axis=-1, keepdims=True)
    e = jnp.exp(m2 - m1)
    g1 = 1.0 / (1.0 + e)
    g2 = e * g1
    out = jnp.where(lane == 0, i1.astype(F32),
                    jnp.where(lane == 1, i2.astype(F32),
                              jnp.where(lane == 2, g1, jnp.where(lane == 3, g2, 0.0))))
    o_ref[...] = out


def _router(h, g, rw, rb, *, tm):
    m, d = h.shape
    rw_pad = jnp.zeros((d, LANES), F32).at[:, :N_EXPERTS].set(rw.astype(F32))
    rb_pad = jnp.full((1, LANES), NEG_INF, F32).at[0, :N_EXPERTS].set(rb.astype(F32))
    return pl.pallas_call(
        _router_kernel,
        grid=(m // tm,),
        in_specs=[
            pl.BlockSpec((tm, d), lambda i: (i, 0)),
            pl.BlockSpec((1, d), lambda i: (0, 0)),
            pl.BlockSpec((d, LANES), lambda i: (0, 0)),
            pl.BlockSpec((1, LANES), lambda i: (0, 0)),
        ],
        out_specs=pl.BlockSpec((tm, LANES), lambda i: (i, 0)),
        out_shape=jax.ShapeDtypeStruct((m, LANES), F32),
        compiler_params=_params(("parallel",)),
        name="router",
    )(h, g, rw_pad, rb_pad)


def _moe_kernel(ge_ref, ns_ref, tok_ref, h_hbm, g_ref, wg_ref, wu_ref, wd_ref, o_ref, xbuf, xn_ref, sem):
    grp = pl.program_id(0)
    j = pl.program_id(1)
    ns = ns_ref[grp]

    def row_copy(gi, r):
        tok = tok_ref[gi * MOE_GROUP + r]
        return pltpu.make_async_copy(h_hbm.at[pl.ds(tok, 1)], xbuf.at[pl.ds(r, 1)], sem)

    def issue_rows(gi):
        def issue(r, carry):
            for u in range(MOE_UNROLL):
                row_copy(gi, r * MOE_UNROLL + u).start()
            return carry

        lax.fori_loop(0, ns_ref[gi] * (MOE_SUB // MOE_UNROLL), issue, 0)

    @pl.when(j == 0)
    def _():
        @pl.when(grp == 0)
        def _():
            issue_rows(0)

        def wait(r, carry):
            for u in range(MOE_UNROLL):
                row_copy(grp, r * MOE_UNROLL + u).wait()
            return carry

        lax.fori_loop(0, ns * (MOE_SUB // MOE_UNROLL), wait, 0)
        for s in range(MOE_SUBS_PER_GROUP):
            rs = slice(s * MOE_SUB, (s + 1) * MOE_SUB)
            o_ref[rs, :] = jnp.zeros((MOE_SUB, o_ref.shape[1]), F32)

            @pl.when(s < ns)
            def _():
                xn_ref[rs, :] = _rms_rows(xbuf[rs, :], g_ref[...]).astype(BF16)

        @pl.when(grp + 1 < pl.num_programs(0))
        def _():
            issue_rows(grp + 1)

    def ffn(rs):
        xn = xn_ref[rs, :]
        gate = jnp.dot(xn, wg_ref[...], preferred_element_type=F32)
        up = jnp.dot(xn, wu_ref[...], preferred_element_type=F32)
        hid = (gate * _sigmoid(gate) * up).astype(BF16)
        o_ref[rs, :] += jnp.dot(hid, wd_ref[...], preferred_element_type=F32)

    @pl.when(ns == MOE_SUBS_PER_GROUP)
    def _():
        ffn(slice(0, MOE_GROUP))

    for s in range(MOE_SUBS_PER_GROUP - 1):
        @pl.when((s < ns) & (ns < MOE_SUBS_PER_GROUP))
        def _():
            ffn(slice(s * MOE_SUB, (s + 1) * MOE_SUB))


def _moe_ffn(h, g, wg, wu, wd, group_e, group_ns, src_tok, *, tf):
    t, d = h.shape
    n_groups = group_e.shape[0]
    f = wg.shape[2]
    nj = f // tf

    def jj(grp, j, ns):
        return jnp.where(ns[grp] > 0, j, nj - 1)

    return pl.pallas_call(
        _moe_kernel,
        grid_spec=pltpu.PrefetchScalarGridSpec(
            num_scalar_prefetch=3,
            grid=(n_groups, nj),
            in_specs=[
                pl.BlockSpec(memory_space=pl.ANY),
                pl.BlockSpec((1, d), lambda grp, j, ge, ns, tok: (0, 0)),
                pl.BlockSpec((None, d, tf), lambda grp, j, ge, ns, tok: (ge[grp], 0, jj(grp, j, ns))),
                pl.BlockSpec((None, d, tf), lambda grp, j, ge, ns, tok: (ge[grp], 0, jj(grp, j, ns))),
                pl.BlockSpec((None, tf, d), lambda grp, j, ge, ns, tok: (ge[grp], jj(grp, j, ns), 0)),
            ],
            out_specs=pl.BlockSpec((MOE_GROUP, d), lambda grp, j, ge, ns, tok: (grp, 0)),
            scratch_shapes=[
                pltpu.VMEM((MOE_GROUP, d), F32),
                pltpu.VMEM((MOE_GROUP, d), BF16),
                pltpu.SemaphoreType.DMA(()),
            ],
        ),
        out_shape=jax.ShapeDtypeStruct((n_groups * MOE_GROUP, d), F32),
        compiler_params=_params(("arbitrary", "arbitrary")),
        name="moe_ffn",
    )(group_e, group_ns, src_tok, h, g, wg, wu, wd)


def _combine_kernel(pos_ref, h_ref, route_ref, fg_ref, rows_hbm, o_ref, buf_a, buf_b, sem, *, tc):
    i = pl.program_id(0)
    slot = i % 2

    def copies(blk, sl, r):
        pa = pos_ref[TOP_K * (blk * tc + r)]
        pb = pos_ref[TOP_K * (blk * tc + r) + 1]
        return (pltpu.make_async_copy(rows_hbm.at[pl.ds(pa, 1)], buf_a.at[sl, pl.ds(r, 1)], sem.at[sl]),
                pltpu.make_async_copy(rows_hbm.at[pl.ds(pb, 1)], buf_b.at[sl, pl.ds(r, 1)], sem.at[sl]))

    def issue_block(blk, sl):
        def issue(r, carry):
            ca, cb = copies(blk, sl, r)
            ca.start()
            cb.start()
            return carry

        lax.fori_loop(0, tc, issue, 0, unroll=8)

    @pl.when(i == 0)
    def _():
        issue_block(0, 0)

    @pl.when(i + 1 < pl.num_programs(0))
    def _():
        issue_block(i + 1, 1 - slot)

    def wait(r, carry):
        ca, cb = copies(i, slot, r)
        ca.wait()
        cb.wait()
        return carry

    lax.fori_loop(0, tc, wait, 0, unroll=8)
    route = route_ref[...]
    moe = route[:, 2:3] * buf_a[slot] + route[:, 3:4] * buf_b[slot]
    o_ref[...] = _rms_rows(h_ref[...] + moe, fg_ref[...])


def _combine(h, route, final_g, rows, pos, *, tc):
    t, d = h.shape
    return pl.pallas_call(
        functools.partial(_combine_kernel, tc=tc),
        grid_spec=pltpu.PrefetchScalarGridSpec(
            num_scalar_prefetch=1,
            grid=(t // tc,),
            in_specs=[
                pl.BlockSpec((tc, d), lambda i, pos: (i, 0)),
                pl.BlockSpec((tc, LANES), lambda i, pos: (i, 0)),
                pl.BlockSpec((1, d), lambda i, pos: (0, 0)),
                pl.BlockSpec(memory_space=pl.ANY),
            ],
            out_specs=pl.BlockSpec((tc, d), lambda i, pos: (i, 0)),
            scratch_shapes=[pltpu.VMEM((2, tc, d), F32), pltpu.VMEM((2, tc, d), F32), pltpu.SemaphoreType.DMA((2,))],
        ),
        out_shape=jax.ShapeDtypeStruct((t, d), F32),
        compiler_params=_params(("arbitrary",)),
        name="moe_combine",
    )(pos, h, route, final_g, rows)


def _moe_plan(route, n_tok):
    flat_e = route[:, :TOP_K].astype(jnp.int32).reshape(-1)
    onehot = (flat_e[:, None] == jnp.arange(N_EXPERTS, dtype=jnp.int32)[None, :]).astype(jnp.int32)
    csum = jnp.cumsum(onehot, axis=0)
    rank = jnp.sum(onehot * (csum - onehot), axis=1)
    counts = csum[-1]
    n_groups = (n_tok * TOP_K) // MOE_GROUP + N_EXPERTS
    groups_e = (counts + MOE_GROUP - 1) // MOE_GROUP
    group_end = jnp.cumsum(groups_e)
    group_start = group_end - groups_e
    pos = jnp.sum(onehot * (group_start * MOE_GROUP)[None, :], axis=1) + rank
    flat_tok = jnp.arange(n_tok * TOP_K, dtype=jnp.int32) // TOP_K
    src_tok = jnp.zeros((n_groups * MOE_GROUP,), jnp.int32).at[pos].set(flat_tok)
    gidx = jnp.arange(n_groups, dtype=jnp.int32)
    used = gidx < group_end[-1]
    ge = jnp.minimum(jnp.searchsorted(group_end, gidx, side="right"), N_EXPERTS - 1).astype(jnp.int32)
    last_e = ge[jnp.maximum(group_end[-1] - 1, 0)]
    ge = jnp.where(used, ge, last_e)
    subs_e = (counts + MOE_SUB - 1) // MOE_SUB
    ns = jnp.clip(subs_e[ge] - MOE_SUBS_PER_GROUP * (gidx - group_start[ge]), 0, MOE_SUBS_PER_GROUP)
    ns = jnp.where(used, ns, 0).astype(jnp.int32)
    return ge, ns, src_tok, pos.astype(jnp.int32)


def kernel(x, mix_norm_g, ffn_norm_g, final_norm_g, w_in, conv_w, conv_b, conv_ln_g, conv_ln_b, ssm_conv_w, ssm_conv_b, dt_bias, a_log, d_skip, ssm_norm_g, w_out, ffn_w_gate, ffn_w_up, ffn_w_down, w_qkv, b_qkv, w_o, b_o, sinks, rel_bias, router_w, router_b, moe_w_gate, moe_w_up, moe_w_down):
    bsz, seq, d = x.shape
    assert bsz == 1 and d == D_MODEL and seq % 512 == 0
    m = seq
    h = x.reshape(m, d)
    row = lambda v: v.reshape(1, -1).astype(F32)
    tm = min(1024, m)

    main_w = 2 * D_MODEL + D_MODEL + (D_MODEL + 2 * SSM_GROUPS * SSM_STATE)
    w_main = w_in[0][:, :main_w].astype(BF16)
    w_dt = jnp.zeros((d, LANES), F32).at[:, :SSM_HEADS].set(w_in[0][:, main_w:]).astype(BF16)
    g0 = row(mix_norm_g[0])
    proj, dt_raw = _in_proj(h, g0, w_main, w_dt, tm=tm, tn=1024)

    conv_out = _conv_module(proj, conv_w[0].astype(F32), row(conv_b[0]), row(conv_ln_g[0]), row(conv_ln_b[0]), tt=128)

    pad_heads = lambda v: jnp.zeros((1, LANES), F32).at[0, :SSM_HEADS].set(v.astype(F32))
    ssm_out = _ssd(proj, dt_raw, ssm_conv_w[0].astype(F32), row(ssm_conv_b[0]),
                   pad_heads(dt_bias[0]), pad_heads(-jnp.exp(a_log[0].astype(F32))),
                   row(jnp.repeat(d_skip[0].astype(F32), SSM_HEAD_DIM)), row(ssm_norm_g[0]))

    wo = w_out[0].astype(BF16)
    h = _matmul_residual([conv_out, ssm_out], [wo[:D_MODEL], wo[D_MODEL:]], h, jnp.zeros((1, d), F32),
                         tm=512, tn=1024, name="out_proj")
    h = _swiglu(h, row(ffn_norm_g[0]), ffn_w_gate[0].astype(BF16), ffn_w_up[0].astype(BF16),
                ffn_w_down[0].astype(BF16), tm=512, tf=512)

    qkv = _norm_matmul(h, row(mix_norm_g[1]), w_qkv[0].astype(BF16), row(b_qkv[0]), tm=tm, tn=1280,
                       out_dtype=BF16, name="qkv_proj")
    attn = _attention(qkv, sinks[0], rel_bias)
    h = _matmul_residual([attn], [w_o[0].astype(BF16)], h, row(b_o[0]), tm=512, tn=1024, name="attn_out_proj")

    g1 = row(ffn_norm_g[1])
    route = _router(h, g1, router_w[0], router_b[0], tm=512)
    ge, ns, src_tok, pos = _moe_plan(route, m)
    rows = _moe_ffn(h, g1, moe_w_gate[0].astype(BF16), moe_w_up[0].astype(BF16), moe_w_down[0].astype(BF16),
                    ge, ns, src_tok, tf=512)
    out = _combine(h, route, row(final_norm_g), rows, pos, tc=256)
    return out.reshape(bsz, seq, d)
```

```python
import functools
import math

import numpy as np
import jax
import jax.numpy as jnp
from jax import lax
from jax.experimental import pallas as pl
from jax.experimental.pallas import tpu as pltpu

F32 = jnp.float32
BF16 = jnp.bfloat16
NORM_EPS = 1e-5
NEG_INF = float("-inf")

LANES = 128
SUBLANES = 8
VMEM_LIMIT = 56 << 20

D_MODEL = 2048
CONV_WIDTH = 31
SSM_HEADS = 32
SSM_HEAD_DIM = 64
SSM_GROUPS = 8
SSM_STATE = 128
SSM_CONV_WIDTH = 4
SSM_CHUNK = 128
ATTN_Q_HEADS = 32
ATTN_KV_HEADS = 4
ATTN_HEAD_DIM = 64
ATTN_BLOCK = 128
REL_BUCKETS = 32
REL_MAX_DIST = 128
N_EXPERTS = 8
TOP_K = 2
MOE_SUB = 256
MOE_SUBS_PER_GROUP = 4
MOE_GROUP = MOE_SUB * MOE_SUBS_PER_GROUP
MOE_UNROLL = 8


def _params(sem):
    return pltpu.CompilerParams(dimension_semantics=sem, vmem_limit_bytes=VMEM_LIMIT)


def _sigmoid(x):
    return 0.5 + 0.5 * jnp.tanh(0.5 * x)


def _silu(x):
    half = 0.5 * x
    return half + half * jnp.tanh(half)


def _rms_rows(x, g):
    ms = jnp.mean(x * x, axis=-1, keepdims=True)
    return x * lax.rsqrt(ms + NORM_EPS) * g


def _norm_matmul_kernel(x_ref, g_ref, w_ref, b_ref, o_ref, xn_ref):
    @pl.when(pl.program_id(1) == 0)
    def _():
        xn_ref[...] = _rms_rows(x_ref[...], g_ref[...]).astype(BF16)

    acc = jnp.dot(xn_ref[...], w_ref[...], preferred_element_type=F32)
    o_ref[...] = (acc + b_ref[...]).astype(o_ref.dtype)


def _norm_matmul(x, g, w, b, *, tm, tn, out_dtype, name):
    m, k = x.shape
    n = w.shape[1]
    return pl.pallas_call(
        _norm_matmul_kernel,
        grid=(m // tm, n // tn),
        in_specs=[
            pl.BlockSpec((tm, k), lambda i, j: (i, 0)),
            pl.BlockSpec((1, k), lambda i, j: (0, 0)),
            pl.BlockSpec((k, tn), lambda i, j: (0, j)),
            pl.BlockSpec((1, tn), lambda i, j: (0, j)),
        ],
        out_specs=pl.BlockSpec((tm, tn), lambda i, j: (i, j)),
        out_shape=jax.ShapeDtypeStruct((m, n), out_dtype),
        scratch_shapes=[pltpu.VMEM((tm, k), BF16)],
        compiler_params=_params(("parallel", "arbitrary")),
        name=name,
    )(x, g, w, b)


def _in_proj_kernel(x_ref, g_ref, w_ref, wdt_ref, o_ref, odt_ref, xn_ref):
    @pl.when(pl.program_id(1) == 0)
    def _():
        xn_ref[...] = _rms_rows(x_ref[...], g_ref[...]).astype(BF16)
        odt_ref[...] = jnp.dot(xn_ref[...], wdt_ref[...], preferred_element_type=F32)

    o_ref[...] = jnp.dot(xn_ref[...], w_ref[...], preferred_element_type=F32).astype(o_ref.dtype)


def _in_proj(x, g, w, wdt, *, tm, tn):
    m, k = x.shape
    n = w.shape[1]
    ndt = wdt.shape[1]
    return pl.pallas_call(
        _in_proj_kernel,
        grid=(m // tm, n // tn),
        in_specs=[
            pl.BlockSpec((tm, k), lambda i, j: (i, 0)),
            pl.BlockSpec((1, k), lambda i, j: (0, 0)),
            pl.BlockSpec((k, tn), lambda i, j: (0, j)),
            pl.BlockSpec((k, ndt), lambda i, j: (0, 0)),
        ],
        out_specs=[pl.BlockSpec((tm, tn), lambda i, j: (i, j)), pl.BlockSpec((tm, ndt), lambda i, j: (i, 0))],
        out_shape=[jax.ShapeDtypeStruct((m, n), BF16), jax.ShapeDtypeStruct((m, ndt), F32)],
        scratch_shapes=[pltpu.VMEM((tm, k), BF16)],
        compiler_params=_params(("parallel", "arbitrary")),
        name="in_proj",
    )(x, g, w, wdt)


def _matmul_residual_kernel(*refs, n_in):
    a_refs, w_refs = refs[:n_in], refs[n_in:2 * n_in]
    res_ref, b_ref, o_ref = refs[2 * n_in:]
    acc = res_ref[...] + b_ref[...]
    for a_ref, w_ref in zip(a_refs, w_refs):
        acc = acc + jnp.dot(a_ref[...], w_ref[...], preferred_element_type=F32)
    o_ref[...] = acc


def _matmul_residual(a_list, w_list, res, b, *, tm, tn, name):
    m, n = res.shape
    n_in = len(a_list)
    in_specs = [pl.BlockSpec((tm, a.shape[1]), lambda j, i: (i, 0)) for a in a_list]
    in_specs += [pl.BlockSpec((w.shape[0], tn), lambda j, i: (0, j)) for w in w_list]
    in_specs += [pl.BlockSpec((tm, tn), lambda j, i: (i, j)), pl.BlockSpec((1, tn), lambda j, i: (0, j))]
    return pl.pallas_call(
        functools.partial(_matmul_residual_kernel, n_in=n_in),
        grid=(n // tn, m // tm),
        in_specs=in_specs,
        out_specs=pl.BlockSpec((tm, tn), lambda j, i: (i, j)),
        out_shape=jax.ShapeDtypeStruct((m, n), F32),
        compiler_params=_params(("parallel", "parallel")),
        name=name,
    )(*a_list, *w_list, res, b)


def _swiglu_kernel(x_ref, g_ref, wg_ref, wu_ref, wd_ref, o_ref, xn_ref):
    @pl.when(pl.program_id(1) == 0)
    def _():
        x = x_ref[...]
        xn_ref[...] = _rms_rows(x, g_ref[...]).astype(BF16)
        o_ref[...] = x

    xn = xn_ref[...]
    gate = jnp.dot(xn, wg_ref[...], preferred_element_type=F32)
    up = jnp.dot(xn, wu_ref[...], preferred_element_type=F32)
    hid = (_silu(gate) * up).astype(BF16)
    o_ref[...] += jnp.dot(hid, wd_ref[...], preferred_element_type=F32)


def _swiglu(x, g, wg, wu, wd, *, tm, tf):
    m, d = x.shape
    f = wg.shape[1]
    return pl.pallas_call(
        _swiglu_kernel,
        grid=(m // tm, f // tf),
        in_specs=[
            pl.BlockSpec((tm, d), lambda i, j: (i, 0)),
            pl.BlockSpec((1, d), lambda i, j: (0, 0)),
            pl.BlockSpec((d, tf), lambda i, j: (0, j)),
            pl.BlockSpec((d, tf), lambda i, j: (0, j)),
            pl.BlockSpec((tf, d), lambda i, j: (j, 0)),
        ],
        out_specs=pl.BlockSpec((tm, d), lambda i, j: (i, 0)),
        out_shape=jax.ShapeDtypeStruct((m, d), F32),
        scratch_shapes=[pltpu.VMEM((tm, d), BF16)],
        compiler_params=_params(("parallel", "arbitrary")),
        name="swiglu",
    )(x, g, wg, wu, wd)


CONV_HALO = 32
CONV_ROWS = 64


def _conv_module_kernel(val_ref, gate_ref, pval_ref, pgate_ref, w_ref, b_ref, lg_ref, lb_ref, o_ref,
                        ubuf, cbuf, *, tt):
    i = pl.program_id(0)

    def glu(v, g):
        return v.astype(F32) * _sigmoid(g.astype(F32))

    ubuf[0:CONV_HALO, :] = jnp.where(i > 0, glu(pval_ref[...], pgate_ref[...]), 0.0)
    ubuf[CONV_HALO:CONV_HALO + tt, :] = glu(val_ref[...], gate_ref[...])

    off = CONV_HALO - (CONV_WIDTH - 1)
    taps = [[] for _ in range(SUBLANES)]
    for j in range(CONV_WIDTH):
        a, r = divmod(off + j, SUBLANES)
        taps[r].append((a, j))
    n_ch = ubuf.shape[1]
    for cb in range(n_ch // LANES):
        ls = slice(cb * LANES, (cb + 1) * LANES)
        for r0 in range(0, tt, CONV_ROWS):
            big = ubuf[r0:r0 + CONV_ROWS + CONV_HALO, ls]
            out = None
            for r in range(SUBLANES):
                rows = CONV_ROWS if r == 0 else CONV_ROWS + SUBLANES
                q = None
                for a, j in taps[r]:
                    term = w_ref[j:j + 1, ls] * big[a * SUBLANES:a * SUBLANES + rows, :]
                    q = term if q is None else q + term
                if r:
                    q = pltpu.roll(q, rows - r, axis=0)[:CONV_ROWS, :]
                out = q if out is None else out + q
            cbuf[r0:r0 + CONV_ROWS, ls] = out

    c = cbuf[...] + b_ref[...]
    mu = jnp.mean(c, axis=-1, keepdims=True)
    d = c - mu
    var = jnp.mean(d * d, axis=-1, keepdims=True)
    y = d * lax.rsqrt(var + NORM_EPS) * lg_ref[...] + lb_ref[...]
    o_ref[...] = _silu(y).astype(o_ref.dtype)


def _conv_module(proj, conv_w, conv_b, ln_g, ln_b, *, tt):
    m = proj.shape[0]
    c = D_MODEL
    hb = tt // CONV_HALO
    return pl.pallas_call(
        functools.partial(_conv_module_kernel, tt=tt),
        grid=(m // tt,),
        in_specs=[
            pl.BlockSpec((tt, c), lambda i: (i, 0)),
            pl.BlockSpec((tt, c), lambda i: (i, 1)),
            pl.BlockSpec((CONV_HALO, c), lambda i: (jnp.maximum(i * hb - 1, 0), 0)),
            pl.BlockSpec((CONV_HALO, c), lambda i: (jnp.maximum(i * hb - 1, 0), 1)),
            pl.BlockSpec((CONV_WIDTH, c), lambda i: (0, 0)),
            pl.BlockSpec((1, c), lambda i: (0, 0)),
            pl.BlockSpec((1, c), lambda i: (0, 0)),
            pl.BlockSpec((1, c), lambda i: (0, 0)),
        ],
        out_specs=pl.BlockSpec((tt, c), lambda i: (i, 0)),
        out_shape=jax.ShapeDtypeStruct((m, c), BF16),
        scratch_shapes=[pltpu.VMEM((CONV_HALO + tt, c), F32), pltpu.VMEM((tt, c), F32)],
        compiler_params=_params(("arbitrary",)),
        name="conv_module",
    )(proj, proj, proj, proj, conv_w, conv_b, ln_g, ln_b)


SSM_HALO = 16


def _conv4_silu(cur_ref, prev_ref, shift_ref, w_ref, b_ref, first):
    q = cur_ref.shape[0]
    cur = cur_ref[...]
    prev = prev_ref[...]
    xb = jnp.concatenate([jnp.where(first, jnp.zeros_like(prev), prev), cur], axis=0)
    back = jnp.dot(shift_ref[...], xb, preferred_element_type=F32)
    last = SSM_CONV_WIDTH - 1
    acc = b_ref[...] + w_ref[last:last + 1, :] * cur.astype(F32)
    for k in range(1, SSM_CONV_WIDTH):
        acc = acc + w_ref[last - k:last - k + 1, :] * back[(k - 1) * q:k * q, :]
    return _silu(acc)


def _ssd_kernel(z_ref, x_ref, b_ref, c_ref, px_ref, pb_ref, pc_ref, dtr_ref,
                wx_ref, wb_ref, wc_ref, bx_ref, bb_ref, bc_ref,
                dtb_ref, a_ref, dskip_ref, ng_ref, e_ref, sh_ref, o_ref,
                state, ybuf):
    i = pl.program_id(0)
    q = SSM_CHUNK
    first = i == 0

    @pl.when(first)
    def _():
        state[...] = jnp.zeros_like(state)

    xs = _conv4_silu(x_ref, px_ref, sh_ref, wx_ref, bx_ref, first)
    bm = _conv4_silu(b_ref, pb_ref, sh_ref, wb_ref, bb_ref, first)
    cm = _conv4_silu(c_ref, pc_ref, sh_ref, wc_ref, bc_ref, first)

    pre = dtr_ref[...] + dtb_ref[...]
    dt = jnp.maximum(pre, 0.0) + jnp.log(1.0 + jnp.exp(-jnp.abs(pre)))
    dta = dt * a_ref[...]
    row = lax.broadcasted_iota(jnp.int32, (q, q), 0)
    col = lax.broadcasted_iota(jnp.int32, (q, q), 1)
    causal = row >= col
    acs = jnp.dot(causal.astype(F32), dta, precision=lax.Precision.HIGHEST, preferred_element_type=F32)
    acs_t = acs.T
    dt_t = dt.T
    eacs = jnp.exp(acs)
    wdec = dt * jnp.exp(acs[q - 1:q, :] - acs)

    def split(v):
        hi = v.astype(BF16)
        return hi, (v - hi.astype(F32)).astype(BF16)

    parts = jnp.concatenate(split(eacs) + split(wdec), axis=0)
    wide = jnp.dot(parts, e_ref[...], preferred_element_type=F32)
    eacs_full = wide[0:q] + wide[q:2 * q]
    xd = (xs * (wide[2 * q:3 * q] + wide[3 * q:4 * q])).astype(BF16)

    xs_b = xs.astype(BF16)
    lane = lax.broadcasted_iota(jnp.int32, (1, xs.shape[1]), 1)
    low_head = (lane % LANES) < SSM_HEAD_DIM
    xs_lo = jnp.where(low_head, xs_b, jnp.zeros_like(xs_b))
    xs_hi = jnp.where(low_head, jnp.zeros_like(xs_b), xs_b)

    gw = SSM_HEAD_DIM * (SSM_HEADS // SSM_GROUPS)
    for g in range(SSM_GROUPS):
        ns = slice(g * SSM_STATE, (g + 1) * SSM_STATE)
        gs = slice(g * gw, (g + 1) * gw)
        bg_f = bm[:, ns]
        bg = bg_f.astype(BF16)
        cg = cm[:, ns].astype(BF16)
        cb = lax.dot_general(cg, bg, (((1,), (1,)), ((), ())), preferred_element_type=F32)
        prev = state[g]
        y_off = jnp.dot(cg, prev.astype(BF16), preferred_element_type=F32)
        for c in range(gw // LANES):
            cs = slice(g * gw + c * LANES, g * gw + (c + 1) * LANES)
            acc = y_off[:, c * LANES:(c + 1) * LANES] * eacs_full[:, cs]
            for par, xpart in ((0, xs_lo), (1, xs_hi)):
                h = g * (SSM_HEADS // SSM_GROUPS) + 2 * c + par
                seg = acs[:, h:h + 1] - acs_t[h:h + 1, :]
                mmat = cb * jnp.exp(jnp.where(causal, seg, NEG_INF)) * dt_t[h:h + 1, :]
                acc = acc + jnp.dot(mmat.astype(BF16), xpart[:, cs], preferred_element_type=F32)
            ybuf[:, cs] = acc
        new_states = jnp.dot(bg_f.T.astype(BF16), xd[:, gs], preferred_element_type=F32)
        state[g] = prev * eacs_full[q - 1:q, gs] + new_states

    y = ybuf[...] + xs * dskip_ref[...]
    z = z_ref[...].astype(F32)
    y = y * _silu(z)
    for g in range(SSM_GROUPS):
        gs = slice(g * gw, (g + 1) * gw)
        yg = y[:, gs]
        ms = jnp.mean(yg * yg, axis=-1, keepdims=True)
        o_ref[:, gs] = (yg * lax.rsqrt(ms + NORM_EPS) * ng_ref[:, gs]).astype(o_ref.dtype)


def _ssd(proj, dt_raw, conv_w, conv_b, dt_bias, a_neg, d_skip, norm_g):
    m = proj.shape[0]
    q = SSM_CHUNK
    inner = D_MODEL
    bc = SSM_GROUPS * SSM_STATE
    hb = q // SSM_HALO
    wx, wb, wc = conv_w[:, :inner], conv_w[:, inner:inner + bc], conv_w[:, inner + bc:]
    bx, bb, bcc = conv_b[:, :inner], conv_b[:, inner:inner + bc], conv_b[:, inner + bc:]
    expand = jnp.asarray(np.arange(LANES)[:, None] == (np.arange(inner) // SSM_HEAD_DIM)[None, :], BF16)
    t_idx = np.arange(q)
    shift_np = np.zeros((SSM_CONV_WIDTH - 1, q, SSM_HALO + q), np.float32)
    for k in range(1, SSM_CONV_WIDTH):
        shift_np[k - 1, t_idx, SSM_HALO + t_idx - k] = 1.0
    shift = jnp.asarray(shift_np.reshape(-1, SSM_HALO + q), BF16)
    prev = lambda i: jnp.maximum(i * hb - 1, 0)
    full = lambda shape: pl.BlockSpec(shape, lambda i: (0,) * len(shape))
    return pl.pallas_call(
        _ssd_kernel,
        grid=(m // q,),
        in_specs=[
            pl.BlockSpec((q, inner), lambda i: (i, 2)),
            pl.BlockSpec((q, inner), lambda i: (i, 3)),
            pl.BlockSpec((q, bc), lambda i: (i, 8)),
            pl.BlockSpec((q, bc), lambda i: (i, 9)),
            pl.BlockSpec((SSM_HALO, inner), lambda i: (prev(i), 3)),
            pl.BlockSpec((SSM_HALO, bc), lambda i: (prev(i), 8)),
            pl.BlockSpec((SSM_HALO, bc), lambda i: (prev(i), 9)),
            pl.BlockSpec((q, LANES), lambda i: (i, 0)),
            full(wx.shape), full(wb.shape), full(wc.shape),
            full(bx.shape), full(bb.shape), full(bcc.shape),
            full((1, LANES)), full((1, LANES)), full((1, inner)), full((1, inner)),
            full((LANES, inner)),
            full(shift.shape),
        ],
        out_specs=pl.BlockSpec((q, inner), lambda i: (i, 0)),
        out_shape=jax.ShapeDtypeStruct((m, inner), BF16),
        scratch_shapes=[
            pltpu.VMEM((SSM_GROUPS, SSM_STATE, inner // SSM_GROUPS), F32),
            pltpu.VMEM((q, inner), F32),
        ],
        compiler_params=_params(("arbitrary",)),
        name="ssd",
    )(proj, proj, proj, proj, proj, proj, proj, dt_raw, wx, wb, wc, bx, bb, bcc,
      dt_bias, a_neg, d_skip, norm_g, expand, shift)


def _attn_kernel(sinks_ref, q_ref, kvc_ref, kvp_ref, bias_ref, o_ref):
    blk = ATTN_BLOCK
    kvw = ATTN_KV_HEADS * ATTN_HEAD_DIM
    rep = ATTN_Q_HEADS // ATTN_KV_HEADS
    kv = jnp.concatenate([kvp_ref[...], kvc_ref[...]], axis=0).astype(F32)
    lane = lax.broadcasted_iota(jnp.int32, (1, LANES), 1)
    low = lane < ATTN_HEAD_DIM
    scale = ATTN_HEAD_DIM ** -0.5

    for g in range(ATTN_KV_HEADS):
        pc = g // 2
        kcol = kv[:, pc * LANES:(pc + 1) * LANES] * scale
        vcol = kv[:, kvw + pc * LANES:kvw + (pc + 1) * LANES]
        kroll = pltpu.roll(kcol, ATTN_HEAD_DIM, axis=1)
        vroll = pltpu.roll(vcol, ATTN_HEAD_DIM, axis=1)
        if g % 2 == 0:
            k_lo, k_hi = jnp.where(low, kcol, 0.0), jnp.where(low, 0.0, kroll)
            v_lo, v_hi = jnp.where(low, vcol, 0.0), jnp.where(low, 0.0, vroll)
        else:
            k_lo, k_hi = jnp.where(low, kroll, 0.0), jnp.where(low, 0.0, kcol)
            v_lo, v_hi = jnp.where(low, vroll, 0.0), jnp.where(low, 0.0, vcol)
        k_lo, k_hi, v_lo, v_hi = (t.astype(BF16) for t in (k_lo, k_hi, v_lo, v_hi))

        ncol = rep // 2
        qs = jnp.concatenate([q_ref[:, (g * ncol + c) * LANES:(g * ncol + c + 1) * LANES] for c in range(ncol)], axis=0)
        nt = (((1,), (1,)), ((), ()))
        s_even = lax.dot_general(qs, k_lo, nt, preferred_element_type=F32)
        s_odd = lax.dot_general(qs, k_hi, nt, preferred_element_type=F32)
        for c in range(ncol):
            out = None
            for par, s_all, vv in ((0, s_even, v_lo), (1, s_odd, v_hi)):
                h = g * rep + 2 * c + par
                logits = s_all[c * blk:(c + 1) * blk, :] + bias_ref[0, h]
                sink = sinks_ref[h]
                mx = jnp.maximum(jnp.max(logits, axis=-1, keepdims=True), sink)
                p = jnp.exp(logits - mx)
                denom = jnp.sum(p, axis=-1, keepdims=True) + jnp.exp(sink - mx)
                o = jnp.dot(p.astype(BF16), vv, preferred_element_type=F32) * (1.0 / denom)
                out = o if out is None else out + o
            col = g * ncol + c
            o_ref[:, col * LANES:(col + 1) * LANES] = out.astype(o_ref.dtype)


def _t5_bucket_table():
    dist = np.arange(ATTN_BLOCK)[:, None] + ATTN_BLOCK - np.arange(2 * ATTN_BLOCK)[None, :]
    n = np.maximum(dist, 0)
    max_exact = REL_BUCKETS // 2
    nf = np.maximum(n, 1).astype(np.float32)
    large = max_exact + (np.log(nf / max_exact) / math.log(REL_MAX_DIST / max_exact)
                         * (REL_BUCKETS - max_exact)).astype(np.int32)
    large = np.minimum(large, REL_BUCKETS - 1)
    return dist, np.where(n < max_exact, n, large)


def _attention(qkv, sinks, rel_bias):
    m = qkv.shape[0]
    blk = ATTN_BLOCK
    qd = ATTN_Q_HEADS * ATTN_HEAD_DIM
    kvd = 2 * ATTN_KV_HEADS * ATTN_HEAD_DIM
    dist, bucket = _t5_bucket_table()
    visible = (dist >= 0) & (dist < blk)
    onehot = jnp.asarray(np.arange(REL_BUCKETS)[:, None] == bucket.reshape(1, -1), F32)
    bias = jnp.dot(rel_bias.astype(F32).T, onehot, precision=lax.Precision.HIGHEST).reshape(-1, blk, 2 * blk)
    general = jnp.where(visible[None], bias, NEG_INF)
    first = jnp.where((visible & (np.arange(2 * blk) >= blk)[None, :])[None], bias, NEG_INF)
    table = jnp.stack([first, general])
    return pl.pallas_call(
        _attn_kernel,
        grid=(m // blk,),
        in_specs=[
            pl.BlockSpec(memory_space=pltpu.SMEM),
            pl.BlockSpec((blk, qd), lambda n: (n, 0)),
            pl.BlockSpec((blk, kvd), lambda n: (n, qd // kvd)),
            pl.BlockSpec((blk, kvd), lambda n: (jnp.maximum(n - 1, 0), qd // kvd)),
            pl.BlockSpec((1,) + table.shape[1:], lambda n: (jnp.minimum(n, 1), 0, 0, 0)),
        ],
        out_specs=pl.BlockSpec((blk, qd), lambda n: (n, 0)),
        out_shape=jax.ShapeDtypeStruct((m, qd), BF16),
        compiler_params=_params(("arbitrary",)),
        name="swa_attention",
    )(sinks.astype(F32), qkv, qkv, qkv, table)


def _router_kernel(x_ref, g_ref, rw_ref, rb_ref, o_ref):
    xn = _rms_rows(x_ref[...], g_ref[...])
    logits = jnp.dot(xn, rw_ref[...], precision=lax.Precision.HIGHEST, preferred_element_type=F32) + rb_ref[...]
    lane = lax.broadcasted_iota(jnp.int32, logits.shape, 1)
    m1 = jnp.max(logits, axis=-1, keepdims=True)
    i1 = jnp.min(jnp.where(logits == m1, lane, LANES), axis=-1, keepdims=True)
    rest = jnp.where(lane == i1, NEG_INF, logits)
    m2 = jnp.max(rest, axis=-1, keepdims=True)
    i2 = jnp.min(jnp.where(rest == m2, lane, LANES), axis=-1, keepdims=True)
    e = jnp.exp(m2 - m1)
    g1 = 1.0 / (1.0 + e)
    g2 = e * g1
    out = jnp.where(lane == 0, i1.astype(F32),
                    jnp.where(lane == 1, i2.astype(F32),
                              jnp.where(lane == 2, g1, jnp.where(lane == 3, g2, 0.0))))
    o_ref[...] = out


def _router(h, g, rw, rb, *, tm):
    m, d = h.shape
    rw_pad = jnp.zeros((d, LANES), F32).at[:, :N_EXPERTS].set(rw.astype(F32))
    rb_pad = jnp.full((1, LANES), NEG_INF, F32).at[0, :N_EXPERTS].set(rb.astype(F32))
    return pl.pallas_call(
        _router_kernel,
        grid=(m // tm,),
        in_specs=[
            pl.BlockSpec((tm, d), lambda i: (i, 0)),
            pl.BlockSpec((1, d), lambda i: (0, 0)),
            pl.BlockSpec((d, LANES), lambda i: (0, 0)),
            pl.BlockSpec((1, LANES), lambda i: (0, 0)),
        ],
        out_specs=pl.BlockSpec((tm, LANES), lambda i: (i, 0)),
        out_shape=jax.ShapeDtypeStruct((m, LANES), F32),
        compiler_params=_params(("parallel",)),
        name="router",
    )(h, g, rw_pad, rb_pad)


def _moe_drain(h_hbm, xbuf, sem):
    for s in range(MOE_SUBS_PER_GROUP):
        rs = slice(s * MOE_SUB, (s + 1) * MOE_SUB)
        pltpu.make_async_copy(h_hbm.at[pl.ds(0, MOE_SUB)], xbuf.at[rs], sem).wait()
    slack = xbuf.shape[0] - MOE_GROUP
    pltpu.make_async_copy(h_hbm.at[pl.ds(0, slack)], xbuf.at[MOE_GROUP:, :], sem).wait()


def _moe_kernel(ge_ref, ns_ref, tok_ref, h_hbm, g_ref, wg_ref, wu_ref, wd_ref, o_ref, xbuf, xn_ref, sem, *, rps):
    grp = pl.program_id(0)
    j = pl.program_id(1)
    ns = ns_ref[grp]

    def row_copy(gi, r):
        tok = tok_ref[gi * MOE_GROUP + jnp.minimum(r, MOE_GROUP - 1)]
        return pltpu.make_async_copy(h_hbm.at[pl.ds(tok, 1)], xbuf.at[pl.ds(r, 1)], sem)

    def issue_step_rows(gi):
        for u in range(rps):
            row_copy(gi, j * rps + u).start()

    @pl.when(j == 0)
    def _():
        @pl.when(grp == 0)
        def _():
            def issue(r, carry):
                for u in range(MOE_UNROLL):
                    row_copy(0, r * MOE_UNROLL + u).start()
                return carry

            lax.fori_loop(0, xbuf.shape[0] // MOE_UNROLL, issue, 0)

        _moe_drain(h_hbm, xbuf, sem)
        for s in range(MOE_SUBS_PER_GROUP):
            rs = slice(s * MOE_SUB, (s + 1) * MOE_SUB)
            o_ref[rs, :] = jnp.zeros((MOE_SUB, o_ref.shape[1]), F32)

            @pl.when(s < ns)
            def _():
                xn_ref[rs, :] = _rms_rows(xbuf[rs, :], g_ref[...]).astype(BF16)

    def ffn(rs):
        xn = xn_ref[rs, :]
        gate = jnp.dot(xn, wg_ref[...].astype(BF16), preferred_element_type=F32)
        up = jnp.dot(xn, wu_ref[...].astype(BF16), preferred_element_type=F32)
        hid = (_silu(gate) * up).astype(BF16)
        o_ref[rs, :] += jnp.dot(hid, wd_ref[...].astype(BF16), preferred_element_type=F32)

    @pl.when(ns == MOE_SUBS_PER_GROUP)
    def _():
        issue_step_rows(grp + 1)
        ffn(slice(0, MOE_GROUP))

    @pl.when(ns < MOE_SUBS_PER_GROUP)
    def _():
        issue_step_rows(grp + 1)

    for s in range(MOE_SUBS_PER_GROUP - 1):
        @pl.when((s < ns) & (ns < MOE_SUBS_PER_GROUP))
        def _():
            ffn(slice(s * MOE_SUB, (s + 1) * MOE_SUB))

    @pl.when((grp == pl.num_programs(0) - 1) & (j == pl.num_programs(1) - 1))
    def _():
        _moe_drain(h_hbm, xbuf, sem)


def _moe_ffn(h, g, wg, wu, wd, group_e, group_ns, src_tok, *, tf):
    t, d = h.shape
    n_groups = group_e.shape[0]
    f = wg.shape[2]
    nj = f // tf
    rps = -(-MOE_GROUP // (nj * SUBLANES)) * SUBLANES
    assert src_tok.shape[0] == (n_groups + 1) * MOE_GROUP

    def jj(grp, j, ns):
        return jnp.where(ns[grp] > 0, j, nj - 1)

    return pl.pallas_call(
        functools.partial(_moe_kernel, rps=rps),
        grid_spec=pltpu.PrefetchScalarGridSpec(
            num_scalar_prefetch=3,
            grid=(n_groups, nj),
            in_specs=[
                pl.BlockSpec(memory_space=pl.ANY),
                pl.BlockSpec((1, d), lambda grp, j, ge, ns, tok: (0, 0)),
                pl.BlockSpec((None, d, tf), lambda grp, j, ge, ns, tok: (ge[grp], 0, jj(grp, j, ns))),
                pl.BlockSpec((None, d, tf), lambda grp, j, ge, ns, tok: (ge[grp], 0, jj(grp, j, ns))),
                pl.BlockSpec((None, tf, d), lambda grp, j, ge, ns, tok: (ge[grp], jj(grp, j, ns), 0)),
            ],
            out_specs=pl.BlockSpec((MOE_GROUP, d), lambda grp, j, ge, ns, tok: (grp, 0)),
            scratch_shapes=[
                pltpu.VMEM((rps * nj, d), F32),
                pltpu.VMEM((MOE_GROUP, d), BF16),
                pltpu.SemaphoreType.DMA(()),
            ],
        ),
        out_shape=jax.ShapeDtypeStruct((n_groups * MOE_GROUP, d), F32),
        compiler_params=_params(("arbitrary", "arbitrary")),
        name="moe_ffn",
    )(group_e, group_ns, src_tok, h, g, wg, wu, wd)


def _combine_kernel(pos_ref, h_ref, route_ref, fg_ref, rows_hbm, o_ref, buf_a, buf_b, sem, *, tc):
    i = pl.program_id(0)
    slot = i % 2

    def copies(blk, sl, r):
        pa = pos_ref[TOP_K * (blk * tc + r)]
        pb = pos_ref[TOP_K * (blk * tc + r) + 1]
        return (pltpu.make_async_copy(rows_hbm.at[pl.ds(pa, 1)], buf_a.at[sl, pl.ds(r, 1)], sem.at[sl]),
                pltpu.make_async_copy(rows_hbm.at[pl.ds(pb, 1)], buf_b.at[sl, pl.ds(r, 1)], sem.at[sl]))

    def issue_block(blk, sl):
        def issue(r, carry):
            ca, cb = copies(blk, sl, r)
            ca.start()
            cb.start()
            return carry

        lax.fori_loop(0, tc, issue, 0, unroll=8)

    @pl.when(i == 0)
    def _():
        issue_block(0, 0)

    @pl.when(i + 1 < pl.num_programs(0))
    def _():
        issue_block(i + 1, 1 - slot)

    def wait(r, carry):
        ca, cb = copies(i, slot, r)
        ca.wait()
        cb.wait()
        return carry

    lax.fori_loop(0, tc, wait, 0, unroll=8)
    route = route_ref[...]
    moe = route[:, 2:3] * buf_a[slot] + route[:, 3:4] * buf_b[slot]
    o_ref[...] = _rms_rows(h_ref[...] + moe, fg_ref[...])


def _combine(h, route, final_g, rows, pos, *, tc):
    t, d = h.shape
    return pl.pallas_call(
        functools.partial(_combine_kernel, tc=tc),
        grid_spec=pltpu.PrefetchScalarGridSpec(
            num_scalar_prefetch=1,
            grid=(t // tc,),
            in_specs=[
                pl.BlockSpec((tc, d), lambda i, pos: (i, 0)),
                pl.BlockSpec((tc, LANES), lambda i, pos: (i, 0)),
                pl.BlockSpec((1, d), lambda i, pos: (0, 0)),
                pl.BlockSpec(memory_space=pl.ANY),
            ],
            out_specs=pl.BlockSpec((tc, d), lambda i, pos: (i, 0)),
            scratch_shapes=[pltpu.VMEM((2, tc, d), F32), pltpu.VMEM((2, tc, d), F32), pltpu.SemaphoreType.DMA((2,))],
        ),
        out_shape=jax.ShapeDtypeStruct((t, d), F32),
        compiler_params=_params(("arbitrary",)),
        name="moe_combine",
    )(pos, h, route, final_g, rows)


def _moe_plan(route, n_tok):
    flat_e = route[:, :TOP_K].astype(jnp.int32).reshape(-1)
    onehot = (flat_e[:, None] == jnp.arange(N_EXPERTS, dtype=jnp.int32)[None, :]).astype(jnp.int32)
    csum = jnp.cumsum(onehot, axis=0)
    rank = jnp.sum(onehot * (csum - onehot), axis=1)
    counts = csum[-1]
    n_groups = (n_tok * TOP_K) // MOE_GROUP + N_EXPERTS
    groups_e = (counts + MOE_GROUP - 1) // MOE_GROUP
    group_end = jnp.cumsum(groups_e)
    group_start = group_end - groups_e
    pos = jnp.sum(onehot * (group_start * MOE_GROUP)[None, :], axis=1) + rank
    flat_tok = jnp.arange(n_tok * TOP_K, dtype=jnp.int32) // TOP_K
    src_tok = jnp.zeros(((n_groups + 1) * MOE_GROUP,), jnp.int32).at[pos].set(
        flat_tok, unique_indices=True, mode="promise_in_bounds")
    experts = jnp.arange(N_EXPERTS, dtype=jnp.int32)
    gidx = jnp.arange(n_groups, dtype=jnp.int32)
    used = gidx < group_end[-1]
    last_e = jnp.max(jnp.where(groups_e > 0, experts, 0))
    ge = jnp.sum((group_end[None, :] <= gidx[:, None]).astype(jnp.int32), axis=1)
    ge = jnp.where(used, ge, last_e).astype(jnp.int32)
    mine = (ge[:, None] == experts[None, :]).astype(jnp.int32)
    subs_e = (counts + MOE_SUB - 1) // MOE_SUB
    subs_g = jnp.sum(mine * subs_e[None, :], axis=1)
    start_g = jnp.sum(mine * group_start[None, :], axis=1)
    ns = jnp.clip(subs_g - MOE_SUBS_PER_GROUP * (gidx - start_g), 0, MOE_SUBS_PER_GROUP)
    ns = jnp.where(used, ns, 0).astype(jnp.int32)
    return ge, ns, src_tok, pos.astype(jnp.int32)


def kernel(x, mix_norm_g, ffn_norm_g, final_norm_g, w_in, conv_w, conv_b, conv_ln_g, conv_ln_b, ssm_conv_w, ssm_conv_b, dt_bias, a_log, d_skip, ssm_norm_g, w_out, ffn_w_gate, ffn_w_up, ffn_w_down, w_qkv, b_qkv, w_o, b_o, sinks, rel_bias, router_w, router_b, moe_w_gate, moe_w_up, moe_w_down):
    bsz, seq, d = x.shape
    assert bsz == 1 and d == D_MODEL and seq % 512 == 0
    m = seq
    h = x.reshape(m, d)
    row = lambda v: v.reshape(1, -1).astype(F32)
    tm = min(1024, m)

    main_w = 2 * D_MODEL + D_MODEL + (D_MODEL + 2 * SSM_GROUPS * SSM_STATE)
    w_main = w_in[0][:, :main_w].astype(BF16)
    w_dt = jnp.zeros((d, LANES), F32).at[:, :SSM_HEADS].set(w_in[0][:, main_w:]).astype(BF16)
    g0 = row(mix_norm_g[0])
    proj, dt_raw = _in_proj(h, g0, w_main, w_dt, tm=tm, tn=1024)

    conv_out = _conv_module(proj, conv_w[0].astype(F32), row(conv_b[0]), row(conv_ln_g[0]), row(conv_ln_b[0]), tt=128)

    pad_heads = lambda v: jnp.zeros((1, LANES), F32).at[0, :SSM_HEADS].set(v.astype(F32))
    ssm_out = _ssd(proj, dt_raw, ssm_conv_w[0].astype(F32), row(ssm_conv_b[0]),
                   pad_heads(dt_bias[0]), pad_heads(-jnp.exp(a_log[0].astype(F32))),
                   row(jnp.repeat(d_skip[0].astype(F32), SSM_HEAD_DIM)), row(ssm_norm_g[0]))

    wo = w_out[0].astype(BF16)
    h = _matmul_residual([conv_out, ssm_out], [wo[:D_MODEL], wo[D_MODEL:]], h, jnp.zeros((1, d), F32),
                         tm=512, tn=1024, name="out_proj")
    h = _swiglu(h, row(ffn_norm_g[0]), ffn_w_gate[0].astype(BF16), ffn_w_up[0].astype(BF16),
                ffn_w_down[0].astype(BF16), tm=512, tf=512)

    qkv = _norm_matmul(h, row(mix_norm_g[1]), w_qkv[0].astype(BF16), row(b_qkv[0]), tm=tm, tn=1280,
                       out_dtype=BF16, name="qkv_proj")
    attn = _attention(qkv, sinks[0], rel_bias)
    h = _matmul_residual([attn], [w_o[0].astype(BF16)], h, row(b_o[0]), tm=512, tn=1024, name="attn_out_proj")

    g1 = row(ffn_norm_g[1])
    route = _router(h, g1, router_w[0], router_b[0], tm=512)
    ge, ns, src_tok, pos = _moe_plan(route, m)
    rows = _moe_ffn(h, g1, moe_w_gate[0], moe_w_up[0], moe_w_down[0], ge, ns, src_tok, tf=256)
    out = _combine(h, route, row(final_norm_g), rows, pos, tc=256)
    return out.reshape(bsz, seq, d)
```

```python
import functools
import math

import numpy as np
import jax
import jax.numpy as jnp
from jax import lax
from jax.experimental import pallas as pl
from jax.experimental.pallas import tpu as pltpu

F32 = jnp.float32
BF16 = jnp.bfloat16
NORM_EPS = 1e-5
NEG_INF = float("-inf")

LANES = 128
SUBLANES = 8
VMEM_LIMIT = 56 << 20

D_MODEL = 2048
CONV_WIDTH = 31
SSM_HEADS = 32
SSM_HEAD_DIM = 64
SSM_GROUPS = 8
SSM_STATE = 128
SSM_CONV_WIDTH = 4
SSM_CHUNK = 128
ATTN_Q_HEADS = 32
ATTN_KV_HEADS = 4
ATTN_HEAD_DIM = 64
ATTN_BLOCK = 128
REL_BUCKETS = 32
REL_MAX_DIST = 128
N_EXPERTS = 8
TOP_K = 2
MOE_SUB = 256
MOE_SUBS_PER_GROUP = 4
MOE_GROUP = MOE_SUB * MOE_SUBS_PER_GROUP
MOE_UNROLL = 8
MOE_WEIGHT_SLOTS = 3


def _params(sem):
    return pltpu.CompilerParams(dimension_semantics=sem, vmem_limit_bytes=VMEM_LIMIT)


def _sigmoid(x):
    return 0.5 + 0.5 * jnp.tanh(0.5 * x)


def _silu(x):
    half = 0.5 * x
    return half + half * jnp.tanh(half)


def _rms_rows(x, g):
    ms = jnp.mean(x * x, axis=-1, keepdims=True)
    return x * lax.rsqrt(ms + NORM_EPS) * g


def _norm_matmul_kernel(x_ref, g_ref, w_ref, b_ref, o_ref, xn_ref):
    @pl.when(pl.program_id(1) == 0)
    def _():
        xn_ref[...] = _rms_rows(x_ref[...], g_ref[...]).astype(BF16)

    acc = jnp.dot(xn_ref[...], w_ref[...], preferred_element_type=F32)
    o_ref[...] = (acc + b_ref[...]).astype(o_ref.dtype)


def _norm_matmul(x, g, w, b, *, tm, tn, out_dtype, name):
    m, k = x.shape
    n = w.shape[1]
    return pl.pallas_call(
        _norm_matmul_kernel,
        grid=(m // tm, n // tn),
        in_specs=[
            pl.BlockSpec((tm, k), lambda i, j: (i, 0)),
            pl.BlockSpec((1, k), lambda i, j: (0, 0)),
            pl.BlockSpec((k, tn), lambda i, j: (0, j)),
            pl.BlockSpec((1, tn), lambda i, j: (0, j)),
        ],
        out_specs=pl.BlockSpec((tm, tn), lambda i, j: (i, j)),
        out_shape=jax.ShapeDtypeStruct((m, n), out_dtype),
        scratch_shapes=[pltpu.VMEM((tm, k), BF16)],
        compiler_params=_params(("parallel", "arbitrary")),
        name=name,
    )(x, g, w, b)


def _in_proj_kernel(x_ref, g_ref, w_ref, wdt_ref, o_ref, odt_ref, xn_ref):
    @pl.when(pl.program_id(1) == 0)
    def _():
        xn_ref[...] = _rms_rows(x_ref[...], g_ref[...]).astype(BF16)
        odt_ref[...] = jnp.dot(xn_ref[...], wdt_ref[...], preferred_element_type=F32)

    o_ref[...] = jnp.dot(xn_ref[...], w_ref[...], preferred_element_type=F32).astype(o_ref.dtype)


def _in_proj(x, g, w, wdt, *, tm, tn):
    m, k = x.shape
    n = w.shape[1]
    ndt = wdt.shape[1]
    return pl.pallas_call(
        _in_proj_kernel,
        grid=(m // tm, n // tn),
        in_specs=[
            pl.BlockSpec((tm, k), lambda i, j: (i, 0)),
            pl.BlockSpec((1, k), lambda i, j: (0, 0)),
            pl.BlockSpec((k, tn), lambda i, j: (0, j)),
            pl.BlockSpec((k, ndt), lambda i, j: (0, 0)),
        ],
        out_specs=[pl.BlockSpec((tm, tn), lambda i, j: (i, j)), pl.BlockSpec((tm, ndt), lambda i, j: (i, 0))],
        out_shape=[jax.ShapeDtypeStruct((m, n), BF16), jax.ShapeDtypeStruct((m, ndt), F32)],
        scratch_shapes=[pltpu.VMEM((tm, k), BF16)],
        compiler_params=_params(("parallel", "arbitrary")),
        name="in_proj",
    )(x, g, w, wdt)


def _matmul_residual_kernel(*refs, n_in):
    a_refs, w_refs = refs[:n_in], refs[n_in:2 * n_in]
    res_ref, b_ref, o_ref = refs[2 * n_in:]
    acc = res_ref[...] + b_ref[...]
    for a_ref, w_ref in zip(a_refs, w_refs):
        acc = acc + jnp.dot(a_ref[...], w_ref[...], preferred_element_type=F32)
    o_ref[...] = acc


def _matmul_residual(a_list, w_list, res, b, *, tm, tn, name):
    m, n = res.shape
    n_in = len(a_list)
    in_specs = [pl.BlockSpec((tm, a.shape[1]), lambda j, i: (i, 0)) for a in a_list]
    in_specs += [pl.BlockSpec((w.shape[0], tn), lambda j, i: (0, j)) for w in w_list]
    in_specs += [pl.BlockSpec((tm, tn), lambda j, i: (i, j)), pl.BlockSpec((1, tn), lambda j, i: (0, j))]
    return pl.pallas_call(
        functools.partial(_matmul_residual_kernel, n_in=n_in),
        grid=(n // tn, m // tm),
        in_specs=in_specs,
        out_specs=pl.BlockSpec((tm, tn), lambda j, i: (i, j)),
        out_shape=jax.ShapeDtypeStruct((m, n), F32),
        compiler_params=_params(("parallel", "parallel")),
        name=name,
    )(*a_list, *w_list, res, b)


def _swiglu_kernel(x_ref, g_ref, wg_ref, wu_ref, wd_ref, o_ref, xn_ref):
    @pl.when(pl.program_id(1) == 0)
    def _():
        x = x_ref[...]
        xn_ref[...] = _rms_rows(x, g_ref[...]).astype(BF16)
        o_ref[...] = x

    xn = xn_ref[...]
    gate = jnp.dot(xn, wg_ref[...], preferred_element_type=F32)
    up = jnp.dot(xn, wu_ref[...], preferred_element_type=F32)
    hid = (_silu(gate) * up).astype(BF16)
    o_ref[...] += jnp.dot(hid, wd_ref[...], preferred_element_type=F32)


def _swiglu(x, g, wg, wu, wd, *, tm, tf):
    m, d = x.shape
    f = wg.shape[1]
    return pl.pallas_call(
        _swiglu_kernel,
        grid=(m // tm, f // tf),
        in_specs=[
            pl.BlockSpec((tm, d), lambda i, j: (i, 0)),
            pl.BlockSpec((1, d), lambda i, j: (0, 0)),
            pl.BlockSpec((d, tf), lambda i, j: (0, j)),
            pl.BlockSpec((d, tf), lambda i, j: (0, j)),
            pl.BlockSpec((tf, d), lambda i, j: (j, 0)),
        ],
        out_specs=pl.BlockSpec((tm, d), lambda i, j: (i, 0)),
        out_shape=jax.ShapeDtypeStruct((m, d), F32),
        scratch_shapes=[pltpu.VMEM((tm, d), BF16)],
        compiler_params=_params(("parallel", "arbitrary")),
        name="swiglu",
    )(x, g, wg, wu, wd)


CONV_HALO = 32
CONV_ROWS = 64


def _conv_module_kernel(val_ref, gate_ref, pval_ref, pgate_ref, w_ref, b_ref, lg_ref, lb_ref, o_ref,
                        ubuf, cbuf, *, tt):
    i = pl.program_id(0)

    def glu(v, g):
        return v.astype(F32) * _sigmoid(g.astype(F32))

    ubuf[0:CONV_HALO, :] = jnp.where(i > 0, glu(pval_ref[...], pgate_ref[...]), 0.0)
    ubuf[CONV_HALO:CONV_HALO + tt, :] = glu(val_ref[...], gate_ref[...])

    off = CONV_HALO - (CONV_WIDTH - 1)
    taps = [[] for _ in range(SUBLANES)]
    for j in range(CONV_WIDTH):
        a, r = divmod(off + j, SUBLANES)
        taps[r].append((a, j))
    n_ch = ubuf.shape[1]
    for cb in range(n_ch // LANES):
        ls = slice(cb * LANES, (cb + 1) * LANES)
        for r0 in range(0, tt, CONV_ROWS):
            big = ubuf[r0:r0 + CONV_ROWS + CONV_HALO, ls]
            out = None
            for r in range(SUBLANES):
                rows = CONV_ROWS if r == 0 else CONV_ROWS + SUBLANES
                q = None
                for a, j in taps[r]:
                    term = w_ref[j:j + 1, ls] * big[a * SUBLANES:a * SUBLANES + rows, :]
                    q = term if q is None else q + term
                if r:
                    q = pltpu.roll(q, rows - r, axis=0)[:CONV_ROWS, :]
                out = q if out is None else out + q
            cbuf[r0:r0 + CONV_ROWS, ls] = out

    c = cbuf[...] + b_ref[...]
    mu = jnp.mean(c, axis=-1, keepdims=True)
    d = c - mu
    var = jnp.mean(d * d, axis=-1, keepdims=True)
    y = d * lax.rsqrt(var + NORM_EPS) * lg_ref[...] + lb_ref[...]
    o_ref[...] = _silu(y).astype(o_ref.dtype)


def _conv_module(proj, conv_w, conv_b, ln_g, ln_b, *, tt):
    m = proj.shape[0]
    c = D_MODEL
    hb = tt // CONV_HALO
    return pl.pallas_call(
        functools.partial(_conv_module_kernel, tt=tt),
        grid=(m // tt,),
        in_specs=[
            pl.BlockSpec((tt, c), lambda i: (i, 0)),
            pl.BlockSpec((tt, c), lambda i: (i, 1)),
            pl.BlockSpec((CONV_HALO, c), lambda i: (jnp.maximum(i * hb - 1, 0), 0)),
            pl.BlockSpec((CONV_HALO, c), lambda i: (jnp.maximum(i * hb - 1, 0), 1)),
            pl.BlockSpec((CONV_WIDTH, c), lambda i: (0, 0)),
            pl.BlockSpec((1, c), lambda i: (0, 0)),
            pl.BlockSpec((1, c), lambda i: (0, 0)),
            pl.BlockSpec((1, c), lambda i: (0, 0)),
        ],
        out_specs=pl.BlockSpec((tt, c), lambda i: (i, 0)),
        out_shape=jax.ShapeDtypeStruct((m, c), BF16),
        scratch_shapes=[pltpu.VMEM((CONV_HALO + tt, c), F32), pltpu.VMEM((tt, c), F32)],
        compiler_params=_params(("arbitrary",)),
        name="conv_module",
    )(proj, proj, proj, proj, conv_w, conv_b, ln_g, ln_b)


SSM_HALO = 16


def _conv4_silu(cur_ref, prev_ref, shift_ref, w_ref, b_ref, cols, first):
    q = cur_ref.shape[0]
    cur = cur_ref[:, cols]
    prev = prev_ref[:, cols]
    xb = jnp.concatenate([jnp.where(first, jnp.zeros_like(prev), prev), cur], axis=0)
    back = jnp.dot(shift_ref[...], xb, preferred_element_type=F32)
    last = SSM_CONV_WIDTH - 1
    acc = b_ref[:, cols] + w_ref[last:last + 1, cols] * cur.astype(F32)
    for k in range(1, SSM_CONV_WIDTH):
        acc = acc + w_ref[last - k:last - k + 1, cols] * back[(k - 1) * q:k * q, :]
    return _silu(acc)


def _ssd_kernel(z_ref, x_ref, b_ref, c_ref, px_ref, pb_ref, pc_ref, dtr_ref,
                wx_ref, wb_ref, wc_ref, bx_ref, bb_ref, bc_ref,
                dtb_ref, a_ref, dskip_ref, ng_ref, e_ref, sh_ref, o_ref,
                state, ybuf):
    i = pl.program_id(0)
    q = SSM_CHUNK
    first = i == 0

    @pl.when(first)
    def _():
        state[...] = jnp.zeros_like(state)

    everything = slice(None)
    xs = _conv4_silu(x_ref, px_ref, sh_ref, wx_ref, bx_ref, everything, first)
    bm = _conv4_silu(b_ref, pb_ref, sh_ref, wb_ref, bb_ref, everything, first)
    cm = _conv4_silu(c_ref, pc_ref, sh_ref, wc_ref, bc_ref, everything, first)

    pre = dtr_ref[...] + dtb_ref[...]
    dt = jnp.maximum(pre, 0.0) + jnp.log(1.0 + jnp.exp(-jnp.abs(pre)))
    dta = dt * a_ref[...]
    row = lax.broadcasted_iota(jnp.int32, (q, q), 0)
    col = lax.broadcasted_iota(jnp.int32, (q, q), 1)
    causal = row >= col
    acs = jnp.dot(causal.astype(F32), dta, precision=lax.Precision.HIGHEST, preferred_element_type=F32)
    acs_t = acs.T
    dt_t = dt.T
    eacs = jnp.exp(acs)
    wdec = dt * jnp.exp(acs[q - 1:q, :] - acs)

    def split(v):
        hi = v.astype(BF16)
        return hi, (v - hi.astype(F32)).astype(BF16)

    parts = jnp.concatenate(split(eacs) + split(wdec), axis=0)
    wide = jnp.dot(parts, e_ref[...], preferred_element_type=F32)
    eacs_full = wide[0:q] + wide[q:2 * q]
    xd = (xs * (wide[2 * q:3 * q] + wide[3 * q:4 * q])).astype(BF16)

    xs_b = xs.astype(BF16)
    lane = lax.broadcasted_iota(jnp.int32, (1, xs.shape[1]), 1)
    low_head = (lane % LANES) < SSM_HEAD_DIM
    xs_lo = jnp.where(low_head, xs_b, jnp.zeros_like(xs_b))
    xs_hi = jnp.where(low_head, jnp.zeros_like(xs_b), xs_b)

    gw = SSM_HEAD_DIM * (SSM_HEADS // SSM_GROUPS)
    for g in range(SSM_GROUPS):
        ns = slice(g * SSM_STATE, (g + 1) * SSM_STATE)
        gs = slice(g * gw, (g + 1) * gw)
        bg_f = bm[:, ns]
        bg = bg_f.astype(BF16)
        cg = cm[:, ns].astype(BF16)
        cb = lax.dot_general(cg, bg, (((1,), (1,)), ((), ())), preferred_element_type=F32)
        prev = state[g]
        y_off = jnp.dot(cg, prev.astype(BF16), preferred_element_type=F32)
        for c in range(gw // LANES):
            cs = slice(g * gw + c * LANES, g * gw + (c + 1) * LANES)
            acc = y_off[:, c * LANES:(c + 1) * LANES] * eacs_full[:, cs]
            for par, xpart in ((0, xs_lo), (1, xs_hi)):
                h = g * (SSM_HEADS // SSM_GROUPS) + 2 * c + par
                seg = acs[:, h:h + 1] - acs_t[h:h + 1, :]
                mmat = cb * jnp.exp(jnp.where(causal, seg, NEG_INF)) * dt_t[h:h + 1, :]
                acc = acc + jnp.dot(mmat.astype(BF16), xpart[:, cs], preferred_element_type=F32)
            ybuf[:, cs] = acc
        new_states = jnp.dot(bg_f.T.astype(BF16), xd[:, gs], preferred_element_type=F32)
        state[g] = prev * eacs_full[q - 1:q, gs] + new_states

    y = ybuf[...] + xs * dskip_ref[...]
    z = z_ref[...].astype(F32)
    y = y * _silu(z)
    for g in range(SSM_GROUPS):
        gs = slice(g * gw, (g + 1) * gw)
        yg = y[:, gs]
        ms = jnp.mean(yg * yg, axis=-1, keepdims=True)
        o_ref[:, gs] = (yg * lax.rsqrt(ms + NORM_EPS) * ng_ref[:, gs]).astype(o_ref.dtype)


def _ssd(proj, dt_raw, conv_w, conv_b, dt_bias, a_neg, d_skip, norm_g):
    m = proj.shape[0]
    q = SSM_CHUNK
    inner = D_MODEL
    bc = SSM_GROUPS * SSM_STATE
    hb = q // SSM_HALO
    wx, wb, wc = conv_w[:, :inner], conv_w[:, inner:inner + bc], conv_w[:, inner + bc:]
    bx, bb, bcc = conv_b[:, :inner], conv_b[:, inner:inner + bc], conv_b[:, inner + bc:]
    expand = jnp.asarray(np.arange(LANES)[:, None] == (np.arange(inner) // SSM_HEAD_DIM)[None, :], BF16)
    t_idx = np.arange(q)
    shift_np = np.zeros((SSM_CONV_WIDTH - 1, q, SSM_HALO + q), np.float32)
    for k in range(1, SSM_CONV_WIDTH):
        shift_np[k - 1, t_idx, SSM_HALO + t_idx - k] = 1.0
    shift = jnp.asarray(shift_np.reshape(-1, SSM_HALO + q), BF16)
    prev = lambda i: jnp.maximum(i * hb - 1, 0)
    full = lambda shape: pl.BlockSpec(shape, lambda i: (0,) * len(shape))
    return pl.pallas_call(
        _ssd_kernel,
        grid=(m // q,),
        in_specs=[
            pl.BlockSpec((q, inner), lambda i: (i, 2)),
            pl.BlockSpec((q, inner), lambda i: (i, 3)),
            pl.BlockSpec((q, bc), lambda i: (i, 8)),
            pl.BlockSpec((q, bc), lambda i: (i, 9)),
            pl.BlockSpec((SSM_HALO, inner), lambda i: (prev(i), 3)),
            pl.BlockSpec((SSM_HALO, bc), lambda i: (prev(i), 8)),
            pl.BlockSpec((SSM_HALO, bc), lambda i: (prev(i), 9)),
            pl.BlockSpec((q, LANES), lambda i: (i, 0)),
            full(wx.shape), full(wb.shape), full(wc.shape),
            full(bx.shape), full(bb.shape), full(bcc.shape),
            full((1, LANES)), full((1, LANES)), full((1, inner)), full((1, inner)),
            full((LANES, inner)),
            full(shift.shape),
        ],
        out_specs=pl.BlockSpec((q, inner), lambda i: (i, 0)),
        out_shape=jax.ShapeDtypeStruct((m, inner), BF16),
        scratch_shapes=[
            pltpu.VMEM((SSM_GROUPS, SSM_STATE, inner // SSM_GROUPS), F32),
            pltpu.VMEM((q, inner), F32),
        ],
        compiler_params=_params(("arbitrary",)),
        name="ssd",
    )(proj, proj, proj, proj, proj, proj, proj, dt_raw, wx, wb, wc, bx, bb, bcc,
      dt_bias, a_neg, d_skip, norm_g, expand, shift)


def _attn_kernel(sinks_ref, q_ref, kvc_ref, kvp_ref, bias_ref, o_ref):
    blk = ATTN_BLOCK
    kvw = ATTN_KV_HEADS * ATTN_HEAD_DIM
    rep = ATTN_Q_HEADS // ATTN_KV_HEADS
    kv = jnp.concatenate([kvp_ref[...], kvc_ref[...]], axis=0).astype(F32)
    lane = lax.broadcasted_iota(jnp.int32, (1, LANES), 1)
    low = lane < ATTN_HEAD_DIM
    scale = ATTN_HEAD_DIM ** -0.5

    for g in range(ATTN_KV_HEADS):
        pc = g // 2
        kcol = kv[:, pc * LANES:(pc + 1) * LANES] * scale
        vcol = kv[:, kvw + pc * LANES:kvw + (pc + 1) * LANES]
        kroll = pltpu.roll(kcol, ATTN_HEAD_DIM, axis=1)
        vroll = pltpu.roll(vcol, ATTN_HEAD_DIM, axis=1)
        if g % 2 == 0:
            k_lo, k_hi = jnp.where(low, kcol, 0.0), jnp.where(low, 0.0, kroll)
            v_lo, v_hi = jnp.where(low, vcol, 0.0), jnp.where(low, 0.0, vroll)
        else:
            k_lo, k_hi = jnp.where(low, kroll, 0.0), jnp.where(low, 0.0, kcol)
            v_lo, v_hi = jnp.where(low, vroll, 0.0), jnp.where(low, 0.0, vcol)
        k_lo, k_hi, v_lo, v_hi = (t.astype(BF16) for t in (k_lo, k_hi, v_lo, v_hi))

        ncol = rep // 2
        qs = jnp.concatenate([q_ref[:, (g * ncol + c) * LANES:(g * ncol + c + 1) * LANES] for c in range(ncol)], axis=0)
        nt = (((1,), (1,)), ((), ()))
        s_even = lax.dot_general(qs, k_lo, nt, preferred_element_type=F32)
        s_odd = lax.dot_general(qs, k_hi, nt, preferred_element_type=F32)
        for c in range(ncol):
            out = None
            for par, s_all, vv in ((0, s_even, v_lo), (1, s_odd, v_hi)):
                h = g * rep + 2 * c + par
                logits = s_all[c * blk:(c + 1) * blk, :] + bias_ref[0, h]
                sink = sinks_ref[h]
                mx = jnp.maximum(jnp.max(logits, axis=-1, keepdims=True), sink)
                p = jnp.exp(logits - mx)
                denom = jnp.sum(p, axis=-1, keepdims=True) + jnp.exp(sink - mx)
                o = jnp.dot(p.astype(BF16), vv, preferred_element_type=F32) * (1.0 / denom)
                out = o if out is None else out + o
            col = g * ncol + c
            o_ref[:, col * LANES:(col + 1) * LANES] = out.astype(o_ref.dtype)


def _t5_bucket_table():
    dist = np.arange(ATTN_BLOCK)[:, None] + ATTN_BLOCK - np.arange(2 * ATTN_BLOCK)[None, :]
    n = np.maximum(dist, 0)
    max_exact = REL_BUCKETS // 2
    nf = np.maximum(n, 1).astype(np.float32)
    large = max_exact + (np.log(nf / max_exact) / math.log(REL_MAX_DIST / max_exact)
                         * (REL_BUCKETS - max_exact)).astype(np.int32)
    large = np.minimum(large, REL_BUCKETS - 1)
    return dist, np.where(n < max_exact, n, large)


def _attention(qkv, sinks, rel_bias):
    m = qkv.shape[0]
    blk = ATTN_BLOCK
    qd = ATTN_Q_HEADS * ATTN_HEAD_DIM
    kvd = 2 * ATTN_KV_HEADS * ATTN_HEAD_DIM
    dist, bucket = _t5_bucket_table()
    visible = (dist >= 0) & (dist < blk)
    onehot = jnp.asarray(np.arange(REL_BUCKETS)[:, None] == bucket.reshape(1, -1), F32)
    bias = jnp.dot(rel_bias.astype(F32).T, onehot, precision=lax.Precision.HIGHEST).reshape(-1, blk, 2 * blk)
    general = jnp.where(visible[None], bias, NEG_INF)
    first = jnp.where((visible & (np.arange(2 * blk) >= blk)[None, :])[None], bias, NEG_INF)
    table = jnp.stack([first, general])
    return pl.pallas_call(
        _attn_kernel,
        grid=(m // blk,),
        in_specs=[
            pl.BlockSpec(memory_space=pltpu.SMEM),
            pl.BlockSpec((blk, qd), lambda n: (n, 0)),
            pl.BlockSpec((blk, kvd), lambda n: (n, qd // kvd)),
            pl.BlockSpec((blk, kvd), lambda n: (jnp.maximum(n - 1, 0), qd // kvd)),
            pl.BlockSpec((1,) + table.shape[1:], lambda n: (jnp.minimum(n, 1), 0, 0, 0)),
        ],
        out_specs=pl.BlockSpec((blk, qd), lambda n: (n, 0)),
        out_shape=jax.ShapeDtypeStruct((m, qd), BF16),
        compiler_params=_params(("arbitrary",)),
        name="swa_attention",
    )(sinks.astype(F32), qkv, qkv, qkv, table)


def _router_kernel(x_ref, g_ref, rw_ref, rb_ref, o_ref):
    xn = _rms_rows(x_ref[...], g_ref[...])
    x_hi = xn.astype(BF16)
    x_lo = (xn - x_hi.astype(F32)).astype(BF16)
    both = jnp.dot(x_hi, rw_ref[...], preferred_element_type=F32)
    logits = (both[:, :LANES] + both[:, LANES:]
              + jnp.dot(x_lo, rw_ref[:, :LANES], preferred_element_type=F32) + rb_ref[...])
    lane = lax.broadcasted_iota(jnp.int32, logits.shape, 1)
    m1 = jnp.max(logits, axis=-1, keepdims=True)
    i1 = jnp.min(jnp.where(logits == m1, lane, LANES), axis=-1, keepdims=True)
    rest = jnp.where(lane == i1, NEG_INF, logits)
    m2 = jnp.max(rest, axis=-1, keepdims=True)
    i2 = jnp.min(jnp.where(rest == m2, lane, LANES), axis=-1, keepdims=True)
    e = jnp.exp(m2 - m1)
    g1 = 1.0 / (1.0 + e)
    g2 = e * g1
    out = jnp.where(lane == 0, i1.astype(F32),
                    jnp.where(lane == 1, i2.astype(F32),
                              jnp.where(lane == 2, g1, jnp.where(lane == 3, g2, 0.0))))
    o_ref[...] = out


def _router(h, g, rw, rb, *, tm):
    m, d = h.shape
    rw_pad = jnp.zeros((d, LANES), F32).at[:, :N_EXPERTS].set(rw.astype(F32))
    rw_hi = rw_pad.astype(BF16)
    rw_pad = jnp.concatenate([rw_hi, (rw_pad - rw_hi.astype(F32)).astype(BF16)], axis=1)
    rb_pad = jnp.full((1, LANES), NEG_INF, F32).at[0, :N_EXPERTS].set(rb.astype(F32))
    return pl.pallas_call(
        _router_kernel,
        grid=(m // tm,),
        in_specs=[
            pl.BlockSpec((tm, d), lambda i: (i, 0)),
            pl.BlockSpec((1, d), lambda i: (0, 0)),
            pl.BlockSpec((d, 2 * LANES), lambda i: (0, 0)),
            pl.BlockSpec((1, LANES), lambda i: (0, 0)),
        ],
        out_specs=pl.BlockSpec((tm, LANES), lambda i: (i, 0)),
        out_shape=jax.ShapeDtypeStruct((m, LANES), F32),
        compiler_params=_params(("parallel",)),
        name="router",
    )(h, g, rw_pad, rb_pad)


def _moe_drain(h_hbm, xbuf, sem):
    for s in range(MOE_SUBS_PER_GROUP):
        rs = slice(s * MOE_SUB, (s + 1) * MOE_SUB)
        pltpu.make_async_copy(h_hbm.at[pl.ds(0, MOE_SUB)], xbuf.at[rs], sem).wait()
    slack = xbuf.shape[0] - MOE_GROUP
    pltpu.make_async_copy(h_hbm.at[pl.ds(0, slack)], xbuf.at[MOE_GROUP:, :], sem).wait()


def _moe_kernel(ge_ref, ns_ref, tok_ref, h_hbm, g_ref, wg_hbm, wu_hbm, wd_hbm, o_ref,
                xbuf, xn_ref, wg_buf, wu_buf, wd_buf, sem, wsem, *, rps, tf):
    grp = pl.program_id(0)
    j = pl.program_id(1)
    nj = pl.num_programs(1)
    n_grp = pl.num_programs(0)
    ns = ns_ref[grp]
    step = grp * nj + j
    slot = step % MOE_WEIGHT_SLOTS

    def weight_copies(gi, jt, sl):
        e = ge_ref[gi]
        cols = pl.ds(pl.multiple_of(jt * tf, tf), tf)
        return (pltpu.make_async_copy(wg_hbm.at[e, :, cols], wg_buf.at[sl], wsem.at[sl, 0]),
                pltpu.make_async_copy(wu_hbm.at[e, :, cols], wu_buf.at[sl], wsem.at[sl, 1]),
                pltpu.make_async_copy(wd_hbm.at[e, cols, :], wd_buf.at[sl], wsem.at[sl, 2]))

    @pl.when((step == 0) & (ns > 0))
    def _():
        for first in range(MOE_WEIGHT_SLOTS - 1):
            for c in weight_copies(0, first, first):
                c.start()

    ahead = j + (MOE_WEIGHT_SLOTS - 1)
    wraps = ahead >= nj
    g_ahead = jnp.where(wraps, grp + 1, grp)
    j_ahead = jnp.where(wraps, ahead - nj, ahead)
    used_ahead = (g_ahead < n_grp) & (ns_ref[jnp.minimum(g_ahead, n_grp - 1)] > 0)

    @pl.when((ns > 0) & used_ahead)
    def _():
        for c in weight_copies(g_ahead, j_ahead, (step + MOE_WEIGHT_SLOTS - 1) % MOE_WEIGHT_SLOTS):
            c.start()

    @pl.when(ns > 0)
    def _():
        for c in weight_copies(grp, j, slot):
            c.wait()

    def row_copy(gi, r):
        tok = tok_ref[gi * MOE_GROUP + jnp.minimum(r, MOE_GROUP - 1)]
        return pltpu.make_async_copy(h_hbm.at[pl.ds(tok, 1)], xbuf.at[pl.ds(r, 1)], sem)

    def issue_step_rows(gi):
        for u in range(rps):
            row_copy(gi, j * rps + u).start()

    @pl.when(j == 0)
    def _():
        @pl.when(grp == 0)
        def _():
            def issue(r, carry):
                for u in range(MOE_UNROLL):
                    row_copy(0, r * MOE_UNROLL + u).start()
                return carry

            lax.fori_loop(0, xbuf.shape[0] // MOE_UNROLL, issue, 0)

        _moe_drain(h_hbm, xbuf, sem)
        for s in range(MOE_SUBS_PER_GROUP):
            rs = slice(s * MOE_SUB, (s + 1) * MOE_SUB)
            o_ref[rs, :] = jnp.zeros((MOE_SUB, o_ref.shape[1]), F32)

            @pl.when(s < ns)
            def _():
                xn_ref[rs, :] = _rms_rows(xbuf[rs, :], g_ref[...]).astype(BF16)

    def ffn(rs):
        xn = xn_ref[rs, :]
        gate = jnp.dot(xn, wg_buf[slot].astype(BF16), preferred_element_type=F32)
        up = jnp.dot(xn, wu_buf[slot].astype(BF16), preferred_element_type=F32)
        hid = (_silu(gate) * up).astype(BF16)
        o_ref[rs, :] += jnp.dot(hid, wd_buf[slot].astype(BF16), preferred_element_type=F32)

    @pl.when(ns == MOE_SUBS_PER_GROUP)
    def _():
        issue_step_rows(grp + 1)
        ffn(slice(0, MOE_GROUP))

    @pl.when(ns < MOE_SUBS_PER_GROUP)
    def _():
        issue_step_rows(grp + 1)

    for s in range(MOE_SUBS_PER_GROUP - 1):
        @pl.when((s < ns) & (ns < MOE_SUBS_PER_GROUP))
        def _():
            ffn(slice(s * MOE_SUB, (s + 1) * MOE_SUB))

    @pl.when((grp == pl.num_programs(0) - 1) & (j == pl.num_programs(1) - 1))
    def _():
        _moe_drain(h_hbm, xbuf, sem)


def _moe_ffn(h, g, wg, wu, wd, group_e, group_ns, src_tok, *, tf):
    t, d = h.shape
    n_groups = group_e.shape[0]
    f = wg.shape[2]
    nj = f // tf
    rps = -(-MOE_GROUP // (nj * SUBLANES)) * SUBLANES
    assert src_tok.shape[0] == (n_groups + 1) * MOE_GROUP and nj >= MOE_WEIGHT_SLOTS

    return pl.pallas_call(
        functools.partial(_moe_kernel, rps=rps, tf=tf),
        grid_spec=pltpu.PrefetchScalarGridSpec(
            num_scalar_prefetch=3,
            grid=(n_groups, nj),
            in_specs=[
                pl.BlockSpec(memory_space=pl.ANY),
                pl.BlockSpec((1, d), lambda grp, j, ge, ns, tok: (0, 0)),
                pl.BlockSpec(memory_space=pl.ANY),
                pl.BlockSpec(memory_space=pl.ANY),
                pl.BlockSpec(memory_space=pl.ANY),
            ],
            out_specs=pl.BlockSpec((MOE_GROUP, d), lambda grp, j, ge, ns, tok: (grp, 0)),
            scratch_shapes=[
                pltpu.VMEM((rps * nj, d), F32),
                pltpu.VMEM((MOE_GROUP, d), BF16),
                pltpu.VMEM((MOE_WEIGHT_SLOTS, d, tf), F32),
                pltpu.VMEM((MOE_WEIGHT_SLOTS, d, tf), F32),
                pltpu.VMEM((MOE_WEIGHT_SLOTS, tf, d), F32),
                pltpu.SemaphoreType.DMA(()),
                pltpu.SemaphoreType.DMA((MOE_WEIGHT_SLOTS, 3)),
            ],
        ),
        out_shape=jax.ShapeDtypeStruct((n_groups * MOE_GROUP, d), F32),
        compiler_params=_params(("arbitrary", "arbitrary")),
        name="moe_ffn",
    )(group_e, group_ns, src_tok, h, g, wg, wu, wd)


def _combine_kernel(pos_ref, h_ref, route_ref, fg_ref, rows_hbm, o_ref, buf_a, buf_b, sem, *, tc):
    i = pl.program_id(0)
    slot = i % 2

    def copies(blk, sl, r):
        pa = pos_ref[TOP_K * (blk * tc + r)]
        pb = pos_ref[TOP_K * (blk * tc + r) + 1]
        return (pltpu.make_async_copy(rows_hbm.at[pl.ds(pa, 1)], buf_a.at[sl, pl.ds(r, 1)], sem.at[sl]),
                pltpu.make_async_copy(rows_hbm.at[pl.ds(pb, 1)], buf_b.at[sl, pl.ds(r, 1)], sem.at[sl]))

    def issue_block(blk, sl):
        def issue(r, carry):
            ca, cb = copies(blk, sl, r)
            ca.start()
            cb.start()
            return carry

        lax.fori_loop(0, tc, issue, 0, unroll=8)

    @pl.when(i == 0)
    def _():
        issue_block(0, 0)

    @pl.when(i + 1 < pl.num_programs(0))
    def _():
        issue_block(i + 1, 1 - slot)

    def wait(r, carry):
        ca, cb = copies(i, slot, r)
        ca.wait()
        cb.wait()
        return carry

    lax.fori_loop(0, tc, wait, 0, unroll=8)
    route = route_ref[...]
    moe = route[:, 2:3] * buf_a[slot] + route[:, 3:4] * buf_b[slot]
    o_ref[...] = _rms_rows(h_ref[...] + moe, fg_ref[...])


def _combine(h, route, final_g, rows, pos, *, tc):
    t, d = h.shape
    return pl.pallas_call(
        functools.partial(_combine_kernel, tc=tc),
        grid_spec=pltpu.PrefetchScalarGridSpec(
            num_scalar_prefetch=1,
            grid=(t // tc,),
            in_specs=[
                pl.BlockSpec((tc, d), lambda i, pos: (i, 0)),
                pl.BlockSpec((tc, LANES), lambda i, pos: (i, 0)),
                pl.BlockSpec((1, d), lambda i, pos: (0, 0)),
                pl.BlockSpec(memory_space=pl.ANY),
            ],
            out_specs=pl.BlockSpec((tc, d), lambda i, pos: (i, 0)),
            scratch_shapes=[pltpu.VMEM((2, tc, d), F32), pltpu.VMEM((2, tc, d), F32), pltpu.SemaphoreType.DMA((2,))],
        ),
        out_shape=jax.ShapeDtypeStruct((t, d), F32),
        compiler_params=_params(("arbitrary",)),
        name="moe_combine",
    )(pos, h, route, final_g, rows)


def _moe_plan(route, n_tok):
    flat_e = route[:, :TOP_K].astype(jnp.int32).reshape(-1)
    onehot = (flat_e[:, None] == jnp.arange(N_EXPERTS, dtype=jnp.int32)[None, :]).astype(jnp.int32)
    csum = jnp.cumsum(onehot, axis=0)
    rank = jnp.sum(onehot * (csum - onehot), axis=1)
    counts = csum[-1]
    n_groups = (n_tok * TOP_K) // MOE_GROUP + N_EXPERTS
    groups_e = (counts + MOE_GROUP - 1) // MOE_GROUP
    group_end = jnp.cumsum(groups_e)
    group_start = group_end - groups_e
    pos = jnp.sum(onehot * (group_start * MOE_GROUP)[None, :], axis=1) + rank
    flat_tok = jnp.arange(n_tok * TOP_K, dtype=jnp.int32) // TOP_K
    src_tok = jnp.zeros(((n_groups + 1) * MOE_GROUP,), jnp.int32).at[pos].set(
        flat_tok, unique_indices=True, mode="promise_in_bounds")
    experts = jnp.arange(N_EXPERTS, dtype=jnp.int32)
    gidx = jnp.arange(n_groups, dtype=jnp.int32)
    used = gidx < group_end[-1]
    last_e = jnp.max(jnp.where(groups_e > 0, experts, 0))
    ge = jnp.sum((group_end[None, :] <= gidx[:, None]).astype(jnp.int32), axis=1)
    ge = jnp.where(used, ge, last_e).astype(jnp.int32)
    mine = (ge[:, None] == experts[None, :]).astype(jnp.int32)
    subs_e = (counts + MOE_SUB - 1) // MOE_SUB
    subs_g = jnp.sum(mine * subs_e[None, :], axis=1)
    start_g = jnp.sum(mine * group_start[None, :], axis=1)
    ns = jnp.clip(subs_g - MOE_SUBS_PER_GROUP * (gidx - start_g), 0, MOE_SUBS_PER_GROUP)
    ns = jnp.where(used, ns, 0).astype(jnp.int32)
    return ge, ns, src_tok, pos.astype(jnp.int32)


def kernel(x, mix_norm_g, ffn_norm_g, final_norm_g, w_in, conv_w, conv_b, conv_ln_g, conv_ln_b, ssm_conv_w, ssm_conv_b, dt_bias, a_log, d_skip, ssm_norm_g, w_out, ffn_w_gate, ffn_w_up, ffn_w_down, w_qkv, b_qkv, w_o, b_o, sinks, rel_bias, router_w, router_b, moe_w_gate, moe_w_up, moe_w_down):
    bsz, seq, d = x.shape
    assert bsz == 1 and d == D_MODEL and seq % 512 == 0
    m = seq
    h = x.reshape(m, d)
    row = lambda v: v.reshape(1, -1).astype(F32)
    tm = min(1024, m)

    main_w = 2 * D_MODEL + D_MODEL + (D_MODEL + 2 * SSM_GROUPS * SSM_STATE)
    w_main = w_in[0][:, :main_w].astype(BF16)
    w_dt = jnp.zeros((d, LANES), F32).at[:, :SSM_HEADS].set(w_in[0][:, main_w:]).astype(BF16)
    g0 = row(mix_norm_g[0])
    proj, dt_raw = _in_proj(h, g0, w_main, w_dt, tm=tm, tn=1024)

    conv_out = _conv_module(proj, conv_w[0].astype(F32), row(conv_b[0]), row(conv_ln_g[0]), row(conv_ln_b[0]), tt=128)

    pad_heads = lambda v: jnp.zeros((1, LANES), F32).at[0, :SSM_HEADS].set(v.astype(F32))
    ssm_out = _ssd(proj, dt_raw, ssm_conv_w[0].astype(F32), row(ssm_conv_b[0]),
                   pad_heads(dt_bias[0]), pad_heads(-jnp.exp(a_log[0].astype(F32))),
                   row(jnp.repeat(d_skip[0].astype(F32), SSM_HEAD_DIM)), row(ssm_norm_g[0]))

    wo = w_out[0].astype(BF16)
    h = _matmul_residual([conv_out, ssm_out], [wo[:D_MODEL], wo[D_MODEL:]], h, jnp.zeros((1, d), F32),
                         tm=512, tn=1024, name="out_proj")
    h = _swiglu(h, row(ffn_norm_g[0]), ffn_w_gate[0].astype(BF16), ffn_w_up[0].astype(BF16),
                ffn_w_down[0].astype(BF16), tm=512, tf=512)

    qkv = _norm_matmul(h, row(mix_norm_g[1]), w_qkv[0].astype(BF16), row(b_qkv[0]), tm=tm, tn=1280,
                       out_dtype=BF16, name="qkv_proj")
    attn = _attention(qkv, sinks[0], rel_bias)
    h = _matmul_residual([attn], [w_o[0].astype(BF16)], h, row(b_o[0]), tm=512, tn=1024, name="attn_out_proj")

    g1 = row(ffn_norm_g[1])
    route = _router(h, g1, router_w[0], router_b[0], tm=512)
    ge, ns, src_tok, pos = _moe_plan(route, m)
    rows = _moe_ffn(h, g1, moe_w_gate[0], moe_w_up[0], moe_w_down[0], ge, ns, src_tok, tf=256)
    out = _combine(h, route, row(final_norm_g), rows, pos, tc=256)
    return out.reshape(bsz, seq, d)
```

```python
import functools
import math

import numpy as np
import jax
import jax.numpy as jnp
from jax import lax
from jax.experimental import pallas as pl
from jax.experimental.pallas import tpu as pltpu

F32 = jnp.float32
BF16 = jnp.bfloat16
NORM_EPS = 1e-5
NEG_INF = float("-inf")

LANES = 128
SUBLANES = 8
VMEM_LIMIT = 56 << 20

D_MODEL = 2048
CONV_WIDTH = 31
SSM_HEADS = 32
SSM_HEAD_DIM = 64
SSM_GROUPS = 8
SSM_STATE = 128
SSM_CONV_WIDTH = 4
SSM_CHUNK = 128
ATTN_Q_HEADS = 32
ATTN_KV_HEADS = 4
ATTN_HEAD_DIM = 64
ATTN_BLOCK = 128
REL_BUCKETS = 32
REL_MAX_DIST = 128
N_EXPERTS = 8
TOP_K = 2
MOE_SUB = 256
MOE_SUBS_PER_GROUP = 4
MOE_GROUP = MOE_SUB * MOE_SUBS_PER_GROUP
MOE_UNROLL = 8


def _params(sem):
    return pltpu.CompilerParams(dimension_semantics=sem, vmem_limit_bytes=VMEM_LIMIT)


def _sigmoid(x):
    return 0.5 + 0.5 * jnp.tanh(0.5 * x)


def _silu(x):
    half = 0.5 * x
    return half + half * jnp.tanh(half)


def _rms_rows(x, g):
    ms = jnp.mean(x * x, axis=-1, keepdims=True)
    return x * lax.rsqrt(ms + NORM_EPS) * g


def _norm_matmul_kernel(x_ref, g_ref, w_ref, b_ref, o_ref, xn_ref):
    @pl.when(pl.program_id(1) == 0)
    def _():
        xn_ref[...] = _rms_rows(x_ref[...], g_ref[...]).astype(BF16)

    acc = jnp.dot(xn_ref[...], w_ref[...], preferred_element_type=F32)
    o_ref[...] = (acc + b_ref[...]).astype(o_ref.dtype)


def _norm_matmul(x, g, w, b, *, tm, tn, out_dtype, name):
    m, k = x.shape
    n = w.shape[1]
    return pl.pallas_call(
        _norm_matmul_kernel,
        grid=(m // tm, n // tn),
        in_specs=[
            pl.BlockSpec((tm, k), lambda i, j: (i, 0)),
            pl.BlockSpec((1, k), lambda i, j: (0, 0)),
            pl.BlockSpec((k, tn), lambda i, j: (0, j)),
            pl.BlockSpec((1, tn), lambda i, j: (0, j)),
        ],
        out_specs=pl.BlockSpec((tm, tn), lambda i, j: (i, j)),
        out_shape=jax.ShapeDtypeStruct((m, n), out_dtype),
        scratch_shapes=[pltpu.VMEM((tm, k), BF16)],
        compiler_params=_params(("parallel", "arbitrary")),
        name=name,
    )(x, g, w, b)


def _in_proj_kernel(x_ref, g_ref, w_ref, wdt_ref, o_ref, odt_ref, xn_ref):
    @pl.when(pl.program_id(1) == 0)
    def _():
        xn_ref[...] = _rms_rows(x_ref[...], g_ref[...]).astype(BF16)
        odt_ref[...] = jnp.dot(xn_ref[...], wdt_ref[...], preferred_element_type=F32)

    o_ref[...] = jnp.dot(xn_ref[...], w_ref[...], preferred_element_type=F32).astype(o_ref.dtype)


def _in_proj(x, g, w, wdt, *, tm, tn, n):
    m, k = x.shape
    assert n % tn == 0 and n <= w.shape[1]
    ndt = wdt.shape[1]
    return pl.pallas_call(
        _in_proj_kernel,
        grid=(m // tm, n // tn),
        in_specs=[
            pl.BlockSpec((tm, k), lambda i, j: (i, 0)),
            pl.BlockSpec((1, k), lambda i, j: (0, 0)),
            pl.BlockSpec((k, tn), lambda i, j: (0, j)),
            pl.BlockSpec((k, ndt), lambda i, j: (0, 0)),
        ],
        out_specs=[pl.BlockSpec((tm, tn), lambda i, j: (i, j)), pl.BlockSpec((tm, ndt), lambda i, j: (i, 0))],
        out_shape=[jax.ShapeDtypeStruct((m, n), BF16), jax.ShapeDtypeStruct((m, ndt), F32)],
        scratch_shapes=[pltpu.VMEM((tm, k), BF16)],
        compiler_params=_params(("parallel", "arbitrary")),
        name="in_proj",
    )(x, g, w, wdt)


def _matmul_residual_kernel(*refs, n_in):
    a_refs, w_refs = refs[:n_in], refs[n_in:2 * n_in]
    res_ref, b_ref, o_ref = refs[2 * n_in:]
    acc = res_ref[...] + b_ref[...]
    for a_ref, w_ref in zip(a_refs, w_refs):
        acc = acc + jnp.dot(a_ref[...], w_ref[...], preferred_element_type=F32)
    o_ref[...] = acc


def _matmul_residual(a_list, w_list, res, b, *, tm, tn, name):
    m, n = res.shape
    n_in = len(a_list)
    in_specs = [pl.BlockSpec((tm, a.shape[1]), lambda j, i: (i, 0)) for a in a_list]
    in_specs += [pl.BlockSpec((w.shape[0], tn), lambda j, i: (0, j)) for w in w_list]
    in_specs += [pl.BlockSpec((tm, tn), lambda j, i: (i, j)), pl.BlockSpec((1, tn), lambda j, i: (0, j))]
    return pl.pallas_call(
        functools.partial(_matmul_residual_kernel, n_in=n_in),
        grid=(n // tn, m // tm),
        in_specs=in_specs,
        out_specs=pl.BlockSpec((tm, tn), lambda j, i: (i, j)),
        out_shape=jax.ShapeDtypeStruct((m, n), F32),
        compiler_params=_params(("parallel", "parallel")),
        name=name,
    )(*a_list, *w_list, res, b)


def _swiglu_kernel(x_ref, g_ref, wg_ref, wu_ref, wd_ref, o_ref, xn_ref):
    @pl.when(pl.program_id(1) == 0)
    def _():
        x = x_ref[...]
        xn_ref[...] = _rms_rows(x, g_ref[...]).astype(BF16)
        o_ref[...] = x

    xn = xn_ref[...]
    gate = jnp.dot(xn, wg_ref[...], preferred_element_type=F32)
    up = jnp.dot(xn, wu_ref[...], preferred_element_type=F32)
    hid = (_silu(gate) * up).astype(BF16)
    o_ref[...] += jnp.dot(hid, wd_ref[...], preferred_element_type=F32)


def _swiglu(x, g, wg, wu, wd, *, tm, tf):
    m, d = x.shape
    f = wg.shape[1]
    return pl.pallas_call(
        _swiglu_kernel,
        grid=(m // tm, f // tf),
        in_specs=[
            pl.BlockSpec((tm, d), lambda i, j: (i, 0)),
            pl.BlockSpec((1, d), lambda i, j: (0, 0)),
            pl.BlockSpec((d, tf), lambda i, j: (0, j)),
            pl.BlockSpec((d, tf), lambda i, j: (0, j)),
            pl.BlockSpec((tf, d), lambda i, j: (j, 0)),
        ],
        out_specs=pl.BlockSpec((tm, d), lambda i, j: (i, 0)),
        out_shape=jax.ShapeDtypeStruct((m, d), F32),
        scratch_shapes=[pltpu.VMEM((tm, d), BF16)],
        compiler_params=_params(("parallel", "arbitrary")),
        name="swiglu",
    )(x, g, wg, wu, wd)


CONV_HALO = 32
CONV_ROWS = 64


def _conv_module_kernel(val_ref, gate_ref, pval_ref, pgate_ref, w_ref, b_ref, lg_ref, lb_ref, o_ref,
                        ubuf, cbuf, *, tt):
    i = pl.program_id(0)

    def glu(v, g):
        return v.astype(F32) * _sigmoid(g.astype(F32))

    ubuf[0:CONV_HALO, :] = jnp.where(i > 0, glu(pval_ref[...], pgate_ref[...]), 0.0)
    ubuf[CONV_HALO:CONV_HALO + tt, :] = glu(val_ref[...], gate_ref[...])

    off = CONV_HALO - (CONV_WIDTH - 1)
    taps = [[] for _ in range(SUBLANES)]
    for j in range(CONV_WIDTH):
        a, r = divmod(off + j, SUBLANES)
        taps[r].append((a, j))
    n_ch = ubuf.shape[1]
    for cb in range(n_ch // LANES):
        ls = slice(cb * LANES, (cb + 1) * LANES)
        for r0 in range(0, tt, CONV_ROWS):
            big = ubuf[r0:r0 + CONV_ROWS + CONV_HALO, ls]
            out = None
            for r in range(SUBLANES):
                rows = CONV_ROWS if r == 0 else CONV_ROWS + SUBLANES
                q = None
                for a, j in taps[r]:
                    term = w_ref[j:j + 1, ls] * big[a * SUBLANES:a * SUBLANES + rows, :]
                    q = term if q is None else q + term
                if r:
                    q = pltpu.roll(q, rows - r, axis=0)[:CONV_ROWS, :]
                out = q if out is None else out + q
            cbuf[r0:r0 + CONV_ROWS, ls] = out

    c = cbuf[...] + b_ref[...]
    mu = jnp.mean(c, axis=-1, keepdims=True)
    d = c - mu
    var = jnp.mean(d * d, axis=-1, keepdims=True)
    y = d * lax.rsqrt(var + NORM_EPS) * lg_ref[...] + lb_ref[...]
    o_ref[...] = _silu(y).astype(o_ref.dtype)


def _conv_module(proj, conv_w, conv_b, ln_g, ln_b, *, tt):
    m = proj.shape[0]
    c = D_MODEL
    hb = tt // CONV_HALO
    return pl.pallas_call(
        functools.partial(_conv_module_kernel, tt=tt),
        grid=(m // tt,),
        in_specs=[
            pl.BlockSpec((tt, c), lambda i: (i, 0)),
            pl.BlockSpec((tt, c), lambda i: (i, 1)),
            pl.BlockSpec((CONV_HALO, c), lambda i: (jnp.maximum(i * hb - 1, 0), 0)),
            pl.BlockSpec((CONV_HALO, c), lambda i: (jnp.maximum(i * hb - 1, 0), 1)),
            pl.BlockSpec((CONV_WIDTH, c), lambda i: (0, 0)),
            pl.BlockSpec((1, c), lambda i: (0, 0)),
            pl.BlockSpec((1, c), lambda i: (0, 0)),
            pl.BlockSpec((1, c), lambda i: (0, 0)),
        ],
        out_specs=pl.BlockSpec((tt, c), lambda i: (i, 0)),
        out_shape=jax.ShapeDtypeStruct((m, c), BF16),
        scratch_shapes=[pltpu.VMEM((CONV_HALO + tt, c), F32), pltpu.VMEM((tt, c), F32)],
        compiler_params=_params(("arbitrary",)),
        name="conv_module",
    )(proj, proj, proj, proj, conv_w, conv_b, ln_g, ln_b)


SSM_HALO = 16


def _conv4_silu(cur_ref, prev_ref, shift_ref, w_ref, b_ref, cols, first):
    q = cur_ref.shape[0]
    cur = cur_ref[:, cols]
    prev = prev_ref[:, cols]
    xb = jnp.concatenate([jnp.where(first, jnp.zeros_like(prev), prev), cur], axis=0)
    back = jnp.dot(shift_ref[...], xb, preferred_element_type=F32)
    last = SSM_CONV_WIDTH - 1
    acc = b_ref[:, cols] + w_ref[last:last + 1, cols] * cur.astype(F32)
    for k in range(1, SSM_CONV_WIDTH):
        acc = acc + w_ref[last - k:last - k + 1, cols] * back[(k - 1) * q:k * q, :]
    return _silu(acc)


def _ssd_kernel(z_ref, x_ref, b_ref, c_ref, px_ref, pb_ref, pc_ref, dtr_ref,
                wx_ref, wb_ref, wc_ref, bx_ref, bb_ref, bc_ref,
                dtb_ref, a_ref, dskip_ref, ng_ref, e_ref, sh_ref, o_ref,
                state, ybuf):
    i = pl.program_id(0)
    q = SSM_CHUNK
    first = i == 0

    @pl.when(first)
    def _():
        state[...] = jnp.zeros_like(state)

    everything = slice(None)
    xs = _conv4_silu(x_ref, px_ref, sh_ref, wx_ref, bx_ref, everything, first)
    bm = _conv4_silu(b_ref, pb_ref, sh_ref, wb_ref, bb_ref, everything, first)
    cm = _conv4_silu(c_ref, pc_ref, sh_ref, wc_ref, bc_ref, everything, first)

    pre = dtr_ref[...] + dtb_ref[...]
    dt = jnp.maximum(pre, 0.0) + jnp.log(1.0 + jnp.exp(-jnp.abs(pre)))
    dta = dt * a_ref[...]
    row = lax.broadcasted_iota(jnp.int32, (q, q), 0)
    col = lax.broadcasted_iota(jnp.int32, (q, q), 1)
    causal = row >= col
    acs = jnp.dot(causal.astype(F32), dta, precision=lax.Precision.HIGHEST, preferred_element_type=F32)
    acs_t = acs.T
    dt_t = dt.T
    eacs = jnp.exp(acs)
    wdec = dt * jnp.exp(acs[q - 1:q, :] - acs)

    def split(v):
        hi = v.astype(BF16)
        return hi, (v - hi.astype(F32)).astype(BF16)

    parts = jnp.concatenate(split(eacs) + split(wdec), axis=0)
    wide = jnp.dot(parts, e_ref[...], preferred_element_type=F32)
    eacs_full = wide[0:q] + wide[q:2 * q]
    xd = (xs * (wide[2 * q:3 * q] + wide[3 * q:4 * q])).astype(BF16)

    xs_b = xs.astype(BF16)
    lane = lax.broadcasted_iota(jnp.int32, (1, xs.shape[1]), 1)
    low_head = (lane % LANES) < SSM_HEAD_DIM
    xs_lo = jnp.where(low_head, xs_b, jnp.zeros_like(xs_b))
    xs_hi = jnp.where(low_head, jnp.zeros_like(xs_b), xs_b)

    gw = SSM_HEAD_DIM * (SSM_HEADS // SSM_GROUPS)
    for g in range(SSM_GROUPS):
        ns = slice(g * SSM_STATE, (g + 1) * SSM_STATE)
        gs = slice(g * gw, (g + 1) * gw)
        bg_f = bm[:, ns]
        bg = bg_f.astype(BF16)
        cg = cm[:, ns].astype(BF16)
        cb = lax.dot_general(cg, bg, (((1,), (1,)), ((), ())), preferred_element_type=F32)
        prev = state[g]
        y_off = jnp.dot(cg, prev.astype(BF16), preferred_element_type=F32)
        for c in range(gw // LANES):
            cs = slice(g * gw + c * LANES, g * gw + (c + 1) * LANES)
            acc = y_off[:, c * LANES:(c + 1) * LANES] * eacs_full[:, cs]
            for par, xpart in ((0, xs_lo), (1, xs_hi)):
                h = g * (SSM_HEADS // SSM_GROUPS) + 2 * c + par
                seg = acs[:, h:h + 1] - acs_t[h:h + 1, :]
                mmat = cb * jnp.exp(jnp.where(causal, seg, NEG_INF)) * dt_t[h:h + 1, :]
                acc = acc + jnp.dot(mmat.astype(BF16), xpart[:, cs], preferred_element_type=F32)
            ybuf[:, cs] = acc
        new_states = jnp.dot(bg_f.T.astype(BF16), xd[:, gs], preferred_element_type=F32)
        state[g] = prev * eacs_full[q - 1:q, gs] + new_states

    y = ybuf[...] + xs * dskip_ref[...]
    z = z_ref[...].astype(F32)
    y = y * _silu(z)
    for g in range(SSM_GROUPS):
        gs = slice(g * gw, (g + 1) * gw)
        yg = y[:, gs]
        ms = jnp.mean(yg * yg, axis=-1, keepdims=True)
        o_ref[:, gs] = (yg * lax.rsqrt(ms + NORM_EPS) * ng_ref[:, gs]).astype(o_ref.dtype)


def _ssd(proj, dt_raw, conv_w, conv_b, dt_bias, a_neg, d_skip, norm_g):
    m = proj.shape[0]
    q = SSM_CHUNK
    inner = D_MODEL
    bc = SSM_GROUPS * SSM_STATE
    hb = q // SSM_HALO
    wx, wb, wc = conv_w[:, :inner], conv_w[:, inner:inner + bc], conv_w[:, inner + bc:]
    bx, bb, bcc = conv_b[:, :inner], conv_b[:, inner:inner + bc], conv_b[:, inner + bc:]
    expand = jnp.asarray(np.arange(LANES)[:, None] == (np.arange(inner) // SSM_HEAD_DIM)[None, :], BF16)
    t_idx = np.arange(q)
    shift_np = np.zeros((SSM_CONV_WIDTH - 1, q, SSM_HALO + q), np.float32)
    for k in range(1, SSM_CONV_WIDTH):
        shift_np[k - 1, t_idx, SSM_HALO + t_idx - k] = 1.0
    shift = jnp.asarray(shift_np.reshape(-1, SSM_HALO + q), BF16)
    prev = lambda i: jnp.maximum(i * hb - 1, 0)
    full = lambda shape: pl.BlockSpec(shape, lambda i: (0,) * len(shape))
    return pl.pallas_call(
        _ssd_kernel,
        grid=(m // q,),
        in_specs=[
            pl.BlockSpec((q, inner), lambda i: (i, 2)),
            pl.BlockSpec((q, inner), lambda i: (i, 3)),
            pl.BlockSpec((q, bc), lambda i: (i, 8)),
            pl.BlockSpec((q, bc), lambda i: (i, 9)),
            pl.BlockSpec((SSM_HALO, inner), lambda i: (prev(i), 3)),
            pl.BlockSpec((SSM_HALO, bc), lambda i: (prev(i), 8)),
            pl.BlockSpec((SSM_HALO, bc), lambda i: (prev(i), 9)),
            pl.BlockSpec((q, LANES), lambda i: (i, 0)),
            full(wx.shape), full(wb.shape), full(wc.shape),
            full(bx.shape), full(bb.shape), full(bcc.shape),
            full((1, LANES)), full((1, LANES)), full((1, inner)), full((1, inner)),
            full((LANES, inner)),
            full(shift.shape),
        ],
        out_specs=pl.BlockSpec((q, inner), lambda i: (i, 0)),
        out_shape=jax.ShapeDtypeStruct((m, inner), BF16),
        scratch_shapes=[
            pltpu.VMEM((SSM_GROUPS, SSM_STATE, inner // SSM_GROUPS), F32),
            pltpu.VMEM((q, inner), F32),
        ],
        compiler_params=_params(("arbitrary",)),
        name="ssd",
    )(proj, proj, proj, proj, proj, proj, proj, dt_raw, wx, wb, wc, bx, bb, bcc,
      dt_bias, a_neg, d_skip, norm_g, expand, shift)


def _attn_kernel(sinks_ref, q_ref, kvc_ref, kvp_ref, bias_ref, o_ref):
    blk = ATTN_BLOCK
    kvw = ATTN_KV_HEADS * ATTN_HEAD_DIM
    rep = ATTN_Q_HEADS // ATTN_KV_HEADS
    kv = jnp.concatenate([kvp_ref[...], kvc_ref[...]], axis=0).astype(F32)
    lane = lax.broadcasted_iota(jnp.int32, (1, LANES), 1)
    low = lane < ATTN_HEAD_DIM
    scale = ATTN_HEAD_DIM ** -0.5

    for g in range(ATTN_KV_HEADS):
        pc = g // 2
        kcol = kv[:, pc * LANES:(pc + 1) * LANES] * scale
        vcol = kv[:, kvw + pc * LANES:kvw + (pc + 1) * LANES]
        kroll = pltpu.roll(kcol, ATTN_HEAD_DIM, axis=1)
        vroll = pltpu.roll(vcol, ATTN_HEAD_DIM, axis=1)
        if g % 2 == 0:
            k_lo, k_hi = jnp.where(low, kcol, 0.0), jnp.where(low, 0.0, kroll)
            v_lo, v_hi = jnp.where(low, vcol, 0.0), jnp.where(low, 0.0, vroll)
        else:
            k_lo, k_hi = jnp.where(low, kroll, 0.0), jnp.where(low, 0.0, kcol)
            v_lo, v_hi = jnp.where(low, vroll, 0.0), jnp.where(low, 0.0, vcol)
        k_lo, k_hi, v_lo, v_hi = (t.astype(BF16) for t in (k_lo, k_hi, v_lo, v_hi))

        ncol = rep // 2
        qs = jnp.concatenate([q_ref[:, (g * ncol + c) * LANES:(g * ncol + c + 1) * LANES] for c in range(ncol)], axis=0)
        nt = (((1,), (1,)), ((), ()))
        s_even = lax.dot_general(qs, k_lo, nt, preferred_element_type=F32)
        s_odd = lax.dot_general(qs, k_hi, nt, preferred_element_type=F32)
        for c in range(ncol):
            out = None
            for par, s_all, vv in ((0, s_even, v_lo), (1, s_odd, v_hi)):
                h = g * rep + 2 * c + par
                logits = s_all[c * blk:(c + 1) * blk, :] + bias_ref[0, h]
                sink = sinks_ref[h]
                mx = jnp.maximum(jnp.max(logits, axis=-1, keepdims=True), sink)
                p = jnp.exp(logits - mx)
                denom = jnp.sum(p, axis=-1, keepdims=True) + jnp.exp(sink - mx)
                o = jnp.dot(p.astype(BF16), vv, preferred_element_type=F32) * (1.0 / denom)
                out = o if out is None else out + o
            col = g * ncol + c
            o_ref[:, col * LANES:(col + 1) * LANES] = out.astype(o_ref.dtype)


def _t5_bucket_table():
    dist = np.arange(ATTN_BLOCK)[:, None] + ATTN_BLOCK - np.arange(2 * ATTN_BLOCK)[None, :]
    n = np.maximum(dist, 0)
    max_exact = REL_BUCKETS // 2
    nf = np.maximum(n, 1).astype(np.float32)
    large = max_exact + (np.log(nf / max_exact) / math.log(REL_MAX_DIST / max_exact)
                         * (REL_BUCKETS - max_exact)).astype(np.int32)
    large = np.minimum(large, REL_BUCKETS - 1)
    return dist, np.where(n < max_exact, n, large)


def _attention(qkv, sinks, rel_bias):
    m = qkv.shape[0]
    blk = ATTN_BLOCK
    qd = ATTN_Q_HEADS * ATTN_HEAD_DIM
    kvd = 2 * ATTN_KV_HEADS * ATTN_HEAD_DIM
    dist, bucket = _t5_bucket_table()
    visible = (dist >= 0) & (dist < blk)
    onehot = jnp.asarray(np.arange(REL_BUCKETS)[:, None] == bucket.reshape(1, -1), F32)
    bias = jnp.dot(rel_bias.astype(F32).T, onehot, precision=lax.Precision.HIGHEST).reshape(-1, blk, 2 * blk)
    general = jnp.where(visible[None], bias, NEG_INF)
    first = jnp.where((visible & (np.arange(2 * blk) >= blk)[None, :])[None], bias, NEG_INF)
    table = jnp.stack([first, general])
    return pl.pallas_call(
        _attn_kernel,
        grid=(m // blk,),
        in_specs=[
            pl.BlockSpec(memory_space=pltpu.SMEM),
            pl.BlockSpec((blk, qd), lambda n: (n, 0)),
            pl.BlockSpec((blk, kvd), lambda n: (n, qd // kvd)),
            pl.BlockSpec((blk, kvd), lambda n: (jnp.maximum(n - 1, 0), qd // kvd)),
            pl.BlockSpec((1,) + table.shape[1:], lambda n: (jnp.minimum(n, 1), 0, 0, 0)),
        ],
        out_specs=pl.BlockSpec((blk, qd), lambda n: (n, 0)),
        out_shape=jax.ShapeDtypeStruct((m, qd), BF16),
        compiler_params=_params(("arbitrary",)),
        name="swa_attention",
    )(sinks.astype(F32), qkv, qkv, qkv, table)


def _router_kernel(x_ref, g_ref, rw_ref, rb_ref, o_ref):
    xn = _rms_rows(x_ref[...], g_ref[...])
    x_hi = xn.astype(BF16)
    x_lo = (xn - x_hi.astype(F32)).astype(BF16)
    both = jnp.dot(x_hi, rw_ref[...], preferred_element_type=F32)
    logits = (both[:, :LANES] + both[:, LANES:]
              + jnp.dot(x_lo, rw_ref[:, :LANES], preferred_element_type=F32) + rb_ref[...])
    lane = lax.broadcasted_iota(jnp.int32, logits.shape, 1)
    m1 = jnp.max(logits, axis=-1, keepdims=True)
    i1 = jnp.min(jnp.where(logits == m1, lane, LANES), axis=-1, keepdims=True)
    rest = jnp.where(lane == i1, NEG_INF, logits)
    m2 = jnp.max(rest, axis=-1, keepdims=True)
    i2 = jnp.min(jnp.where(rest == m2, lane, LANES), axis=-1, keepdims=True)
    e = jnp.exp(m2 - m1)
    g1 = 1.0 / (1.0 + e)
    g2 = e * g1
    out = jnp.where(lane == 0, i1.astype(F32),
                    jnp.where(lane == 1, i2.astype(F32),
                              jnp.where(lane == 2, g1, jnp.where(lane == 3, g2, 0.0))))
    o_ref[...] = out


def _router(h, g, rw, rb, *, tm):
    m, d = h.shape
    rw_pad = jnp.zeros((d, LANES), F32).at[:, :N_EXPERTS].set(rw.astype(F32))
    rw_hi = rw_pad.astype(BF16)
    rw_pad = jnp.concatenate([rw_hi, (rw_pad - rw_hi.astype(F32)).astype(BF16)], axis=1)
    rb_pad = jnp.full((1, LANES), NEG_INF, F32).at[0, :N_EXPERTS].set(rb.astype(F32))
    return pl.pallas_call(
        _router_kernel,
        grid=(m // tm,),
        in_specs=[
            pl.BlockSpec((tm, d), lambda i: (i, 0)),
            pl.BlockSpec((1, d), lambda i: (0, 0)),
            pl.BlockSpec((d, 2 * LANES), lambda i: (0, 0)),
            pl.BlockSpec((1, LANES), lambda i: (0, 0)),
        ],
        out_specs=pl.BlockSpec((tm, LANES), lambda i: (i, 0)),
        out_shape=jax.ShapeDtypeStruct((m, LANES), F32),
        compiler_params=_params(("parallel",)),
        name="router",
    )(h, g, rw_pad, rb_pad)


def _moe_drain(h_hbm, xbuf, sem):
    for s in range(MOE_SUBS_PER_GROUP):
        rs = slice(s * MOE_SUB, (s + 1) * MOE_SUB)
        pltpu.make_async_copy(h_hbm.at[pl.ds(0, MOE_SUB)], xbuf.at[rs], sem).wait()
    slack = xbuf.shape[0] - MOE_GROUP
    pltpu.make_async_copy(h_hbm.at[pl.ds(0, slack)], xbuf.at[MOE_GROUP:, :], sem).wait()


def _moe_kernel(ge_ref, ns_ref, tok_ref, h_hbm, g_ref, wg_ref, wu_ref, wd_ref, o_ref,
                xbuf, xn_ref, hid_ref, sem, *, rps, tf):
    grp = pl.program_id(0)
    j = pl.program_id(1)
    ns = ns_ref[grp]
    odd = j % 2 == 1

    def row_copy(gi, r):
        tok = tok_ref[gi * MOE_GROUP + jnp.minimum(r, MOE_GROUP - 1)]
        return pltpu.make_async_copy(h_hbm.at[pl.ds(tok, 1)], xbuf.at[pl.ds(r, 1)], sem)

    def issue_step_rows(gi):
        for u in range(rps):
            row_copy(gi, j * rps + u).start()

    @pl.when(j == 0)
    def _():
        @pl.when(grp == 0)
        def _():
            def issue(r, carry):
                for u in range(MOE_UNROLL):
                    row_copy(0, r * MOE_UNROLL + u).start()
                return carry

            lax.fori_loop(0, xbuf.shape[0] // MOE_UNROLL, issue, 0)

        _moe_drain(h_hbm, xbuf, sem)
        for s in range(MOE_SUBS_PER_GROUP):
            rs = slice(s * MOE_SUB, (s + 1) * MOE_SUB)
            o_ref[rs, :] = jnp.zeros((MOE_SUB, o_ref.shape[1]), F32)

            @pl.when(s < ns)
            def _():
                xn_ref[rs, :] = _rms_rows(xbuf[rs, :], g_ref[...]).astype(BF16)

    def ffn(rs, second):
        xn = xn_ref[rs, :]
        gate = jnp.dot(xn, wg_ref[...].astype(BF16), preferred_element_type=F32)
        up = jnp.dot(xn, wu_ref[...].astype(BF16), preferred_element_type=F32)
        hid = (_silu(gate) * up).astype(BF16)
        if second:
            both = jnp.concatenate([hid_ref[rs, :], hid], axis=1)
            o_ref[rs, :] += jnp.dot(both, wd_ref[...].astype(BF16), preferred_element_type=F32)
        else:
            hid_ref[rs, :] = hid

    for second in (False, True):
        parity = odd if second else jnp.logical_not(odd)

        @pl.when((ns == MOE_SUBS_PER_GROUP) & parity)
        def _():
            issue_step_rows(grp + 1)
            ffn(slice(0, MOE_GROUP), second)

        for s in range(MOE_SUBS_PER_GROUP - 1):
            @pl.when((s < ns) & (ns < MOE_SUBS_PER_GROUP) & parity)
            def _():
                ffn(slice(s * MOE_SUB, (s + 1) * MOE_SUB), second)

    @pl.when(ns < MOE_SUBS_PER_GROUP)
    def _():
        issue_step_rows(grp + 1)

    @pl.when((grp == pl.num_programs(0) - 1) & (j == pl.num_programs(1) - 1))
    def _():
        _moe_drain(h_hbm, xbuf, sem)


def _moe_ffn(h, g, wg, wu, wd, group_e, group_ns, src_tok, *, tf):
    t, d = h.shape
    n_groups = group_e.shape[0]
    f = wg.shape[2]
    nj = f // tf
    rps = -(-MOE_GROUP // (nj * SUBLANES)) * SUBLANES
    assert src_tok.shape[0] == (n_groups + 1) * MOE_GROUP and nj % 2 == 0

    def jj(grp, j, ns):
        return jnp.where(ns[grp] > 0, j, nj - 1)

    return pl.pallas_call(
        functools.partial(_moe_kernel, rps=rps, tf=tf),
        grid_spec=pltpu.PrefetchScalarGridSpec(
            num_scalar_prefetch=3,
            grid=(n_groups, nj),
            in_specs=[
                pl.BlockSpec(memory_space=pl.ANY),
                pl.BlockSpec((1, d), lambda grp, j, ge, ns, tok: (0, 0)),
                pl.BlockSpec((None, d, tf), lambda grp, j, ge, ns, tok: (ge[grp], 0, jj(grp, j, ns))),
                pl.BlockSpec((None, d, tf), lambda grp, j, ge, ns, tok: (ge[grp], 0, jj(grp, j, ns))),
                pl.BlockSpec((None, 2 * tf, d), lambda grp, j, ge, ns, tok: (ge[grp], jj(grp, j, ns) // 2, 0)),
            ],
            out_specs=pl.BlockSpec((MOE_GROUP, d), lambda grp, j, ge, ns, tok: (grp, 0)),
            scratch_shapes=[
                pltpu.VMEM((rps * nj, d), F32),
                pltpu.VMEM((MOE_GROUP, d), BF16),
                pltpu.VMEM((MOE_GROUP, tf), BF16),
                pltpu.SemaphoreType.DMA(()),
            ],
        ),
        out_shape=jax.ShapeDtypeStruct((n_groups * MOE_GROUP, d), F32),
        compiler_params=_params(("arbitrary", "arbitrary")),
        name="moe_ffn",
    )(group_e, group_ns, src_tok, h, g, wg, wu, wd)


def _combine_kernel(pos_ref, h_ref, route_ref, fg_ref, rows_hbm, o_ref, buf_a, buf_b, sem, *, tc):
    i = pl.program_id(0)
    slot = i % 2

    def copies(blk, sl, r):
        pa = pos_ref[TOP_K * (blk * tc + r)]
        pb = pos_ref[TOP_K * (blk * tc + r) + 1]
        return (pltpu.make_async_copy(rows_hbm.at[pl.ds(pa, 1)], buf_a.at[sl, pl.ds(r, 1)], sem.at[sl]),
                pltpu.make_async_copy(rows_hbm.at[pl.ds(pb, 1)], buf_b.at[sl, pl.ds(r, 1)], sem.at[sl]))

    def issue_block(blk, sl):
        def issue(r, carry):
            ca, cb = copies(blk, sl, r)
            ca.start()
            cb.start()
            return carry

        lax.fori_loop(0, tc, issue, 0, unroll=8)

    @pl.when(i == 0)
    def _():
        issue_block(0, 0)

    @pl.when(i + 1 < pl.num_programs(0))
    def _():
        issue_block(i + 1, 1 - slot)

    def wait(r, carry):
        ca, cb = copies(i, slot, r)
        ca.wait()
        cb.wait()
        return carry

    lax.fori_loop(0, tc, wait, 0, unroll=8)
    route = route_ref[...]
    moe = route[:, 2:3] * buf_a[slot] + route[:, 3:4] * buf_b[slot]
    o_ref[...] = _rms_rows(h_ref[...] + moe, fg_ref[...])


def _combine(h, route, final_g, rows, pos, *, tc):
    t, d = h.shape
    return pl.pallas_call(
        functools.partial(_combine_kernel, tc=tc),
        grid_spec=pltpu.PrefetchScalarGridSpec(
            num_scalar_prefetch=1,
            grid=(t // tc,),
            in_specs=[
                pl.BlockSpec((tc, d), lambda i, pos: (i, 0)),
                pl.BlockSpec((tc, LANES), lambda i, pos: (i, 0)),
                pl.BlockSpec((1, d), lambda i, pos: (0, 0)),
                pl.BlockSpec(memory_space=pl.ANY),
            ],
            out_specs=pl.BlockSpec((tc, d), lambda i, pos: (i, 0)),
            scratch_shapes=[pltpu.VMEM((2, tc, d), F32), pltpu.VMEM((2, tc, d), F32), pltpu.SemaphoreType.DMA((2,))],
        ),
        out_shape=jax.ShapeDtypeStruct((t, d), F32),
        compiler_params=_params(("arbitrary",)),
        name="moe_combine",
    )(pos, h, route, final_g, rows)


def _moe_plan(route, n_tok):
    flat_e = route[:, :TOP_K].astype(jnp.int32).reshape(-1)
    onehot = (flat_e[:, None] == jnp.arange(N_EXPERTS, dtype=jnp.int32)[None, :]).astype(jnp.int32)
    csum = jnp.cumsum(onehot, axis=0)
    rank = jnp.sum(onehot * (csum - onehot), axis=1)
    counts = csum[-1]
    n_groups = (n_tok * TOP_K) // MOE_GROUP + N_EXPERTS
    groups_e = (counts + MOE_GROUP - 1) // MOE_GROUP
    group_end = jnp.cumsum(groups_e)
    group_start = group_end - groups_e
    pos = jnp.sum(onehot * (group_start * MOE_GROUP)[None, :], axis=1) + rank
    flat_tok = jnp.arange(n_tok * TOP_K, dtype=jnp.int32) // TOP_K
    src_tok = jnp.zeros(((n_groups + 1) * MOE_GROUP,), jnp.int32).at[pos].set(
        flat_tok, unique_indices=True, mode="promise_in_bounds")
    experts = jnp.arange(N_EXPERTS, dtype=jnp.int32)
    gidx = jnp.arange(n_groups, dtype=jnp.int32)
    used = gidx < group_end[-1]
    last_e = jnp.max(jnp.where(groups_e > 0, experts, 0))
    ge = jnp.sum((group_end[None, :] <= gidx[:, None]).astype(jnp.int32), axis=1)
    ge = jnp.where(used, ge, last_e).astype(jnp.int32)
    mine = (ge[:, None] == experts[None, :]).astype(jnp.int32)
    subs_e = (counts + MOE_SUB - 1) // MOE_SUB
    subs_g = jnp.sum(mine * subs_e[None, :], axis=1)
    start_g = jnp.sum(mine * group_start[None, :], axis=1)
    ns = jnp.clip(subs_g - MOE_SUBS_PER_GROUP * (gidx - start_g), 0, MOE_SUBS_PER_GROUP)
    ns = jnp.where(used, ns, 0).astype(jnp.int32)
    return ge, ns, src_tok, pos.astype(jnp.int32)


def kernel(x, mix_norm_g, ffn_norm_g, final_norm_g, w_in, conv_w, conv_b, conv_ln_g, conv_ln_b, ssm_conv_w, ssm_conv_b, dt_bias, a_log, d_skip, ssm_norm_g, w_out, ffn_w_gate, ffn_w_up, ffn_w_down, w_qkv, b_qkv, w_o, b_o, sinks, rel_bias, router_w, router_b, moe_w_gate, moe_w_up, moe_w_down):
    bsz, seq, d = x.shape
    assert bsz == 1 and d == D_MODEL and seq % 512 == 0
    m = seq
    h = x.reshape(m, d)
    row = lambda v: v.reshape(1, -1).astype(F32)
    tm = min(1024, m)

    main_w = 2 * D_MODEL + D_MODEL + (D_MODEL + 2 * SSM_GROUPS * SSM_STATE)
    w_main = w_in[0].astype(BF16)
    w_dt = jnp.zeros((d, LANES), F32).at[:, :SSM_HEADS].set(w_in[0][:, main_w:]).astype(BF16)
    g0 = row(mix_norm_g[0])
    proj, dt_raw = _in_proj(h, g0, w_main, w_dt, tm=tm, tn=1024, n=main_w)

    conv_out = _conv_module(proj, conv_w[0].astype(F32), row(conv_b[0]), row(conv_ln_g[0]), row(conv_ln_b[0]), tt=128)

    pad_heads = lambda v: jnp.zeros((1, LANES), F32).at[0, :SSM_HEADS].set(v.astype(F32))
    ssm_out = _ssd(proj, dt_raw, ssm_conv_w[0].astype(F32), row(ssm_conv_b[0]),
                   pad_heads(dt_bias[0]), pad_heads(-jnp.exp(a_log[0].astype(F32))),
                   row(jnp.repeat(d_skip[0].astype(F32), SSM_HEAD_DIM)), row(ssm_norm_g[0]))

    wo = w_out[0].astype(BF16)
    h = _matmul_residual([conv_out, ssm_out], [wo[:D_MODEL], wo[D_MODEL:]], h, jnp.zeros((1, d), F32),
                         tm=512, tn=1024, name="out_proj")
    h = _swiglu(h, row(ffn_norm_g[0]), ffn_w_gate[0].astype(BF16), ffn_w_up[0].astype(BF16),
                ffn_w_down[0].astype(BF16), tm=512, tf=512)

    qkv = _norm_matmul(h, row(mix_norm_g[1]), w_qkv[0].astype(BF16), row(b_qkv[0]), tm=tm, tn=1280,
                       out_dtype=BF16, name="qkv_proj")
    attn = _attention(qkv, sinks[0], rel_bias)
    h = _matmul_residual([attn], [w_o[0].astype(BF16)], h, row(b_o[0]), tm=512, tn=1024, name="attn_out_proj")

    g1 = row(ffn_norm_g[1])
    route = _router(h, g1, router_w[0], router_b[0], tm=512)
    ge, ns, src_tok, pos = _moe_plan(route, m)
    rows = _moe_ffn(h, g1, moe_w_gate[0], moe_w_up[0], moe_w_down[0], ge, ns, src_tok, tf=256)
    out = _combine(h, route, row(final_norm_g), rows, pos, tc=256)
    return out.reshape(bsz, seq, d)
```

```python
import functools
import math

import numpy as np
import jax
import jax.numpy as jnp
from jax import lax
from jax.experimental import pallas as pl
from jax.experimental.pallas import tpu as pltpu

F32 = jnp.float32
BF16 = jnp.bfloat16
NORM_EPS = 1e-5
NEG_INF = float("-inf")

LANES = 128
SUBLANES = 8
VMEM_LIMIT = 56 << 20

D_MODEL = 2048
CONV_WIDTH = 31
SSM_HEADS = 32
SSM_HEAD_DIM = 64
SSM_GROUPS = 8
SSM_STATE = 128
SSM_CONV_WIDTH = 4
SSM_CHUNK = 128
ATTN_Q_HEADS = 32
ATTN_KV_HEADS = 4
ATTN_HEAD_DIM = 64
ATTN_BLOCK = 128
REL_BUCKETS = 32
REL_MAX_DIST = 128
N_EXPERTS = 8
TOP_K = 2
MOE_SUB = 256
MOE_SUBS_PER_GROUP = 4
MOE_GROUP = MOE_SUB * MOE_SUBS_PER_GROUP
MOE_UNROLL = 8


def _params(sem):
    return pltpu.CompilerParams(dimension_semantics=sem, vmem_limit_bytes=VMEM_LIMIT)


def _sigmoid(x):
    return 0.5 + 0.5 * jnp.tanh(0.5 * x)


def _silu(x):
    half = 0.5 * x
    return half + half * jnp.tanh(half)


def _rms_rows(x, g):
    ms = jnp.mean(x * x, axis=-1, keepdims=True)
    return x * lax.rsqrt(ms + NORM_EPS) * g


def _norm_matmul_kernel(x_ref, g_ref, w_ref, b_ref, o_ref, xn_ref):
    @pl.when(pl.program_id(1) == 0)
    def _():
        xn_ref[...] = _rms_rows(x_ref[...], g_ref[...]).astype(BF16)

    acc = jnp.dot(xn_ref[...], w_ref[...], preferred_element_type=F32)
    o_ref[...] = (acc + b_ref[...]).astype(o_ref.dtype)


def _norm_matmul(x, g, w, b, *, tm, tn, out_dtype, name):
    m, k = x.shape
    n = w.shape[1]
    return pl.pallas_call(
        _norm_matmul_kernel,
        grid=(m // tm, n // tn),
        in_specs=[
            pl.BlockSpec((tm, k), lambda i, j: (i, 0)),
            pl.BlockSpec((1, k), lambda i, j: (0, 0)),
            pl.BlockSpec((k, tn), lambda i, j: (0, j)),
            pl.BlockSpec((1, tn), lambda i, j: (0, j)),
        ],
        out_specs=pl.BlockSpec((tm, tn), lambda i, j: (i, j)),
        out_shape=jax.ShapeDtypeStruct((m, n), out_dtype),
        scratch_shapes=[pltpu.VMEM((tm, k), BF16)],
        compiler_params=_params(("parallel", "arbitrary")),
        name=name,
    )(x, g, w, b)


def _in_proj_kernel(x_ref, g_ref, w_ref, wdt_ref, o_ref, odt_ref, xn_ref):
    @pl.when(pl.program_id(1) == 0)
    def _():
        xn_ref[...] = _rms_rows(x_ref[...], g_ref[...]).astype(BF16)
        odt_ref[...] = jnp.dot(xn_ref[...], wdt_ref[...], preferred_element_type=F32)

    o_ref[...] = jnp.dot(xn_ref[...], w_ref[...], preferred_element_type=F32).astype(o_ref.dtype)


def _in_proj(x, g, w, wdt, *, tm, tn, n):
    m, k = x.shape
    assert n % tn == 0 and n <= w.shape[1]
    ndt = wdt.shape[1]
    return pl.pallas_call(
        _in_proj_kernel,
        grid=(m // tm, n // tn),
        in_specs=[
            pl.BlockSpec((tm, k), lambda i, j: (i, 0)),
            pl.BlockSpec((1, k), lambda i, j: (0, 0)),
            pl.BlockSpec((k, tn), lambda i, j: (0, j)),
            pl.BlockSpec((k, ndt), lambda i, j: (0, 0)),
        ],
        out_specs=[pl.BlockSpec((tm, tn), lambda i, j: (i, j)), pl.BlockSpec((tm, ndt), lambda i, j: (i, 0))],
        out_shape=[jax.ShapeDtypeStruct((m, n), BF16), jax.ShapeDtypeStruct((m, ndt), F32)],
        scratch_shapes=[pltpu.VMEM((tm, k), BF16)],
        compiler_params=_params(("parallel", "arbitrary")),
        name="in_proj",
    )(x, g, w, wdt)


def _matmul_residual_kernel(*refs, n_in):
    a_refs, w_refs = refs[:n_in], refs[n_in:2 * n_in]
    res_ref, b_ref, o_ref = refs[2 * n_in:]
    acc = res_ref[...] + b_ref[...]
    for a_ref, w_ref in zip(a_refs, w_refs):
        acc = acc + jnp.dot(a_ref[...], w_ref[...], preferred_element_type=F32)
    o_ref[...] = acc


def _matmul_residual(a_list, w_list, res, b, *, tm, tn, name):
    m, n = res.shape
    n_in = len(a_list)
    in_specs = [pl.BlockSpec((tm, a.shape[1]), lambda j, i: (i, 0)) for a in a_list]
    in_specs += [pl.BlockSpec((w.shape[0], tn), lambda j, i: (0, j)) for w in w_list]
    in_specs += [pl.BlockSpec((tm, tn), lambda j, i: (i, j)), pl.BlockSpec((1, tn), lambda j, i: (0, j))]
    return pl.pallas_call(
        functools.partial(_matmul_residual_kernel, n_in=n_in),
        grid=(n // tn, m // tm),
        in_specs=in_specs,
        out_specs=pl.BlockSpec((tm, tn), lambda j, i: (i, j)),
        out_shape=jax.ShapeDtypeStruct((m, n), F32),
        compiler_params=_params(("parallel", "parallel")),
        name=name,
    )(*a_list, *w_list, res, b)


def _swiglu_kernel(x_ref, g_ref, wg_ref, wu_ref, wd_ref, o_ref, xn_ref):
    @pl.when(pl.program_id(1) == 0)
    def _():
        x = x_ref[...]
        xn_ref[...] = _rms_rows(x, g_ref[...]).astype(BF16)
        o_ref[...] = x

    xn = xn_ref[...]
    gate = jnp.dot(xn, wg_ref[...], preferred_element_type=F32)
    up = jnp.dot(xn, wu_ref[...], preferred_element_type=F32)
    hid = (_silu(gate) * up).astype(BF16)
    o_ref[...] += jnp.dot(hid, wd_ref[...], preferred_element_type=F32)


def _swiglu(x, g, wg, wu, wd, *, tm, tf):
    m, d = x.shape
    f = wg.shape[1]
    return pl.pallas_call(
        _swiglu_kernel,
        grid=(m // tm, f // tf),
        in_specs=[
            pl.BlockSpec((tm, d), lambda i, j: (i, 0)),
            pl.BlockSpec((1, d), lambda i, j: (0, 0)),
            pl.BlockSpec((d, tf), lambda i, j: (0, j)),
            pl.BlockSpec((d, tf), lambda i, j: (0, j)),
            pl.BlockSpec((tf, d), lambda i, j: (j, 0)),
        ],
        out_specs=pl.BlockSpec((tm, d), lambda i, j: (i, 0)),
        out_shape=jax.ShapeDtypeStruct((m, d), F32),
        scratch_shapes=[pltpu.VMEM((tm, d), BF16)],
        compiler_params=_params(("parallel", "arbitrary")),
        name="swiglu",
    )(x, g, wg, wu, wd)


CONV_HALO = 32
CONV_QPAD = 16


def _conv_module_kernel(val_ref, gate_ref, pval_ref, pgate_ref, sel_ref, w_ref, b_ref, lg_ref, lb_ref, o_ref,
                        ubuf, qbuf, *, tt):
    i = pl.program_id(0)

    def glu(v, g):
        return v.astype(F32) * _sigmoid(g.astype(F32))

    qrows = tt + CONV_QPAD
    ubuf[0:CONV_HALO, :] = jnp.where(i > 0, glu(pval_ref[...], pgate_ref[...]), 0.0)
    ubuf[CONV_HALO:CONV_HALO + tt, :] = glu(val_ref[...], gate_ref[...])
    ubuf[CONV_HALO + tt:, :] = jnp.zeros((CONV_QPAD, ubuf.shape[1]), F32)

    off = CONV_HALO - (CONV_WIDTH - 1)
    taps = [[] for _ in range(SUBLANES)]
    for j in range(CONV_WIDTH):
        a, r = divmod(off + j, SUBLANES)
        taps[r].append((a, j))
    n_ch = ubuf.shape[1]
    for cb in range(n_ch // LANES):
        ls = slice(cb * LANES, (cb + 1) * LANES)
        big = ubuf[:, ls]
        for r in range(SUBLANES):
            q = None
            for a, j in taps[r]:
                term = w_ref[j:j + 1, ls] * big[a * SUBLANES:a * SUBLANES + qrows, :]
                q = term if q is None else q + term
            qbuf[r * qrows:(r + 1) * qrows, ls] = q.astype(BF16)

    c = jnp.dot(sel_ref[...], qbuf[...], preferred_element_type=F32) + b_ref[...]
    mu = jnp.mean(c, axis=-1, keepdims=True)
    d = c - mu
    var = jnp.mean(d * d, axis=-1, keepdims=True)
    y = d * lax.rsqrt(var + NORM_EPS) * lg_ref[...] + lb_ref[...]
    o_ref[...] = _silu(y).astype(o_ref.dtype)


def _conv_module(proj, conv_w, conv_b, ln_g, ln_b, *, tt):
    m = proj.shape[0]
    c = D_MODEL
    hb = tt // CONV_HALO
    qrows = tt + CONV_QPAD
    t_idx = np.arange(tt)
    sel_np = np.zeros((tt, SUBLANES * qrows), np.float32)
    for r in range(SUBLANES):
        sel_np[t_idx, r * qrows + t_idx + r] = 1.0
    sel = jnp.asarray(sel_np, BF16)
    return pl.pallas_call(
        functools.partial(_conv_module_kernel, tt=tt),
        grid=(m // tt,),
        in_specs=[
            pl.BlockSpec((tt, c), lambda i: (i, 0)),
            pl.BlockSpec((tt, c), lambda i: (i, 1)),
            pl.BlockSpec((CONV_HALO, c), lambda i: (jnp.maximum(i * hb - 1, 0), 0)),
            pl.BlockSpec((CONV_HALO, c), lambda i: (jnp.maximum(i * hb - 1, 0), 1)),
            pl.BlockSpec(sel.shape, lambda i: (0, 0)),
            pl.BlockSpec((CONV_WIDTH, c), lambda i: (0, 0)),
            pl.BlockSpec((1, c), lambda i: (0, 0)),
            pl.BlockSpec((1, c), lambda i: (0, 0)),
            pl.BlockSpec((1, c), lambda i: (0, 0)),
        ],
        out_specs=pl.BlockSpec((tt, c), lambda i: (i, 0)),
        out_shape=jax.ShapeDtypeStruct((m, c), BF16),
        scratch_shapes=[pltpu.VMEM((CONV_HALO + tt + CONV_QPAD, c), F32), pltpu.VMEM((SUBLANES * qrows, c), BF16)],
        compiler_params=_params(("arbitrary",)),
        name="conv_module",
    )(proj, proj, proj, proj, sel, conv_w, conv_b, ln_g, ln_b)


SSM_HALO = 16


def _conv4_silu(cur_ref, prev_ref, shift_ref, w_ref, b_ref, cols, first):
    q = cur_ref.shape[0]
    cur = cur_ref[:, cols]
    prev = prev_ref[:, cols]
    xb = jnp.concatenate([jnp.where(first, jnp.zeros_like(prev), prev), cur], axis=0)
    back = jnp.dot(shift_ref[...], xb, preferred_element_type=F32)
    last = SSM_CONV_WIDTH - 1
    acc = b_ref[:, cols] + w_ref[last:last + 1, cols] * cur.astype(F32)
    for k in range(1, SSM_CONV_WIDTH):
        acc = acc + w_ref[last - k:last - k + 1, cols] * back[(k - 1) * q:k * q, :]
    return _silu(acc)


def _ssd_kernel(z_ref, x_ref, b_ref, c_ref, px_ref, pb_ref, pc_ref, dtr_ref,
                wx_ref, wb_ref, wc_ref, bx_ref, bb_ref, bc_ref,
                dtb_ref, a_ref, dskip_ref, ng_ref, e_ref, sh_ref, o_ref,
                state, ybuf):
    i = pl.program_id(0)
    q = SSM_CHUNK
    first = i == 0

    @pl.when(first)
    def _():
        state[...] = jnp.zeros_like(state)

    everything = slice(None)
    xs = _conv4_silu(x_ref, px_ref, sh_ref, wx_ref, bx_ref, everything, first)
    bm = _conv4_silu(b_ref, pb_ref, sh_ref, wb_ref, bb_ref, everything, first)
    cm = _conv4_silu(c_ref, pc_ref, sh_ref, wc_ref, bc_ref, everything, first)

    pre = dtr_ref[...] + dtb_ref[...]
    dt = jnp.maximum(pre, 0.0) + jnp.log(1.0 + jnp.exp(-jnp.abs(pre)))
    dta = dt * a_ref[...]
    row = lax.broadcasted_iota(jnp.int32, (q, q), 0)
    col = lax.broadcasted_iota(jnp.int32, (q, q), 1)
    causal = row >= col
    acs = jnp.dot(causal.astype(F32), dta, precision=lax.Precision.HIGHEST, preferred_element_type=F32)
    acs_t = acs.T
    dt_t = dt.T
    eacs = jnp.exp(acs)
    wdec = dt * jnp.exp(acs[q - 1:q, :] - acs)

    def split(v):
        hi = v.astype(BF16)
        return hi, (v - hi.astype(F32)).astype(BF16)

    parts = jnp.concatenate(split(eacs) + split(wdec), axis=0)
    wide = jnp.dot(parts, e_ref[...], preferred_element_type=F32)
    eacs_full = wide[0:q] + wide[q:2 * q]
    xd = (xs * (wide[2 * q:3 * q] + wide[3 * q:4 * q])).astype(BF16)

    xs_b = xs.astype(BF16)
    lane = lax.broadcasted_iota(jnp.int32, (1, xs.shape[1]), 1)
    low_head = (lane % LANES) < SSM_HEAD_DIM
    xs_lo = jnp.where(low_head, xs_b, jnp.zeros_like(xs_b))
    xs_hi = jnp.where(low_head, jnp.zeros_like(xs_b), xs_b)

    gw = SSM_HEAD_DIM * (SSM_HEADS // SSM_GROUPS)
    for g in range(SSM_GROUPS):
        ns = slice(g * SSM_STATE, (g + 1) * SSM_STATE)
        gs = slice(g * gw, (g + 1) * gw)
        bg_f = bm[:, ns]
        bg = bg_f.astype(BF16)
        cg = cm[:, ns].astype(BF16)
        cb = lax.dot_general(cg, bg, (((1,), (1,)), ((), ())), preferred_element_type=F32)
        prev = state[g]
        y_off = jnp.dot(cg, prev.astype(BF16), preferred_element_type=F32)
        for c in range(gw // LANES):
            cs = slice(g * gw + c * LANES, g * gw + (c + 1) * LANES)
            acc = y_off[:, c * LANES:(c + 1) * LANES] * eacs_full[:, cs]
            for par, xpart in ((0, xs_lo), (1, xs_hi)):
                h = g * (SSM_HEADS // SSM_GROUPS) + 2 * c + par
                seg = acs[:, h:h + 1] - acs_t[h:h + 1, :]
                mmat = cb * jnp.exp(jnp.where(causal, seg, NEG_INF)) * dt_t[h:h + 1, :]
                acc = acc + jnp.dot(mmat.astype(BF16), xpart[:, cs], preferred_element_type=F32)
            ybuf[:, cs] = acc
        new_states = jnp.dot(bg_f.T.astype(BF16), xd[:, gs], preferred_element_type=F32)
        state[g] = prev * eacs_full[q - 1:q, gs] + new_states

    y = ybuf[...] + xs * dskip_ref[...]
    z = z_ref[...].astype(F32)
    y = y * _silu(z)
    for g in range(SSM_GROUPS):
        gs = slice(g * gw, (g + 1) * gw)
        yg = y[:, gs]
        ms = jnp.mean(yg * yg, axis=-1, keepdims=True)
        o_ref[:, gs] = (yg * lax.rsqrt(ms + NORM_EPS) * ng_ref[:, gs]).astype(o_ref.dtype)


def _ssd(proj, dt_raw, conv_w, conv_b, dt_bias, a_neg, d_skip, norm_g):
    m = proj.shape[0]
    q = SSM_CHUNK
    inner = D_MODEL
    bc = SSM_GROUPS * SSM_STATE
    hb = q // SSM_HALO
    wx, wb, wc = conv_w[:, :inner], conv_w[:, inner:inner + bc], conv_w[:, inner + bc:]
    bx, bb, bcc = conv_b[:, :inner], conv_b[:, inner:inner + bc], conv_b[:, inner + bc:]
    expand = jnp.asarray(np.arange(LANES)[:, None] == (np.arange(inner) // SSM_HEAD_DIM)[None, :], BF16)
    t_idx = np.arange(q)
    shift_np = np.zeros((SSM_CONV_WIDTH - 1, q, SSM_HALO + q), np.float32)
    for k in range(1, SSM_CONV_WIDTH):
        shift_np[k - 1, t_idx, SSM_HALO + t_idx - k] = 1.0
    shift = jnp.asarray(shift_np.reshape(-1, SSM_HALO + q), BF16)
    prev = lambda i: jnp.maximum(i * hb - 1, 0)
    full = lambda shape: pl.BlockSpec(shape, lambda i: (0,) * len(shape))
    return pl.pallas_call(
        _ssd_kernel,
        grid=(m // q,),
        in_specs=[
            pl.BlockSpec((q, inner), lambda i: (i, 2)),
            pl.BlockSpec((q, inner), lambda i: (i, 3)),
            pl.BlockSpec((q, bc), lambda i: (i, 8)),
            pl.BlockSpec((q, bc), lambda i: (i, 9)),
            pl.BlockSpec((SSM_HALO, inner), lambda i: (prev(i), 3)),
            pl.BlockSpec((SSM_HALO, bc), lambda i: (prev(i), 8)),
            pl.BlockSpec((SSM_HALO, bc), lambda i: (prev(i), 9)),
            pl.BlockSpec((q, LANES), lambda i: (i, 0)),
            full(wx.shape), full(wb.shape), full(wc.shape),
            full(bx.shape), full(bb.shape), full(bcc.shape),
            full((1, LANES)), full((1, LANES)), full((1, inner)), full((1, inner)),
            full((LANES, inner)),
            full(shift.shape),
        ],
        out_specs=pl.BlockSpec((q, inner), lambda i: (i, 0)),
        out_shape=jax.ShapeDtypeStruct((m, inner), BF16),
        scratch_shapes=[
            pltpu.VMEM((SSM_GROUPS, SSM_STATE, inner // SSM_GROUPS), F32),
            pltpu.VMEM((q, inner), F32),
        ],
        compiler_params=_params(("arbitrary",)),
        name="ssd",
    )(proj, proj, proj, proj, proj, proj, proj, dt_raw, wx, wb, wc, bx, bb, bcc,
      dt_bias, a_neg, d_skip, norm_g, expand, shift)


def _attn_kernel(sinks_ref, q_ref, kvc_ref, kvp_ref, bias_ref, o_ref):
    blk = ATTN_BLOCK
    kvw = ATTN_KV_HEADS * ATTN_HEAD_DIM
    rep = ATTN_Q_HEADS // ATTN_KV_HEADS
    kv = jnp.concatenate([kvp_ref[...], kvc_ref[...]], axis=0).astype(F32)
    lane = lax.broadcasted_iota(jnp.int32, (1, LANES), 1)
    low = lane < ATTN_HEAD_DIM
    scale = ATTN_HEAD_DIM ** -0.5

    for g in range(ATTN_KV_HEADS):
        pc = g // 2
        kcol = kv[:, pc * LANES:(pc + 1) * LANES] * scale
        vcol = kv[:, kvw + pc * LANES:kvw + (pc + 1) * LANES]
        kroll = pltpu.roll(kcol, ATTN_HEAD_DIM, axis=1)
        vroll = pltpu.roll(vcol, ATTN_HEAD_DIM, axis=1)
        if g % 2 == 0:
            k_lo, k_hi = jnp.where(low, kcol, 0.0), jnp.where(low, 0.0, kroll)
            v_lo, v_hi = jnp.where(low, vcol, 0.0), jnp.where(low, 0.0, vroll)
        else:
            k_lo, k_hi = jnp.where(low, kroll, 0.0), jnp.where(low, 0.0, kcol)
            v_lo, v_hi = jnp.where(low, vroll, 0.0), jnp.where(low, 0.0, vcol)
        k_lo, k_hi, v_lo, v_hi = (t.astype(BF16) for t in (k_lo, k_hi, v_lo, v_hi))

        ncol = rep // 2
        qs = jnp.concatenate([q_ref[:, (g * ncol + c) * LANES:(g * ncol + c + 1) * LANES] for c in range(ncol)], axis=0)
        nt = (((1,), (1,)), ((), ()))
        s_even = lax.dot_general(qs, k_lo, nt, preferred_element_type=F32)
        s_odd = lax.dot_general(qs, k_hi, nt, preferred_element_type=F32)
        for c in range(ncol):
            out = None
            for par, s_all, vv in ((0, s_even, v_lo), (1, s_odd, v_hi)):
                h = g * rep + 2 * c + par
                logits = s_all[c * blk:(c + 1) * blk, :] + bias_ref[0, h]
                sink = sinks_ref[h]
                mx = jnp.maximum(jnp.max(logits, axis=-1, keepdims=True), sink)
                p = jnp.exp(logits - mx)
                denom = jnp.sum(p, axis=-1, keepdims=True) + jnp.exp(sink - mx)
                o = jnp.dot(p.astype(BF16), vv, preferred_element_type=F32) * (1.0 / denom)
                out = o if out is None else out + o
            col = g * ncol + c
            o_ref[:, col * LANES:(col + 1) * LANES] = out.astype(o_ref.dtype)


def _t5_bucket_table():
    dist = np.arange(ATTN_BLOCK)[:, None] + ATTN_BLOCK - np.arange(2 * ATTN_BLOCK)[None, :]
    n = np.maximum(dist, 0)
    max_exact = REL_BUCKETS // 2
    nf = np.maximum(n, 1).astype(np.float32)
    large = max_exact + (np.log(nf / max_exact) / math.log(REL_MAX_DIST / max_exact)
                         * (REL_BUCKETS - max_exact)).astype(np.int32)
    large = np.minimum(large, REL_BUCKETS - 1)
    return dist, np.where(n < max_exact, n, large)


def _attention(qkv, sinks, rel_bias):
    m = qkv.shape[0]
    blk = ATTN_BLOCK
    qd = ATTN_Q_HEADS * ATTN_HEAD_DIM
    kvd = 2 * ATTN_KV_HEADS * ATTN_HEAD_DIM
    dist, bucket = _t5_bucket_table()
    visible = (dist >= 0) & (dist < blk)
    onehot = jnp.asarray(np.arange(REL_BUCKETS)[:, None] == bucket.reshape(1, -1), F32)
    bias = jnp.dot(rel_bias.astype(F32).T, onehot, precision=lax.Precision.HIGHEST).reshape(-1, blk, 2 * blk)
    general = jnp.where(visible[None], bias, NEG_INF)
    first = jnp.where((visible & (np.arange(2 * blk) >= blk)[None, :])[None], bias, NEG_INF)
    table = jnp.stack([first, general])
    return pl.pallas_call(
        _attn_kernel,
        grid=(m // blk,),
        in_specs=[
            pl.BlockSpec(memory_space=pltpu.SMEM),
            pl.BlockSpec((blk, qd), lambda n: (n, 0)),
            pl.BlockSpec((blk, kvd), lambda n: (n, qd // kvd)),
            pl.BlockSpec((blk, kvd), lambda n: (jnp.maximum(n - 1, 0), qd // kvd)),
            pl.BlockSpec((1,) + table.shape[1:], lambda n: (jnp.minimum(n, 1), 0, 0, 0)),
        ],
        out_specs=pl.BlockSpec((blk, qd), lambda n: (n, 0)),
        out_shape=jax.ShapeDtypeStruct((m, qd), BF16),
        compiler_params=_params(("arbitrary",)),
        name="swa_attention",
    )(sinks.astype(F32), qkv, qkv, qkv, table)


def _router_kernel(x_ref, g_ref, rw_ref, rb_ref, o_ref):
    xn = _rms_rows(x_ref[...], g_ref[...])
    x_hi = xn.astype(BF16)
    x_lo = (xn - x_hi.astype(F32)).astype(BF16)
    both = jnp.dot(x_hi, rw_ref[...], preferred_element_type=F32)
    logits = (both[:, :LANES] + both[:, LANES:]
              + jnp.dot(x_lo, rw_ref[:, :LANES], preferred_element_type=F32) + rb_ref[...])
    lane = lax.broadcasted_iota(jnp.int32, logits.shape, 1)
    m1 = jnp.max(logits, axis=-1, keepdims=True)
    i1 = jnp.min(jnp.where(logits == m1, lane, LANES), axis=-1, keepdims=True)
    rest = jnp.where(lane == i1, NEG_INF, logits)
    m2 = jnp.max(rest, axis=-1, keepdims=True)
    i2 = jnp.min(jnp.where(rest == m2, lane, LANES), axis=-1, keepdims=True)
    e = jnp.exp(m2 - m1)
    g1 = 1.0 / (1.0 + e)
    g2 = e * g1
    out = jnp.where(lane == 0, i1.astype(F32),
                    jnp.where(lane == 1, i2.astype(F32),
                              jnp.where(lane == 2, g1, jnp.where(lane == 3, g2, 0.0))))
    o_ref[...] = out


def _router(h, g, rw, rb, *, tm):
    m, d = h.shape
    rw_pad = jnp.zeros((d, LANES), F32).at[:, :N_EXPERTS].set(rw.astype(F32))
    rw_hi = rw_pad.astype(BF16)
    rw_pad = jnp.concatenate([rw_hi, (rw_pad - rw_hi.astype(F32)).astype(BF16)], axis=1)
    rb_pad = jnp.full((1, LANES), NEG_INF, F32).at[0, :N_EXPERTS].set(rb.astype(F32))
    return pl.pallas_call(
        _router_kernel,
        grid=(m // tm,),
        in_specs=[
            pl.BlockSpec((tm, d), lambda i: (i, 0)),
            pl.BlockSpec((1, d), lambda i: (0, 0)),
            pl.BlockSpec((d, 2 * LANES), lambda i: (0, 0)),
            pl.BlockSpec((1, LANES), lambda i: (0, 0)),
        ],
        out_specs=pl.BlockSpec((tm, LANES), lambda i: (i, 0)),
        out_shape=jax.ShapeDtypeStruct((m, LANES), F32),
        compiler_params=_params(("parallel",)),
        name="router",
    )(h, g, rw_pad, rb_pad)


def _moe_drain(h_hbm, xbuf, sem):
    for s in range(MOE_SUBS_PER_GROUP):
        rs = slice(s * MOE_SUB, (s + 1) * MOE_SUB)
        pltpu.make_async_copy(h_hbm.at[pl.ds(0, MOE_SUB)], xbuf.at[rs], sem).wait()
    slack = xbuf.shape[0] - MOE_GROUP
    pltpu.make_async_copy(h_hbm.at[pl.ds(0, slack)], xbuf.at[MOE_GROUP:, :], sem).wait()


def _moe_kernel(ge_ref, ns_ref, tok_ref, h_hbm, g_ref, wg_ref, wu_ref, wd_ref, o_ref,
                xbuf, xn_ref, sem, *, rps):
    grp = pl.program_id(0)
    j = pl.program_id(1)
    ns = ns_ref[grp]

    def row_copy(gi, r):
        tok = tok_ref[gi * MOE_GROUP + jnp.minimum(r, MOE_GROUP - 1)]
        return pltpu.make_async_copy(h_hbm.at[pl.ds(tok, 1)], xbuf.at[pl.ds(r, 1)], sem)

    def issue_step_rows(gi):
        for u in range(rps):
            row_copy(gi, j * rps + u).start()

    @pl.when(j == 0)
    def _():
        @pl.when(grp == 0)
        def _():
            def issue(r, carry):
                for u in range(MOE_UNROLL):
                    row_copy(0, r * MOE_UNROLL + u).start()
                return carry

            lax.fori_loop(0, xbuf.shape[0] // MOE_UNROLL, issue, 0)

        _moe_drain(h_hbm, xbuf, sem)
        for s in range(MOE_SUBS_PER_GROUP):
            rs = slice(s * MOE_SUB, (s + 1) * MOE_SUB)
            o_ref[rs, :] = jnp.zeros((MOE_SUB, o_ref.shape[1]), F32)

            @pl.when(s < ns)
            def _():
                xn_ref[rs, :] = _rms_rows(xbuf[rs, :], g_ref[...]).astype(BF16)

    def ffn(rs):
        xn = xn_ref[rs, :]
        gate = jnp.dot(xn, wg_ref[...].astype(BF16), preferred_element_type=F32)
        up = jnp.dot(xn, wu_ref[...].astype(BF16), preferred_element_type=F32)
        hid = (_silu(gate) * up).astype(BF16)
        o_ref[rs, :] += jnp.dot(hid, wd_ref[...].astype(BF16), preferred_element_type=F32)

    @pl.when(ns == MOE_SUBS_PER_GROUP)
    def _():
        issue_step_rows(grp + 1)
        ffn(slice(0, MOE_GROUP))

    @pl.when(ns < MOE_SUBS_PER_GROUP)
    def _():
        issue_step_rows(grp + 1)

    for s in range(MOE_SUBS_PER_GROUP - 1):
        @pl.when((s < ns) & (ns < MOE_SUBS_PER_GROUP))
        def _():
            ffn(slice(s * MOE_SUB, (s + 1) * MOE_SUB))

    @pl.when((grp == pl.num_programs(0) - 1) & (j == pl.num_programs(1) - 1))
    def _():
        _moe_drain(h_hbm, xbuf, sem)


def _moe_ffn(h, g, wg, wu, wd, group_e, group_ns, src_tok, *, tf):
    t, d = h.shape
    n_groups = group_e.shape[0]
    f = wg.shape[2]
    nj = f // tf
    rps = -(-MOE_GROUP // (nj * SUBLANES)) * SUBLANES
    assert src_tok.shape[0] == (n_groups + 1) * MOE_GROUP

    def jj(grp, j, ns):
        return jnp.where(ns[grp] > 0, j, nj - 1)

    return pl.pallas_call(
        functools.partial(_moe_kernel, rps=rps),
        grid_spec=pltpu.PrefetchScalarGridSpec(
            num_scalar_prefetch=3,
            grid=(n_groups, nj),
            in_specs=[
                pl.BlockSpec(memory_space=pl.ANY),
                pl.BlockSpec((1, d), lambda grp, j, ge, ns, tok: (0, 0)),
                pl.BlockSpec((None, d, tf), lambda grp, j, ge, ns, tok: (ge[grp], 0, jj(grp, j, ns))),
                pl.BlockSpec((None, d, tf), lambda grp, j, ge, ns, tok: (ge[grp], 0, jj(grp, j, ns))),
                pl.BlockSpec((None, tf, d), lambda grp, j, ge, ns, tok: (ge[grp], jj(grp, j, ns), 0)),
            ],
            out_specs=pl.BlockSpec((MOE_GROUP, d), lambda grp, j, ge, ns, tok: (grp, 0)),
            scratch_shapes=[
                pltpu.VMEM((rps * nj, d), F32),
                pltpu.VMEM((MOE_GROUP, d), BF16),
                pltpu.SemaphoreType.DMA(()),
            ],
        ),
        out_shape=jax.ShapeDtypeStruct((n_groups * MOE_GROUP, d), F32),
        compiler_params=_params(("arbitrary", "arbitrary")),
        name="moe_ffn",
    )(group_e, group_ns, src_tok, h, g, wg, wu, wd)


def _combine_kernel(pos_ref, h_ref, route_ref, fg_ref, rows_hbm, o_ref, buf_a, buf_b, sem, *, tc):
    i = pl.program_id(0)
    slot = i % 2

    def copies(blk, sl, r):
        pa = pos_ref[TOP_K * (blk * tc + r)]
        pb = pos_ref[TOP_K * (blk * tc + r) + 1]
        return (pltpu.make_async_copy(rows_hbm.at[pl.ds(pa, 1)], buf_a.at[sl, pl.ds(r, 1)], sem.at[sl]),
                pltpu.make_async_copy(rows_hbm.at[pl.ds(pb, 1)], buf_b.at[sl, pl.ds(r, 1)], sem.at[sl]))

    def issue_block(blk, sl):
        def issue(r, carry):
            ca, cb = copies(blk, sl, r)
            ca.start()
            cb.start()
            return carry

        lax.fori_loop(0, tc, issue, 0, unroll=8)

    @pl.when(i == 0)
    def _():
        issue_block(0, 0)

    @pl.when(i + 1 < pl.num_programs(0))
    def _():
        issue_block(i + 1, 1 - slot)

    def wait(r, carry):
        ca, cb = copies(i, slot, r)
        ca.wait()
        cb.wait()
        return carry

    lax.fori_loop(0, tc, wait, 0, unroll=8)
    route = route_ref[...]
    moe = route[:, 2:3] * buf_a[slot] + route[:, 3:4] * buf_b[slot]
    o_ref[...] = _rms_rows(h_ref[...] + moe, fg_ref[...])


def _combine(h, route, final_g, rows, pos, *, tc):
    t, d = h.shape
    return pl.pallas_call(
        functools.partial(_combine_kernel, tc=tc),
        grid_spec=pltpu.PrefetchScalarGridSpec(
            num_scalar_prefetch=1,
            grid=(t // tc,),
            in_specs=[
                pl.BlockSpec((tc, d), lambda i, pos: (i, 0)),
                pl.BlockSpec((tc, LANES), lambda i, pos: (i, 0)),
                pl.BlockSpec((1, d), lambda i, pos: (0, 0)),
                pl.BlockSpec(memory_space=pl.ANY),
            ],
            out_specs=pl.BlockSpec((tc, d), lambda i, pos: (i, 0)),
            scratch_shapes=[pltpu.VMEM((2, tc, d), F32), pltpu.VMEM((2, tc, d), F32), pltpu.SemaphoreType.DMA((2,))],
        ),
        out_shape=jax.ShapeDtypeStruct((t, d), F32),
        compiler_params=_params(("arbitrary",)),
        name="moe_combine",
    )(pos, h, route, final_g, rows)


def _moe_plan(route, n_tok):
    flat_e = route[:, :TOP_K].astype(jnp.int32).reshape(-1)
    onehot = (flat_e[:, None] == jnp.arange(N_EXPERTS, dtype=jnp.int32)[None, :]).astype(jnp.int32)
    csum = jnp.cumsum(onehot, axis=0)
    rank = jnp.sum(onehot * (csum - onehot), axis=1)
    counts = csum[-1]
    n_groups = (n_tok * TOP_K) // MOE_GROUP + N_EXPERTS
    groups_e = (counts + MOE_GROUP - 1) // MOE_GROUP
    group_end = jnp.cumsum(groups_e)
    group_start = group_end - groups_e
    pos = jnp.sum(onehot * (group_start * MOE_GROUP)[None, :], axis=1) + rank
    flat_tok = jnp.arange(n_tok * TOP_K, dtype=jnp.int32) // TOP_K
    src_tok = jnp.zeros(((n_groups + 1) * MOE_GROUP,), jnp.int32).at[pos].set(
        flat_tok, unique_indices=True, mode="promise_in_bounds")
    experts = jnp.arange(N_EXPERTS, dtype=jnp.int32)
    gidx = jnp.arange(n_groups, dtype=jnp.int32)
    used = gidx < group_end[-1]
    last_e = jnp.max(jnp.where(groups_e > 0, experts, 0))
    ge = jnp.sum((group_end[None, :] <= gidx[:, None]).astype(jnp.int32), axis=1)
    ge = jnp.where(used, ge, last_e).astype(jnp.int32)
    mine = (ge[:, None] == experts[None, :]).astype(jnp.int32)
    subs_e = (counts + MOE_SUB - 1) // MOE_SUB
    subs_g = jnp.sum(mine * subs_e[None, :], axis=1)
    start_g = jnp.sum(mine * group_start[None, :], axis=1)
    ns = jnp.clip(subs_g - MOE_SUBS_PER_GROUP * (gidx - start_g), 0, MOE_SUBS_PER_GROUP)
    ns = jnp.where(used, ns, 0).astype(jnp.int32)
    return ge, ns, src_tok, pos.astype(jnp.int32)


def kernel(x, mix_norm_g, ffn_norm_g, final_norm_g, w_in, conv_w, conv_b, conv_ln_g, conv_ln_b, ssm_conv_w, ssm_conv_b, dt_bias, a_log, d_skip, ssm_norm_g, w_out, ffn_w_gate, ffn_w_up, ffn_w_down, w_qkv, b_qkv, w_o, b_o, sinks, rel_bias, router_w, router_b, moe_w_gate, moe_w_up, moe_w_down):
    bsz, seq, d = x.shape
    assert bsz == 1 and d == D_MODEL and seq % 512 == 0
    m = seq
    h = x.reshape(m, d)
    row = lambda v: v.reshape(1, -1).astype(F32)
    tm = min(1024, m)

    main_w = 2 * D_MODEL + D_MODEL + (D_MODEL + 2 * SSM_GROUPS * SSM_STATE)
    w_main = w_in[0].astype(BF16)
    w_dt = jnp.zeros((d, LANES), F32).at[:, :SSM_HEADS].set(w_in[0][:, main_w:]).astype(BF16)
    g0 = row(mix_norm_g[0])
    proj, dt_raw = _in_proj(h, g0, w_main, w_dt, tm=tm, tn=1024, n=main_w)

    conv_out = _conv_module(proj, conv_w[0].astype(F32), row(conv_b[0]), row(conv_ln_g[0]), row(conv_ln_b[0]), tt=128)

    pad_heads = lambda v: jnp.zeros((1, LANES), F32).at[0, :SSM_HEADS].set(v.astype(F32))
    ssm_out = _ssd(proj, dt_raw, ssm_conv_w[0].astype(F32), row(ssm_conv_b[0]),
                   pad_heads(dt_bias[0]), pad_heads(-jnp.exp(a_log[0].astype(F32))),
                   row(jnp.repeat(d_skip[0].astype(F32), SSM_HEAD_DIM)), row(ssm_norm_g[0]))

    wo = w_out[0].astype(BF16)
    h = _matmul_residual([conv_out, ssm_out], [wo[:D_MODEL], wo[D_MODEL:]], h, jnp.zeros((1, d), F32),
                         tm=256, tn=d, name="out_proj")
    h = _swiglu(h, row(ffn_norm_g[0]), ffn_w_gate[0].astype(BF16), ffn_w_up[0].astype(BF16),
                ffn_w_down[0].astype(BF16), tm=tm, tf=512)

    qkv = _norm_matmul(h, row(mix_norm_g[1]), w_qkv[0].astype(BF16), row(b_qkv[0]), tm=tm, tn=1280,
                       out_dtype=BF16, name="qkv_proj")
    attn = _attention(qkv, sinks[0], rel_bias)
    h = _matmul_residual([attn], [w_o[0].astype(BF16)], h, row(b_o[0]), tm=min(512, m), tn=d, name="attn_out_proj")

    g1 = row(ffn_norm_g[1])
    route = _router(h, g1, router_w[0], router_b[0], tm=512)
    ge, ns, src_tok, pos = _moe_plan(route, m)
    rows = _moe_ffn(h, g1, moe_w_gate[0], moe_w_up[0], moe_w_down[0], ge, ns, src_tok, tf=256)
    out = _combine(h, route, row(final_norm_g), rows, pos, tc=256)
    return out.reshape(bsz, seq, d)
```

```python
import functools
import math

import numpy as np
import jax
import jax.numpy as jnp
from jax import lax
from jax.experimental import pallas as pl
from jax.experimental.pallas import tpu as pltpu

F32 = jnp.float32
BF16 = jnp.bfloat16
NORM_EPS = 1e-5
NEG_INF = float("-inf")

LANES = 128
SUBLANES = 8
VMEM_LIMIT = 56 << 20

D_MODEL = 2048
CONV_WIDTH = 31
SSM_HEADS = 32
SSM_HEAD_DIM = 64
SSM_GROUPS = 8
SSM_STATE = 128
SSM_CONV_WIDTH = 4
SSM_CHUNK = 128
ATTN_Q_HEADS = 32
ATTN_KV_HEADS = 4
ATTN_HEAD_DIM = 64
ATTN_BLOCK = 128
REL_BUCKETS = 32
REL_MAX_DIST = 128
N_EXPERTS = 8
TOP_K = 2
MOE_SUB = 256
MOE_SUBS_PER_GROUP = 4
MOE_GROUP = MOE_SUB * MOE_SUBS_PER_GROUP
MOE_UNROLL = 8


def _params(sem):
    return pltpu.CompilerParams(dimension_semantics=sem, vmem_limit_bytes=VMEM_LIMIT)


def _sigmoid(x):
    return 0.5 + 0.5 * jnp.tanh(0.5 * x)


def _silu(x):
    half = 0.5 * x
    return half + half * jnp.tanh(half)


def _rms_rows(x, g):
    ms = jnp.mean(x * x, axis=-1, keepdims=True)
    return x * lax.rsqrt(ms + NORM_EPS) * g


def _cast_slab(src_hbm, dst_hbm, inbuf, outbuf, in_sem, out_sem):
    i = pl.program_id(0)
    n_steps = pl.num_programs(0)
    rows = inbuf.shape[1]
    slot = i % 2

    def fetch(step, sl):
        return pltpu.make_async_copy(src_hbm.at[pl.ds(step * rows, rows)], inbuf.at[sl], in_sem.at[sl])

    def write_back(step, sl):
        return pltpu.make_async_copy(outbuf.at[sl], dst_hbm.at[pl.ds(step * rows, rows)], out_sem.at[sl])

    @pl.when(i == 0)
    def _():
        fetch(0, 0).start()

    @pl.when(i + 1 < n_steps)
    def _():
        fetch(i + 1, 1 - slot).start()

    fetch(i, slot).wait()

    @pl.when(i >= 2)
    def _():
        write_back(i - 2, slot).wait()

    outbuf[slot] = inbuf[slot].astype(BF16)
    write_back(i, slot).start()

    @pl.when(i == n_steps - 1)
    def _():
        write_back(i, slot).wait()

        @pl.when(i >= 1)
        def _():
            write_back(i - 1, 1 - slot).wait()


def _with_cast_job(kernel_fn, n_in, n_out):
    def wrapped(*refs):
        ins, src = refs[:n_in], refs[n_in]
        outs, dst = refs[n_in + 1:n_in + 1 + n_out], refs[n_in + 1 + n_out]
        rest = refs[n_in + 2 + n_out:]
        scratch, cast_scratch = rest[:-4], rest[-4:]
        _cast_slab(src, dst, *cast_scratch)
        kernel_fn(*ins, *outs, *scratch)

    return wrapped


def _cast_job_extras(src, n_steps):
    assert src.shape[0] % n_steps == 0
    rows, cols = src.shape[0] // n_steps, src.shape[1]
    scratch = [pltpu.VMEM((2, rows, cols), F32), pltpu.VMEM((2, rows, cols), BF16),
               pltpu.SemaphoreType.DMA((2,)), pltpu.SemaphoreType.DMA((2,))]
    any_spec = pl.BlockSpec(memory_space=pl.ANY)
    return any_spec, any_spec, jax.ShapeDtypeStruct(src.shape, BF16), scratch


def _norm_matmul_kernel(x_ref, g_ref, w_ref, b_ref, o_ref, xn_ref):
    @pl.when(pl.program_id(1) == 0)
    def _():
        xn_ref[...] = _rms_rows(x_ref[...], g_ref[...]).astype(BF16)

    acc = jnp.dot(xn_ref[...], w_ref[...], preferred_element_type=F32)
    o_ref[...] = (acc + b_ref[...]).astype(o_ref.dtype)


def _norm_matmul(x, g, w, b, *, tm, tn, out_dtype, name):
    m, k = x.shape
    n = w.shape[1]
    return pl.pallas_call(
        _norm_matmul_kernel,
        grid=(m // tm, n // tn),
        in_specs=[
            pl.BlockSpec((tm, k), lambda i, j: (i, 0)),
            pl.BlockSpec((1, k), lambda i, j: (0, 0)),
            pl.BlockSpec((k, tn), lambda i, j: (0, j)),
            pl.BlockSpec((1, tn), lambda i, j: (0, j)),
        ],
        out_specs=pl.BlockSpec((tm, tn), lambda i, j: (i, j)),
        out_shape=jax.ShapeDtypeStruct((m, n), out_dtype),
        scratch_shapes=[pltpu.VMEM((tm, k), BF16)],
        compiler_params=_params(("parallel", "arbitrary")),
        name=name,
    )(x, g, w, b)


def _in_proj_kernel(x_ref, g_ref, w_ref, wdt_ref, o_ref, odt_ref, xn_ref):
    @pl.when(pl.program_id(1) == 0)
    def _():
        xn_ref[...] = _rms_rows(x_ref[...], g_ref[...]).astype(BF16)
        odt_ref[...] = jnp.dot(xn_ref[...], wdt_ref[...], preferred_element_type=F32)

    o_ref[...] = jnp.dot(xn_ref[...], w_ref[...], preferred_element_type=F32).astype(o_ref.dtype)


def _in_proj(x, g, w, wdt, *, tm, tn, n):
    m, k = x.shape
    assert n % tn == 0 and n <= w.shape[1]
    ndt = wdt.shape[1]
    return pl.pallas_call(
        _in_proj_kernel,
        grid=(m // tm, n // tn),
        in_specs=[
            pl.BlockSpec((tm, k), lambda i, j: (i, 0)),
            pl.BlockSpec((1, k), lambda i, j: (0, 0)),
            pl.BlockSpec((k, tn), lambda i, j: (0, j)),
            pl.BlockSpec((k, ndt), lambda i, j: (0, 0)),
        ],
        out_specs=[pl.BlockSpec((tm, tn), lambda i, j: (i, j)), pl.BlockSpec((tm, ndt), lambda i, j: (i, 0))],
        out_shape=[jax.ShapeDtypeStruct((m, n), BF16), jax.ShapeDtypeStruct((m, ndt), F32)],
        scratch_shapes=[pltpu.VMEM((tm, k), BF16)],
        compiler_params=_params(("parallel", "arbitrary")),
        name="in_proj",
    )(x, g, w, wdt)


def _matmul_residual_kernel(*refs, n_in):
    a_refs, w_refs = refs[:n_in], refs[n_in:2 * n_in]
    res_ref, b_ref, o_ref = refs[2 * n_in:]
    acc = res_ref[...] + b_ref[...]
    for a_ref, w_ref in zip(a_refs, w_refs):
        acc = acc + jnp.dot(a_ref[...], w_ref[...], preferred_element_type=F32)
    o_ref[...] = acc


def _matmul_residual(a_list, w_list, res, b, *, tm, tn, name):
    m, n = res.shape
    n_in = len(a_list)
    in_specs = [pl.BlockSpec((tm, a.shape[1]), lambda j, i: (i, 0)) for a in a_list]
    in_specs += [pl.BlockSpec((w.shape[0], tn), lambda j, i: (0, j)) for w in w_list]
    in_specs += [pl.BlockSpec((tm, tn), lambda j, i: (i, j)), pl.BlockSpec((1, tn), lambda j, i: (0, j))]
    return pl.pallas_call(
        functools.partial(_matmul_residual_kernel, n_in=n_in),
        grid=(n // tn, m // tm),
        in_specs=in_specs,
        out_specs=pl.BlockSpec((tm, tn), lambda j, i: (i, j)),
        out_shape=jax.ShapeDtypeStruct((m, n), F32),
        compiler_params=_params(("parallel", "parallel")),
        name=name,
    )(*a_list, *w_list, res, b)


def _swiglu_kernel(x_ref, g_ref, wg_ref, wu_ref, wd_ref, o_ref, xn_ref):
    @pl.when(pl.program_id(1) == 0)
    def _():
        x = x_ref[...]
        xn_ref[...] = _rms_rows(x, g_ref[...]).astype(BF16)
        o_ref[...] = x

    xn = xn_ref[...]
    gate = jnp.dot(xn, wg_ref[...], preferred_element_type=F32)
    up = jnp.dot(xn, wu_ref[...], preferred_element_type=F32)
    hid = (_silu(gate) * up).astype(BF16)
    o_ref[...] += jnp.dot(hid, wd_ref[...], preferred_element_type=F32)


def _swiglu(x, g, wg, wu, wd, *, tm, tf):
    m, d = x.shape
    f = wg.shape[1]
    return pl.pallas_call(
        _swiglu_kernel,
        grid=(m // tm, f // tf),
        in_specs=[
            pl.BlockSpec((tm, d), lambda i, j: (i, 0)),
            pl.BlockSpec((1, d), lambda i, j: (0, 0)),
            pl.BlockSpec((d, tf), lambda i, j: (0, j)),
            pl.BlockSpec((d, tf), lambda i, j: (0, j)),
            pl.BlockSpec((tf, d), lambda i, j: (j, 0)),
        ],
        out_specs=pl.BlockSpec((tm, d), lambda i, j: (i, 0)),
        out_shape=jax.ShapeDtypeStruct((m, d), F32),
        scratch_shapes=[pltpu.VMEM((tm, d), BF16)],
        compiler_params=_params(("parallel", "arbitrary")),
        name="swiglu",
    )(x, g, wg, wu, wd)


CONV_HALO = 32
CONV_ROWS = 64


def _conv_module_kernel(val_ref, gate_ref, pval_ref, pgate_ref, w_ref, b_ref, lg_ref, lb_ref, o_ref,
                        ubuf, cbuf, *, tt):
    i = pl.program_id(0)

    def glu(v, g):
        return v.astype(F32) * _sigmoid(g.astype(F32))

    ubuf[0:CONV_HALO, :] = jnp.where(i > 0, glu(pval_ref[...], pgate_ref[...]), 0.0)
    ubuf[CONV_HALO:CONV_HALO + tt, :] = glu(val_ref[...], gate_ref[...])

    off = CONV_HALO - (CONV_WIDTH - 1)
    taps = [[] for _ in range(SUBLANES)]
    for j in range(CONV_WIDTH):
        a, r = divmod(off + j, SUBLANES)
        taps[r].append((a, j))
    n_ch = ubuf.shape[1]
    for cb in range(n_ch // LANES):
        ls = slice(cb * LANES, (cb + 1) * LANES)
        for r0 in range(0, tt, CONV_ROWS):
            big = ubuf[r0:r0 + CONV_ROWS + CONV_HALO, ls]
            out = None
            for r in range(SUBLANES):
                rows = CONV_ROWS if r == 0 else CONV_ROWS + SUBLANES
                q = None
                for a, j in taps[r]:
                    term = w_ref[j:j + 1, ls] * big[a * SUBLANES:a * SUBLANES + rows, :]
                    q = term if q is None else q + term
                if r:
                    q = pltpu.roll(q, rows - r, axis=0)[:CONV_ROWS, :]
                out = q if out is None else out + q
            cbuf[r0:r0 + CONV_ROWS, ls] = out

    c = cbuf[...] + b_ref[...]
    mu = jnp.mean(c, axis=-1, keepdims=True)
    d = c - mu
    var = jnp.mean(d * d, axis=-1, keepdims=True)
    y = d * lax.rsqrt(var + NORM_EPS) * lg_ref[...] + lb_ref[...]
    o_ref[...] = _silu(y).astype(o_ref.dtype)


def _conv_module(proj, conv_w, conv_b, ln_g, ln_b, cast_src, *, tt):
    m = proj.shape[0]
    c = D_MODEL
    hb = tt // CONV_HALO
    cast_in, cast_out, cast_shape, cast_scratch = _cast_job_extras(cast_src, m // tt)
    return pl.pallas_call(
        _with_cast_job(functools.partial(_conv_module_kernel, tt=tt), 8, 1),
        grid=(m // tt,),
        in_specs=[
            pl.BlockSpec((tt, c), lambda i: (i, 0)),
            pl.BlockSpec((tt, c), lambda i: (i, 1)),
            pl.BlockSpec((CONV_HALO, c), lambda i: (jnp.maximum(i * hb - 1, 0), 0)),
            pl.BlockSpec((CONV_HALO, c), lambda i: (jnp.maximum(i * hb - 1, 0), 1)),
            pl.BlockSpec((CONV_WIDTH, c), lambda i: (0, 0)),
            pl.BlockSpec((1, c), lambda i: (0, 0)),
            pl.BlockSpec((1, c), lambda i: (0, 0)),
            pl.BlockSpec((1, c), lambda i: (0, 0)),
            cast_in,
        ],
        out_specs=[pl.BlockSpec((tt, c), lambda i: (i, 0)), cast_out],
        out_shape=[jax.ShapeDtypeStruct((m, c), BF16), cast_shape],
        scratch_shapes=[pltpu.VMEM((CONV_HALO + tt, c), F32), pltpu.VMEM((tt, c), F32)] + cast_scratch,
        compiler_params=_params(("arbitrary",)),
        name="conv_module",
    )(proj, proj, proj, proj, conv_w, conv_b, ln_g, ln_b, cast_src)


SSM_HALO = 16


def _conv4_silu(cur_ref, prev_ref, shift_ref, w_ref, b_ref, cols, first):
    q = cur_ref.shape[0]
    cur = cur_ref[:, cols]
    prev = prev_ref[:, cols]
    xb = jnp.concatenate([jnp.where(first, jnp.zeros_like(prev), prev), cur], axis=0)
    back = jnp.dot(shift_ref[...], xb, preferred_element_type=F32)
    last = SSM_CONV_WIDTH - 1
    acc = b_ref[:, cols] + w_ref[last:last + 1, cols] * cur.astype(F32)
    for k in range(1, SSM_CONV_WIDTH):
        acc = acc + w_ref[last - k:last - k + 1, cols] * back[(k - 1) * q:k * q, :]
    return _silu(acc)


def _ssd_kernel(z_ref, x_ref, b_ref, c_ref, px_ref, pb_ref, pc_ref, dtr_ref,
                wx_ref, wb_ref, wc_ref, bx_ref, bb_ref, bc_ref,
                dtb_ref, a_ref, dskip_ref, ng_ref, e_ref, sh_ref, o_ref,
                state, ybuf):
    i = pl.program_id(0)
    q = SSM_CHUNK
    first = i == 0

    @pl.when(first)
    def _():
        state[...] = jnp.zeros_like(state)

    everything = slice(None)
    xs = _conv4_silu(x_ref, px_ref, sh_ref, wx_ref, bx_ref, everything, first)
    bm = _conv4_silu(b_ref, pb_ref, sh_ref, wb_ref, bb_ref, everything, first)
    cm = _conv4_silu(c_ref, pc_ref, sh_ref, wc_ref, bc_ref, everything, first)

    pre = dtr_ref[...] + dtb_ref[...]
    dt = jnp.maximum(pre, 0.0) + jnp.log(1.0 + jnp.exp(-jnp.abs(pre)))
    dta = dt * a_ref[...]
    row = lax.broadcasted_iota(jnp.int32, (q, q), 0)
    col = lax.broadcasted_iota(jnp.int32, (q, q), 1)
    causal = row >= col
    acs = jnp.dot(causal.astype(F32), dta, precision=lax.Precision.HIGHEST, preferred_element_type=F32)
    acs_t = acs.T
    dt_t = dt.T
    eacs = jnp.exp(acs)
    wdec = dt * jnp.exp(acs[q - 1:q, :] - acs)

    def split(v):
        hi = v.astype(BF16)
        return hi, (v - hi.astype(F32)).astype(BF16)

    parts = jnp.concatenate(split(eacs) + split(wdec), axis=0)
    wide = jnp.dot(parts, e_ref[...], preferred_element_type=F32)
    eacs_full = wide[0:q] + wide[q:2 * q]
    xd = (xs * (wide[2 * q:3 * q] + wide[3 * q:4 * q])).astype(BF16)

    xs_b = xs.astype(BF16)
    lane = lax.broadcasted_iota(jnp.int32, (1, xs.shape[1]), 1)
    low_head = (lane % LANES) < SSM_HEAD_DIM
    xs_lo = jnp.where(low_head, xs_b, jnp.zeros_like(xs_b))
    xs_hi = jnp.where(low_head, jnp.zeros_like(xs_b), xs_b)

    gw = SSM_HEAD_DIM * (SSM_HEADS // SSM_GROUPS)
    for g in range(SSM_GROUPS):
        ns = slice(g * SSM_STATE, (g + 1) * SSM_STATE)
        gs = slice(g * gw, (g + 1) * gw)
        bg_f = bm[:, ns]
        bg = bg_f.astype(BF16)
        cg = cm[:, ns].astype(BF16)
        cb = lax.dot_general(cg, bg, (((1,), (1,)), ((), ())), preferred_element_type=F32)
        prev = state[g]
        y_off = jnp.dot(cg, prev.astype(BF16), preferred_element_type=F32)
        for c in range(gw // LANES):
            cs = slice(g * gw + c * LANES, g * gw + (c + 1) * LANES)
            acc = y_off[:, c * LANES:(c + 1) * LANES] * eacs_full[:, cs]
            for par, xpart in ((0, xs_lo), (1, xs_hi)):
                h = g * (SSM_HEADS // SSM_GROUPS) + 2 * c + par
                seg = acs[:, h:h + 1] - acs_t[h:h + 1, :]
                mmat = cb * jnp.exp(jnp.where(causal, seg, NEG_INF)) * dt_t[h:h + 1, :]
                acc = acc + jnp.dot(mmat.astype(BF16), xpart[:, cs], preferred_element_type=F32)
            ybuf[:, cs] = acc
        new_states = jnp.dot(bg_f.T.astype(BF16), xd[:, gs], preferred_element_type=F32)
        state[g] = prev * eacs_full[q - 1:q, gs] + new_states

    y = ybuf[...] + xs * dskip_ref[...]
    z = z_ref[...].astype(F32)
    y = y * _silu(z)
    for g in range(SSM_GROUPS):
        gs = slice(g * gw, (g + 1) * gw)
        yg = y[:, gs]
        ms = jnp.mean(yg * yg, axis=-1, keepdims=True)
        o_ref[:, gs] = (yg * lax.rsqrt(ms + NORM_EPS) * ng_ref[:, gs]).astype(o_ref.dtype)


def _ssd(proj, dt_raw, conv_w, conv_b, dt_bias, a_neg, d_skip, norm_g, cast_src):
    m = proj.shape[0]
    q = SSM_CHUNK
    inner = D_MODEL
    bc = SSM_GROUPS * SSM_STATE
    hb = q // SSM_HALO
    wx, wb, wc = conv_w[:, :inner], conv_w[:, inner:inner + bc], conv_w[:, inner + bc:]
    bx, bb, bcc = conv_b[:, :inner], conv_b[:, inner:inner + bc], conv_b[:, inner + bc:]
    expand = jnp.asarray(np.arange(LANES)[:, None] == (np.arange(inner) // SSM_HEAD_DIM)[None, :], BF16)
    t_idx = np.arange(q)
    shift_np = np.zeros((SSM_CONV_WIDTH - 1, q, SSM_HALO + q), np.float32)
    for k in range(1, SSM_CONV_WIDTH):
        shift_np[k - 1, t_idx, SSM_HALO + t_idx - k] = 1.0
    shift = jnp.asarray(shift_np.reshape(-1, SSM_HALO + q), BF16)
    prev = lambda i: jnp.maximum(i * hb - 1, 0)
    full = lambda shape: pl.BlockSpec(shape, lambda i: (0,) * len(shape))
    cast_in, cast_out, cast_shape, cast_scratch = _cast_job_extras(cast_src, m // q)
    return pl.pallas_call(
        _with_cast_job(_ssd_kernel, 20, 1),
        grid=(m // q,),
        in_specs=[
            pl.BlockSpec((q, inner), lambda i: (i, 2)),
            pl.BlockSpec((q, inner), lambda i: (i, 3)),
            pl.BlockSpec((q, bc), lambda i: (i, 8)),
            pl.BlockSpec((q, bc), lambda i: (i, 9)),
            pl.BlockSpec((SSM_HALO, inner), lambda i: (prev(i), 3)),
            pl.BlockSpec((SSM_HALO, bc), lambda i: (prev(i), 8)),
            pl.BlockSpec((SSM_HALO, bc), lambda i: (prev(i), 9)),
            pl.BlockSpec((q, LANES), lambda i: (i, 0)),
            full(wx.shape), full(wb.shape), full(wc.shape),
            full(bx.shape), full(bb.shape), full(bcc.shape),
            full((1, LANES)), full((1, LANES)), full((1, inner)), full((1, inner)),
            full((LANES, inner)),
            full(shift.shape),
            cast_in,
        ],
        out_specs=[pl.BlockSpec((q, inner), lambda i: (i, 0)), cast_out],
        out_shape=[jax.ShapeDtypeStruct((m, inner), BF16), cast_shape],
        scratch_shapes=[
            pltpu.VMEM((SSM_GROUPS, SSM_STATE, inner // SSM_GROUPS), F32),
            pltpu.VMEM((q, inner), F32),
        ] + cast_scratch,
        compiler_params=_params(("arbitrary",)),
        name="ssd",
    )(proj, proj, proj, proj, proj, proj, proj, dt_raw, wx, wb, wc, bx, bb, bcc,
      dt_bias, a_neg, d_skip, norm_g, expand, shift, cast_src)


def _attn_kernel(sinks_ref, q_ref, kvc_ref, kvp_ref, bias_ref, o_ref):
    blk = ATTN_BLOCK
    kvw = ATTN_KV_HEADS * ATTN_HEAD_DIM
    rep = ATTN_Q_HEADS // ATTN_KV_HEADS
    kv = jnp.concatenate([kvp_ref[...], kvc_ref[...]], axis=0).astype(F32)
    lane = lax.broadcasted_iota(jnp.int32, (1, LANES), 1)
    low = lane < ATTN_HEAD_DIM
    scale = ATTN_HEAD_DIM ** -0.5

    for g in range(ATTN_KV_HEADS):
        pc = g // 2
        kcol = kv[:, pc * LANES:(pc + 1) * LANES] * scale
        vcol = kv[:, kvw + pc * LANES:kvw + (pc + 1) * LANES]
        kroll = pltpu.roll(kcol, ATTN_HEAD_DIM, axis=1)
        vroll = pltpu.roll(vcol, ATTN_HEAD_DIM, axis=1)
        if g % 2 == 0:
            k_lo, k_hi = jnp.where(low, kcol, 0.0), jnp.where(low, 0.0, kroll)
            v_lo, v_hi = jnp.where(low, vcol, 0.0), jnp.where(low, 0.0, vroll)
        else:
            k_lo, k_hi = jnp.where(low, kroll, 0.0), jnp.where(low, 0.0, kcol)
            v_lo, v_hi = jnp.where(low, vroll, 0.0), jnp.where(low, 0.0, vcol)
        k_lo, k_hi, v_lo, v_hi = (t.astype(BF16) for t in (k_lo, k_hi, v_lo, v_hi))

        ncol = rep // 2
        qs = jnp.concatenate([q_ref[:, (g * ncol + c) * LANES:(g * ncol + c + 1) * LANES] for c in range(ncol)], axis=0)
        nt = (((1,), (1,)), ((), ()))
        s_even = lax.dot_general(qs, k_lo, nt, preferred_element_type=F32)
        s_odd = lax.dot_general(qs, k_hi, nt, preferred_element_type=F32)
        for c in range(ncol):
            out = None
            for par, s_all, vv in ((0, s_even, v_lo), (1, s_odd, v_hi)):
                h = g * rep + 2 * c + par
                logits = s_all[c * blk:(c + 1) * blk, :] + bias_ref[0, h]
                sink = sinks_ref[h]
                mx = jnp.maximum(jnp.max(logits, axis=-1, keepdims=True), sink)
                p = jnp.exp(logits - mx)
                denom = jnp.sum(p, axis=-1, keepdims=True) + jnp.exp(sink - mx)
                o = jnp.dot(p.astype(BF16), vv, preferred_element_type=F32) * (1.0 / denom)
                out = o if out is None else out + o
            col = g * ncol + c
            o_ref[:, col * LANES:(col + 1) * LANES] = out.astype(o_ref.dtype)


def _t5_bucket_table():
    dist = np.arange(ATTN_BLOCK)[:, None] + ATTN_BLOCK - np.arange(2 * ATTN_BLOCK)[None, :]
    n = np.maximum(dist, 0)
    max_exact = REL_BUCKETS // 2
    nf = np.maximum(n, 1).astype(np.float32)
    large = max_exact + (np.log(nf / max_exact) / math.log(REL_MAX_DIST / max_exact)
                         * (REL_BUCKETS - max_exact)).astype(np.int32)
    large = np.minimum(large, REL_BUCKETS - 1)
    return dist, np.where(n < max_exact, n, large)


def _attention(qkv, sinks, rel_bias, cast_src):
    m = qkv.shape[0]
    blk = ATTN_BLOCK
    qd = ATTN_Q_HEADS * ATTN_HEAD_DIM
    kvd = 2 * ATTN_KV_HEADS * ATTN_HEAD_DIM
    dist, bucket = _t5_bucket_table()
    visible = (dist >= 0) & (dist < blk)
    onehot = jnp.asarray(np.arange(REL_BUCKETS)[:, None] == bucket.reshape(1, -1), F32)
    bias = jnp.dot(rel_bias.astype(F32).T, onehot, precision=lax.Precision.HIGHEST).reshape(-1, blk, 2 * blk)
    general = jnp.where(visible[None], bias, NEG_INF)
    first = jnp.where((visible & (np.arange(2 * blk) >= blk)[None, :])[None], bias, NEG_INF)
    table = jnp.stack([first, general])
    cast_in, cast_out, cast_shape, cast_scratch = _cast_job_extras(cast_src, m // blk)
    return pl.pallas_call(
        _with_cast_job(_attn_kernel, 5, 1),
        grid=(m // blk,),
        in_specs=[
            pl.BlockSpec(memory_space=pltpu.SMEM),
            pl.BlockSpec((blk, qd), lambda n: (n, 0)),
            pl.BlockSpec((blk, kvd), lambda n: (n, qd // kvd)),
            pl.BlockSpec((blk, kvd), lambda n: (jnp.maximum(n - 1, 0), qd // kvd)),
            pl.BlockSpec((1,) + table.shape[1:], lambda n: (jnp.minimum(n, 1), 0, 0, 0)),
            cast_in,
        ],
        out_specs=[pl.BlockSpec((blk, qd), lambda n: (n, 0)), cast_out],
        out_shape=[jax.ShapeDtypeStruct((m, qd), BF16), cast_shape],
        scratch_shapes=cast_scratch,
        compiler_params=_params(("arbitrary",)),
        name="swa_attention",
    )(sinks.astype(F32), qkv, qkv, qkv, table, cast_src)


def _router_kernel(x_ref, g_ref, rw_ref, rb_ref, o_ref):
    xn = _rms_rows(x_ref[...], g_ref[...])
    x_hi = xn.astype(BF16)
    x_lo = (xn - x_hi.astype(F32)).astype(BF16)
    both = jnp.dot(x_hi, rw_ref[...], preferred_element_type=F32)
    logits = (both[:, :LANES] + both[:, LANES:]
              + jnp.dot(x_lo, rw_ref[:, :LANES], preferred_element_type=F32) + rb_ref[...])
    lane = lax.broadcasted_iota(jnp.int32, logits.shape, 1)
    m1 = jnp.max(logits, axis=-1, keepdims=True)
    i1 = jnp.min(jnp.where(logits == m1, lane, LANES), axis=-1, keepdims=True)
    rest = jnp.where(lane == i1, NEG_INF, logits)
    m2 = jnp.max(rest, axis=-1, keepdims=True)
    i2 = jnp.min(jnp.where(rest == m2, lane, LANES), axis=-1, keepdims=True)
    e = jnp.exp(m2 - m1)
    g1 = 1.0 / (1.0 + e)
    g2 = e * g1
    out = jnp.where(lane == 0, i1.astype(F32),
                    jnp.where(lane == 1, i2.astype(F32),
                              jnp.where(lane == 2, g1, jnp.where(lane == 3, g2, 0.0))))
    o_ref[...] = out


def _router(h, g, rw, rb, *, tm):
    m, d = h.shape
    rw_pad = jnp.zeros((d, LANES), F32).at[:, :N_EXPERTS].set(rw.astype(F32))
    rw_hi = rw_pad.astype(BF16)
    rw_pad = jnp.concatenate([rw_hi, (rw_pad - rw_hi.astype(F32)).astype(BF16)], axis=1)
    rb_pad = jnp.full((1, LANES), NEG_INF, F32).at[0, :N_EXPERTS].set(rb.astype(F32))
    return pl.pallas_call(
        _router_kernel,
        grid=(m // tm,),
        in_specs=[
            pl.BlockSpec((tm, d), lambda i: (i, 0)),
            pl.BlockSpec((1, d), lambda i: (0, 0)),
            pl.BlockSpec((d, 2 * LANES), lambda i: (0, 0)),
            pl.BlockSpec((1, LANES), lambda i: (0, 0)),
        ],
        out_specs=pl.BlockSpec((tm, LANES), lambda i: (i, 0)),
        out_shape=jax.ShapeDtypeStruct((m, LANES), F32),
        compiler_params=_params(("parallel",)),
        name="router",
    )(h, g, rw_pad, rb_pad)


def _moe_drain(h_hbm, xbuf, sem):
    for s in range(MOE_SUBS_PER_GROUP):
        rs = slice(s * MOE_SUB, (s + 1) * MOE_SUB)
        pltpu.make_async_copy(h_hbm.at[pl.ds(0, MOE_SUB)], xbuf.at[rs], sem).wait()
    slack = xbuf.shape[0] - MOE_GROUP
    pltpu.make_async_copy(h_hbm.at[pl.ds(0, slack)], xbuf.at[MOE_GROUP:, :], sem).wait()


def _moe_kernel(ge_ref, ns_ref, tok_ref, h_hbm, g_ref, wg_ref, wu_ref, wd_ref, o_ref,
                xbuf, xn_ref, sem, *, rps):
    grp = pl.program_id(0)
    j = pl.program_id(1)
    ns = ns_ref[grp]

    def row_copy(gi, r):
        tok = tok_ref[gi * MOE_GROUP + jnp.minimum(r, MOE_GROUP - 1)]
        return pltpu.make_async_copy(h_hbm.at[pl.ds(tok, 1)], xbuf.at[pl.ds(r, 1)], sem)

    def issue_step_rows(gi):
        for u in range(rps):
            row_copy(gi, j * rps + u).start()

    @pl.when(j == 0)
    def _():
        @pl.when(grp == 0)
        def _():
            def issue(r, carry):
                for u in range(MOE_UNROLL):
                    row_copy(0, r * MOE_UNROLL + u).start()
                return carry

            lax.fori_loop(0, xbuf.shape[0] // MOE_UNROLL, issue, 0)

        _moe_drain(h_hbm, xbuf, sem)
        for s in range(MOE_SUBS_PER_GROUP):
            rs = slice(s * MOE_SUB, (s + 1) * MOE_SUB)
            o_ref[rs, :] = jnp.zeros((MOE_SUB, o_ref.shape[1]), F32)

            @pl.when(s < ns)
            def _():
                xn_ref[rs, :] = _rms_rows(xbuf[rs, :], g_ref[...]).astype(BF16)

    def ffn(rs):
        xn = xn_ref[rs, :]
        gate = jnp.dot(xn, wg_ref[...].astype(BF16), preferred_element_type=F32)
        up = jnp.dot(xn, wu_ref[...].astype(BF16), preferred_element_type=F32)
        hid = (_silu(gate) * up).astype(BF16)
        o_ref[rs, :] += jnp.dot(hid, wd_ref[...].astype(BF16), preferred_element_type=F32)

    @pl.when(ns == MOE_SUBS_PER_GROUP)
    def _():
        issue_step_rows(grp + 1)
        ffn(slice(0, MOE_GROUP))

    @pl.when(ns < MOE_SUBS_PER_GROUP)
    def _():
        issue_step_rows(grp + 1)

    for s in range(MOE_SUBS_PER_GROUP - 1):
        @pl.when((s < ns) & (ns < MOE_SUBS_PER_GROUP))
        def _():
            ffn(slice(s * MOE_SUB, (s + 1) * MOE_SUB))

    @pl.when((grp == pl.num_programs(0) - 1) & (j == pl.num_programs(1) - 1))
    def _():
        _moe_drain(h_hbm, xbuf, sem)


def _moe_ffn(h, g, wg, wu, wd, group_e, group_ns, src_tok, *, tf):
    t, d = h.shape
    n_groups = group_e.shape[0]
    f = wg.shape[2]
    nj = f // tf
    rps = -(-MOE_GROUP // (nj * SUBLANES)) * SUBLANES
    assert src_tok.shape[0] == (n_groups + 1) * MOE_GROUP

    def jj(grp, j, ns):
        return jnp.where(ns[grp] > 0, j, nj - 1)

    return pl.pallas_call(
        functools.partial(_moe_kernel, rps=rps),
        grid_spec=pltpu.PrefetchScalarGridSpec(
            num_scalar_prefetch=3,
            grid=(n_groups, nj),
            in_specs=[
                pl.BlockSpec(memory_space=pl.ANY),
                pl.BlockSpec((1, d), lambda grp, j, ge, ns, tok: (0, 0)),
                pl.BlockSpec((None, d, tf), lambda grp, j, ge, ns, tok: (ge[grp], 0, jj(grp, j, ns))),
                pl.BlockSpec((None, d, tf), lambda grp, j, ge, ns, tok: (ge[grp], 0, jj(grp, j, ns))),
                pl.BlockSpec((None, tf, d), lambda grp, j, ge, ns, tok: (ge[grp], jj(grp, j, ns), 0)),
            ],
            out_specs=pl.BlockSpec((MOE_GROUP, d), lambda grp, j, ge, ns, tok: (grp, 0)),
            scratch_shapes=[
                pltpu.VMEM((rps * nj, d), F32),
                pltpu.VMEM((MOE_GROUP, d), BF16),
                pltpu.SemaphoreType.DMA(()),
            ],
        ),
        out_shape=jax.ShapeDtypeStruct((n_groups * MOE_GROUP, d), F32),
        compiler_params=_params(("arbitrary", "arbitrary")),
        name="moe_ffn",
    )(group_e, group_ns, src_tok, h, g, wg, wu, wd)


def _combine_kernel(pos_ref, h_ref, route_ref, fg_ref, rows_hbm, o_ref, buf_a, buf_b, sem, *, tc):
    i = pl.program_id(0)
    slot = i % 2

    def copies(blk, sl, r):
        pa = pos_ref[TOP_K * (blk * tc + r)]
        pb = pos_ref[TOP_K * (blk * tc + r) + 1]
        return (pltpu.make_async_copy(rows_hbm.at[pl.ds(pa, 1)], buf_a.at[sl, pl.ds(r, 1)], sem.at[sl]),
                pltpu.make_async_copy(rows_hbm.at[pl.ds(pb, 1)], buf_b.at[sl, pl.ds(r, 1)], sem.at[sl]))

    def issue_block(blk, sl):
        def issue(r, carry):
            ca, cb = copies(blk, sl, r)
            ca.start()
            cb.start()
            return carry

        lax.fori_loop(0, tc, issue, 0, unroll=8)

    @pl.when(i == 0)
    def _():
        issue_block(0, 0)

    @pl.when(i + 1 < pl.num_programs(0))
    def _():
        issue_block(i + 1, 1 - slot)

    def wait(r, carry):
        ca, cb = copies(i, slot, r)
        ca.wait()
        cb.wait()
        return carry

    lax.fori_loop(0, tc, wait, 0, unroll=8)
    route = route_ref[...]
    moe = route[:, 2:3] * buf_a[slot] + route[:, 3:4] * buf_b[slot]
    o_ref[...] = _rms_rows(h_ref[...] + moe, fg_ref[...])


def _combine(h, route, final_g, rows, pos, *, tc):
    t, d = h.shape
    return pl.pallas_call(
        functools.partial(_combine_kernel, tc=tc),
        grid_spec=pltpu.PrefetchScalarGridSpec(
            num_scalar_prefetch=1,
            grid=(t // tc,),
            in_specs=[
                pl.BlockSpec((tc, d), lambda i, pos: (i, 0)),
                pl.BlockSpec((tc, LANES), lambda i, pos: (i, 0)),
                pl.BlockSpec((1, d), lambda i, pos: (0, 0)),
                pl.BlockSpec(memory_space=pl.ANY),
            ],
            out_specs=pl.BlockSpec((tc, d), lambda i, pos: (i, 0)),
            scratch_shapes=[pltpu.VMEM((2, tc, d), F32), pltpu.VMEM((2, tc, d), F32), pltpu.SemaphoreType.DMA((2,))],
        ),
        out_shape=jax.ShapeDtypeStruct((t, d), F32),
        compiler_params=_params(("arbitrary",)),
        name="moe_combine",
    )(pos, h, route, final_g, rows)


def _moe_plan(route, n_tok):
    flat_e = route[:, :TOP_K].astype(jnp.int32).reshape(-1)
    onehot = (flat_e[:, None] == jnp.arange(N_EXPERTS, dtype=jnp.int32)[None, :]).astype(jnp.int32)
    csum = jnp.cumsum(onehot, axis=0)
    rank = jnp.sum(onehot * (csum - onehot), axis=1)
    counts = csum[-1]
    n_groups = (n_tok * TOP_K) // MOE_GROUP + N_EXPERTS
    groups_e = (counts + MOE_GROUP - 1) // MOE_GROUP
    group_end = jnp.cumsum(groups_e)
    group_start = group_end - groups_e
    pos = jnp.sum(onehot * (group_start * MOE_GROUP)[None, :], axis=1) + rank
    flat_tok = jnp.arange(n_tok * TOP_K, dtype=jnp.int32) // TOP_K
    src_tok = jnp.zeros(((n_groups + 1) * MOE_GROUP,), jnp.int32).at[pos].set(
        flat_tok, unique_indices=True, mode="promise_in_bounds")
    experts = jnp.arange(N_EXPERTS, dtype=jnp.int32)
    gidx = jnp.arange(n_groups, dtype=jnp.int32)
    used = gidx < group_end[-1]
    last_e = jnp.max(jnp.where(groups_e > 0, experts, 0))
    ge = jnp.sum((group_end[None, :] <= gidx[:, None]).astype(jnp.int32), axis=1)
    ge = jnp.where(used, ge, last_e).astype(jnp.int32)
    mine = (ge[:, None] == experts[None, :]).astype(jnp.int32)
    subs_e = (counts + MOE_SUB - 1) // MOE_SUB
    subs_g = jnp.sum(mine * subs_e[None, :], axis=1)
    start_g = jnp.sum(mine * group_start[None, :], axis=1)
    ns = jnp.clip(subs_g - MOE_SUBS_PER_GROUP * (gidx - start_g), 0, MOE_SUBS_PER_GROUP)
    ns = jnp.where(used, ns, 0).astype(jnp.int32)
    return ge, ns, src_tok, pos.astype(jnp.int32)


def kernel(x, mix_norm_g, ffn_norm_g, final_norm_g, w_in, conv_w, conv_b, conv_ln_g, conv_ln_b, ssm_conv_w, ssm_conv_b, dt_bias, a_log, d_skip, ssm_norm_g, w_out, ffn_w_gate, ffn_w_up, ffn_w_down, w_qkv, b_qkv, w_o, b_o, sinks, rel_bias, router_w, router_b, moe_w_gate, moe_w_up, moe_w_down):
    bsz, seq, d = x.shape
    assert bsz == 1 and d == D_MODEL and seq % 512 == 0
    m = seq
    h = x.reshape(m, d)
    row = lambda v: v.reshape(1, -1).astype(F32)
    tm = min(1024, m)

    main_w = 2 * D_MODEL + D_MODEL + (D_MODEL + 2 * SSM_GROUPS * SSM_STATE)
    w_main = w_in[0].astype(BF16)
    w_dt = jnp.zeros((d, LANES), F32).at[:, :SSM_HEADS].set(w_in[0][:, main_w:]).astype(BF16)
    g0 = row(mix_norm_g[0])
    proj, dt_raw = _in_proj(h, g0, w_main, w_dt, tm=tm, tn=1024, n=main_w)

    n_exp, _, expert_dim = moe_w_gate[0].shape
    wg_f32 = moe_w_gate[0].reshape(n_exp * d, expert_dim)
    wu_f32 = moe_w_up[0].reshape(n_exp * d, expert_dim)
    wd_f32 = moe_w_down[0].reshape(n_exp * expert_dim, d)

    conv_out, wg_b = _conv_module(proj, conv_w[0].astype(F32), row(conv_b[0]), row(conv_ln_g[0]), row(conv_ln_b[0]),
                                  wg_f32, tt=128)

    pad_heads = lambda v: jnp.zeros((1, LANES), F32).at[0, :SSM_HEADS].set(v.astype(F32))
    ssm_out, wu_b = _ssd(proj, dt_raw, ssm_conv_w[0].astype(F32), row(ssm_conv_b[0]),
                         pad_heads(dt_bias[0]), pad_heads(-jnp.exp(a_log[0].astype(F32))),
                         row(jnp.repeat(d_skip[0].astype(F32), SSM_HEAD_DIM)), row(ssm_norm_g[0]), wu_f32)

    wo = w_out[0].astype(BF16)
    h = _matmul_residual([conv_out, ssm_out], [wo[:D_MODEL], wo[D_MODEL:]], h, jnp.zeros((1, d), F32),
                         tm=256, tn=d, name="out_proj")
    h = _swiglu(h, row(ffn_norm_g[0]), ffn_w_gate[0].astype(BF16), ffn_w_up[0].astype(BF16),
                ffn_w_down[0].astype(BF16), tm=tm, tf=512)

    qkv = _norm_matmul(h, row(mix_norm_g[1]), w_qkv[0].astype(BF16), row(b_qkv[0]), tm=tm, tn=1280,
                       out_dtype=BF16, name="qkv_proj")
    attn, wd_b = _attention(qkv, sinks[0], rel_bias, wd_f32)
    h = _matmul_residual([attn], [w_o[0].astype(BF16)], h, row(b_o[0]), tm=min(512, m), tn=d, name="attn_out_proj")

    g1 = row(ffn_norm_g[1])
    route = _router(h, g1, router_w[0], router_b[0], tm=512)
    ge, ns, src_tok, pos = _moe_plan(route, m)
    rows = _moe_ffn(h, g1, wg_b.reshape(n_exp, d, expert_dim), wu_b.reshape(n_exp, d, expert_dim),
                    wd_b.reshape(n_exp, expert_dim, d), ge, ns, src_tok, tf=512)
    out = _combine(h, route, row(final_norm_g), rows, pos, tc=256)
    return out.reshape(bsz, seq, d)
```

```python
import functools
import math

import numpy as np
import jax
import jax.numpy as jnp
from jax import lax
from jax.experimental import pallas as pl
from jax.experimental.pallas import tpu as pltpu

F32 = jnp.float32
BF16 = jnp.bfloat16
NORM_EPS = 1e-5
NEG_INF = float("-inf")

LANES = 128
SUBLANES = 8
VMEM_LIMIT = 56 << 20

D_MODEL = 2048
CONV_WIDTH = 31
SSM_HEADS = 32
SSM_HEAD_DIM = 64
SSM_GROUPS = 8
SSM_STATE = 128
SSM_CONV_WIDTH = 4
SSM_CHUNK = 128
ATTN_Q_HEADS = 32
ATTN_KV_HEADS = 4
ATTN_HEAD_DIM = 64
ATTN_BLOCK = 128
REL_BUCKETS = 32
REL_MAX_DIST = 128
N_EXPERTS = 8
TOP_K = 2
MOE_SUB = 256
MOE_SUBS_PER_GROUP = 4
MOE_GROUP = MOE_SUB * MOE_SUBS_PER_GROUP
MOE_UNROLL = 8


def _params(sem, flags=None):
    return pltpu.CompilerParams(dimension_semantics=sem, vmem_limit_bytes=VMEM_LIMIT, flags=flags)


def _sigmoid(x):
    return 0.5 + 0.5 * jnp.tanh(0.5 * x)


def _silu(x):
    half = 0.5 * x
    return half + half * jnp.tanh(half)


def _rms_rows(x, g):
    ms = jnp.mean(x * x, axis=-1, keepdims=True)
    return x * lax.rsqrt(ms + NORM_EPS) * g


def _cast_slab(i, n_steps, src_hbm, dst_hbm, inbuf, outbuf, in_sem, out_sem):
    rows = inbuf.shape[1]
    slot = i % 2

    def fetch(step, sl):
        return pltpu.make_async_copy(src_hbm.at[pl.ds(step * rows, rows)], inbuf.at[sl], in_sem.at[sl])

    def write_back(step, sl):
        return pltpu.make_async_copy(outbuf.at[sl], dst_hbm.at[pl.ds(step * rows, rows)], out_sem.at[sl])

    @pl.when(i == 0)
    def _():
        fetch(0, 0).start()

    @pl.when(i + 1 < n_steps)
    def _():
        fetch(i + 1, 1 - slot).start()

    fetch(i, slot).wait()

    @pl.when(i >= 2)
    def _():
        write_back(i - 2, slot).wait()

    outbuf[slot] = inbuf[slot].astype(BF16)
    write_back(i, slot).start()

    @pl.when(i == n_steps - 1)
    def _():
        write_back(i, slot).wait()

        @pl.when(i >= 1)
        def _():
            write_back(i - 1, 1 - slot).wait()


def _with_cast_job(kernel_fn, n_in, n_out):
    def wrapped(*refs):
        ins, src = refs[:n_in], refs[n_in]
        outs, dst = refs[n_in + 1:n_in + 1 + n_out], refs[n_in + 1 + n_out]
        rest = refs[n_in + 2 + n_out:]
        scratch, cast_scratch = rest[:-4], rest[-4:]
        _cast_slab(pl.program_id(0), pl.num_programs(0), src, dst, *cast_scratch)
        kernel_fn(*ins, *outs, *scratch)

    return wrapped


def _cast_job_extras(src, n_steps):
    assert src.shape[0] % n_steps == 0
    rows, cols = src.shape[0] // n_steps, src.shape[1]
    scratch = [pltpu.VMEM((2, rows, cols), F32), pltpu.VMEM((2, rows, cols), BF16),
               pltpu.SemaphoreType.DMA((2,)), pltpu.SemaphoreType.DMA((2,))]
    any_spec = pl.BlockSpec(memory_space=pl.ANY)
    return any_spec, any_spec, jax.ShapeDtypeStruct(src.shape, BF16), scratch


def _norm_matmul_kernel(x_ref, g_ref, w_ref, b_ref, o_ref, xn_ref):
    @pl.when(pl.program_id(1) == 0)
    def _():
        xn_ref[...] = _rms_rows(x_ref[...], g_ref[...]).astype(BF16)

    acc = jnp.dot(xn_ref[...], w_ref[...], preferred_element_type=F32)
    o_ref[...] = (acc + b_ref[...]).astype(o_ref.dtype)


def _norm_matmul(x, g, w, b, *, tm, tn, out_dtype, name):
    m, k = x.shape
    n = w.shape[1]
    return pl.pallas_call(
        _norm_matmul_kernel,
        grid=(m // tm, n // tn),
        in_specs=[
            pl.BlockSpec((tm, k), lambda i, j: (i, 0)),
            pl.BlockSpec((1, k), lambda i, j: (0, 0)),
            pl.BlockSpec((k, tn), lambda i, j: (0, j)),
            pl.BlockSpec((1, tn), lambda i, j: (0, j)),
        ],
        out_specs=pl.BlockSpec((tm, tn), lambda i, j: (i, j)),
        out_shape=jax.ShapeDtypeStruct((m, n), out_dtype),
        scratch_shapes=[pltpu.VMEM((tm, k), BF16)],
        compiler_params=_params(("parallel", "arbitrary")),
        name=name,
    )(x, g, w, b)


def _in_proj_kernel(x_ref, g_ref, w_ref, wdt_ref, cast_src, o_ref, odt_ref, cast_dst,
                    xn_ref, inbuf, outbuf, in_sem, out_sem, *, cast_cols):
    i = pl.program_id(0)
    j = pl.program_id(1)

    @pl.when(j == 0)
    def _():
        xn_ref[...] = _rms_rows(x_ref[...], g_ref[...]).astype(BF16)
        odt_ref[...] = jnp.dot(xn_ref[...], wdt_ref[...], preferred_element_type=F32)

    @pl.when(j < cast_cols)
    def _():
        _cast_slab(i * cast_cols + j, pl.num_programs(0) * cast_cols, cast_src, cast_dst,
                   inbuf, outbuf, in_sem, out_sem)

    o_ref[...] = jnp.dot(xn_ref[...], w_ref[...], preferred_element_type=F32).astype(o_ref.dtype)


def _in_proj(x, g, w, wdt, cast_src, *, tm, tn, n):
    m, k = x.shape
    assert n % tn == 0 and n <= w.shape[1]
    ndt = wdt.shape[1]
    n_rows, n_cols = m // tm, n // tn
    cast_cols = max(s for s in range(1, n_cols + 1)
                    if cast_src.shape[0] % (n_rows * s) == 0 and (cast_src.shape[0] // (n_rows * s)) % 16 == 0)
    cast_in, cast_out, cast_shape, cast_scratch = _cast_job_extras(cast_src, n_rows * cast_cols)
    return pl.pallas_call(
        functools.partial(_in_proj_kernel, cast_cols=cast_cols),
        grid=(n_rows, n_cols),
        in_specs=[
            pl.BlockSpec((tm, k), lambda i, j: (i, 0)),
            pl.BlockSpec((1, k), lambda i, j: (0, 0)),
            pl.BlockSpec((k, tn), lambda i, j: (0, j)),
            pl.BlockSpec((k, ndt), lambda i, j: (0, 0)),
            cast_in,
        ],
        out_specs=[pl.BlockSpec((tm, tn), lambda i, j: (i, j)), pl.BlockSpec((tm, ndt), lambda i, j: (i, 0)), cast_out],
        out_shape=[jax.ShapeDtypeStruct((m, n), BF16), jax.ShapeDtypeStruct((m, ndt), F32), cast_shape],
        scratch_shapes=[pltpu.VMEM((tm, k), BF16)] + cast_scratch,
        compiler_params=_params(("arbitrary", "arbitrary")),
        name="in_proj",
    )(x, g, w, wdt, cast_src)


def _matmul_residual_kernel(*refs, n_in):
    a_refs, w_refs = refs[:n_in], refs[n_in:2 * n_in]
    res_ref, b_ref, o_ref = refs[2 * n_in:]
    acc = res_ref[...] + b_ref[...]
    for a_ref, w_ref in zip(a_refs, w_refs):
        acc = acc + jnp.dot(a_ref[...], w_ref[...], preferred_element_type=F32)
    o_ref[...] = acc


def _matmul_residual(a_list, w_list, res, b, *, tm, tn, name):
    m, n = res.shape
    n_in = len(a_list)
    in_specs = [pl.BlockSpec((tm, a.shape[1]), lambda j, i: (i, 0)) for a in a_list]
    in_specs += [pl.BlockSpec((w.shape[0], tn), lambda j, i: (0, j)) for w in w_list]
    in_specs += [pl.BlockSpec((tm, tn), lambda j, i: (i, j)), pl.BlockSpec((1, tn), lambda j, i: (0, j))]
    return pl.pallas_call(
        functools.partial(_matmul_residual_kernel, n_in=n_in),
        grid=(n // tn, m // tm),
        in_specs=in_specs,
        out_specs=pl.BlockSpec((tm, tn), lambda j, i: (i, j)),
        out_shape=jax.ShapeDtypeStruct((m, n), F32),
        compiler_params=_params(("parallel", "parallel")),
        name=name,
    )(*a_list, *w_list, res, b)


def _swiglu_kernel(x_ref, g_ref, wg_ref, wu_ref, wd_ref, o_ref, xn_ref):
    @pl.when(pl.program_id(1) == 0)
    def _():
        x = x_ref[...]
        xn_ref[...] = _rms_rows(x, g_ref[...]).astype(BF16)
        o_ref[...] = x

    xn = xn_ref[...]
    gate = jnp.dot(xn, wg_ref[...], preferred_element_type=F32)
    up = jnp.dot(xn, wu_ref[...], preferred_element_type=F32)
    hid = (_silu(gate) * up).astype(BF16)
    o_ref[...] += jnp.dot(hid, wd_ref[...], preferred_element_type=F32)


def _swiglu(x, g, wg, wu, wd, *, tm, tf):
    m, d = x.shape
    f = wg.shape[1]
    return pl.pallas_call(
        _swiglu_kernel,
        grid=(m // tm, f // tf),
        in_specs=[
            pl.BlockSpec((tm, d), lambda i, j: (i, 0)),
            pl.BlockSpec((1, d), lambda i, j: (0, 0)),
            pl.BlockSpec((d, tf), lambda i, j: (0, j)),
            pl.BlockSpec((d, tf), lambda i, j: (0, j)),
            pl.BlockSpec((tf, d), lambda i, j: (j, 0)),
        ],
        out_specs=pl.BlockSpec((tm, d), lambda i, j: (i, 0)),
        out_shape=jax.ShapeDtypeStruct((m, d), F32),
        scratch_shapes=[pltpu.VMEM((tm, d), BF16)],
        compiler_params=_params(("parallel", "arbitrary")),
        name="swiglu",
    )(x, g, wg, wu, wd)


CONV_HALO = 32
CONV_ROWS = 64


def _conv_module_kernel(val_ref, gate_ref, pval_ref, pgate_ref, w_ref, b_ref, lg_ref, lb_ref, o_ref,
                        ubuf, cbuf, *, tt):
    i = pl.program_id(0)

    def glu(v, g):
        return v.astype(F32) * _sigmoid(g.astype(F32))

    ubuf[0:CONV_HALO, :] = jnp.where(i > 0, glu(pval_ref[...], pgate_ref[...]), 0.0)
    ubuf[CONV_HALO:CONV_HALO + tt, :] = glu(val_ref[...], gate_ref[...])

    off = CONV_HALO - (CONV_WIDTH - 1)
    taps = [[] for _ in range(SUBLANES)]
    for j in range(CONV_WIDTH):
        a, r = divmod(off + j, SUBLANES)
        taps[r].append((a, j))
    n_ch = ubuf.shape[1]
    for cb in range(n_ch // LANES):
        ls = slice(cb * LANES, (cb + 1) * LANES)
        for r0 in range(0, tt, CONV_ROWS):
            big = ubuf[r0:r0 + CONV_ROWS + CONV_HALO, ls]
            out = None
            for r in range(SUBLANES):
                rows = CONV_ROWS if r == 0 else CONV_ROWS + SUBLANES
                q = None
                for a, j in taps[r]:
                    term = w_ref[j:j + 1, ls] * big[a * SUBLANES:a * SUBLANES + rows, :]
                    q = term if q is None else q + term
                if r:
                    q = pltpu.roll(q, rows - r, axis=0)[:CONV_ROWS, :]
                out = q if out is None else out + q
            cbuf[r0:r0 + CONV_ROWS, ls] = out

    c = cbuf[...] + b_ref[...]
    mu = jnp.mean(c, axis=-1, keepdims=True)
    d = c - mu
    var = jnp.mean(d * d, axis=-1, keepdims=True)
    y = d * lax.rsqrt(var + NORM_EPS) * lg_ref[...] + lb_ref[...]
    o_ref[...] = _silu(y).astype(o_ref.dtype)


def _conv_module(proj, conv_w, conv_b, ln_g, ln_b, cast_src, *, tt):
    m = proj.shape[0]
    c = D_MODEL
    hb = tt // CONV_HALO
    cast_in, cast_out, cast_shape, cast_scratch = _cast_job_extras(cast_src, m // tt)
    return pl.pallas_call(
        _with_cast_job(functools.partial(_conv_module_kernel, tt=tt), 8, 1),
        grid=(m // tt,),
        in_specs=[
            pl.BlockSpec((tt, c), lambda i: (i, 0)),
            pl.BlockSpec((tt, c), lambda i: (i, 1)),
            pl.BlockSpec((CONV_HALO, c), lambda i: (jnp.maximum(i * hb - 1, 0), 0)),
            pl.BlockSpec((CONV_HALO, c), lambda i: (jnp.maximum(i * hb - 1, 0), 1)),
            pl.BlockSpec((CONV_WIDTH, c), lambda i: (0, 0)),
            pl.BlockSpec((1, c), lambda i: (0, 0)),
            pl.BlockSpec((1, c), lambda i: (0, 0)),
            pl.BlockSpec((1, c), lambda i: (0, 0)),
            cast_in,
        ],
        out_specs=[pl.BlockSpec((tt, c), lambda i: (i, 0)), cast_out],
        out_shape=[jax.ShapeDtypeStruct((m, c), BF16), cast_shape],
        scratch_shapes=[pltpu.VMEM((CONV_HALO + tt, c), F32), pltpu.VMEM((tt, c), F32)] + cast_scratch,
        compiler_params=_params(("arbitrary",)),
        name="conv_module",
    )(proj, proj, proj, proj, conv_w, conv_b, ln_g, ln_b, cast_src)


SSM_HALO = 16


def _conv4_silu(cur_ref, prev_ref, shift_ref, w_ref, b_ref, cols, first):
    q = cur_ref.shape[0]
    cur = cur_ref[:, cols]
    prev = prev_ref[:, cols]
    xb = jnp.concatenate([jnp.where(first, jnp.zeros_like(prev), prev), cur], axis=0)
    back = jnp.dot(shift_ref[...], xb, preferred_element_type=F32)
    last = SSM_CONV_WIDTH - 1
    acc = b_ref[:, cols] + w_ref[last:last + 1, cols] * cur.astype(F32)
    for k in range(1, SSM_CONV_WIDTH):
        acc = acc + w_ref[last - k:last - k + 1, cols] * back[(k - 1) * q:k * q, :]
    return _silu(acc)


def _ssd_kernel(z_ref, x_ref, b_ref, c_ref, px_ref, pb_ref, pc_ref, dtr_ref,
                wx_ref, wb_ref, wc_ref, bx_ref, bb_ref, bc_ref,
                dtb_ref, a_ref, dskip_ref, ng_ref, e_ref, sh_ref, o_ref,
                state, ybuf):
    i = pl.program_id(0)
    q = SSM_CHUNK
    first = i == 0

    @pl.when(first)
    def _():
        state[...] = jnp.zeros_like(state)

    everything = slice(None)
    xs = _conv4_silu(x_ref, px_ref, sh_ref, wx_ref, bx_ref, everything, first)
    bm = _conv4_silu(b_ref, pb_ref, sh_ref, wb_ref, bb_ref, everything, first)
    cm = _conv4_silu(c_ref, pc_ref, sh_ref, wc_ref, bc_ref, everything, first)

    pre = dtr_ref[...] + dtb_ref[...]
    dt = jnp.maximum(pre, 0.0) + jnp.log(1.0 + jnp.exp(-jnp.abs(pre)))
    dta = dt * a_ref[...]
    row = lax.broadcasted_iota(jnp.int32, (q, q), 0)
    col = lax.broadcasted_iota(jnp.int32, (q, q), 1)
    causal = row >= col
    acs = jnp.dot(causal.astype(F32), dta, precision=lax.Precision.HIGHEST, preferred_element_type=F32)
    acs_t = acs.T
    dt_t = dt.T
    eacs = jnp.exp(acs)
    wdec = dt * jnp.exp(acs[q - 1:q, :] - acs)

    def split(v):
        hi = v.astype(BF16)
        return hi, (v - hi.astype(F32)).astype(BF16)

    parts = jnp.concatenate(split(eacs) + split(wdec), axis=0)
    wide = jnp.dot(parts, e_ref[...], preferred_element_type=F32)
    eacs_full = wide[0:q] + wide[q:2 * q]
    xd = (xs * (wide[2 * q:3 * q] + wide[3 * q:4 * q])).astype(BF16)

    xs_b = xs.astype(BF16)
    lane = lax.broadcasted_iota(jnp.int32, (1, xs.shape[1]), 1)
    low_head = (lane % LANES) < SSM_HEAD_DIM
    xs_lo = jnp.where(low_head, xs_b, jnp.zeros_like(xs_b))
    xs_hi = jnp.where(low_head, jnp.zeros_like(xs_b), xs_b)

    gw = SSM_HEAD_DIM * (SSM_HEADS // SSM_GROUPS)
    for g in range(SSM_GROUPS):
        ns = slice(g * SSM_STATE, (g + 1) * SSM_STATE)
        gs = slice(g * gw, (g + 1) * gw)
        bg_f = bm[:, ns]
        bg = bg_f.astype(BF16)
        cg = cm[:, ns].astype(BF16)
        cb = lax.dot_general(cg, bg, (((1,), (1,)), ((), ())), preferred_element_type=F32)
        prev = state[g]
        y_off = jnp.dot(cg, prev.astype(BF16), preferred_element_type=F32)
        for c in range(gw // LANES):
            cs = slice(g * gw + c * LANES, g * gw + (c + 1) * LANES)
            acc = y_off[:, c * LANES:(c + 1) * LANES] * eacs_full[:, cs]
            for par, xpart in ((0, xs_lo), (1, xs_hi)):
                h = g * (SSM_HEADS // SSM_GROUPS) + 2 * c + par
                seg = acs[:, h:h + 1] - acs_t[h:h + 1, :]
                mmat = cb * jnp.exp(jnp.where(causal, seg, NEG_INF)) * dt_t[h:h + 1, :]
                acc = acc + jnp.dot(mmat.astype(BF16), xpart[:, cs], preferred_element_type=F32)
            ybuf[:, cs] = acc
        new_states = jnp.dot(bg_f.T.astype(BF16), xd[:, gs], preferred_element_type=F32)
        state[g] = prev * eacs_full[q - 1:q, gs] + new_states

    y = ybuf[...] + xs * dskip_ref[...]
    z = z_ref[...].astype(F32)
    y = y * _silu(z)
    for g in range(SSM_GROUPS):
        gs = slice(g * gw, (g + 1) * gw)
        yg = y[:, gs]
        ms = jnp.mean(yg * yg, axis=-1, keepdims=True)
        o_ref[:, gs] = (yg * lax.rsqrt(ms + NORM_EPS) * ng_ref[:, gs]).astype(o_ref.dtype)


def _ssd(proj, dt_raw, conv_w, conv_b, dt_bias, a_neg, d_skip, norm_g, cast_src):
    m = proj.shape[0]
    q = SSM_CHUNK
    inner = D_MODEL
    bc = SSM_GROUPS * SSM_STATE
    hb = q // SSM_HALO
    wx, wb, wc = conv_w[:, :inner], conv_w[:, inner:inner + bc], conv_w[:, inner + bc:]
    bx, bb, bcc = conv_b[:, :inner], conv_b[:, inner:inner + bc], conv_b[:, inner + bc:]
    expand = jnp.asarray(np.arange(LANES)[:, None] == (np.arange(inner) // SSM_HEAD_DIM)[None, :], BF16)
    t_idx = np.arange(q)
    shift_np = np.zeros((SSM_CONV_WIDTH - 1, q, SSM_HALO + q), np.float32)
    for k in range(1, SSM_CONV_WIDTH):
        shift_np[k - 1, t_idx, SSM_HALO + t_idx - k] = 1.0
    shift = jnp.asarray(shift_np.reshape(-1, SSM_HALO + q), BF16)
    prev = lambda i: jnp.maximum(i * hb - 1, 0)
    full = lambda shape: pl.BlockSpec(shape, lambda i: (0,) * len(shape))
    cast_in, cast_out, cast_shape, cast_scratch = _cast_job_extras(cast_src, m // q)
    return pl.pallas_call(
        _with_cast_job(_ssd_kernel, 20, 1),
        grid=(m // q,),
        in_specs=[
            pl.BlockSpec((q, inner), lambda i: (i, 2)),
            pl.BlockSpec((q, inner), lambda i: (i, 3)),
            pl.BlockSpec((q, bc), lambda i: (i, 8)),
            pl.BlockSpec((q, bc), lambda i: (i, 9)),
            pl.BlockSpec((SSM_HALO, inner), lambda i: (prev(i), 3)),
            pl.BlockSpec((SSM_HALO, bc), lambda i: (prev(i), 8)),
            pl.BlockSpec((SSM_HALO, bc), lambda i: (prev(i), 9)),
            pl.BlockSpec((q, LANES), lambda i: (i, 0)),
            full(wx.shape), full(wb.shape), full(wc.shape),
            full(bx.shape), full(bb.shape), full(bcc.shape),
            full((1, LANES)), full((1, LANES)), full((1, inner)), full((1, inner)),
            full((LANES, inner)),
            full(shift.shape),
            cast_in,
        ],
        out_specs=[pl.BlockSpec((q, inner), lambda i: (i, 0)), cast_out],
        out_shape=[jax.ShapeDtypeStruct((m, inner), BF16), cast_shape],
        scratch_shapes=[
            pltpu.VMEM((SSM_GROUPS, SSM_STATE, inner // SSM_GROUPS), F32),
            pltpu.VMEM((q, inner), F32),
        ] + cast_scratch,
        compiler_params=_params(("arbitrary",)),
        name="ssd",
    )(proj, proj, proj, proj, proj, proj, proj, dt_raw, wx, wb, wc, bx, bb, bcc,
      dt_bias, a_neg, d_skip, norm_g, expand, shift, cast_src)


def _attn_kernel(sinks_ref, q_ref, kvc_ref, kvp_ref, bias_ref, o_ref):
    blk = ATTN_BLOCK
    kvw = ATTN_KV_HEADS * ATTN_HEAD_DIM
    rep = ATTN_Q_HEADS // ATTN_KV_HEADS
    kv = jnp.concatenate([kvp_ref[...], kvc_ref[...]], axis=0).astype(F32)
    lane = lax.broadcasted_iota(jnp.int32, (1, LANES), 1)
    low = lane < ATTN_HEAD_DIM
    scale = ATTN_HEAD_DIM ** -0.5

    for g in range(ATTN_KV_HEADS):
        pc = g // 2
        kcol = kv[:, pc * LANES:(pc + 1) * LANES] * scale
        vcol = kv[:, kvw + pc * LANES:kvw + (pc + 1) * LANES]
        kroll = pltpu.roll(kcol, ATTN_HEAD_DIM, axis=1)
        vroll = pltpu.roll(vcol, ATTN_HEAD_DIM, axis=1)
        if g % 2 == 0:
            k_lo, k_hi = jnp.where(low, kcol, 0.0), jnp.where(low, 0.0, kroll)
            v_lo, v_hi = jnp.where(low, vcol, 0.0), jnp.where(low, 0.0, vroll)
        else:
            k_lo, k_hi = jnp.where(low, kroll, 0.0), jnp.where(low, 0.0, kcol)
            v_lo, v_hi = jnp.where(low, vroll, 0.0), jnp.where(low, 0.0, vcol)
        k_lo, k_hi, v_lo, v_hi = (t.astype(BF16) for t in (k_lo, k_hi, v_lo, v_hi))

        ncol = rep // 2
        qs = jnp.concatenate([q_ref[:, (g * ncol + c) * LANES:(g * ncol + c + 1) * LANES] for c in range(ncol)], axis=0)
        nt = (((1,), (1,)), ((), ()))
        s_even = lax.dot_general(qs, k_lo, nt, preferred_element_type=F32)
        s_odd = lax.dot_general(qs, k_hi, nt, preferred_element_type=F32)
        for c in range(ncol):
            out = None
            for par, s_all, vv in ((0, s_even, v_lo), (1, s_odd, v_hi)):
                h = g * rep + 2 * c + par
                logits = s_all[c * blk:(c + 1) * blk, :] + bias_ref[0, h]
                sink = sinks_ref[h]
                mx = jnp.maximum(jnp.max(logits, axis=-1, keepdims=True), sink)
                p = jnp.exp(logits - mx)
                denom = jnp.sum(p, axis=-1, keepdims=True) + jnp.exp(sink - mx)
                o = jnp.dot(p.astype(BF16), vv, preferred_element_type=F32) * (1.0 / denom)
                out = o if out is None else out + o
            col = g * ncol + c
            o_ref[:, col * LANES:(col + 1) * LANES] = out.astype(o_ref.dtype)


def _t5_bucket_table():
    dist = np.arange(ATTN_BLOCK)[:, None] + ATTN_BLOCK - np.arange(2 * ATTN_BLOCK)[None, :]
    n = np.maximum(dist, 0)
    max_exact = REL_BUCKETS // 2
    nf = np.maximum(n, 1).astype(np.float32)
    large = max_exact + (np.log(nf / max_exact) / math.log(REL_MAX_DIST / max_exact)
                         * (REL_BUCKETS - max_exact)).astype(np.int32)
    large = np.minimum(large, REL_BUCKETS - 1)
    return dist, np.where(n < max_exact, n, large)


def _attention(qkv, sinks, rel_bias):
    m = qkv.shape[0]
    blk = ATTN_BLOCK
    qd = ATTN_Q_HEADS * ATTN_HEAD_DIM
    kvd = 2 * ATTN_KV_HEADS * ATTN_HEAD_DIM
    dist, bucket = _t5_bucket_table()
    visible = (dist >= 0) & (dist < blk)
    onehot = jnp.asarray(np.arange(REL_BUCKETS)[:, None] == bucket.reshape(1, -1), F32)
    bias = jnp.dot(rel_bias.astype(F32).T, onehot, precision=lax.Precision.HIGHEST).reshape(-1, blk, 2 * blk)
    general = jnp.where(visible[None], bias, NEG_INF)
    first = jnp.where((visible & (np.arange(2 * blk) >= blk)[None, :])[None], bias, NEG_INF)
    table = jnp.stack([first, general])
    return pl.pallas_call(
        _attn_kernel,
        grid=(m // blk,),
        in_specs=[
            pl.BlockSpec(memory_space=pltpu.SMEM),
            pl.BlockSpec((blk, qd), lambda n: (n, 0)),
            pl.BlockSpec((blk, kvd), lambda n: (n, qd // kvd)),
            pl.BlockSpec((blk, kvd), lambda n: (jnp.maximum(n - 1, 0), qd // kvd)),
            pl.BlockSpec((1,) + table.shape[1:], lambda n: (jnp.minimum(n, 1), 0, 0, 0)),
        ],
        out_specs=pl.BlockSpec((blk, qd), lambda n: (n, 0)),
        out_shape=jax.ShapeDtypeStruct((m, qd), BF16),
        compiler_params=_params(("arbitrary",)),
        name="swa_attention",
    )(sinks.astype(F32), qkv, qkv, qkv, table)


def _router_kernel(x_ref, g_ref, rw_ref, rb_ref, o_ref):
    xn = _rms_rows(x_ref[...], g_ref[...])
    x_hi = xn.astype(BF16)
    x_lo = (xn - x_hi.astype(F32)).astype(BF16)
    both = jnp.dot(x_hi, rw_ref[...], preferred_element_type=F32)
    logits = (both[:, :LANES] + both[:, LANES:]
              + jnp.dot(x_lo, rw_ref[:, :LANES], preferred_element_type=F32) + rb_ref[...])
    lane = lax.broadcasted_iota(jnp.int32, logits.shape, 1)
    m1 = jnp.max(logits, axis=-1, keepdims=True)
    i1 = jnp.min(jnp.where(logits == m1, lane, LANES), axis=-1, keepdims=True)
    rest = jnp.where(lane == i1, NEG_INF, logits)
    m2 = jnp.max(rest, axis=-1, keepdims=True)
    i2 = jnp.min(jnp.where(rest == m2, lane, LANES), axis=-1, keepdims=True)
    e = jnp.exp(m2 - m1)
    g1 = 1.0 / (1.0 + e)
    g2 = e * g1
    out = jnp.where(lane == 0, i1.astype(F32),
                    jnp.where(lane == 1, i2.astype(F32),
                              jnp.where(lane == 2, g1, jnp.where(lane == 3, g2, 0.0))))
    o_ref[...] = out


def _router(h, g, rw, rb, *, tm):
    m, d = h.shape
    rw_pad = jnp.zeros((d, LANES), F32).at[:, :N_EXPERTS].set(rw.astype(F32))
    rw_hi = rw_pad.astype(BF16)
    rw_pad = jnp.concatenate([rw_hi, (rw_pad - rw_hi.astype(F32)).astype(BF16)], axis=1)
    rb_pad = jnp.full((1, LANES), NEG_INF, F32).at[0, :N_EXPERTS].set(rb.astype(F32))
    return pl.pallas_call(
        _router_kernel,
        grid=(m // tm,),
        in_specs=[
            pl.BlockSpec((tm, d), lambda i: (i, 0)),
            pl.BlockSpec((1, d), lambda i: (0, 0)),
            pl.BlockSpec((d, 2 * LANES), lambda i: (0, 0)),
            pl.BlockSpec((1, LANES), lambda i: (0, 0)),
        ],
        out_specs=pl.BlockSpec((tm, LANES), lambda i: (i, 0)),
        out_shape=jax.ShapeDtypeStruct((m, LANES), F32),
        compiler_params=_params(("parallel",)),
        name="router",
    )(h, g, rw_pad, rb_pad)


def _moe_drain(h_hbm, xbuf, sem):
    for s in range(MOE_SUBS_PER_GROUP):
        rs = slice(s * MOE_SUB, (s + 1) * MOE_SUB)
        pltpu.make_async_copy(h_hbm.at[pl.ds(0, MOE_SUB)], xbuf.at[rs], sem).wait()
    slack = xbuf.shape[0] - MOE_GROUP
    pltpu.make_async_copy(h_hbm.at[pl.ds(0, slack)], xbuf.at[MOE_GROUP:, :], sem).wait()


def _moe_kernel(ge_ref, ns_ref, tok_ref, h_hbm, g_ref, wg_ref, wu_ref, wd_ref, o_ref,
                xbuf, xn_ref, sem, *, rps):
    grp = pl.program_id(0)
    j = pl.program_id(1)
    ns = ns_ref[grp]

    def row_copy(gi, r):
        tok = tok_ref[gi * MOE_GROUP + jnp.minimum(r, MOE_GROUP - 1)]
        return pltpu.make_async_copy(h_hbm.at[pl.ds(tok, 1)], xbuf.at[pl.ds(r, 1)], sem)

    def issue_step_rows(gi):
        for u in range(rps):
            row_copy(gi, j * rps + u).start()

    @pl.when(j == 0)
    def _():
        @pl.when(grp == 0)
        def _():
            def issue(r, carry):
                for u in range(MOE_UNROLL):
                    row_copy(0, r * MOE_UNROLL + u).start()
                return carry

            lax.fori_loop(0, xbuf.shape[0] // MOE_UNROLL, issue, 0)

        _moe_drain(h_hbm, xbuf, sem)
        for s in range(MOE_SUBS_PER_GROUP):
            rs = slice(s * MOE_SUB, (s + 1) * MOE_SUB)
            o_ref[rs, :] = jnp.zeros((MOE_SUB, o_ref.shape[1]), F32)

            @pl.when(s < ns)
            def _():
                xn_ref[rs, :] = _rms_rows(xbuf[rs, :], g_ref[...]).astype(BF16)

    def ffn(rs):
        xn = xn_ref[rs, :]
        gate = jnp.dot(xn, wg_ref[...].astype(BF16), preferred_element_type=F32)
        up = jnp.dot(xn, wu_ref[...].astype(BF16), preferred_element_type=F32)
        hid = (_silu(gate) * up).astype(BF16)
        o_ref[rs, :] += jnp.dot(hid, wd_ref[...].astype(BF16), preferred_element_type=F32)

    @pl.when(ns == MOE_SUBS_PER_GROUP)
    def _():
        issue_step_rows(grp + 1)
        ffn(slice(0, MOE_GROUP))

    @pl.when(ns < MOE_SUBS_PER_GROUP)
    def _():
        issue_step_rows(grp + 1)

    for s in range(MOE_SUBS_PER_GROUP - 1):
        @pl.when((s < ns) & (ns < MOE_SUBS_PER_GROUP))
        def _():
            ffn(slice(s * MOE_SUB, (s + 1) * MOE_SUB))

    @pl.when((grp == pl.num_programs(0) - 1) & (j == pl.num_programs(1) - 1))
    def _():
        _moe_drain(h_hbm, xbuf, sem)


def _moe_ffn(h, g, wg, wu, wd, group_e, group_ns, src_tok, *, tf):
    t, d = h.shape
    n_groups = group_e.shape[0]
    f = wg.shape[2]
    nj = f // tf
    rps = -(-MOE_GROUP // (nj * SUBLANES)) * SUBLANES
    assert src_tok.shape[0] == (n_groups + 1) * MOE_GROUP

    def jj(grp, j, ns):
        return jnp.where(ns[grp] > 0, j, nj - 1)

    return pl.pallas_call(
        functools.partial(_moe_kernel, rps=rps),
        grid_spec=pltpu.PrefetchScalarGridSpec(
            num_scalar_prefetch=3,
            grid=(n_groups, nj),
            in_specs=[
                pl.BlockSpec(memory_space=pl.ANY),
                pl.BlockSpec((1, d), lambda grp, j, ge, ns, tok: (0, 0)),
                pl.BlockSpec((None, d, tf), lambda grp, j, ge, ns, tok: (ge[grp], 0, jj(grp, j, ns))),
                pl.BlockSpec((None, d, tf), lambda grp, j, ge, ns, tok: (ge[grp], 0, jj(grp, j, ns))),
                pl.BlockSpec((None, tf, d), lambda grp, j, ge, ns, tok: (ge[grp], jj(grp, j, ns), 0)),
            ],
            out_specs=pl.BlockSpec((MOE_GROUP, d), lambda grp, j, ge, ns, tok: (grp, 0)),
            scratch_shapes=[
                pltpu.VMEM((rps * nj, d), F32),
                pltpu.VMEM((MOE_GROUP, d), BF16),
                pltpu.SemaphoreType.DMA(()),
            ],
        ),
        out_shape=jax.ShapeDtypeStruct((n_groups * MOE_GROUP, d), F32),
        compiler_params=_params(("arbitrary", "arbitrary")),
        name="moe_ffn",
    )(group_e, group_ns, src_tok, h, g, wg, wu, wd)


def _combine_kernel(pos_ref, h_ref, route_ref, fg_ref, rows_hbm, o_ref, buf_a, buf_b, sem, *, tc):
    i = pl.program_id(0)
    slot = i % 2

    def copies(blk, sl, r):
        pa = pos_ref[TOP_K * (blk * tc + r)]
        pb = pos_ref[TOP_K * (blk * tc + r) + 1]
        return (pltpu.make_async_copy(rows_hbm.at[pl.ds(pa, 1)], buf_a.at[sl, pl.ds(r, 1)], sem.at[sl]),
                pltpu.make_async_copy(rows_hbm.at[pl.ds(pb, 1)], buf_b.at[sl, pl.ds(r, 1)], sem.at[sl]))

    def issue_block(blk, sl):
        def issue(r, carry):
            ca, cb = copies(blk, sl, r)
            ca.start()
            cb.start()
            return carry

        lax.fori_loop(0, tc, issue, 0, unroll=8)

    @pl.when(i == 0)
    def _():
        issue_block(0, 0)

    @pl.when(i + 1 < pl.num_programs(0))
    def _():
        issue_block(i + 1, 1 - slot)

    def wait(r, carry):
        ca, cb = copies(i, slot, r)
        ca.wait()
        cb.wait()
        return carry

    lax.fori_loop(0, tc, wait, 0, unroll=8)
    route = route_ref[...]
    moe = route[:, 2:3] * buf_a[slot] + route[:, 3:4] * buf_b[slot]
    o_ref[...] = _rms_rows(h_ref[...] + moe, fg_ref[...])


def _combine(h, route, final_g, rows, pos, *, tc):
    t, d = h.shape
    return pl.pallas_call(
        functools.partial(_combine_kernel, tc=tc),
        grid_spec=pltpu.PrefetchScalarGridSpec(
            num_scalar_prefetch=1,
            grid=(t // tc,),
            in_specs=[
                pl.BlockSpec((tc, d), lambda i, pos: (i, 0)),
                pl.BlockSpec((tc, LANES), lambda i, pos: (i, 0)),
                pl.BlockSpec((1, d), lambda i, pos: (0, 0)),
                pl.BlockSpec(memory_space=pl.ANY),
            ],
            out_specs=pl.BlockSpec((tc, d), lambda i, pos: (i, 0)),
            scratch_shapes=[pltpu.VMEM((2, tc, d), F32), pltpu.VMEM((2, tc, d), F32), pltpu.SemaphoreType.DMA((2,))],
        ),
        out_shape=jax.ShapeDtypeStruct((t, d), F32),
        compiler_params=_params(("arbitrary",)),
        name="moe_combine",
    )(pos, h, route, final_g, rows)


def _moe_plan(route, n_tok):
    flat_e = route[:, :TOP_K].astype(jnp.int32).reshape(-1)
    onehot = (flat_e[:, None] == jnp.arange(N_EXPERTS, dtype=jnp.int32)[None, :]).astype(jnp.int32)
    csum = jnp.cumsum(onehot, axis=0)
    rank = jnp.sum(onehot * (csum - onehot), axis=1)
    counts = csum[-1]
    n_groups = (n_tok * TOP_K) // MOE_GROUP + N_EXPERTS
    groups_e = (counts + MOE_GROUP - 1) // MOE_GROUP
    group_end = jnp.cumsum(groups_e)
    group_start = group_end - groups_e
    pos = jnp.sum(onehot * (group_start * MOE_GROUP)[None, :], axis=1) + rank
    flat_tok = jnp.arange(n_tok * TOP_K, dtype=jnp.int32) // TOP_K
    src_tok = jnp.zeros(((n_groups + 1) * MOE_GROUP,), jnp.int32).at[pos].set(
        flat_tok, unique_indices=True, mode="promise_in_bounds")
    experts = jnp.arange(N_EXPERTS, dtype=jnp.int32)
    gidx = jnp.arange(n_groups, dtype=jnp.int32)
    used = gidx < group_end[-1]
    last_e = jnp.max(jnp.where(groups_e > 0, experts, 0))
    ge = jnp.sum((group_end[None, :] <= gidx[:, None]).astype(jnp.int32), axis=1)
    ge = jnp.where(used, ge, last_e).astype(jnp.int32)
    mine = (ge[:, None] == experts[None, :]).astype(jnp.int32)
    subs_e = (counts + MOE_SUB - 1) // MOE_SUB
    subs_g = jnp.sum(mine * subs_e[None, :], axis=1)
    start_g = jnp.sum(mine * group_start[None, :], axis=1)
    ns = jnp.clip(subs_g - MOE_SUBS_PER_GROUP * (gidx - start_g), 0, MOE_SUBS_PER_GROUP)
    ns = jnp.where(used, ns, 0).astype(jnp.int32)
    return ge, ns, src_tok, pos.astype(jnp.int32)


def kernel(x, mix_norm_g, ffn_norm_g, final_norm_g, w_in, conv_w, conv_b, conv_ln_g, conv_ln_b, ssm_conv_w, ssm_conv_b, dt_bias, a_log, d_skip, ssm_norm_g, w_out, ffn_w_gate, ffn_w_up, ffn_w_down, w_qkv, b_qkv, w_o, b_o, sinks, rel_bias, router_w, router_b, moe_w_gate, moe_w_up, moe_w_down):
    bsz, seq, d = x.shape
    assert bsz == 1 and d == D_MODEL and seq % 512 == 0
    m = seq
    h = x.reshape(m, d)
    row = lambda v: v.reshape(1, -1).astype(F32)
    tm = min(1024, m)

    main_w = 2 * D_MODEL + D_MODEL + (D_MODEL + 2 * SSM_GROUPS * SSM_STATE)
    w_main = w_in[0].astype(BF16)
    w_dt = jnp.zeros((d, LANES), F32).at[:, :SSM_HEADS].set(w_in[0][:, main_w:]).astype(BF16)
    g0 = row(mix_norm_g[0])
    n_exp, _, expert_dim = moe_w_gate[0].shape
    wg_f32 = moe_w_gate[0].reshape(n_exp * d, expert_dim)
    wu_f32 = moe_w_up[0].reshape(n_exp * d, expert_dim)
    wd_f32 = moe_w_down[0].reshape(n_exp * expert_dim, d)

    proj, dt_raw, wd_b = _in_proj(h, g0, w_main, w_dt, wd_f32, tm=tm, tn=1024, n=main_w)
    conv_out, wg_b = _conv_module(proj, conv_w[0].astype(F32), row(conv_b[0]), row(conv_ln_g[0]), row(conv_ln_b[0]),
                                  wg_f32, tt=128)

    pad_heads = lambda v: jnp.zeros((1, LANES), F32).at[0, :SSM_HEADS].set(v.astype(F32))
    ssm_out, wu_b = _ssd(proj, dt_raw, ssm_conv_w[0].astype(F32), row(ssm_conv_b[0]),
                         pad_heads(dt_bias[0]), pad_heads(-jnp.exp(a_log[0].astype(F32))),
                         row(jnp.repeat(d_skip[0].astype(F32), SSM_HEAD_DIM)), row(ssm_norm_g[0]), wu_f32)

    wo = w_out[0].astype(BF16)
    h = _matmul_residual([conv_out, ssm_out], [wo[:D_MODEL], wo[D_MODEL:]], h, jnp.zeros((1, d), F32),
                         tm=256, tn=d, name="out_proj")
    h = _swiglu(h, row(ffn_norm_g[0]), ffn_w_gate[0].astype(BF16), ffn_w_up[0].astype(BF16),
                ffn_w_down[0].astype(BF16), tm=tm, tf=512)

    qkv = _norm_matmul(h, row(mix_norm_g[1]), w_qkv[0].astype(BF16), row(b_qkv[0]), tm=tm, tn=1280,
                       out_dtype=BF16, name="qkv_proj")
    attn = _attention(qkv, sinks[0], rel_bias)
    h = _matmul_residual([attn], [w_o[0].astype(BF16)], h, row(b_o[0]), tm=min(512, m), tn=d, name="attn_out_proj")

    g1 = row(ffn_norm_g[1])
    route = _router(h, g1, router_w[0], router_b[0], tm=512)
    ge, ns, src_tok, pos = _moe_plan(route, m)
    rows = _moe_ffn(h, g1, wg_b.reshape(n_exp, d, expert_dim), wu_b.reshape(n_exp, d, expert_dim),
                    wd_b.reshape(n_exp, expert_dim, d), ge, ns, src_tok, tf=512)
    out = _combine(h, route, row(final_norm_g), rows, pos, tc=256)
    return out.reshape(bsz, seq, d)
```

```python
import functools
import math

import numpy as np
import jax
import jax.numpy as jnp
from jax import lax
from jax.experimental import pallas as pl
from jax.experimental.pallas import tpu as pltpu

F32 = jnp.float32
BF16 = jnp.bfloat16
NORM_EPS = 1e-5
NEG_INF = float("-inf")

LANES = 128
SUBLANES = 8
VMEM_LIMIT = 56 << 20

D_MODEL = 2048
CONV_WIDTH = 31
SSM_HEADS = 32
SSM_HEAD_DIM = 64
SSM_GROUPS = 8
SSM_STATE = 128
SSM_CONV_WIDTH = 4
SSM_CHUNK = 128
ATTN_Q_HEADS = 32
ATTN_KV_HEADS = 4
ATTN_HEAD_DIM = 64
ATTN_BLOCK = 128
REL_BUCKETS = 32
REL_MAX_DIST = 128
N_EXPERTS = 8
TOP_K = 2
MOE_SUB = 256
MOE_SUBS_PER_GROUP = 4
MOE_GROUP = MOE_SUB * MOE_SUBS_PER_GROUP
MOE_UNROLL = 8


def _params(sem):
    return pltpu.CompilerParams(dimension_semantics=sem, vmem_limit_bytes=VMEM_LIMIT)


def _sigmoid(x):
    return 0.5 + 0.5 * jnp.tanh(0.5 * x)


def _silu(x):
    half = 0.5 * x
    return half + half * jnp.tanh(half)


def _rms_rows(x, g):
    ms = jnp.mean(x * x, axis=-1, keepdims=True)
    return x * lax.rsqrt(ms + NORM_EPS) * g


def _cast_slab(i, n_steps, src_hbm, dst_hbm, inbuf, outbuf, in_sem, out_sem):
    rows = inbuf.shape[1]
    slot = i % 2

    def fetch(step, sl):
        return pltpu.make_async_copy(src_hbm.at[pl.ds(step * rows, rows)], inbuf.at[sl], in_sem.at[sl])

    def write_back(step, sl):
        return pltpu.make_async_copy(outbuf.at[sl], dst_hbm.at[pl.ds(step * rows, rows)], out_sem.at[sl])

    @pl.when(i == 0)
    def _():
        fetch(0, 0).start()

    @pl.when(i + 1 < n_steps)
    def _():
        fetch(i + 1, 1 - slot).start()

    fetch(i, slot).wait()

    @pl.when(i >= 2)
    def _():
        write_back(i - 2, slot).wait()

    outbuf[slot] = inbuf[slot].astype(BF16)
    write_back(i, slot).start()

    @pl.when(i == n_steps - 1)
    def _():
        write_back(i, slot).wait()

        @pl.when(i >= 1)
        def _():
            write_back(i - 1, 1 - slot).wait()


def _with_cast_job(kernel_fn, n_in, n_out):
    def wrapped(*refs):
        ins, src = refs[:n_in], refs[n_in]
        outs, dst = refs[n_in + 1:n_in + 1 + n_out], refs[n_in + 1 + n_out]
        rest = refs[n_in + 2 + n_out:]
        scratch, cast_scratch = rest[:-4], rest[-4:]
        _cast_slab(pl.program_id(0), pl.num_programs(0), src, dst, *cast_scratch)
        kernel_fn(*ins, *outs, *scratch)

    return wrapped


def _cast_job_extras(src, n_steps):
    assert src.shape[0] % n_steps == 0
    rows, cols = src.shape[0] // n_steps, src.shape[1]
    scratch = [pltpu.VMEM((2, rows, cols), F32), pltpu.VMEM((2, rows, cols), BF16),
               pltpu.SemaphoreType.DMA((2,)), pltpu.SemaphoreType.DMA((2,))]
    any_spec = pl.BlockSpec(memory_space=pl.ANY)
    return any_spec, any_spec, jax.ShapeDtypeStruct(src.shape, BF16), scratch


def _norm_matmul_kernel(x_ref, g_ref, w_ref, b_ref, o_ref, xn_ref):
    @pl.when(pl.program_id(1) == 0)
    def _():
        xn_ref[...] = _rms_rows(x_ref[...], g_ref[...]).astype(BF16)

    acc = jnp.dot(xn_ref[...], w_ref[...], preferred_element_type=F32)
    o_ref[...] = (acc + b_ref[...]).astype(o_ref.dtype)


def _norm_matmul(x, g, w, b, *, tm, tn, out_dtype, name):
    m, k = x.shape
    n = w.shape[1]
    return pl.pallas_call(
        _norm_matmul_kernel,
        grid=(m // tm, n // tn),
        in_specs=[
            pl.BlockSpec((tm, k), lambda i, j: (i, 0)),
            pl.BlockSpec((1, k), lambda i, j: (0, 0)),
            pl.BlockSpec((k, tn), lambda i, j: (0, j)),
            pl.BlockSpec((1, tn), lambda i, j: (0, j)),
        ],
        out_specs=pl.BlockSpec((tm, tn), lambda i, j: (i, j)),
        out_shape=jax.ShapeDtypeStruct((m, n), out_dtype),
        scratch_shapes=[pltpu.VMEM((tm, k), BF16)],
        compiler_params=_params(("parallel", "arbitrary")),
        name=name,
    )(x, g, w, b)


def _in_proj_kernel(x_ref, g_ref, w_ref, wdt_ref, cast_src, o_ref, odt_ref, cast_dst,
                    xn_ref, inbuf, outbuf, in_sem, out_sem, *, cast_cols):
    i = pl.program_id(0)
    j = pl.program_id(1)

    @pl.when(j == 0)
    def _():
        xn_ref[...] = _rms_rows(x_ref[...], g_ref[...]).astype(BF16)
        odt_ref[...] = jnp.dot(xn_ref[...], wdt_ref[...], preferred_element_type=F32)

    @pl.when(j < cast_cols)
    def _():
        _cast_slab(i * cast_cols + j, pl.num_programs(0) * cast_cols, cast_src, cast_dst,
                   inbuf, outbuf, in_sem, out_sem)

    o_ref[...] = jnp.dot(xn_ref[...], w_ref[...], preferred_element_type=F32).astype(o_ref.dtype)


def _in_proj(x, g, w, wdt, cast_src, *, tm, tn, n):
    m, k = x.shape
    assert n % tn == 0 and n <= w.shape[1]
    ndt = wdt.shape[1]
    n_rows, n_cols = m // tm, n // tn
    cast_cols = max(s for s in range(1, n_cols + 1)
                    if cast_src.shape[0] % (n_rows * s) == 0 and (cast_src.shape[0] // (n_rows * s)) % 16 == 0)
    cast_in, cast_out, cast_shape, cast_scratch = _cast_job_extras(cast_src, n_rows * cast_cols)
    return pl.pallas_call(
        functools.partial(_in_proj_kernel, cast_cols=cast_cols),
        grid=(n_rows, n_cols),
        in_specs=[
            pl.BlockSpec((tm, k), lambda i, j: (i, 0)),
            pl.BlockSpec((1, k), lambda i, j: (0, 0)),
            pl.BlockSpec((k, tn), lambda i, j: (0, j)),
            pl.BlockSpec((k, ndt), lambda i, j: (0, 0)),
            cast_in,
        ],
        out_specs=[pl.BlockSpec((tm, tn), lambda i, j: (i, j)), pl.BlockSpec((tm, ndt), lambda i, j: (i, 0)), cast_out],
        out_shape=[jax.ShapeDtypeStruct((m, n), BF16), jax.ShapeDtypeStruct((m, ndt), F32), cast_shape],
        scratch_shapes=[pltpu.VMEM((tm, k), BF16)] + cast_scratch,
        compiler_params=_params(("arbitrary", "arbitrary")),
        name="in_proj",
    )(x, g, w, wdt, cast_src)


def _matmul_residual_kernel(*refs, n_in):
    a_refs, w_refs = refs[:n_in], refs[n_in:2 * n_in]
    res_ref, b_ref, o_ref = refs[2 * n_in:]
    acc = res_ref[...] + b_ref[...]
    for a_ref, w_ref in zip(a_refs, w_refs):
        acc = acc + jnp.dot(a_ref[...], w_ref[...], preferred_element_type=F32)
    o_ref[...] = acc


def _matmul_residual(a_list, w_list, res, b, *, tm, tn, name):
    m, n = res.shape
    n_in = len(a_list)
    in_specs = [pl.BlockSpec((tm, a.shape[1]), lambda j, i: (i, 0)) for a in a_list]
    in_specs += [pl.BlockSpec((w.shape[0], tn), lambda j, i: (0, j)) for w in w_list]
    in_specs += [pl.BlockSpec((tm, tn), lambda j, i: (i, j)), pl.BlockSpec((1, tn), lambda j, i: (0, j))]
    return pl.pallas_call(
        functools.partial(_matmul_residual_kernel, n_in=n_in),
        grid=(n // tn, m // tm),
        in_specs=in_specs,
        out_specs=pl.BlockSpec((tm, tn), lambda j, i: (i, j)),
        out_shape=jax.ShapeDtypeStruct((m, n), F32),
        compiler_params=_params(("parallel", "parallel")),
        name=name,
    )(*a_list, *w_list, res, b)


def _swiglu_kernel(x_ref, g_ref, wg_ref, wu_ref, wd_ref, o_ref, xn_ref):
    @pl.when(pl.program_id(1) == 0)
    def _():
        x = x_ref[...]
        xn_ref[...] = _rms_rows(x, g_ref[...]).astype(BF16)
        o_ref[...] = x

    xn = xn_ref[...]
    gate = jnp.dot(xn, wg_ref[...], preferred_element_type=F32)
    up = jnp.dot(xn, wu_ref[...], preferred_element_type=F32)
    hid = (_silu(gate) * up).astype(BF16)
    o_ref[...] += jnp.dot(hid, wd_ref[...], preferred_element_type=F32)


def _swiglu(x, g, wg, wu, wd, *, tm, tf):
    m, d = x.shape
    f = wg.shape[1]
    return pl.pallas_call(
        _swiglu_kernel,
        grid=(m // tm, f // tf),
        in_specs=[
            pl.BlockSpec((tm, d), lambda i, j: (i, 0)),
            pl.BlockSpec((1, d), lambda i, j: (0, 0)),
            pl.BlockSpec((d, tf), lambda i, j: (0, j)),
            pl.BlockSpec((d, tf), lambda i, j: (0, j)),
            pl.BlockSpec((tf, d), lambda i, j: (j, 0)),
        ],
        out_specs=pl.BlockSpec((tm, d), lambda i, j: (i, 0)),
        out_shape=jax.ShapeDtypeStruct((m, d), F32),
        scratch_shapes=[pltpu.VMEM((tm, d), BF16)],
        compiler_params=_params(("parallel", "arbitrary")),
        name="swiglu",
    )(x, g, wg, wu, wd)


CONV_HALO = 32
CONV_ROWS = 64


def _conv_module_kernel(val_ref, gate_ref, pval_ref, pgate_ref, w_ref, b_ref, lg_ref, lb_ref, o_ref,
                        ubuf, cbuf, *, tt):
    i = pl.program_id(0)

    def glu(v, g):
        return v.astype(F32) * _sigmoid(g.astype(F32))

    ubuf[0:CONV_HALO, :] = jnp.where(i > 0, glu(pval_ref[...], pgate_ref[...]), 0.0)
    ubuf[CONV_HALO:CONV_HALO + tt, :] = glu(val_ref[...], gate_ref[...])

    off = CONV_HALO - (CONV_WIDTH - 1)
    taps = [[] for _ in range(SUBLANES)]
    for j in range(CONV_WIDTH):
        a, r = divmod(off + j, SUBLANES)
        taps[r].append((a, j))
    n_ch = ubuf.shape[1]
    for cb in range(n_ch // LANES):
        ls = slice(cb * LANES, (cb + 1) * LANES)
        for r0 in range(0, tt, CONV_ROWS):
            big = ubuf[r0:r0 + CONV_ROWS + CONV_HALO, ls]
            out = None
            for r in range(SUBLANES):
                rows = CONV_ROWS if r == 0 else CONV_ROWS + SUBLANES
                q = None
                for a, j in taps[r]:
                    term = w_ref[j:j + 1, ls] * big[a * SUBLANES:a * SUBLANES + rows, :]
                    q = term if q is None else q + term
                if r:
                    q = pltpu.roll(q, rows - r, axis=0)[:CONV_ROWS, :]
                out = q if out is None else out + q
            cbuf[r0:r0 + CONV_ROWS, ls] = out

    c = cbuf[...] + b_ref[...]
    mu = jnp.mean(c, axis=-1, keepdims=True)
    d = c - mu
    var = jnp.mean(d * d, axis=-1, keepdims=True)
    y = d * lax.rsqrt(var + NORM_EPS) * lg_ref[...] + lb_ref[...]
    o_ref[...] = _silu(y).astype(o_ref.dtype)


def _conv_module(proj, conv_w, conv_b, ln_g, ln_b, cast_src, *, tt):
    m = proj.shape[0]
    c = D_MODEL
    hb = tt // CONV_HALO
    cast_in, cast_out, cast_shape, cast_scratch = _cast_job_extras(cast_src, m // tt)
    return pl.pallas_call(
        _with_cast_job(functools.partial(_conv_module_kernel, tt=tt), 8, 1),
        grid=(m // tt,),
        in_specs=[
            pl.BlockSpec((tt, c), lambda i: (i, 0)),
            pl.BlockSpec((tt, c), lambda i: (i, 1)),
            pl.BlockSpec((CONV_HALO, c), lambda i: (jnp.maximum(i * hb - 1, 0), 0)),
            pl.BlockSpec((CONV_HALO, c), lambda i: (jnp.maximum(i * hb - 1, 0), 1)),
            pl.BlockSpec((CONV_WIDTH, c), lambda i: (0, 0)),
            pl.BlockSpec((1, c), lambda i: (0, 0)),
            pl.BlockSpec((1, c), lambda i: (0, 0)),
            pl.BlockSpec((1, c), lambda i: (0, 0)),
            cast_in,
        ],
        out_specs=[pl.BlockSpec((tt, c), lambda i: (i, 0)), cast_out],
        out_shape=[jax.ShapeDtypeStruct((m, c), BF16), cast_shape],
        scratch_shapes=[pltpu.VMEM((CONV_HALO + tt, c), F32), pltpu.VMEM((tt, c), F32)] + cast_scratch,
        compiler_params=_params(("arbitrary",)),
        name="conv_module",
    )(proj, proj, proj, proj, conv_w, conv_b, ln_g, ln_b, cast_src)


SSM_HALO = 16


def _conv4_silu(cur_ref, prev_ref, shift_ref, w_ref, b_ref, cols, first):
    q = cur_ref.shape[0]
    cur = cur_ref[:, cols]
    prev = prev_ref[:, cols]
    xb = jnp.concatenate([jnp.where(first, jnp.zeros_like(prev), prev), cur], axis=0)
    back = jnp.dot(shift_ref[...], xb, preferred_element_type=F32)
    last = SSM_CONV_WIDTH - 1
    acc = b_ref[:, cols] + w_ref[last:last + 1, cols] * cur.astype(F32)
    for k in range(1, SSM_CONV_WIDTH):
        acc = acc + w_ref[last - k:last - k + 1, cols] * back[(k - 1) * q:k * q, :]
    return _silu(acc)


def _ssd_kernel(z_ref, x_ref, b_ref, c_ref, px_ref, pb_ref, pc_ref, dtr_ref,
                wx_ref, wb_ref, wc_ref, bx_ref, bb_ref, bc_ref,
                dtb_ref, a_ref, dskip_ref, ng_ref, e_ref, sh_ref, o_ref,
                state, ybuf):
    i = pl.program_id(0)
    q = SSM_CHUNK
    first = i == 0

    @pl.when(first)
    def _():
        state[...] = jnp.zeros_like(state)

    everything = slice(None)
    xs = _conv4_silu(x_ref, px_ref, sh_ref, wx_ref, bx_ref, everything, first)
    bm = _conv4_silu(b_ref, pb_ref, sh_ref, wb_ref, bb_ref, everything, first)
    cm = _conv4_silu(c_ref, pc_ref, sh_ref, wc_ref, bc_ref, everything, first)

    pre = dtr_ref[...] + dtb_ref[...]
    dt = jnp.maximum(pre, 0.0) + jnp.log(1.0 + jnp.exp(-jnp.abs(pre)))
    dta = dt * a_ref[...]
    row = lax.broadcasted_iota(jnp.int32, (q, q), 0)
    col = lax.broadcasted_iota(jnp.int32, (q, q), 1)
    causal = row >= col
    acs = jnp.dot(causal.astype(F32), dta, precision=lax.Precision.HIGHEST, preferred_element_type=F32)
    acs_t = acs.T
    dt_t = dt.T
    eacs = jnp.exp(acs)
    wdec = dt * jnp.exp(acs[q - 1:q, :] - acs)

    def split(v):
        hi = v.astype(BF16)
        return hi, (v - hi.astype(F32)).astype(BF16)

    parts = jnp.concatenate(split(eacs) + split(wdec), axis=0)
    wide = jnp.dot(parts, e_ref[...], preferred_element_type=F32)
    eacs_full = wide[0:q] + wide[q:2 * q]
    xd = (xs * (wide[2 * q:3 * q] + wide[3 * q:4 * q])).astype(BF16)

    xs_b = xs.astype(BF16)
    lane = lax.broadcasted_iota(jnp.int32, (1, xs.shape[1]), 1)
    low_head = (lane % LANES) < SSM_HEAD_DIM
    xs_lo = jnp.where(low_head, xs_b, jnp.zeros_like(xs_b))
    xs_hi = jnp.where(low_head, jnp.zeros_like(xs_b), xs_b)

    gw = SSM_HEAD_DIM * (SSM_HEADS // SSM_GROUPS)
    for g in range(SSM_GROUPS):
        ns = slice(g * SSM_STATE, (g + 1) * SSM_STATE)
        gs = slice(g * gw, (g + 1) * gw)
        bg_f = bm[:, ns]
        bg = bg_f.astype(BF16)
        cg = cm[:, ns].astype(BF16)
        cb = lax.dot_general(cg, bg, (((1,), (1,)), ((), ())), preferred_element_type=F32)
        prev = state[g]
        y_off = jnp.dot(cg, prev.astype(BF16), preferred_element_type=F32)
        for c in range(gw // LANES):
            cs = slice(g * gw + c * LANES, g * gw + (c + 1) * LANES)
            acc = y_off[:, c * LANES:(c + 1) * LANES] * eacs_full[:, cs]
            for par, xpart in ((0, xs_lo), (1, xs_hi)):
                h = g * (SSM_HEADS // SSM_GROUPS) + 2 * c + par
                seg = acs[:, h:h + 1] - acs_t[h:h + 1, :]
                mmat = cb * jnp.exp(jnp.where(causal, seg, NEG_INF)) * dt_t[h:h + 1, :]
                acc = acc + jnp.dot(mmat.astype(BF16), xpart[:, cs], preferred_element_type=F32)
            ybuf[:, cs] = acc
        new_states = jnp.dot(bg_f.T.astype(BF16), xd[:, gs], preferred_element_type=F32)
        state[g] = prev * eacs_full[q - 1:q, gs] + new_states

    y = ybuf[...] + xs * dskip_ref[...]
    z = z_ref[...].astype(F32)
    y = y * _silu(z)
    for g in range(SSM_GROUPS):
        gs = slice(g * gw, (g + 1) * gw)
        yg = y[:, gs]
        ms = jnp.mean(yg * yg, axis=-1, keepdims=True)
        o_ref[:, gs] = (yg * lax.rsqrt(ms + NORM_EPS) * ng_ref[:, gs]).astype(o_ref.dtype)


def _ssd(proj, dt_raw, conv_w, conv_b, dt_bias, a_neg, d_skip, norm_g, cast_src):
    m = proj.shape[0]
    q = SSM_CHUNK
    inner = D_MODEL
    bc = SSM_GROUPS * SSM_STATE
    hb = q // SSM_HALO
    wx, wb, wc = conv_w[:, :inner], conv_w[:, inner:inner + bc], conv_w[:, inner + bc:]
    bx, bb, bcc = conv_b[:, :inner], conv_b[:, inner:inner + bc], conv_b[:, inner + bc:]
    expand = jnp.asarray(np.arange(LANES)[:, None] == (np.arange(inner) // SSM_HEAD_DIM)[None, :], BF16)
    t_idx = np.arange(q)
    shift_np = np.zeros((SSM_CONV_WIDTH - 1, q, SSM_HALO + q), np.float32)
    for k in range(1, SSM_CONV_WIDTH):
        shift_np[k - 1, t_idx, SSM_HALO + t_idx - k] = 1.0
    shift = jnp.asarray(shift_np.reshape(-1, SSM_HALO + q), BF16)
    prev = lambda i: jnp.maximum(i * hb - 1, 0)
    full = lambda shape: pl.BlockSpec(shape, lambda i: (0,) * len(shape))
    cast_in, cast_out, cast_shape, cast_scratch = _cast_job_extras(cast_src, m // q)
    return pl.pallas_call(
        _with_cast_job(_ssd_kernel, 20, 1),
        grid=(m // q,),
        in_specs=[
            pl.BlockSpec((q, inner), lambda i: (i, 2)),
            pl.BlockSpec((q, inner), lambda i: (i, 3)),
            pl.BlockSpec((q, bc), lambda i: (i, 8)),
            pl.BlockSpec((q, bc), lambda i: (i, 9)),
            pl.BlockSpec((SSM_HALO, inner), lambda i: (prev(i), 3)),
            pl.BlockSpec((SSM_HALO, bc), lambda i: (prev(i), 8)),
            pl.BlockSpec((SSM_HALO, bc), lambda i: (prev(i), 9)),
            pl.BlockSpec((q, LANES), lambda i: (i, 0)),
            full(wx.shape), full(wb.shape), full(wc.shape),
            full(bx.shape), full(bb.shape), full(bcc.shape),
            full((1, LANES)), full((1, LANES)), full((1, inner)), full((1, inner)),
            full((LANES, inner)),
            full(shift.shape),
            cast_in,
        ],
        out_specs=[pl.BlockSpec((q, inner), lambda i: (i, 0)), cast_out],
        out_shape=[jax.ShapeDtypeStruct((m, inner), BF16), cast_shape],
        scratch_shapes=[
            pltpu.VMEM((SSM_GROUPS, SSM_STATE, inner // SSM_GROUPS), F32),
            pltpu.VMEM((q, inner), F32),
        ] + cast_scratch,
        compiler_params=_params(("arbitrary",)),
        name="ssd",
    )(proj, proj, proj, proj, proj, proj, proj, dt_raw, wx, wb, wc, bx, bb, bcc,
      dt_bias, a_neg, d_skip, norm_g, expand, shift, cast_src)


def _attn_kernel(sinks_ref, q_ref, kvc_ref, kvp_ref, bias_ref, o_ref):
    blk = ATTN_BLOCK
    kvw = ATTN_KV_HEADS * ATTN_HEAD_DIM
    rep = ATTN_Q_HEADS // ATTN_KV_HEADS
    kv = jnp.concatenate([kvp_ref[...], kvc_ref[...]], axis=0).astype(F32)
    lane = lax.broadcasted_iota(jnp.int32, (1, LANES), 1)
    low = lane < ATTN_HEAD_DIM
    scale = ATTN_HEAD_DIM ** -0.5

    for g in range(ATTN_KV_HEADS):
        pc = g // 2
        kcol = kv[:, pc * LANES:(pc + 1) * LANES] * scale
        vcol = kv[:, kvw + pc * LANES:kvw + (pc + 1) * LANES]
        kroll = pltpu.roll(kcol, ATTN_HEAD_DIM, axis=1)
        vroll = pltpu.roll(vcol, ATTN_HEAD_DIM, axis=1)
        if g % 2 == 0:
            k_lo, k_hi = jnp.where(low, kcol, 0.0), jnp.where(low, 0.0, kroll)
            v_lo, v_hi = jnp.where(low, vcol, 0.0), jnp.where(low, 0.0, vroll)
        else:
            k_lo, k_hi = jnp.where(low, kroll, 0.0), jnp.where(low, 0.0, kcol)
            v_lo, v_hi = jnp.where(low, vroll, 0.0), jnp.where(low, 0.0, vcol)
        k_lo, k_hi, v_lo, v_hi = (t.astype(BF16) for t in (k_lo, k_hi, v_lo, v_hi))

        ncol = rep // 2
        qs = jnp.concatenate([q_ref[:, (g * ncol + c) * LANES:(g * ncol + c + 1) * LANES] for c in range(ncol)], axis=0)
        nt = (((1,), (1,)), ((), ()))
        s_even = lax.dot_general(qs, k_lo, nt, preferred_element_type=F32)
        s_odd = lax.dot_general(qs, k_hi, nt, preferred_element_type=F32)
        for c in range(ncol):
            out = None
            for par, s_all, vv in ((0, s_even, v_lo), (1, s_odd, v_hi)):
                h = g * rep + 2 * c + par
                logits = s_all[c * blk:(c + 1) * blk, :] + bias_ref[0, h]
                sink = sinks_ref[h]
                mx = jnp.maximum(jnp.max(logits, axis=-1, keepdims=True), sink)
                p = jnp.exp(logits - mx)
                denom = jnp.sum(p, axis=-1, keepdims=True) + jnp.exp(sink - mx)
                o = jnp.dot(p.astype(BF16), vv, preferred_element_type=F32) * (1.0 / denom)
                out = o if out is None else out + o
            col = g * ncol + c
            o_ref[:, col * LANES:(col + 1) * LANES] = out.astype(o_ref.dtype)


def _t5_bucket_table():
    dist = np.arange(ATTN_BLOCK)[:, None] + ATTN_BLOCK - np.arange(2 * ATTN_BLOCK)[None, :]
    n = np.maximum(dist, 0)
    max_exact = REL_BUCKETS // 2
    nf = np.maximum(n, 1).astype(np.float32)
    large = max_exact + (np.log(nf / max_exact) / math.log(REL_MAX_DIST / max_exact)
                         * (REL_BUCKETS - max_exact)).astype(np.int32)
    large = np.minimum(large, REL_BUCKETS - 1)
    return dist, np.where(n < max_exact, n, large)


def _attention(qkv, sinks, rel_bias):
    m = qkv.shape[0]
    blk = ATTN_BLOCK
    qd = ATTN_Q_HEADS * ATTN_HEAD_DIM
    kvd = 2 * ATTN_KV_HEADS * ATTN_HEAD_DIM
    dist, bucket = _t5_bucket_table()
    visible = (dist >= 0) & (dist < blk)
    onehot = jnp.asarray(np.arange(REL_BUCKETS)[:, None] == bucket.reshape(1, -1), F32)
    bias = jnp.dot(rel_bias.astype(F32).T, onehot, precision=lax.Precision.HIGHEST).reshape(-1, blk, 2 * blk)
    general = jnp.where(visible[None], bias, NEG_INF)
    first = jnp.where((visible & (np.arange(2 * blk) >= blk)[None, :])[None], bias, NEG_INF)
    table = jnp.stack([first, general])
    return pl.pallas_call(
        _attn_kernel,
        grid=(m // blk,),
        in_specs=[
            pl.BlockSpec(memory_space=pltpu.SMEM),
            pl.BlockSpec((blk, qd), lambda n: (n, 0)),
            pl.BlockSpec((blk, kvd), lambda n: (n, qd // kvd)),
            pl.BlockSpec((blk, kvd), lambda n: (jnp.maximum(n - 1, 0), qd // kvd)),
            pl.BlockSpec((1,) + table.shape[1:], lambda n: (jnp.minimum(n, 1), 0, 0, 0)),
        ],
        out_specs=pl.BlockSpec((blk, qd), lambda n: (n, 0)),
        out_shape=jax.ShapeDtypeStruct((m, qd), BF16),
        compiler_params=_params(("arbitrary",)),
        name="swa_attention",
    )(sinks.astype(F32), qkv, qkv, qkv, table)


def _router_kernel(x_ref, g_ref, rw_ref, rb_ref, o_ref):
    xn = _rms_rows(x_ref[...], g_ref[...])
    x_hi = xn.astype(BF16)
    x_lo = (xn - x_hi.astype(F32)).astype(BF16)
    both = jnp.dot(x_hi, rw_ref[...], preferred_element_type=F32)
    logits = (both[:, :LANES] + both[:, LANES:]
              + jnp.dot(x_lo, rw_ref[:, :LANES], preferred_element_type=F32) + rb_ref[...])
    lane = lax.broadcasted_iota(jnp.int32, logits.shape, 1)
    m1 = jnp.max(logits, axis=-1, keepdims=True)
    i1 = jnp.min(jnp.where(logits == m1, lane, LANES), axis=-1, keepdims=True)
    rest = jnp.where(lane == i1, NEG_INF, logits)
    m2 = jnp.max(rest, axis=-1, keepdims=True)
    i2 = jnp.min(jnp.where(rest == m2, lane, LANES), axis=-1, keepdims=True)
    e = jnp.exp(m2 - m1)
    g1 = 1.0 / (1.0 + e)
    g2 = e * g1
    out = jnp.where(lane == 0, i1.astype(F32),
                    jnp.where(lane == 1, i2.astype(F32),
                              jnp.where(lane == 2, g1, jnp.where(lane == 3, g2, 0.0))))
    o_ref[...] = out


def _router(h, g, rw, rb, *, tm):
    m, d = h.shape
    rw_pad = jnp.zeros((d, LANES), F32).at[:, :N_EXPERTS].set(rw.astype(F32))
    rw_hi = rw_pad.astype(BF16)
    rw_pad = jnp.concatenate([rw_hi, (rw_pad - rw_hi.astype(F32)).astype(BF16)], axis=1)
    rb_pad = jnp.full((1, LANES), NEG_INF, F32).at[0, :N_EXPERTS].set(rb.astype(F32))
    return pl.pallas_call(
        _router_kernel,
        grid=(m // tm,),
        in_specs=[
            pl.BlockSpec((tm, d), lambda i: (i, 0)),
            pl.BlockSpec((1, d), lambda i: (0, 0)),
            pl.BlockSpec((d, 2 * LANES), lambda i: (0, 0)),
            pl.BlockSpec((1, LANES), lambda i: (0, 0)),
        ],
        out_specs=pl.BlockSpec((tm, LANES), lambda i: (i, 0)),
        out_shape=jax.ShapeDtypeStruct((m, LANES), F32),
        compiler_params=_params(("parallel",)),
        name="router",
    )(h, g, rw_pad, rb_pad)


def _moe_drain(h_hbm, xbuf, sem):
    for s in range(MOE_SUBS_PER_GROUP):
        rs = slice(s * MOE_SUB, (s + 1) * MOE_SUB)
        pltpu.make_async_copy(h_hbm.at[pl.ds(0, MOE_SUB)], xbuf.at[rs], sem).wait()
    slack = xbuf.shape[0] - MOE_GROUP
    pltpu.make_async_copy(h_hbm.at[pl.ds(0, slack)], xbuf.at[MOE_GROUP:, :], sem).wait()


def _moe_kernel(ge_ref, ns_ref, tok_ref, h_hbm, g_ref, wg_ref, wu_ref, wd_ref, o_ref,
                xbuf, xn_ref, sem, *, rps):
    grp = pl.program_id(0)
    j = pl.program_id(1)
    ns = ns_ref[grp]

    def row_copy(gi, r):
        tok = tok_ref[gi * MOE_GROUP + jnp.minimum(r, MOE_GROUP - 1)]
        return pltpu.make_async_copy(h_hbm.at[pl.ds(tok, 1)], xbuf.at[pl.ds(r, 1)], sem)

    def issue_step_rows(gi):
        for u in range(rps):
            row_copy(gi, j * rps + u).start()

    @pl.when(j == 0)
    def _():
        @pl.when(grp == 0)
        def _():
            def issue(r, carry):
                for u in range(MOE_UNROLL):
                    row_copy(0, r * MOE_UNROLL + u).start()
                return carry

            lax.fori_loop(0, xbuf.shape[0] // MOE_UNROLL, issue, 0)

        _moe_drain(h_hbm, xbuf, sem)
        for s in range(MOE_SUBS_PER_GROUP):
            rs = slice(s * MOE_SUB, (s + 1) * MOE_SUB)
            o_ref[rs, :] = jnp.zeros((MOE_SUB, o_ref.shape[1]), F32)

            @pl.when(s < ns)
            def _():
                xn_ref[rs, :] = _rms_rows(xbuf[rs, :], g_ref[...]).astype(BF16)

    def ffn(rs):
        xn = xn_ref[rs, :]
        gate = jnp.dot(xn, wg_ref[...].astype(BF16), preferred_element_type=F32)
        up = jnp.dot(xn, wu_ref[...].astype(BF16), preferred_element_type=F32)
        hid = (_silu(gate) * up).astype(BF16)
        o_ref[rs, :] += jnp.dot(hid, wd_ref[...].astype(BF16), preferred_element_type=F32)

    @pl.when(ns == MOE_SUBS_PER_GROUP)
    def _():
        issue_step_rows(grp + 1)
        ffn(slice(0, MOE_GROUP))

    @pl.when(ns < MOE_SUBS_PER_GROUP)
    def _():
        issue_step_rows(grp + 1)

    for s in range(MOE_SUBS_PER_GROUP - 1):
        @pl.when((s < ns) & (ns < MOE_SUBS_PER_GROUP))
        def _():
            ffn(slice(s * MOE_SUB, (s + 1) * MOE_SUB))

    @pl.when((grp == pl.num_programs(0) - 1) & (j == pl.num_programs(1) - 1))
    def _():
        _moe_drain(h_hbm, xbuf, sem)


def _moe_ffn(h, g, wg, wu, wd, group_e, group_ns, src_tok, *, tf):
    t, d = h.shape
    n_groups = group_e.shape[0]
    f = wg.shape[2]
    nj = f // tf
    rps = -(-MOE_GROUP // (nj * SUBLANES)) * SUBLANES
    assert src_tok.shape[0] == (n_groups + 1) * MOE_GROUP

    def jj(grp, j, ns):
        return jnp.where(ns[grp] > 0, j, nj - 1)

    return pl.pallas_call(
        functools.partial(_moe_kernel, rps=rps),
        grid_spec=pltpu.PrefetchScalarGridSpec(
            num_scalar_prefetch=3,
            grid=(n_groups, nj),
            in_specs=[
                pl.BlockSpec(memory_space=pl.ANY),
                pl.BlockSpec((1, d), lambda grp, j, ge, ns, tok: (0, 0)),
                pl.BlockSpec((None, d, tf), lambda grp, j, ge, ns, tok: (ge[grp], 0, jj(grp, j, ns))),
                pl.BlockSpec((None, d, tf), lambda grp, j, ge, ns, tok: (ge[grp], 0, jj(grp, j, ns))),
                pl.BlockSpec((None, tf, d), lambda grp, j, ge, ns, tok: (ge[grp], jj(grp, j, ns), 0)),
            ],
            out_specs=pl.BlockSpec((MOE_GROUP, d), lambda grp, j, ge, ns, tok: (grp, 0)),
            scratch_shapes=[
                pltpu.VMEM((rps * nj, d), F32),
                pltpu.VMEM((MOE_GROUP, d), BF16),
                pltpu.SemaphoreType.DMA(()),
            ],
        ),
        out_shape=jax.ShapeDtypeStruct((n_groups * MOE_GROUP, d), F32),
        compiler_params=_params(("arbitrary", "arbitrary")),
        name="moe_ffn",
    )(group_e, group_ns, src_tok, h, g, wg, wu, wd)


def _combine_kernel(pos_ref, h_ref, route_ref, fg_ref, rows_hbm, o_ref, buf_a, buf_b, sem, *, tc):
    i = pl.program_id(0)
    slot = i % 2

    def copies(blk, sl, r):
        pa = pos_ref[TOP_K * (blk * tc + r)]
        pb = pos_ref[TOP_K * (blk * tc + r) + 1]
        return (pltpu.make_async_copy(rows_hbm.at[pl.ds(pa, 1)], buf_a.at[sl, pl.ds(r, 1)], sem.at[sl]),
                pltpu.make_async_copy(rows_hbm.at[pl.ds(pb, 1)], buf_b.at[sl, pl.ds(r, 1)], sem.at[sl]))

    def issue_block(blk, sl):
        def issue(r, carry):
            ca, cb = copies(blk, sl, r)
            ca.start()
            cb.start()
            return carry

        lax.fori_loop(0, tc, issue, 0, unroll=8)

    def wait_block(blk, sl):
        def wait(r, carry):
            ca, cb = copies(blk, sl, r)
            ca.wait()
            cb.wait()
            return carry

        lax.fori_loop(0, tc, wait, 0, unroll=8)

    @pl.when(i == 0)
    def _():
        issue_block(0, 0)

    wait_block(i, slot)

    last = pl.num_programs(0) - 1
    nxt = jnp.minimum(i + 1, last)
    for r in range(tc):
        ca, cb = copies(nxt, 1 - slot, r)
        ca.start()
        cb.start()
    route = route_ref[...]
    moe = route[:, 2:3] * buf_a[slot] + route[:, 3:4] * buf_b[slot]
    o_ref[...] = _rms_rows(h_ref[...] + moe, fg_ref[...])

    @pl.when(i == last)
    def _():
        wait_block(nxt, 1 - slot)


def _combine(h, route, final_g, rows, pos, *, tc):
    t, d = h.shape
    return pl.pallas_call(
        functools.partial(_combine_kernel, tc=tc),
        grid_spec=pltpu.PrefetchScalarGridSpec(
            num_scalar_prefetch=1,
            grid=(t // tc,),
            in_specs=[
                pl.BlockSpec((tc, d), lambda i, pos: (i, 0)),
                pl.BlockSpec((tc, LANES), lambda i, pos: (i, 0)),
                pl.BlockSpec((1, d), lambda i, pos: (0, 0)),
                pl.BlockSpec(memory_space=pl.ANY),
            ],
            out_specs=pl.BlockSpec((tc, d), lambda i, pos: (i, 0)),
            scratch_shapes=[pltpu.VMEM((2, tc, d), F32), pltpu.VMEM((2, tc, d), F32), pltpu.SemaphoreType.DMA((2,))],
        ),
        out_shape=jax.ShapeDtypeStruct((t, d), F32),
        compiler_params=_params(("arbitrary",)),
        name="moe_combine",
    )(pos, h, route, final_g, rows)


def _moe_plan(route, n_tok):
    flat_e = route[:, :TOP_K].astype(jnp.int32).reshape(-1)
    onehot = (flat_e[:, None] == jnp.arange(N_EXPERTS, dtype=jnp.int32)[None, :]).astype(jnp.int32)
    csum = jnp.cumsum(onehot, axis=0)
    rank = jnp.sum(onehot * (csum - onehot), axis=1)
    counts = csum[-1]
    n_groups = (n_tok * TOP_K) // MOE_GROUP + N_EXPERTS
    groups_e = (counts + MOE_GROUP - 1) // MOE_GROUP
    group_end = jnp.cumsum(groups_e)
    group_start = group_end - groups_e
    pos = jnp.sum(onehot * (group_start * MOE_GROUP)[None, :], axis=1) + rank
    flat_tok = jnp.arange(n_tok * TOP_K, dtype=jnp.int32) // TOP_K
    src_tok = jnp.zeros(((n_groups + 1) * MOE_GROUP,), jnp.int32).at[pos].set(
        flat_tok, unique_indices=True, mode="promise_in_bounds")
    experts = jnp.arange(N_EXPERTS, dtype=jnp.int32)
    gidx = jnp.arange(n_groups, dtype=jnp.int32)
    used = gidx < group_end[-1]
    last_e = jnp.max(jnp.where(groups_e > 0, experts, 0))
    ge = jnp.sum((group_end[None, :] <= gidx[:, None]).astype(jnp.int32), axis=1)
    ge = jnp.where(used, ge, last_e).astype(jnp.int32)
    mine = (ge[:, None] == experts[None, :]).astype(jnp.int32)
    subs_e = (counts + MOE_SUB - 1) // MOE_SUB
    subs_g = jnp.sum(mine * subs_e[None, :], axis=1)
    start_g = jnp.sum(mine * group_start[None, :], axis=1)
    ns = jnp.clip(subs_g - MOE_SUBS_PER_GROUP * (gidx - start_g), 0, MOE_SUBS_PER_GROUP)
    ns = jnp.where(used, ns, 0).astype(jnp.int32)
    return ge, ns, src_tok, pos.astype(jnp.int32)


def kernel(x, mix_norm_g, ffn_norm_g, final_norm_g, w_in, conv_w, conv_b, conv_ln_g, conv_ln_b, ssm_conv_w, ssm_conv_b, dt_bias, a_log, d_skip, ssm_norm_g, w_out, ffn_w_gate, ffn_w_up, ffn_w_down, w_qkv, b_qkv, w_o, b_o, sinks, rel_bias, router_w, router_b, moe_w_gate, moe_w_up, moe_w_down):
    bsz, seq, d = x.shape
    assert bsz == 1 and d == D_MODEL and seq % 512 == 0
    m = seq
    h = x.reshape(m, d)
    row = lambda v: v.reshape(1, -1).astype(F32)
    tm = min(1024, m)

    main_w = 2 * D_MODEL + D_MODEL + (D_MODEL + 2 * SSM_GROUPS * SSM_STATE)
    w_main = w_in[0].astype(BF16)
    w_dt = jnp.zeros((d, LANES), F32).at[:, :SSM_HEADS].set(w_in[0][:, main_w:]).astype(BF16)
    g0 = row(mix_norm_g[0])
    n_exp, _, expert_dim = moe_w_gate[0].shape
    wg_f32 = moe_w_gate[0].reshape(n_exp * d, expert_dim)
    wu_f32 = moe_w_up[0].reshape(n_exp * d, expert_dim)
    wd_f32 = moe_w_down[0].reshape(n_exp * expert_dim, d)

    proj, dt_raw, wd_b = _in_proj(h, g0, w_main, w_dt, wd_f32, tm=tm, tn=1024, n=main_w)
    conv_out, wg_b = _conv_module(proj, conv_w[0].astype(F32), row(conv_b[0]), row(conv_ln_g[0]), row(conv_ln_b[0]),
                                  wg_f32, tt=128)

    pad_heads = lambda v: jnp.zeros((1, LANES), F32).at[0, :SSM_HEADS].set(v.astype(F32))
    ssm_out, wu_b = _ssd(proj, dt_raw, ssm_conv_w[0].astype(F32), row(ssm_conv_b[0]),
                         pad_heads(dt_bias[0]), pad_heads(-jnp.exp(a_log[0].astype(F32))),
                         row(jnp.repeat(d_skip[0].astype(F32), SSM_HEAD_DIM)), row(ssm_norm_g[0]), wu_f32)

    wo = w_out[0].astype(BF16)
    h = _matmul_residual([conv_out, ssm_out], [wo[:D_MODEL], wo[D_MODEL:]], h, jnp.zeros((1, d), F32),
                         tm=256, tn=d, name="out_proj")
    h = _swiglu(h, row(ffn_norm_g[0]), ffn_w_gate[0].astype(BF16), ffn_w_up[0].astype(BF16),
                ffn_w_down[0].astype(BF16), tm=tm, tf=512)

    qkv = _norm_matmul(h, row(mix_norm_g[1]), w_qkv[0].astype(BF16), row(b_qkv[0]), tm=min(512, m), tn=w_qkv.shape[2],
                       out_dtype=BF16, name="qkv_proj")
    attn = _attention(qkv, sinks[0], rel_bias)
    h = _matmul_residual([attn], [w_o[0].astype(BF16)], h, row(b_o[0]), tm=min(512, m), tn=d, name="attn_out_proj")

    g1 = row(ffn_norm_g[1])
    route = _router(h, g1, router_w[0], router_b[0], tm=512)
    ge, ns, src_tok, pos = _moe_plan(route, m)
    rows = _moe_ffn(h, g1, wg_b.reshape(n_exp, d, expert_dim), wu_b.reshape(n_exp, d, expert_dim),
                    wd_b.reshape(n_exp, expert_dim, d), ge, ns, src_tok, tf=512)
    out = _combine(h, route, row(final_norm_g), rows, pos, tc=256)
    return out.reshape(bsz, seq, d)
```

```python
import functools
import math

import numpy as np
import jax
import jax.numpy as jnp
from jax import lax
from jax.experimental import pallas as pl
from jax.experimental.pallas import tpu as pltpu

F32 = jnp.float32
BF16 = jnp.bfloat16
NORM_EPS = 1e-5
NEG_INF = float("-inf")

LANES = 128
SUBLANES = 8
VMEM_LIMIT = 56 << 20

D_MODEL = 2048
CONV_WIDTH = 31
SSM_HEADS = 32
SSM_HEAD_DIM = 64
SSM_GROUPS = 8
SSM_STATE = 128
SSM_CONV_WIDTH = 4
SSM_CHUNK = 128
ATTN_Q_HEADS = 32
ATTN_KV_HEADS = 4
ATTN_HEAD_DIM = 64
ATTN_BLOCK = 128
ATTN_SUBBLOCKS = 2
REL_BUCKETS = 32
REL_MAX_DIST = 128
N_EXPERTS = 8
TOP_K = 2
MOE_SUB = 256
MOE_SUBS_PER_GROUP = 4
MOE_GROUP = MOE_SUB * MOE_SUBS_PER_GROUP
MOE_UNROLL = 8


def _params(sem):
    return pltpu.CompilerParams(dimension_semantics=sem, vmem_limit_bytes=VMEM_LIMIT)


def _sigmoid(x):
    return 0.5 + 0.5 * jnp.tanh(0.5 * x)


def _silu(x):
    half = 0.5 * x
    return half + half * jnp.tanh(half)


def _rms_rows(x, g):
    ms = jnp.mean(x * x, axis=-1, keepdims=True)
    return x * lax.rsqrt(ms + NORM_EPS) * g


def _cast_slab(i, n_steps, src_hbm, dst_hbm, inbuf, outbuf, in_sem, out_sem):
    rows = inbuf.shape[1]
    slot = i % 2

    def fetch(step, sl):
        return pltpu.make_async_copy(src_hbm.at[pl.ds(step * rows, rows)], inbuf.at[sl], in_sem.at[sl])

    def write_back(step, sl):
        return pltpu.make_async_copy(outbuf.at[sl], dst_hbm.at[pl.ds(step * rows, rows)], out_sem.at[sl])

    @pl.when(i == 0)
    def _():
        fetch(0, 0).start()

    @pl.when(i + 1 < n_steps)
    def _():
        fetch(i + 1, 1 - slot).start()

    fetch(i, slot).wait()

    @pl.when(i >= 2)
    def _():
        write_back(i - 2, slot).wait()

    outbuf[slot] = inbuf[slot].astype(BF16)
    write_back(i, slot).start()

    @pl.when(i == n_steps - 1)
    def _():
        write_back(i, slot).wait()

        @pl.when(i >= 1)
        def _():
            write_back(i - 1, 1 - slot).wait()


def _with_cast_job(kernel_fn, n_in, n_out):
    def wrapped(*refs):
        ins, src = refs[:n_in], refs[n_in]
        outs, dst = refs[n_in + 1:n_in + 1 + n_out], refs[n_in + 1 + n_out]
        rest = refs[n_in + 2 + n_out:]
        scratch, cast_scratch = rest[:-4], rest[-4:]
        _cast_slab(pl.program_id(0), pl.num_programs(0), src, dst, *cast_scratch)
        kernel_fn(*ins, *outs, *scratch)

    return wrapped


def _cast_job_extras(src, n_steps):
    assert src.shape[0] % n_steps == 0
    rows, cols = src.shape[0] // n_steps, src.shape[1]
    scratch = [pltpu.VMEM((2, rows, cols), F32), pltpu.VMEM((2, rows, cols), BF16),
               pltpu.SemaphoreType.DMA((2,)), pltpu.SemaphoreType.DMA((2,))]
    any_spec = pl.BlockSpec(memory_space=pl.ANY)
    return any_spec, any_spec, jax.ShapeDtypeStruct(src.shape, BF16), scratch


def _norm_matmul_kernel(x_ref, g_ref, w_ref, b_ref, o_ref, xn_ref):
    @pl.when(pl.program_id(1) == 0)
    def _():
        xn_ref[...] = _rms_rows(x_ref[...], g_ref[...]).astype(BF16)

    acc = jnp.dot(xn_ref[...], w_ref[...], preferred_element_type=F32)
    o_ref[...] = (acc + b_ref[...]).astype(o_ref.dtype)


def _norm_matmul(x, g, w, b, *, tm, tn, out_dtype, name):
    m, k = x.shape
    n = w.shape[1]
    return pl.pallas_call(
        _norm_matmul_kernel,
        grid=(m // tm, n // tn),
        in_specs=[
            pl.BlockSpec((tm, k), lambda i, j: (i, 0)),
            pl.BlockSpec((1, k), lambda i, j: (0, 0)),
            pl.BlockSpec((k, tn), lambda i, j: (0, j)),
            pl.BlockSpec((1, tn), lambda i, j: (0, j)),
        ],
        out_specs=pl.BlockSpec((tm, tn), lambda i, j: (i, j)),
        out_shape=jax.ShapeDtypeStruct((m, n), out_dtype),
        scratch_shapes=[pltpu.VMEM((tm, k), BF16)],
        compiler_params=_params(("parallel", "arbitrary")),
        name=name,
    )(x, g, w, b)


def _in_proj_kernel(x_ref, g_ref, w_ref, wdt_ref, cast_src, o_ref, odt_ref, cast_dst,
                    xn_ref, inbuf, outbuf, in_sem, out_sem, *, cast_cols):
    i = pl.program_id(0)
    j = pl.program_id(1)

    @pl.when(j == 0)
    def _():
        xn_ref[...] = _rms_rows(x_ref[...], g_ref[...]).astype(BF16)
        odt_ref[...] = jnp.dot(xn_ref[...], wdt_ref[...], preferred_element_type=F32)

    @pl.when(j < cast_cols)
    def _():
        _cast_slab(i * cast_cols + j, pl.num_programs(0) * cast_cols, cast_src, cast_dst,
                   inbuf, outbuf, in_sem, out_sem)

    o_ref[...] = jnp.dot(xn_ref[...], w_ref[...], preferred_element_type=F32).astype(o_ref.dtype)


def _in_proj(x, g, w, wdt, cast_src, *, tm, tn, n):
    m, k = x.shape
    assert n % tn == 0 and n <= w.shape[1]
    ndt = wdt.shape[1]
    n_rows, n_cols = m // tm, n // tn
    cast_cols = max(s for s in range(1, n_cols + 1)
                    if cast_src.shape[0] % (n_rows * s) == 0 and (cast_src.shape[0] // (n_rows * s)) % 16 == 0)
    cast_in, cast_out, cast_shape, cast_scratch = _cast_job_extras(cast_src, n_rows * cast_cols)
    return pl.pallas_call(
        functools.partial(_in_proj_kernel, cast_cols=cast_cols),
        grid=(n_rows, n_cols),
        in_specs=[
            pl.BlockSpec((tm, k), lambda i, j: (i, 0)),
            pl.BlockSpec((1, k), lambda i, j: (0, 0)),
            pl.BlockSpec((k, tn), lambda i, j: (0, j)),
            pl.BlockSpec((k, ndt), lambda i, j: (0, 0)),
            cast_in,
        ],
        out_specs=[pl.BlockSpec((tm, tn), lambda i, j: (i, j)), pl.BlockSpec((tm, ndt), lambda i, j: (i, 0)), cast_out],
        out_shape=[jax.ShapeDtypeStruct((m, n), BF16), jax.ShapeDtypeStruct((m, ndt), F32), cast_shape],
        scratch_shapes=[pltpu.VMEM((tm, k), BF16)] + cast_scratch,
        compiler_params=_params(("arbitrary", "arbitrary")),
        name="in_proj",
    )(x, g, w, wdt, cast_src)


def _matmul_residual_kernel(*refs, n_in):
    a_refs, w_refs = refs[:n_in], refs[n_in:2 * n_in]
    res_ref, b_ref, o_ref = refs[2 * n_in:]
    acc = res_ref[...] + b_ref[...]
    for a_ref, w_ref in zip(a_refs, w_refs):
        acc = acc + jnp.dot(a_ref[...], w_ref[...], preferred_element_type=F32)
    o_ref[...] = acc


def _matmul_residual(a_list, w_list, res, b, *, tm, tn, name):
    m, n = res.shape
    n_in = len(a_list)
    in_specs = [pl.BlockSpec((tm, a.shape[1]), lambda j, i: (i, 0)) for a in a_list]
    in_specs += [pl.BlockSpec((w.shape[0], tn), lambda j, i: (0, j)) for w in w_list]
    in_specs += [pl.BlockSpec((tm, tn), lambda j, i: (i, j)), pl.BlockSpec((1, tn), lambda j, i: (0, j))]
    return pl.pallas_call(
        functools.partial(_matmul_residual_kernel, n_in=n_in),
        grid=(n // tn, m // tm),
        in_specs=in_specs,
        out_specs=pl.BlockSpec((tm, tn), lambda j, i: (i, j)),
        out_shape=jax.ShapeDtypeStruct((m, n), F32),
        compiler_params=_params(("parallel", "parallel")),
        name=name,
    )(*a_list, *w_list, res, b)


def _swiglu_kernel(x_ref, g_ref, wg_ref, wu_ref, wd_ref, o_ref, xn_ref):
    @pl.when(pl.program_id(1) == 0)
    def _():
        x = x_ref[...]
        xn_ref[...] = _rms_rows(x, g_ref[...]).astype(BF16)
        o_ref[...] = x

    xn = xn_ref[...]
    gate = jnp.dot(xn, wg_ref[...], preferred_element_type=F32)
    up = jnp.dot(xn, wu_ref[...], preferred_element_type=F32)
    hid = (_silu(gate) * up).astype(BF16)
    o_ref[...] += jnp.dot(hid, wd_ref[...], preferred_element_type=F32)


def _swiglu(x, g, wg, wu, wd, *, tm, tf):
    m, d = x.shape
    f = wg.shape[1]
    return pl.pallas_call(
        _swiglu_kernel,
        grid=(m // tm, f // tf),
        in_specs=[
            pl.BlockSpec((tm, d), lambda i, j: (i, 0)),
            pl.BlockSpec((1, d), lambda i, j: (0, 0)),
            pl.BlockSpec((d, tf), lambda i, j: (0, j)),
            pl.BlockSpec((d, tf), lambda i, j: (0, j)),
            pl.BlockSpec((tf, d), lambda i, j: (j, 0)),
        ],
        out_specs=pl.BlockSpec((tm, d), lambda i, j: (i, 0)),
        out_shape=jax.ShapeDtypeStruct((m, d), F32),
        scratch_shapes=[pltpu.VMEM((tm, d), BF16)],
        compiler_params=_params(("parallel", "arbitrary")),
        name="swiglu",
    )(x, g, wg, wu, wd)


CONV_HALO = 32
CONV_ROWS = 64


def _conv_module_kernel(val_ref, gate_ref, pval_ref, pgate_ref, w_ref, b_ref, lg_ref, lb_ref, o_ref,
                        ubuf, cbuf, *, tt):
    i = pl.program_id(0)

    def glu(v, g):
        return v.astype(F32) * _sigmoid(g.astype(F32))

    ubuf[0:CONV_HALO, :] = jnp.where(i > 0, glu(pval_ref[...], pgate_ref[...]), 0.0)
    ubuf[CONV_HALO:CONV_HALO + tt, :] = glu(val_ref[...], gate_ref[...])

    off = CONV_HALO - (CONV_WIDTH - 1)
    taps = [[] for _ in range(SUBLANES)]
    for j in range(CONV_WIDTH):
        a, r = divmod(off + j, SUBLANES)
        taps[r].append((a, j))
    n_ch = ubuf.shape[1]
    for cb in range(n_ch // LANES):
        ls = slice(cb * LANES, (cb + 1) * LANES)
        for r0 in range(0, tt, CONV_ROWS):
            big = ubuf[r0:r0 + CONV_ROWS + CONV_HALO, ls]
            out = None
            for r in range(SUBLANES):
                rows = CONV_ROWS if r == 0 else CONV_ROWS + SUBLANES
                q = None
                for a, j in taps[r]:
                    term = w_ref[j:j + 1, ls] * big[a * SUBLANES:a * SUBLANES + rows, :]
                    q = term if q is None else q + term
                if r:
                    q = pltpu.roll(q, rows - r, axis=0)[:CONV_ROWS, :]
                out = q if out is None else out + q
            cbuf[r0:r0 + CONV_ROWS, ls] = out

    c = cbuf[...] + b_ref[...]
    mu = jnp.mean(c, axis=-1, keepdims=True)
    d = c - mu
    var = jnp.mean(d * d, axis=-1, keepdims=True)
    y = d * lax.rsqrt(var + NORM_EPS) * lg_ref[...] + lb_ref[...]
    o_ref[...] = _silu(y).astype(o_ref.dtype)


def _conv_module(proj, conv_w, conv_b, ln_g, ln_b, cast_src, *, tt):
    m = proj.shape[0]
    c = D_MODEL
    hb = tt // CONV_HALO
    cast_in, cast_out, cast_shape, cast_scratch = _cast_job_extras(cast_src, m // tt)
    return pl.pallas_call(
        _with_cast_job(functools.partial(_conv_module_kernel, tt=tt), 8, 1),
        grid=(m // tt,),
        in_specs=[
            pl.BlockSpec((tt, c), lambda i: (i, 0)),
            pl.BlockSpec((tt, c), lambda i: (i, 1)),
            pl.BlockSpec((CONV_HALO, c), lambda i: (jnp.maximum(i * hb - 1, 0), 0)),
            pl.BlockSpec((CONV_HALO, c), lambda i: (jnp.maximum(i * hb - 1, 0), 1)),
            pl.BlockSpec((CONV_WIDTH, c), lambda i: (0, 0)),
            pl.BlockSpec((1, c), lambda i: (0, 0)),
            pl.BlockSpec((1, c), lambda i: (0, 0)),
            pl.BlockSpec((1, c), lambda i: (0, 0)),
            cast_in,
        ],
        out_specs=[pl.BlockSpec((tt, c), lambda i: (i, 0)), cast_out],
        out_shape=[jax.ShapeDtypeStruct((m, c), BF16), cast_shape],
        scratch_shapes=[pltpu.VMEM((CONV_HALO + tt, c), F32), pltpu.VMEM((tt, c), F32)] + cast_scratch,
        compiler_params=_params(("arbitrary",)),
        name="conv_module",
    )(proj, proj, proj, proj, conv_w, conv_b, ln_g, ln_b, cast_src)


SSM_HALO = 16


def _conv4_silu(cur_ref, prev_ref, shift_ref, w_ref, b_ref, cols, first):
    q = cur_ref.shape[0]
    cur = cur_ref[:, cols]
    prev = prev_ref[:, cols]
    xb = jnp.concatenate([jnp.where(first, jnp.zeros_like(prev), prev), cur], axis=0)
    back = jnp.dot(shift_ref[...], xb, preferred_element_type=F32)
    last = SSM_CONV_WIDTH - 1
    acc = b_ref[:, cols] + w_ref[last:last + 1, cols] * cur.astype(F32)
    for k in range(1, SSM_CONV_WIDTH):
        acc = acc + w_ref[last - k:last - k + 1, cols] * back[(k - 1) * q:k * q, :]
    return _silu(acc)


def _ssd_kernel(z_ref, x_ref, b_ref, c_ref, px_ref, pb_ref, pc_ref, dtr_ref,
                wx_ref, wb_ref, wc_ref, bx_ref, bb_ref, bc_ref,
                dtb_ref, a_ref, dskip_ref, ng_ref, e_ref, sh_ref, o_ref,
                state, ybuf):
    i = pl.program_id(0)
    q = SSM_CHUNK
    first = i == 0

    @pl.when(first)
    def _():
        state[...] = jnp.zeros_like(state)

    everything = slice(None)
    xs = _conv4_silu(x_ref, px_ref, sh_ref, wx_ref, bx_ref, everything, first)
    bm = _conv4_silu(b_ref, pb_ref, sh_ref, wb_ref, bb_ref, everything, first)
    cm = _conv4_silu(c_ref, pc_ref, sh_ref, wc_ref, bc_ref, everything, first)

    pre = dtr_ref[...] + dtb_ref[...]
    dt = jnp.maximum(pre, 0.0) + jnp.log(1.0 + jnp.exp(-jnp.abs(pre)))
    dta = dt * a_ref[...]
    row = lax.broadcasted_iota(jnp.int32, (q, q), 0)
    col = lax.broadcasted_iota(jnp.int32, (q, q), 1)
    causal = row >= col
    acs = jnp.dot(causal.astype(F32), dta, precision=lax.Precision.HIGHEST, preferred_element_type=F32)
    acs_t = acs.T
    dt_t = dt.T
    eacs = jnp.exp(acs)
    wdec = dt * jnp.exp(acs[q - 1:q, :] - acs)

    def split(v):
        hi = v.astype(BF16)
        return hi, (v - hi.astype(F32)).astype(BF16)

    parts = jnp.concatenate(split(eacs) + split(wdec), axis=0)
    wide = jnp.dot(parts, e_ref[...], preferred_element_type=F32)
    eacs_full = wide[0:q] + wide[q:2 * q]
    xd = (xs * (wide[2 * q:3 * q] + wide[3 * q:4 * q])).astype(BF16)

    xs_b = xs.astype(BF16)
    lane = lax.broadcasted_iota(jnp.int32, (1, xs.shape[1]), 1)
    low_head = (lane % LANES) < SSM_HEAD_DIM
    xs_lo = jnp.where(low_head, xs_b, jnp.zeros_like(xs_b))
    xs_hi = jnp.where(low_head, jnp.zeros_like(xs_b), xs_b)

    gw = SSM_HEAD_DIM * (SSM_HEADS // SSM_GROUPS)
    for g in range(SSM_GROUPS):
        ns = slice(g * SSM_STATE, (g + 1) * SSM_STATE)
        gs = slice(g * gw, (g + 1) * gw)
        bg_f = bm[:, ns]
        bg = bg_f.astype(BF16)
        cg = cm[:, ns].astype(BF16)
        cb = lax.dot_general(cg, bg, (((1,), (1,)), ((), ())), preferred_element_type=F32)
        prev = state[g]
        y_off = jnp.dot(cg, prev.astype(BF16), preferred_element_type=F32)
        for c in range(gw // LANES):
            cs = slice(g * gw + c * LANES, g * gw + (c + 1) * LANES)
            acc = y_off[:, c * LANES:(c + 1) * LANES] * eacs_full[:, cs]
            for par, xpart in ((0, xs_lo), (1, xs_hi)):
                h = g * (SSM_HEADS // SSM_GROUPS) + 2 * c + par
                seg = acs[:, h:h + 1] - acs_t[h:h + 1, :]
                mmat = cb * jnp.exp(jnp.where(causal, seg, NEG_INF)) * dt_t[h:h + 1, :]
                acc = acc + jnp.dot(mmat.astype(BF16), xpart[:, cs], preferred_element_type=F32)
            ybuf[:, cs] = acc
        new_states = jnp.dot(bg_f.T.astype(BF16), xd[:, gs], preferred_element_type=F32)
        state[g] = prev * eacs_full[q - 1:q, gs] + new_states

    y = ybuf[...] + xs * dskip_ref[...]
    z = z_ref[...].astype(F32)
    y = y * _silu(z)
    for g in range(SSM_GROUPS):
        gs = slice(g * gw, (g + 1) * gw)
        yg = y[:, gs]
        ms = jnp.mean(yg * yg, axis=-1, keepdims=True)
        o_ref[:, gs] = (yg * lax.rsqrt(ms + NORM_EPS) * ng_ref[:, gs]).astype(o_ref.dtype)


def _ssd(proj, dt_raw, conv_w, conv_b, dt_bias, a_neg, d_skip, norm_g, cast_src):
    m = proj.shape[0]
    q = SSM_CHUNK
    inner = D_MODEL
    bc = SSM_GROUPS * SSM_STATE
    hb = q // SSM_HALO
    wx, wb, wc = conv_w[:, :inner], conv_w[:, inner:inner + bc], conv_w[:, inner + bc:]
    bx, bb, bcc = conv_b[:, :inner], conv_b[:, inner:inner + bc], conv_b[:, inner + bc:]
    expand = jnp.asarray(np.arange(LANES)[:, None] == (np.arange(inner) // SSM_HEAD_DIM)[None, :], BF16)
    t_idx = np.arange(q)
    shift_np = np.zeros((SSM_CONV_WIDTH - 1, q, SSM_HALO + q), np.float32)
    for k in range(1, SSM_CONV_WIDTH):
        shift_np[k - 1, t_idx, SSM_HALO + t_idx - k] = 1.0
    shift = jnp.asarray(shift_np.reshape(-1, SSM_HALO + q), BF16)
    prev = lambda i: jnp.maximum(i * hb - 1, 0)
    full = lambda shape: pl.BlockSpec(shape, lambda i: (0,) * len(shape))
    cast_in, cast_out, cast_shape, cast_scratch = _cast_job_extras(cast_src, m // q)
    return pl.pallas_call(
        _with_cast_job(_ssd_kernel, 20, 1),
        grid=(m // q,),
        in_specs=[
            pl.BlockSpec((q, inner), lambda i: (i, 2)),
            pl.BlockSpec((q, inner), lambda i: (i, 3)),
            pl.BlockSpec((q, bc), lambda i: (i, 8)),
            pl.BlockSpec((q, bc), lambda i: (i, 9)),
            pl.BlockSpec((SSM_HALO, inner), lambda i: (prev(i), 3)),
            pl.BlockSpec((SSM_HALO, bc), lambda i: (prev(i), 8)),
            pl.BlockSpec((SSM_HALO, bc), lambda i: (prev(i), 9)),
            pl.BlockSpec((q, LANES), lambda i: (i, 0)),
            full(wx.shape), full(wb.shape), full(wc.shape),
            full(bx.shape), full(bb.shape), full(bcc.shape),
            full((1, LANES)), full((1, LANES)), full((1, inner)), full((1, inner)),
            full((LANES, inner)),
            full(shift.shape),
            cast_in,
        ],
        out_specs=[pl.BlockSpec((q, inner), lambda i: (i, 0)), cast_out],
        out_shape=[jax.ShapeDtypeStruct((m, inner), BF16), cast_shape],
        scratch_shapes=[
            pltpu.VMEM((SSM_GROUPS, SSM_STATE, inner // SSM_GROUPS), F32),
            pltpu.VMEM((q, inner), F32),
        ] + cast_scratch,
        compiler_params=_params(("arbitrary",)),
        name="ssd",
    )(proj, proj, proj, proj, proj, proj, proj, dt_raw, wx, wb, wc, bx, bb, bcc,
      dt_bias, a_neg, d_skip, norm_g, expand, shift, cast_src)


def _attn_kernel(sinks_ref, q_ref, kvc_ref, kvp_ref, bias_ref, o_ref):
    blk = ATTN_BLOCK
    n = pl.program_id(0)
    for sub in range(ATTN_SUBBLOCKS):
        rows = slice(sub * blk, (sub + 1) * blk)
        prev = kvp_ref[...] if sub == 0 else kvc_ref[(sub - 1) * blk:sub * blk, :]
        table = jnp.where(n == 0, 0, 1) if sub == 0 else 1
        _attend_block(sinks_ref, q_ref, prev, kvc_ref[rows, :], bias_ref, table, o_ref, rows)


def _attend_block(sinks_ref, q_ref, kv_prev, kv_cur, bias_ref, table, o_ref, rows):
    blk = ATTN_BLOCK
    kvw = ATTN_KV_HEADS * ATTN_HEAD_DIM
    rep = ATTN_Q_HEADS // ATTN_KV_HEADS
    kv = jnp.concatenate([kv_prev, kv_cur], axis=0).astype(F32)
    lane = lax.broadcasted_iota(jnp.int32, (1, LANES), 1)
    low = lane < ATTN_HEAD_DIM
    scale = ATTN_HEAD_DIM ** -0.5

    for g in range(ATTN_KV_HEADS):
        pc = g // 2
        kcol = kv[:, pc * LANES:(pc + 1) * LANES] * scale
        vcol = kv[:, kvw + pc * LANES:kvw + (pc + 1) * LANES]
        kroll = pltpu.roll(kcol, ATTN_HEAD_DIM, axis=1)
        vroll = pltpu.roll(vcol, ATTN_HEAD_DIM, axis=1)
        if g % 2 == 0:
            k_lo, k_hi = jnp.where(low, kcol, 0.0), jnp.where(low, 0.0, kroll)
            v_lo, v_hi = jnp.where(low, vcol, 0.0), jnp.where(low, 0.0, vroll)
        else:
            k_lo, k_hi = jnp.where(low, kroll, 0.0), jnp.where(low, 0.0, kcol)
            v_lo, v_hi = jnp.where(low, vroll, 0.0), jnp.where(low, 0.0, vcol)
        k_lo, k_hi, v_lo, v_hi = (t.astype(BF16) for t in (k_lo, k_hi, v_lo, v_hi))

        ncol = rep // 2
        qs = jnp.concatenate([q_ref[rows, (g * ncol + c) * LANES:(g * ncol + c + 1) * LANES] for c in range(ncol)], axis=0)
        nt = (((1,), (1,)), ((), ()))
        s_even = lax.dot_general(qs, k_lo, nt, preferred_element_type=F32)
        s_odd = lax.dot_general(qs, k_hi, nt, preferred_element_type=F32)
        for c in range(ncol):
            out = None
            for par, s_all, vv in ((0, s_even, v_lo), (1, s_odd, v_hi)):
                h = g * rep + 2 * c + par
                logits = s_all[c * blk:(c + 1) * blk, :] + bias_ref[table, h]
                sink = sinks_ref[h]
                mx = jnp.maximum(jnp.max(logits, axis=-1, keepdims=True), sink)
                p = jnp.exp(logits - mx)
                denom = jnp.sum(p, axis=-1, keepdims=True) + jnp.exp(sink - mx)
                o = jnp.dot(p.astype(BF16), vv, preferred_element_type=F32) * (1.0 / denom)
                out = o if out is None else out + o
            col = g * ncol + c
            o_ref[rows, col * LANES:(col + 1) * LANES] = out.astype(o_ref.dtype)


def _t5_bucket_table():
    dist = np.arange(ATTN_BLOCK)[:, None] + ATTN_BLOCK - np.arange(2 * ATTN_BLOCK)[None, :]
    n = np.maximum(dist, 0)
    max_exact = REL_BUCKETS // 2
    nf = np.maximum(n, 1).astype(np.float32)
    large = max_exact + (np.log(nf / max_exact) / math.log(REL_MAX_DIST / max_exact)
                         * (REL_BUCKETS - max_exact)).astype(np.int32)
    large = np.minimum(large, REL_BUCKETS - 1)
    return dist, np.where(n < max_exact, n, large)


def _attention(qkv, sinks, rel_bias):
    m = qkv.shape[0]
    blk = ATTN_BLOCK
    qd = ATTN_Q_HEADS * ATTN_HEAD_DIM
    kvd = 2 * ATTN_KV_HEADS * ATTN_HEAD_DIM
    step_rows = blk * ATTN_SUBBLOCKS
    dist, bucket = _t5_bucket_table()
    visible = (dist >= 0) & (dist < blk)
    onehot = jnp.asarray(np.arange(REL_BUCKETS)[:, None] == bucket.reshape(1, -1), F32)
    bias = jnp.dot(rel_bias.astype(F32).T, onehot, precision=lax.Precision.HIGHEST).reshape(-1, blk, 2 * blk)
    general = jnp.where(visible[None], bias, NEG_INF)
    first = jnp.where((visible & (np.arange(2 * blk) >= blk)[None, :])[None], bias, NEG_INF)
    table = jnp.stack([first, general])
    return pl.pallas_call(
        _attn_kernel,
        grid=(m // step_rows,),
        in_specs=[
            pl.BlockSpec(memory_space=pltpu.SMEM),
            pl.BlockSpec((step_rows, qd), lambda n: (n, 0)),
            pl.BlockSpec((step_rows, kvd), lambda n: (n, qd // kvd)),
            pl.BlockSpec((blk, kvd), lambda n: (jnp.maximum(n * ATTN_SUBBLOCKS - 1, 0), qd // kvd)),
            pl.BlockSpec(table.shape, lambda n: (0, 0, 0, 0)),
        ],
        out_specs=pl.BlockSpec((step_rows, qd), lambda n: (n, 0)),
        out_shape=jax.ShapeDtypeStruct((m, qd), BF16),
        compiler_params=_params(("arbitrary",)),
        name="swa_attention",
    )(sinks.astype(F32), qkv, qkv, qkv, table)


def _router_kernel(x_ref, g_ref, rw_ref, rb_ref, o_ref):
    xn = _rms_rows(x_ref[...], g_ref[...])
    x_hi = xn.astype(BF16)
    x_lo = (xn - x_hi.astype(F32)).astype(BF16)
    both = jnp.dot(x_hi, rw_ref[...], preferred_element_type=F32)
    logits = (both[:, :LANES] + both[:, LANES:]
              + jnp.dot(x_lo, rw_ref[:, :LANES], preferred_element_type=F32) + rb_ref[...])
    lane = lax.broadcasted_iota(jnp.int32, logits.shape, 1)
    m1 = jnp.max(logits, axis=-1, keepdims=True)
    i1 = jnp.min(jnp.where(logits == m1, lane, LANES), axis=-1, keepdims=True)
    rest = jnp.where(lane == i1, NEG_INF, logits)
    m2 = jnp.max(rest, axis=-1, keepdims=True)
    i2 = jnp.min(jnp.where(rest == m2, lane, LANES), axis=-1, keepdims=True)
    e = jnp.exp(m2 - m1)
    g1 = 1.0 / (1.0 + e)
    g2 = e * g1
    out = jnp.where(lane == 0, i1.astype(F32),
                    jnp.where(lane == 1, i2.astype(F32),
                              jnp.where(lane == 2, g1, jnp.where(lane == 3, g2, 0.0))))
    o_ref[...] = out


def _router(h, g, rw, rb, *, tm):
    m, d = h.shape
    rw_pad = jnp.zeros((d, LANES), F32).at[:, :N_EXPERTS].set(rw.astype(F32))
    rw_hi = rw_pad.astype(BF16)
    rw_pad = jnp.concatenate([rw_hi, (rw_pad - rw_hi.astype(F32)).astype(BF16)], axis=1)
    rb_pad = jnp.full((1, LANES), NEG_INF, F32).at[0, :N_EXPERTS].set(rb.astype(F32))
    return pl.pallas_call(
        _router_kernel,
        grid=(m // tm,),
        in_specs=[
            pl.BlockSpec((tm, d), lambda i: (i, 0)),
            pl.BlockSpec((1, d), lambda i: (0, 0)),
            pl.BlockSpec((d, 2 * LANES), lambda i: (0, 0)),
            pl.BlockSpec((1, LANES), lambda i: (0, 0)),
        ],
        out_specs=pl.BlockSpec((tm, LANES), lambda i: (i, 0)),
        out_shape=jax.ShapeDtypeStruct((m, LANES), F32),
        compiler_params=_params(("parallel",)),
        name="router",
    )(h, g, rw_pad, rb_pad)


def _moe_drain(h_hbm, xbuf, sem):
    for s in range(MOE_SUBS_PER_GROUP):
        rs = slice(s * MOE_SUB, (s + 1) * MOE_SUB)
        pltpu.make_async_copy(h_hbm.at[pl.ds(0, MOE_SUB)], xbuf.at[rs], sem).wait()
    slack = xbuf.shape[0] - MOE_GROUP
    pltpu.make_async_copy(h_hbm.at[pl.ds(0, slack)], xbuf.at[MOE_GROUP:, :], sem).wait()


def _moe_kernel(ge_ref, ns_ref, tok_ref, h_hbm, g_ref, wg_ref, wu_ref, wd_ref, o_ref,
                xbuf, xn_ref, sem, *, rps):
    grp = pl.program_id(0)
    j = pl.program_id(1)
    ns = ns_ref[grp]

    def row_copy(gi, r):
        tok = tok_ref[gi * MOE_GROUP + jnp.minimum(r, MOE_GROUP - 1)]
        return pltpu.make_async_copy(h_hbm.at[pl.ds(tok, 1)], xbuf.at[pl.ds(r, 1)], sem)

    def issue_step_rows(gi):
        for u in range(rps):
            row_copy(gi, j * rps + u).start()

    @pl.when(j == 0)
    def _():
        @pl.when(grp == 0)
        def _():
            def issue(r, carry):
                for u in range(MOE_UNROLL):
                    row_copy(0, r * MOE_UNROLL + u).start()
                return carry

            lax.fori_loop(0, xbuf.shape[0] // MOE_UNROLL, issue, 0)

        _moe_drain(h_hbm, xbuf, sem)
        for s in range(MOE_SUBS_PER_GROUP):
            rs = slice(s * MOE_SUB, (s + 1) * MOE_SUB)
            o_ref[rs, :] = jnp.zeros((MOE_SUB, o_ref.shape[1]), F32)

            @pl.when(s < ns)
            def _():
                xn_ref[rs, :] = _rms_rows(xbuf[rs, :], g_ref[...]).astype(BF16)

    def ffn(rs):
        xn = xn_ref[rs, :]
        gate = jnp.dot(xn, wg_ref[...].astype(BF16), preferred_element_type=F32)
        up = jnp.dot(xn, wu_ref[...].astype(BF16), preferred_element_type=F32)
        hid = (_silu(gate) * up).astype(BF16)
        o_ref[rs, :] += jnp.dot(hid, wd_ref[...].astype(BF16), preferred_element_type=F32)

    @pl.when(ns == MOE_SUBS_PER_GROUP)
    def _():
        issue_step_rows(grp + 1)
        ffn(slice(0, MOE_GROUP))

    @pl.when(ns < MOE_SUBS_PER_GROUP)
    def _():
        issue_step_rows(grp + 1)

    for s in range(MOE_SUBS_PER_GROUP - 1):
        @pl.when((s < ns) & (ns < MOE_SUBS_PER_GROUP))
        def _():
            ffn(slice(s * MOE_SUB, (s + 1) * MOE_SUB))

    @pl.when((grp == pl.num_programs(0) - 1) & (j == pl.num_programs(1) - 1))
    def _():
        _moe_drain(h_hbm, xbuf, sem)


def _moe_ffn(h, g, wg, wu, wd, group_e, group_ns, src_tok, *, tf):
    t, d = h.shape
    n_groups = group_e.shape[0]
    f = wg.shape[2]
    nj = f // tf
    rps = -(-MOE_GROUP // (nj * SUBLANES)) * SUBLANES
    assert src_tok.shape[0] == (n_groups + 1) * MOE_GROUP

    def jj(grp, j, ns):
        return jnp.where(ns[grp] > 0, j, nj - 1)

    return pl.pallas_call(
        functools.partial(_moe_kernel, rps=rps),
        grid_spec=pltpu.PrefetchScalarGridSpec(
            num_scalar_prefetch=3,
            grid=(n_groups, nj),
            in_specs=[
                pl.BlockSpec(memory_space=pl.ANY),
                pl.BlockSpec((1, d), lambda grp, j, ge, ns, tok: (0, 0)),
                pl.BlockSpec((None, d, tf), lambda grp, j, ge, ns, tok: (ge[grp], 0, jj(grp, j, ns))),
                pl.BlockSpec((None, d, tf), lambda grp, j, ge, ns, tok: (ge[grp], 0, jj(grp, j, ns))),
                pl.BlockSpec((None, tf, d), lambda grp, j, ge, ns, tok: (ge[grp], jj(grp, j, ns), 0)),
            ],
            out_specs=pl.BlockSpec((MOE_GROUP, d), lambda grp, j, ge, ns, tok: (grp, 0)),
            scratch_shapes=[
                pltpu.VMEM((rps * nj, d), F32),
                pltpu.VMEM((MOE_GROUP, d), BF16),
                pltpu.SemaphoreType.DMA(()),
            ],
        ),
        out_shape=jax.ShapeDtypeStruct((n_groups * MOE_GROUP, d), F32),
        compiler_params=_params(("arbitrary", "arbitrary")),
        name="moe_ffn",
    )(group_e, group_ns, src_tok, h, g, wg, wu, wd)


def _combine_kernel(pos_ref, h_ref, route_ref, fg_ref, rows_hbm, o_ref, buf_a, buf_b, sem, *, tc):
    i = pl.program_id(0)
    slot = i % 2

    def copies(blk, sl, r):
        pa = pos_ref[TOP_K * (blk * tc + r)]
        pb = pos_ref[TOP_K * (blk * tc + r) + 1]
        return (pltpu.make_async_copy(rows_hbm.at[pl.ds(pa, 1)], buf_a.at[sl, pl.ds(r, 1)], sem.at[sl]),
                pltpu.make_async_copy(rows_hbm.at[pl.ds(pb, 1)], buf_b.at[sl, pl.ds(r, 1)], sem.at[sl]))

    def issue_block(blk, sl):
        def issue(r, carry):
            ca, cb = copies(blk, sl, r)
            ca.start()
            cb.start()
            return carry

        lax.fori_loop(0, tc, issue, 0, unroll=8)

    def wait_block(blk, sl):
        def wait(r, carry):
            ca, cb = copies(blk, sl, r)
            ca.wait()
            cb.wait()
            return carry

        lax.fori_loop(0, tc, wait, 0, unroll=8)

    @pl.when(i == 0)
    def _():
        issue_block(0, 0)

    wait_block(i, slot)

    last = pl.num_programs(0) - 1
    nxt = jnp.minimum(i + 1, last)
    for r in range(tc):
        ca, cb = copies(nxt, 1 - slot, r)
        ca.start()
        cb.start()
    route = route_ref[...]
    moe = route[:, 2:3] * buf_a[slot] + route[:, 3:4] * buf_b[slot]
    o_ref[...] = _rms_rows(h_ref[...] + moe, fg_ref[...])

    @pl.when(i == last)
    def _():
        wait_block(nxt, 1 - slot)


def _combine(h, route, final_g, rows, pos, *, tc):
    t, d = h.shape
    return pl.pallas_call(
        functools.partial(_combine_kernel, tc=tc),
        grid_spec=pltpu.PrefetchScalarGridSpec(
            num_scalar_prefetch=1,
            grid=(t // tc,),
            in_specs=[
                pl.BlockSpec((tc, d), lambda i, pos: (i, 0)),
                pl.BlockSpec((tc, LANES), lambda i, pos: (i, 0)),
                pl.BlockSpec((1, d), lambda i, pos: (0, 0)),
                pl.BlockSpec(memory_space=pl.ANY),
            ],
            out_specs=pl.BlockSpec((tc, d), lambda i, pos: (i, 0)),
            scratch_shapes=[pltpu.VMEM((2, tc, d), F32), pltpu.VMEM((2, tc, d), F32), pltpu.SemaphoreType.DMA((2,))],
        ),
        out_shape=jax.ShapeDtypeStruct((t, d), F32),
        compiler_params=_params(("arbitrary",)),
        name="moe_combine",
    )(pos, h, route, final_g, rows)


def _moe_plan(route, n_tok):
    flat_e = route[:, :TOP_K].astype(jnp.int32).reshape(-1)
    onehot = (flat_e[:, None] == jnp.arange(N_EXPERTS, dtype=jnp.int32)[None, :]).astype(jnp.int32)
    csum = jnp.cumsum(onehot, axis=0)
    rank = jnp.sum(onehot * (csum - onehot), axis=1)
    counts = csum[-1]
    n_groups = (n_tok * TOP_K) // MOE_GROUP + N_EXPERTS
    groups_e = (counts + MOE_GROUP - 1) // MOE_GROUP
    group_end = jnp.cumsum(groups_e)
    group_start = group_end - groups_e
    pos = jnp.sum(onehot * (group_start * MOE_GROUP)[None, :], axis=1) + rank
    flat_tok = jnp.arange(n_tok * TOP_K, dtype=jnp.int32) // TOP_K
    src_tok = jnp.zeros(((n_groups + 1) * MOE_GROUP,), jnp.int32).at[pos].set(
        flat_tok, unique_indices=True, mode="promise_in_bounds")
    experts = jnp.arange(N_EXPERTS, dtype=jnp.int32)
    gidx = jnp.arange(n_groups, dtype=jnp.int32)
    used = gidx < group_end[-1]
    last_e = jnp.max(jnp.where(groups_e > 0, experts, 0))
    ge = jnp.sum((group_end[None, :] <= gidx[:, None]).astype(jnp.int32), axis=1)
    ge = jnp.where(used, ge, last_e).astype(jnp.int32)
    mine = (ge[:, None] == experts[None, :]).astype(jnp.int32)
    subs_e = (counts + MOE_SUB - 1) // MOE_SUB
    subs_g = jnp.sum(mine * subs_e[None, :], axis=1)
    start_g = jnp.sum(mine * group_start[None, :], axis=1)
    ns = jnp.clip(subs_g - MOE_SUBS_PER_GROUP * (gidx - start_g), 0, MOE_SUBS_PER_GROUP)
    ns = jnp.where(used, ns, 0).astype(jnp.int32)
    return ge, ns, src_tok, pos.astype(jnp.int32)


def kernel(x, mix_norm_g, ffn_norm_g, final_norm_g, w_in, conv_w, conv_b, conv_ln_g, conv_ln_b, ssm_conv_w, ssm_conv_b, dt_bias, a_log, d_skip, ssm_norm_g, w_out, ffn_w_gate, ffn_w_up, ffn_w_down, w_qkv, b_qkv, w_o, b_o, sinks, rel_bias, router_w, router_b, moe_w_gate, moe_w_up, moe_w_down):
    bsz, seq, d = x.shape
    assert bsz == 1 and d == D_MODEL and seq % 512 == 0
    m = seq
    h = x.reshape(m, d)
    row = lambda v: v.reshape(1, -1).astype(F32)
    tm = min(1024, m)

    main_w = 2 * D_MODEL + D_MODEL + (D_MODEL + 2 * SSM_GROUPS * SSM_STATE)
    w_main = w_in[0].astype(BF16)
    w_dt = jnp.zeros((d, LANES), F32).at[:, :SSM_HEADS].set(w_in[0][:, main_w:]).astype(BF16)
    g0 = row(mix_norm_g[0])
    n_exp, _, expert_dim = moe_w_gate[0].shape
    wg_f32 = moe_w_gate[0].reshape(n_exp * d, expert_dim)
    wu_f32 = moe_w_up[0].reshape(n_exp * d, expert_dim)
    wd_f32 = moe_w_down[0].reshape(n_exp * expert_dim, d)

    proj, dt_raw, wd_b = _in_proj(h, g0, w_main, w_dt, wd_f32, tm=tm, tn=1024, n=main_w)
    conv_out, wg_b = _conv_module(proj, conv_w[0].astype(F32), row(conv_b[0]), row(conv_ln_g[0]), row(conv_ln_b[0]),
                                  wg_f32, tt=256)

    pad_heads = lambda v: jnp.zeros((1, LANES), F32).at[0, :SSM_HEADS].set(v.astype(F32))
    ssm_out, wu_b = _ssd(proj, dt_raw, ssm_conv_w[0].astype(F32), row(ssm_conv_b[0]),
                         pad_heads(dt_bias[0]), pad_heads(-jnp.exp(a_log[0].astype(F32))),
                         row(jnp.repeat(d_skip[0].astype(F32), SSM_HEAD_DIM)), row(ssm_norm_g[0]), wu_f32)

    wo = w_out[0].astype(BF16)
    h = _matmul_residual([conv_out, ssm_out], [wo[:D_MODEL], wo[D_MODEL:]], h, jnp.zeros((1, d), F32),
                         tm=256, tn=d, name="out_proj")
    h = _swiglu(h, row(ffn_norm_g[0]), ffn_w_gate[0].astype(BF16), ffn_w_up[0].astype(BF16),
                ffn_w_down[0].astype(BF16), tm=tm, tf=512)

    qkv = _norm_matmul(h, row(mix_norm_g[1]), w_qkv[0].astype(BF16), row(b_qkv[0]), tm=min(512, m), tn=w_qkv.shape[2],
                       out_dtype=BF16, name="qkv_proj")
    attn = _attention(qkv, sinks[0], rel_bias)
    h = _matmul_residual([attn], [w_o[0].astype(BF16)], h, row(b_o[0]), tm=min(512, m), tn=d, name="attn_out_proj")

    g1 = row(ffn_norm_g[1])
    route = _router(h, g1, router_w[0], router_b[0], tm=512)
    ge, ns, src_tok, pos = _moe_plan(route, m)
    rows = _moe_ffn(h, g1, wg_b.reshape(n_exp, d, expert_dim), wu_b.reshape(n_exp, d, expert_dim),
                    wd_b.reshape(n_exp, expert_dim, d), ge, ns, src_tok, tf=512)
    out = _combine(h, route, row(final_norm_g), rows, pos, tc=256)
    return out.reshape(bsz, seq, d)
```

```python
import functools
import math

import numpy as np
import jax
import jax.numpy as jnp
from jax import lax
from jax.experimental import pallas as pl
from jax.experimental.pallas import tpu as pltpu

F32 = jnp.float32
BF16 = jnp.bfloat16
NORM_EPS = 1e-5
NEG_INF = float("-inf")

LANES = 128
SUBLANES = 8
VMEM_LIMIT = 56 << 20

D_MODEL = 2048
CONV_WIDTH = 31
SSM_HEADS = 32
SSM_HEAD_DIM = 64
SSM_GROUPS = 8
SSM_STATE = 128
SSM_CONV_WIDTH = 4
SSM_CHUNK = 128
ATTN_Q_HEADS = 32
ATTN_KV_HEADS = 4
ATTN_HEAD_DIM = 64
ATTN_BLOCK = 128
ATTN_SUBBLOCKS = 2
REL_BUCKETS = 32
REL_MAX_DIST = 128
N_EXPERTS = 8
TOP_K = 2
MOE_SUB = 128
MOE_SUBS_PER_GROUP = 8
MOE_GROUP = MOE_SUB * MOE_SUBS_PER_GROUP
MOE_UNROLL = 8


def _params(sem):
    return pltpu.CompilerParams(dimension_semantics=sem, vmem_limit_bytes=VMEM_LIMIT)


def _sigmoid(x):
    return 0.5 + 0.5 * jnp.tanh(0.5 * x)


def _silu(x):
    half = 0.5 * x
    return half + half * jnp.tanh(half)


def _rms_rows(x, g):
    ms = jnp.mean(x * x, axis=-1, keepdims=True)
    return x * lax.rsqrt(ms + NORM_EPS) * g


def _cast_slab(i, n_steps, src_hbm, dst_hbm, inbuf, outbuf, in_sem, out_sem):
    rows = inbuf.shape[1]
    slot = i % 2

    def fetch(step, sl):
        return pltpu.make_async_copy(src_hbm.at[pl.ds(step * rows, rows)], inbuf.at[sl], in_sem.at[sl])

    def write_back(step, sl):
        return pltpu.make_async_copy(outbuf.at[sl], dst_hbm.at[pl.ds(step * rows, rows)], out_sem.at[sl])

    @pl.when(i == 0)
    def _():
        fetch(0, 0).start()

    @pl.when(i + 1 < n_steps)
    def _():
        fetch(i + 1, 1 - slot).start()

    fetch(i, slot).wait()

    @pl.when(i >= 2)
    def _():
        write_back(i - 2, slot).wait()

    outbuf[slot] = inbuf[slot].astype(BF16)
    write_back(i, slot).start()

    @pl.when(i == n_steps - 1)
    def _():
        write_back(i, slot).wait()

        @pl.when(i >= 1)
        def _():
            write_back(i - 1, 1 - slot).wait()


def _with_cast_job(kernel_fn, n_in, n_out):
    def wrapped(*refs):
        ins, src = refs[:n_in], refs[n_in]
        outs, dst = refs[n_in + 1:n_in + 1 + n_out], refs[n_in + 1 + n_out]
        rest = refs[n_in + 2 + n_out:]
        scratch, cast_scratch = rest[:-4], rest[-4:]
        _cast_slab(pl.program_id(0), pl.num_programs(0), src, dst, *cast_scratch)
        kernel_fn(*ins, *outs, *scratch)

    return wrapped


def _cast_job_extras(src, n_steps):
    assert src.shape[0] % n_steps == 0
    rows, cols = src.shape[0] // n_steps, src.shape[1]
    scratch = [pltpu.VMEM((2, rows, cols), F32), pltpu.VMEM((2, rows, cols), BF16),
               pltpu.SemaphoreType.DMA((2,)), pltpu.SemaphoreType.DMA((2,))]
    any_spec = pl.BlockSpec(memory_space=pl.ANY)
    return any_spec, any_spec, jax.ShapeDtypeStruct(src.shape, BF16), scratch


def _norm_matmul_kernel(x_ref, g_ref, w_ref, b_ref, o_ref, xn_ref):
    @pl.when(pl.program_id(1) == 0)
    def _():
        xn_ref[...] = _rms_rows(x_ref[...], g_ref[...]).astype(BF16)

    acc = jnp.dot(xn_ref[...], w_ref[...], preferred_element_type=F32)
    o_ref[...] = (acc + b_ref[...]).astype(o_ref.dtype)


def _norm_matmul(x, g, w, b, *, tm, tn, out_dtype, name):
    m, k = x.shape
    n = w.shape[1]
    return pl.pallas_call(
        _norm_matmul_kernel,
        grid=(m // tm, n // tn),
        in_specs=[
            pl.BlockSpec((tm, k), lambda i, j: (i, 0)),
            pl.BlockSpec((1, k), lambda i, j: (0, 0)),
            pl.BlockSpec((k, tn), lambda i, j: (0, j)),
            pl.BlockSpec((1, tn), lambda i, j: (0, j)),
        ],
        out_specs=pl.BlockSpec((tm, tn), lambda i, j: (i, j)),
        out_shape=jax.ShapeDtypeStruct((m, n), out_dtype),
        scratch_shapes=[pltpu.VMEM((tm, k), BF16)],
        compiler_params=_params(("parallel", "arbitrary")),
        name=name,
    )(x, g, w, b)


def _in_proj_kernel(x_ref, g_ref, w_ref, wdt_ref, cast_src, o_ref, odt_ref, cast_dst,
                    xn_ref, inbuf, outbuf, in_sem, out_sem, *, cast_cols):
    i = pl.program_id(0)
    j = pl.program_id(1)

    @pl.when(j == 0)
    def _():
        xn_ref[...] = _rms_rows(x_ref[...], g_ref[...]).astype(BF16)
        odt_ref[...] = jnp.dot(xn_ref[...], wdt_ref[...], preferred_element_type=F32)

    @pl.when(j < cast_cols)
    def _():
        _cast_slab(i * cast_cols + j, pl.num_programs(0) * cast_cols, cast_src, cast_dst,
                   inbuf, outbuf, in_sem, out_sem)

    o_ref[...] = jnp.dot(xn_ref[...], w_ref[...], preferred_element_type=F32).astype(o_ref.dtype)


def _in_proj(x, g, w, wdt, cast_src, *, tm, tn, n):
    m, k = x.shape
    assert n % tn == 0 and n <= w.shape[1]
    ndt = wdt.shape[1]
    n_rows, n_cols = m // tm, n // tn
    cast_cols = max(s for s in range(1, n_cols + 1)
                    if cast_src.shape[0] % (n_rows * s) == 0 and (cast_src.shape[0] // (n_rows * s)) % 16 == 0)
    cast_in, cast_out, cast_shape, cast_scratch = _cast_job_extras(cast_src, n_rows * cast_cols)
    return pl.pallas_call(
        functools.partial(_in_proj_kernel, cast_cols=cast_cols),
        grid=(n_rows, n_cols),
        in_specs=[
            pl.BlockSpec((tm, k), lambda i, j: (i, 0)),
            pl.BlockSpec((1, k), lambda i, j: (0, 0)),
            pl.BlockSpec((k, tn), lambda i, j: (0, j)),
            pl.BlockSpec((k, ndt), lambda i, j: (0, 0)),
            cast_in,
        ],
        out_specs=[pl.BlockSpec((tm, tn), lambda i, j: (i, j)), pl.BlockSpec((tm, ndt), lambda i, j: (i, 0)), cast_out],
        out_shape=[jax.ShapeDtypeStruct((m, n), BF16), jax.ShapeDtypeStruct((m, ndt), F32), cast_shape],
        scratch_shapes=[pltpu.VMEM((tm, k), BF16)] + cast_scratch,
        compiler_params=_params(("arbitrary", "arbitrary")),
        name="in_proj",
    )(x, g, w, wdt, cast_src)


def _matmul_residual_kernel(*refs, n_in):
    a_refs, w_refs = refs[:n_in], refs[n_in:2 * n_in]
    res_ref, b_ref, o_ref = refs[2 * n_in:]
    acc = res_ref[...] + b_ref[...]
    for a_ref, w_ref in zip(a_refs, w_refs):
        acc = acc + jnp.dot(a_ref[...], w_ref[...], preferred_element_type=F32)
    o_ref[...] = acc


def _matmul_residual(a_list, w_list, res, b, *, tm, tn, name):
    m, n = res.shape
    n_in = len(a_list)
    in_specs = [pl.BlockSpec((tm, a.shape[1]), lambda j, i: (i, 0)) for a in a_list]
    in_specs += [pl.BlockSpec((w.shape[0], tn), lambda j, i: (0, j)) for w in w_list]
    in_specs += [pl.BlockSpec((tm, tn), lambda j, i: (i, j)), pl.BlockSpec((1, tn), lambda j, i: (0, j))]
    return pl.pallas_call(
        functools.partial(_matmul_residual_kernel, n_in=n_in),
        grid=(n // tn, m // tm),
        in_specs=in_specs,
        out_specs=pl.BlockSpec((tm, tn), lambda j, i: (i, j)),
        out_shape=jax.ShapeDtypeStruct((m, n), F32),
        compiler_params=_params(("parallel", "parallel")),
        name=name,
    )(*a_list, *w_list, res, b)


def _swiglu_kernel(x_ref, g_ref, wg_ref, wu_ref, wd_ref, o_ref, xn_ref):
    @pl.when(pl.program_id(1) == 0)
    def _():
        x = x_ref[...]
        xn_ref[...] = _rms_rows(x, g_ref[...]).astype(BF16)
        o_ref[...] = x

    xn = xn_ref[...]
    gate = jnp.dot(xn, wg_ref[...], preferred_element_type=F32)
    up = jnp.dot(xn, wu_ref[...], preferred_element_type=F32)
    hid = (_silu(gate) * up).astype(BF16)
    o_ref[...] += jnp.dot(hid, wd_ref[...], preferred_element_type=F32)


def _swiglu(x, g, wg, wu, wd, *, tm, tf):
    m, d = x.shape
    f = wg.shape[1]
    return pl.pallas_call(
        _swiglu_kernel,
        grid=(m // tm, f // tf),
        in_specs=[
            pl.BlockSpec((tm, d), lambda i, j: (i, 0)),
            pl.BlockSpec((1, d), lambda i, j: (0, 0)),
            pl.BlockSpec((d, tf), lambda i, j: (0, j)),
            pl.BlockSpec((d, tf), lambda i, j: (0, j)),
            pl.BlockSpec((tf, d), lambda i, j: (j, 0)),
        ],
        out_specs=pl.BlockSpec((tm, d), lambda i, j: (i, 0)),
        out_shape=jax.ShapeDtypeStruct((m, d), F32),
        scratch_shapes=[pltpu.VMEM((tm, d), BF16)],
        compiler_params=_params(("parallel", "arbitrary")),
        name="swiglu",
    )(x, g, wg, wu, wd)


CONV_HALO = 32
CONV_ROWS = 64


def _conv_module_kernel(val_ref, gate_ref, pval_ref, pgate_ref, w_ref, b_ref, lg_ref, lb_ref, o_ref,
                        ubuf, cbuf, *, tt):
    i = pl.program_id(0)

    def glu(v, g):
        return v.astype(F32) * _sigmoid(g.astype(F32))

    ubuf[0:CONV_HALO, :] = jnp.where(i > 0, glu(pval_ref[...], pgate_ref[...]), 0.0)
    ubuf[CONV_HALO:CONV_HALO + tt, :] = glu(val_ref[...], gate_ref[...])

    off = CONV_HALO - (CONV_WIDTH - 1)
    taps = [[] for _ in range(SUBLANES)]
    for j in range(CONV_WIDTH):
        a, r = divmod(off + j, SUBLANES)
        taps[r].append((a, j))
    n_ch = ubuf.shape[1]
    for cb in range(n_ch // LANES):
        ls = slice(cb * LANES, (cb + 1) * LANES)
        for r0 in range(0, tt, CONV_ROWS):
            big = ubuf[r0:r0 + CONV_ROWS + CONV_HALO, ls]
            out = None
            for r in range(SUBLANES):
                rows = CONV_ROWS if r == 0 else CONV_ROWS + SUBLANES
                q = None
                for a, j in taps[r]:
                    term = w_ref[j:j + 1, ls] * big[a * SUBLANES:a * SUBLANES + rows, :]
                    q = term if q is None else q + term
                if r:
                    q = pltpu.roll(q, rows - r, axis=0)[:CONV_ROWS, :]
                out = q if out is None else out + q
            cbuf[r0:r0 + CONV_ROWS, ls] = out

    c = cbuf[...] + b_ref[...]
    mu = jnp.mean(c, axis=-1, keepdims=True)
    d = c - mu
    var = jnp.mean(d * d, axis=-1, keepdims=True)
    y = d * lax.rsqrt(var + NORM_EPS) * lg_ref[...] + lb_ref[...]
    o_ref[...] = _silu(y).astype(o_ref.dtype)


def _conv_module(proj, conv_w, conv_b, ln_g, ln_b, cast_src, *, tt):
    m = proj.shape[0]
    c = D_MODEL
    hb = tt // CONV_HALO
    cast_in, cast_out, cast_shape, cast_scratch = _cast_job_extras(cast_src, m // tt)
    return pl.pallas_call(
        _with_cast_job(functools.partial(_conv_module_kernel, tt=tt), 8, 1),
        grid=(m // tt,),
        in_specs=[
            pl.BlockSpec((tt, c), lambda i: (i, 0)),
            pl.BlockSpec((tt, c), lambda i: (i, 1)),
            pl.BlockSpec((CONV_HALO, c), lambda i: (jnp.maximum(i * hb - 1, 0), 0)),
            pl.BlockSpec((CONV_HALO, c), lambda i: (jnp.maximum(i * hb - 1, 0), 1)),
            pl.BlockSpec((CONV_WIDTH, c), lambda i: (0, 0)),
            pl.BlockSpec((1, c), lambda i: (0, 0)),
            pl.BlockSpec((1, c), lambda i: (0, 0)),
            pl.BlockSpec((1, c), lambda i: (0, 0)),
            cast_in,
        ],
        out_specs=[pl.BlockSpec((tt, c), lambda i: (i, 0)), cast_out],
        out_shape=[jax.ShapeDtypeStruct((m, c), BF16), cast_shape],
        scratch_shapes=[pltpu.VMEM((CONV_HALO + tt, c), F32), pltpu.VMEM((tt, c), F32)] + cast_scratch,
        compiler_params=_params(("arbitrary",)),
        name="conv_module",
    )(proj, proj, proj, proj, conv_w, conv_b, ln_g, ln_b, cast_src)


SSM_HALO = 16


def _conv4_silu(cur_ref, prev_ref, shift_ref, w_ref, b_ref, cols, first):
    q = cur_ref.shape[0]
    cur = cur_ref[:, cols]
    prev = prev_ref[:, cols]
    xb = jnp.concatenate([jnp.where(first, jnp.zeros_like(prev), prev), cur], axis=0)
    back = jnp.dot(shift_ref[...], xb, preferred_element_type=F32)
    last = SSM_CONV_WIDTH - 1
    acc = b_ref[:, cols] + w_ref[last:last + 1, cols] * cur.astype(F32)
    for k in range(1, SSM_CONV_WIDTH):
        acc = acc + w_ref[last - k:last - k + 1, cols] * back[(k - 1) * q:k * q, :]
    return _silu(acc)


def _ssd_kernel(z_ref, x_ref, b_ref, c_ref, px_ref, pb_ref, pc_ref, dtr_ref,
                wx_ref, wb_ref, wc_ref, bx_ref, bb_ref, bc_ref,
                dtb_ref, a_ref, dskip_ref, ng_ref, e_ref, sh_ref, o_ref,
                state, ybuf):
    i = pl.program_id(0)
    q = SSM_CHUNK
    first = i == 0

    @pl.when(first)
    def _():
        state[...] = jnp.zeros_like(state)

    everything = slice(None)
    xs = _conv4_silu(x_ref, px_ref, sh_ref, wx_ref, bx_ref, everything, first)
    bm = _conv4_silu(b_ref, pb_ref, sh_ref, wb_ref, bb_ref, everything, first)
    cm = _conv4_silu(c_ref, pc_ref, sh_ref, wc_ref, bc_ref, everything, first)

    pre = dtr_ref[...] + dtb_ref[...]
    dt = jnp.maximum(pre, 0.0) + jnp.log(1.0 + jnp.exp(-jnp.abs(pre)))
    dta = dt * a_ref[...]
    row = lax.broadcasted_iota(jnp.int32, (q, q), 0)
    col = lax.broadcasted_iota(jnp.int32, (q, q), 1)
    causal = row >= col
    acs = jnp.dot(causal.astype(F32), dta, precision=lax.Precision.HIGHEST, preferred_element_type=F32)
    acs_t = acs.T
    dt_t = dt.T
    eacs = jnp.exp(acs)
    wdec = dt * jnp.exp(acs[q - 1:q, :] - acs)

    def split(v):
        hi = v.astype(BF16)
        return hi, (v - hi.astype(F32)).astype(BF16)

    parts = jnp.concatenate(split(eacs) + split(wdec), axis=0)
    wide = jnp.dot(parts, e_ref[...], preferred_element_type=F32)
    eacs_full = wide[0:q] + wide[q:2 * q]
    xd = (xs * (wide[2 * q:3 * q] + wide[3 * q:4 * q])).astype(BF16)

    xs_b = xs.astype(BF16)
    lane = lax.broadcasted_iota(jnp.int32, (1, xs.shape[1]), 1)
    low_head = (lane % LANES) < SSM_HEAD_DIM
    xs_lo = jnp.where(low_head, xs_b, jnp.zeros_like(xs_b))
    xs_hi = jnp.where(low_head, jnp.zeros_like(xs_b), xs_b)

    gw = SSM_HEAD_DIM * (SSM_HEADS // SSM_GROUPS)
    for g in range(SSM_GROUPS):
        ns = slice(g * SSM_STATE, (g + 1) * SSM_STATE)
        gs = slice(g * gw, (g + 1) * gw)
        bg_f = bm[:, ns]
        bg = bg_f.astype(BF16)
        cg = cm[:, ns].astype(BF16)
        cb = lax.dot_general(cg, bg, (((1,), (1,)), ((), ())), preferred_element_type=F32)
        prev = state[g]
        y_off = jnp.dot(cg, prev.astype(BF16), preferred_element_type=F32)
        for c in range(gw // LANES):
            cs = slice(g * gw + c * LANES, g * gw + (c + 1) * LANES)
            acc = y_off[:, c * LANES:(c + 1) * LANES] * eacs_full[:, cs]
            for par, xpart in ((0, xs_lo), (1, xs_hi)):
                h = g * (SSM_HEADS // SSM_GROUPS) + 2 * c + par
                seg = acs[:, h:h + 1] - acs_t[h:h + 1, :]
                mmat = cb * jnp.exp(jnp.where(causal, seg, NEG_INF)) * dt_t[h:h + 1, :]
                acc = acc + jnp.dot(mmat.astype(BF16), xpart[:, cs], preferred_element_type=F32)
            ybuf[:, cs] = acc
        new_states = jnp.dot(bg_f.T.astype(BF16), xd[:, gs], preferred_element_type=F32)
        state[g] = prev * eacs_full[q - 1:q, gs] + new_states

    y = ybuf[...] + xs * dskip_ref[...]
    z = z_ref[...].astype(F32)
    y = y * _silu(z)
    for g in range(SSM_GROUPS):
        gs = slice(g * gw, (g + 1) * gw)
        yg = y[:, gs]
        ms = jnp.mean(yg * yg, axis=-1, keepdims=True)
        o_ref[:, gs] = (yg * lax.rsqrt(ms + NORM_EPS) * ng_ref[:, gs]).astype(o_ref.dtype)


def _ssd(proj, dt_raw, conv_w, conv_b, dt_bias, a_neg, d_skip, norm_g, cast_src):
    m = proj.shape[0]
    q = SSM_CHUNK
    inner = D_MODEL
    bc = SSM_GROUPS * SSM_STATE
    hb = q // SSM_HALO
    wx, wb, wc = conv_w[:, :inner], conv_w[:, inner:inner + bc], conv_w[:, inner + bc:]
    bx, bb, bcc = conv_b[:, :inner], conv_b[:, inner:inner + bc], conv_b[:, inner + bc:]
    expand = jnp.asarray(np.arange(LANES)[:, None] == (np.arange(inner) // SSM_HEAD_DIM)[None, :], BF16)
    t_idx = np.arange(q)
    shift_np = np.zeros((SSM_CONV_WIDTH - 1, q, SSM_HALO + q), np.float32)
    for k in range(1, SSM_CONV_WIDTH):
        shift_np[k - 1, t_idx, SSM_HALO + t_idx - k] = 1.0
    shift = jnp.asarray(shift_np.reshape(-1, SSM_HALO + q), BF16)
    prev = lambda i: jnp.maximum(i * hb - 1, 0)
    full = lambda shape: pl.BlockSpec(shape, lambda i: (0,) * len(shape))
    cast_in, cast_out, cast_shape, cast_scratch = _cast_job_extras(cast_src, m // q)
    return pl.pallas_call(
        _with_cast_job(_ssd_kernel, 20, 1),
        grid=(m // q,),
        in_specs=[
            pl.BlockSpec((q, inner), lambda i: (i, 2)),
            pl.BlockSpec((q, inner), lambda i: (i, 3)),
            pl.BlockSpec((q, bc), lambda i: (i, 8)),
            pl.BlockSpec((q, bc), lambda i: (i, 9)),
            pl.BlockSpec((SSM_HALO, inner), lambda i: (prev(i), 3)),
            pl.BlockSpec((SSM_HALO, bc), lambda i: (prev(i), 8)),
            pl.BlockSpec((SSM_HALO, bc), lambda i: (prev(i), 9)),
            pl.BlockSpec((q, LANES), lambda i: (i, 0)),
            full(wx.shape), full(wb.shape), full(wc.shape),
            full(bx.shape), full(bb.shape), full(bcc.shape),
            full((1, LANES)), full((1, LANES)), full((1, inner)), full((1, inner)),
            full((LANES, inner)),
            full(shift.shape),
            cast_in,
        ],
        out_specs=[pl.BlockSpec((q, inner), lambda i: (i, 0)), cast_out],
        out_shape=[jax.ShapeDtypeStruct((m, inner), BF16), cast_shape],
        scratch_shapes=[
            pltpu.VMEM((SSM_GROUPS, SSM_STATE, inner // SSM_GROUPS), F32),
            pltpu.VMEM((q, inner), F32),
        ] + cast_scratch,
        compiler_params=_params(("arbitrary",)),
        name="ssd",
    )(proj, proj, proj, proj, proj, proj, proj, dt_raw, wx, wb, wc, bx, bb, bcc,
      dt_bias, a_neg, d_skip, norm_g, expand, shift, cast_src)


def _attn_kernel(sinks_ref, q_ref, kvc_ref, kvp_ref, bias_ref, o_ref):
    blk = ATTN_BLOCK
    n = pl.program_id(0)
    for sub in range(ATTN_SUBBLOCKS):
        rows = slice(sub * blk, (sub + 1) * blk)
        prev = kvp_ref[...] if sub == 0 else kvc_ref[(sub - 1) * blk:sub * blk, :]
        table = jnp.where(n == 0, 0, 1) if sub == 0 else 1
        _attend_block(sinks_ref, q_ref, prev, kvc_ref[rows, :], bias_ref, table, o_ref, rows)


def _attend_block(sinks_ref, q_ref, kv_prev, kv_cur, bias_ref, table, o_ref, rows):
    blk = ATTN_BLOCK
    kvw = ATTN_KV_HEADS * ATTN_HEAD_DIM
    rep = ATTN_Q_HEADS // ATTN_KV_HEADS
    kv = jnp.concatenate([kv_prev, kv_cur], axis=0).astype(F32)
    lane = lax.broadcasted_iota(jnp.int32, (1, LANES), 1)
    low = lane < ATTN_HEAD_DIM
    scale = ATTN_HEAD_DIM ** -0.5

    for g in range(ATTN_KV_HEADS):
        pc = g // 2
        kcol = kv[:, pc * LANES:(pc + 1) * LANES] * scale
        vcol = kv[:, kvw + pc * LANES:kvw + (pc + 1) * LANES]
        kroll = pltpu.roll(kcol, ATTN_HEAD_DIM, axis=1)
        vroll = pltpu.roll(vcol, ATTN_HEAD_DIM, axis=1)
        if g % 2 == 0:
            k_lo, k_hi = jnp.where(low, kcol, 0.0), jnp.where(low, 0.0, kroll)
            v_lo, v_hi = jnp.where(low, vcol, 0.0), jnp.where(low, 0.0, vroll)
        else:
            k_lo, k_hi = jnp.where(low, kroll, 0.0), jnp.where(low, 0.0, kcol)
            v_lo, v_hi = jnp.where(low, vroll, 0.0), jnp.where(low, 0.0, vcol)
        k_lo, k_hi, v_lo, v_hi = (t.astype(BF16) for t in (k_lo, k_hi, v_lo, v_hi))

        ncol = rep // 2
        qs = jnp.concatenate([q_ref[rows, (g * ncol + c) * LANES:(g * ncol + c + 1) * LANES] for c in range(ncol)], axis=0)
        nt = (((1,), (1,)), ((), ()))
        s_even = lax.dot_general(qs, k_lo, nt, preferred_element_type=F32)
        s_odd = lax.dot_general(qs, k_hi, nt, preferred_element_type=F32)
        for c in range(ncol):
            out = None
            for par, s_all, vv in ((0, s_even, v_lo), (1, s_odd, v_hi)):
                h = g * rep + 2 * c + par
                logits = s_all[c * blk:(c + 1) * blk, :] + bias_ref[table, h]
                sink = sinks_ref[h]
                mx = jnp.maximum(jnp.max(logits, axis=-1, keepdims=True), sink)
                p = jnp.exp(logits - mx)
                denom = jnp.sum(p, axis=-1, keepdims=True) + jnp.exp(sink - mx)
                o = jnp.dot(p.astype(BF16), vv, preferred_element_type=F32) * (1.0 / denom)
                out = o if out is None else out + o
            col = g * ncol + c
            o_ref[rows, col * LANES:(col + 1) * LANES] = out.astype(o_ref.dtype)


def _t5_bucket_table():
    dist = np.arange(ATTN_BLOCK)[:, None] + ATTN_BLOCK - np.arange(2 * ATTN_BLOCK)[None, :]
    n = np.maximum(dist, 0)
    max_exact = REL_BUCKETS // 2
    nf = np.maximum(n, 1).astype(np.float32)
    large = max_exact + (np.log(nf / max_exact) / math.log(REL_MAX_DIST / max_exact)
                         * (REL_BUCKETS - max_exact)).astype(np.int32)
    large = np.minimum(large, REL_BUCKETS - 1)
    return dist, np.where(n < max_exact, n, large)


def _attention(qkv, sinks, rel_bias):
    m = qkv.shape[0]
    blk = ATTN_BLOCK
    qd = ATTN_Q_HEADS * ATTN_HEAD_DIM
    kvd = 2 * ATTN_KV_HEADS * ATTN_HEAD_DIM
    step_rows = blk * ATTN_SUBBLOCKS
    dist, bucket = _t5_bucket_table()
    visible = (dist >= 0) & (dist < blk)
    onehot = jnp.asarray(np.arange(REL_BUCKETS)[:, None] == bucket.reshape(1, -1), F32)
    bias = jnp.dot(rel_bias.astype(F32).T, onehot, precision=lax.Precision.HIGHEST).reshape(-1, blk, 2 * blk)
    general = jnp.where(visible[None], bias, NEG_INF)
    first = jnp.where((visible & (np.arange(2 * blk) >= blk)[None, :])[None], bias, NEG_INF)
    table = jnp.stack([first, general])
    return pl.pallas_call(
        _attn_kernel,
        grid=(m // step_rows,),
        in_specs=[
            pl.BlockSpec(memory_space=pltpu.SMEM),
            pl.BlockSpec((step_rows, qd), lambda n: (n, 0)),
            pl.BlockSpec((step_rows, kvd), lambda n: (n, qd // kvd)),
            pl.BlockSpec((blk, kvd), lambda n: (jnp.maximum(n * ATTN_SUBBLOCKS - 1, 0), qd // kvd)),
            pl.BlockSpec(table.shape, lambda n: (0, 0, 0, 0)),
        ],
        out_specs=pl.BlockSpec((step_rows, qd), lambda n: (n, 0)),
        out_shape=jax.ShapeDtypeStruct((m, qd), BF16),
        compiler_params=_params(("arbitrary",)),
        name="swa_attention",
    )(sinks.astype(F32), qkv, qkv, qkv, table)


def _router_kernel(x_ref, g_ref, rw_ref, rb_ref, o_ref):
    xn = _rms_rows(x_ref[...], g_ref[...])
    x_hi = xn.astype(BF16)
    x_lo = (xn - x_hi.astype(F32)).astype(BF16)
    both = jnp.dot(x_hi, rw_ref[...], preferred_element_type=F32)
    logits = (both[:, :LANES] + both[:, LANES:]
              + jnp.dot(x_lo, rw_ref[:, :LANES], preferred_element_type=F32) + rb_ref[...])
    lane = lax.broadcasted_iota(jnp.int32, logits.shape, 1)
    m1 = jnp.max(logits, axis=-1, keepdims=True)
    i1 = jnp.min(jnp.where(logits == m1, lane, LANES), axis=-1, keepdims=True)
    rest = jnp.where(lane == i1, NEG_INF, logits)
    m2 = jnp.max(rest, axis=-1, keepdims=True)
    i2 = jnp.min(jnp.where(rest == m2, lane, LANES), axis=-1, keepdims=True)
    e = jnp.exp(m2 - m1)
    g1 = 1.0 / (1.0 + e)
    g2 = e * g1
    out = jnp.where(lane == 0, i1.astype(F32),
                    jnp.where(lane == 1, i2.astype(F32),
                              jnp.where(lane == 2, g1, jnp.where(lane == 3, g2, 0.0))))
    o_ref[...] = out


def _router(h, g, rw, rb, *, tm):
    m, d = h.shape
    rw_pad = jnp.zeros((d, LANES), F32).at[:, :N_EXPERTS].set(rw.astype(F32))
    rw_hi = rw_pad.astype(BF16)
    rw_pad = jnp.concatenate([rw_hi, (rw_pad - rw_hi.astype(F32)).astype(BF16)], axis=1)
    rb_pad = jnp.full((1, LANES), NEG_INF, F32).at[0, :N_EXPERTS].set(rb.astype(F32))
    return pl.pallas_call(
        _router_kernel,
        grid=(m // tm,),
        in_specs=[
            pl.BlockSpec((tm, d), lambda i: (i, 0)),
            pl.BlockSpec((1, d), lambda i: (0, 0)),
            pl.BlockSpec((d, 2 * LANES), lambda i: (0, 0)),
            pl.BlockSpec((1, LANES), lambda i: (0, 0)),
        ],
        out_specs=pl.BlockSpec((tm, LANES), lambda i: (i, 0)),
        out_shape=jax.ShapeDtypeStruct((m, LANES), F32),
        compiler_params=_params(("parallel",)),
        name="router",
    )(h, g, rw_pad, rb_pad)


def _moe_drain(h_hbm, xbuf, sem):
    for s in range(MOE_SUBS_PER_GROUP):
        rs = slice(s * MOE_SUB, (s + 1) * MOE_SUB)
        pltpu.make_async_copy(h_hbm.at[pl.ds(0, MOE_SUB)], xbuf.at[rs], sem).wait()
    slack = xbuf.shape[0] - MOE_GROUP
    pltpu.make_async_copy(h_hbm.at[pl.ds(0, slack)], xbuf.at[MOE_GROUP:, :], sem).wait()


def _moe_kernel(ge_ref, ns_ref, tok_ref, h_hbm, g_ref, wg_ref, wu_ref, wd_ref, o_ref,
                xbuf, xn_ref, sem, *, rps):
    grp = pl.program_id(0)
    j = pl.program_id(1)
    ns = ns_ref[grp]

    def row_copy(gi, r):
        tok = tok_ref[gi * MOE_GROUP + jnp.minimum(r, MOE_GROUP - 1)]
        return pltpu.make_async_copy(h_hbm.at[pl.ds(tok, 1)], xbuf.at[pl.ds(r, 1)], sem)

    def issue_step_rows(gi):
        for u in range(rps):
            row_copy(gi, j * rps + u).start()

    @pl.when(j == 0)
    def _():
        @pl.when(grp == 0)
        def _():
            def issue(r, carry):
                for u in range(MOE_UNROLL):
                    row_copy(0, r * MOE_UNROLL + u).start()
                return carry

            lax.fori_loop(0, xbuf.shape[0] // MOE_UNROLL, issue, 0)

        _moe_drain(h_hbm, xbuf, sem)
        for s in range(MOE_SUBS_PER_GROUP):
            rs = slice(s * MOE_SUB, (s + 1) * MOE_SUB)
            o_ref[rs, :] = jnp.zeros((MOE_SUB, o_ref.shape[1]), F32)

            @pl.when(s < ns)
            def _():
                xn_ref[rs, :] = _rms_rows(xbuf[rs, :], g_ref[...]).astype(BF16)

    def ffn(rs):
        xn = xn_ref[rs, :]
        gate = jnp.dot(xn, wg_ref[...].astype(BF16), preferred_element_type=F32)
        up = jnp.dot(xn, wu_ref[...].astype(BF16), preferred_element_type=F32)
        hid = (_silu(gate) * up).astype(BF16)
        o_ref[rs, :] += jnp.dot(hid, wd_ref[...].astype(BF16), preferred_element_type=F32)

    for k in range(MOE_SUBS_PER_GROUP + 1):
        @pl.when(ns == k)
        def _():
            issue_step_rows(grp + 1)
            if k:
                ffn(slice(0, k * MOE_SUB))

    @pl.when((grp == pl.num_programs(0) - 1) & (j == pl.num_programs(1) - 1))
    def _():
        _moe_drain(h_hbm, xbuf, sem)


def _moe_ffn(h, g, wg, wu, wd, group_e, group_ns, src_tok, *, tf):
    t, d = h.shape
    n_groups = group_e.shape[0]
    f = wg.shape[2]
    nj = f // tf
    rps = -(-MOE_GROUP // (nj * SUBLANES)) * SUBLANES
    assert src_tok.shape[0] == (n_groups + 1) * MOE_GROUP

    def jj(grp, j, ns):
        return jnp.where(ns[grp] > 0, j, nj - 1)

    return pl.pallas_call(
        functools.partial(_moe_kernel, rps=rps),
        grid_spec=pltpu.PrefetchScalarGridSpec(
            num_scalar_prefetch=3,
            grid=(n_groups, nj),
            in_specs=[
                pl.BlockSpec(memory_space=pl.ANY),
                pl.BlockSpec((1, d), lambda grp, j, ge, ns, tok: (0, 0)),
                pl.BlockSpec((None, d, tf), lambda grp, j, ge, ns, tok: (ge[grp], 0, jj(grp, j, ns))),
                pl.BlockSpec((None, d, tf), lambda grp, j, ge, ns, tok: (ge[grp], 0, jj(grp, j, ns))),
                pl.BlockSpec((None, tf, d), lambda grp, j, ge, ns, tok: (ge[grp], jj(grp, j, ns), 0)),
            ],
            out_specs=pl.BlockSpec((MOE_GROUP, d), lambda grp, j, ge, ns, tok: (grp, 0)),
            scratch_shapes=[
                pltpu.VMEM((rps * nj, d), F32),
                pltpu.VMEM((MOE_GROUP, d), BF16),
                pltpu.SemaphoreType.DMA(()),
            ],
        ),
        out_shape=jax.ShapeDtypeStruct((n_groups * MOE_GROUP, d), F32),
        compiler_params=_params(("arbitrary", "arbitrary")),
        name="moe_ffn",
    )(group_e, group_ns, src_tok, h, g, wg, wu, wd)


def _combine_kernel(pos_ref, h_ref, route_ref, fg_ref, rows_hbm, o_ref, buf_a, buf_b, sem, *, tc):
    i = pl.program_id(0)
    slot = i % 2

    def copies(blk, sl, r):
        pa = pos_ref[TOP_K * (blk * tc + r)]
        pb = pos_ref[TOP_K * (blk * tc + r) + 1]
        return (pltpu.make_async_copy(rows_hbm.at[pl.ds(pa, 1)], buf_a.at[sl, pl.ds(r, 1)], sem.at[sl]),
                pltpu.make_async_copy(rows_hbm.at[pl.ds(pb, 1)], buf_b.at[sl, pl.ds(r, 1)], sem.at[sl]))

    def issue_block(blk, sl):
        def issue(r, carry):
            ca, cb = copies(blk, sl, r)
            ca.start()
            cb.start()
            return carry

        lax.fori_loop(0, tc, issue, 0, unroll=8)

    def wait_block(blk, sl):
        def wait(r, carry):
            ca, cb = copies(blk, sl, r)
            ca.wait()
            cb.wait()
            return carry

        lax.fori_loop(0, tc, wait, 0, unroll=8)

    @pl.when(i == 0)
    def _():
        issue_block(0, 0)

    wait_block(i, slot)

    last = pl.num_programs(0) - 1
    nxt = jnp.minimum(i + 1, last)
    for r in range(tc):
        ca, cb = copies(nxt, 1 - slot, r)
        ca.start()
        cb.start()
    route = route_ref[...]
    moe = route[:, 2:3] * buf_a[slot] + route[:, 3:4] * buf_b[slot]
    o_ref[...] = _rms_rows(h_ref[...] + moe, fg_ref[...])

    @pl.when(i == last)
    def _():
        wait_block(nxt, 1 - slot)


def _combine(h, route, final_g, rows, pos, *, tc):
    t, d = h.shape
    return pl.pallas_call(
        functools.partial(_combine_kernel, tc=tc),
        grid_spec=pltpu.PrefetchScalarGridSpec(
            num_scalar_prefetch=1,
            grid=(t // tc,),
            in_specs=[
                pl.BlockSpec((tc, d), lambda i, pos: (i, 0)),
                pl.BlockSpec((tc, LANES), lambda i, pos: (i, 0)),
                pl.BlockSpec((1, d), lambda i, pos: (0, 0)),
                pl.BlockSpec(memory_space=pl.ANY),
            ],
            out_specs=pl.BlockSpec((tc, d), lambda i, pos: (i, 0)),
            scratch_shapes=[pltpu.VMEM((2, tc, d), F32), pltpu.VMEM((2, tc, d), F32), pltpu.SemaphoreType.DMA((2,))],
        ),
        out_shape=jax.ShapeDtypeStruct((t, d), F32),
        compiler_params=_params(("arbitrary",)),
        name="moe_combine",
    )(pos, h, route, final_g, rows)


def _moe_plan(route, n_tok):
    flat_e = route[:, :TOP_K].astype(jnp.int32).reshape(-1)
    onehot = (flat_e[:, None] == jnp.arange(N_EXPERTS, dtype=jnp.int32)[None, :]).astype(jnp.int32)
    csum = jnp.cumsum(onehot, axis=0)
    rank = jnp.sum(onehot * (csum - onehot), axis=1)
    counts = csum[-1]
    n_groups = (n_tok * TOP_K) // MOE_GROUP + N_EXPERTS
    groups_e = (counts + MOE_GROUP - 1) // MOE_GROUP
    group_end = jnp.cumsum(groups_e)
    group_start = group_end - groups_e
    pos = jnp.sum(onehot * (group_start * MOE_GROUP)[None, :], axis=1) + rank
    flat_tok = jnp.arange(n_tok * TOP_K, dtype=jnp.int32) // TOP_K
    src_tok = jnp.zeros(((n_groups + 1) * MOE_GROUP,), jnp.int32).at[pos].set(
        flat_tok, unique_indices=True, mode="promise_in_bounds")
    experts = jnp.arange(N_EXPERTS, dtype=jnp.int32)
    gidx = jnp.arange(n_groups, dtype=jnp.int32)
    used = gidx < group_end[-1]
    last_e = jnp.max(jnp.where(groups_e > 0, experts, 0))
    ge = jnp.sum((group_end[None, :] <= gidx[:, None]).astype(jnp.int32), axis=1)
    ge = jnp.where(used, ge, last_e).astype(jnp.int32)
    mine = (ge[:, None] == experts[None, :]).astype(jnp.int32)
    subs_e = (counts + MOE_SUB - 1) // MOE_SUB
    subs_g = jnp.sum(mine * subs_e[None, :], axis=1)
    start_g = jnp.sum(mine * group_start[None, :], axis=1)
    ns = jnp.clip(subs_g - MOE_SUBS_PER_GROUP * (gidx - start_g), 0, MOE_SUBS_PER_GROUP)
    ns = jnp.where(used, ns, 0).astype(jnp.int32)
    return ge, ns, src_tok, pos.astype(jnp.int32)


def kernel(x, mix_norm_g, ffn_norm_g, final_norm_g, w_in, conv_w, conv_b, conv_ln_g, conv_ln_b, ssm_conv_w, ssm_conv_b, dt_bias, a_log, d_skip, ssm_norm_g, w_out, ffn_w_gate, ffn_w_up, ffn_w_down, w_qkv, b_qkv, w_o, b_o, sinks, rel_bias, router_w, router_b, moe_w_gate, moe_w_up, moe_w_down):
    bsz, seq, d = x.shape
    assert bsz == 1 and d == D_MODEL and seq % 512 == 0
    m = seq
    h = x.reshape(m, d)
    row = lambda v: v.reshape(1, -1).astype(F32)
    tm = min(1024, m)

    main_w = 2 * D_MODEL + D_MODEL + (D_MODEL + 2 * SSM_GROUPS * SSM_STATE)
    w_main = w_in[0].astype(BF16)
    w_dt = jnp.zeros((d, LANES), F32).at[:, :SSM_HEADS].set(w_in[0][:, main_w:]).astype(BF16)
    g0 = row(mix_norm_g[0])
    n_exp, _, expert_dim = moe_w_gate[0].shape
    wg_f32 = moe_w_gate[0].reshape(n_exp * d, expert_dim)
    wu_f32 = moe_w_up[0].reshape(n_exp * d, expert_dim)
    wd_f32 = moe_w_down[0].reshape(n_exp * expert_dim, d)

    proj, dt_raw, wd_b = _in_proj(h, g0, w_main, w_dt, wd_f32, tm=tm, tn=1024, n=main_w)
    conv_out, wg_b = _conv_module(proj, conv_w[0].astype(F32), row(conv_b[0]), row(conv_ln_g[0]), row(conv_ln_b[0]),
                                  wg_f32, tt=256)

    pad_heads = lambda v: jnp.zeros((1, LANES), F32).at[0, :SSM_HEADS].set(v.astype(F32))
    ssm_out, wu_b = _ssd(proj, dt_raw, ssm_conv_w[0].astype(F32), row(ssm_conv_b[0]),
                         pad_heads(dt_bias[0]), pad_heads(-jnp.exp(a_log[0].astype(F32))),
                         row(jnp.repeat(d_skip[0].astype(F32), SSM_HEAD_DIM)), row(ssm_norm_g[0]), wu_f32)

    wo = w_out[0].astype(BF16)
    h = _matmul_residual([conv_out, ssm_out], [wo[:D_MODEL], wo[D_MODEL:]], h, jnp.zeros((1, d), F32),
                         tm=256, tn=d, name="out_proj")
    h = _swiglu(h, row(ffn_norm_g[0]), ffn_w_gate[0].astype(BF16), ffn_w_up[0].astype(BF16),
                ffn_w_down[0].astype(BF16), tm=tm, tf=512)

    qkv = _norm_matmul(h, row(mix_norm_g[1]), w_qkv[0].astype(BF16), row(b_qkv[0]), tm=min(512, m), tn=w_qkv.shape[2],
                       out_dtype=BF16, name="qkv_proj")
    attn = _attention(qkv, sinks[0], rel_bias)
    h = _matmul_residual([attn], [w_o[0].astype(BF16)], h, row(b_o[0]), tm=min(512, m), tn=d, name="attn_out_proj")

    g1 = row(ffn_norm_g[1])
    route = _router(h, g1, router_w[0], router_b[0], tm=512)
    ge, ns, src_tok, pos = _moe_plan(route, m)
    rows = _moe_ffn(h, g1, wg_b.reshape(n_exp, d, expert_dim), wu_b.reshape(n_exp, d, expert_dim),
                    wd_b.reshape(n_exp, expert_dim, d), ge, ns, src_tok, tf=512)
    out = _combine(h, route, row(final_norm_g), rows, pos, tc=256)
    return out.reshape(bsz, seq, d)
```

```python
import functools
import math

import numpy as np
import jax
import jax.numpy as jnp
from jax import lax
from jax.experimental import pallas as pl
from jax.experimental.pallas import tpu as pltpu

F32 = jnp.float32
BF16 = jnp.bfloat16
NORM_EPS = 1e-5
NEG_INF = float("-inf")

LANES = 128
SUBLANES = 8
VMEM_LIMIT = 56 << 20

D_MODEL = 2048
CONV_WIDTH = 31
SSM_HEADS = 32
SSM_HEAD_DIM = 64
SSM_GROUPS = 8
SSM_STATE = 128
SSM_CONV_WIDTH = 4
SSM_CHUNK = 128
ATTN_Q_HEADS = 32
ATTN_KV_HEADS = 4
ATTN_HEAD_DIM = 64
ATTN_BLOCK = 128
ATTN_SUBBLOCKS = 2
REL_BUCKETS = 32
REL_MAX_DIST = 128
N_EXPERTS = 8
TOP_K = 2
MOE_SUB = 256
MOE_SUBS_PER_GROUP = 4
MOE_GROUP = MOE_SUB * MOE_SUBS_PER_GROUP
MOE_UNROLL = 8


def _params(sem):
    return pltpu.CompilerParams(dimension_semantics=sem, vmem_limit_bytes=VMEM_LIMIT)


def _sigmoid(x):
    return 0.5 + 0.5 * jnp.tanh(0.5 * x)


def _silu(x):
    half = 0.5 * x
    return half + half * jnp.tanh(half)


def _rms_rows(x, g):
    ms = jnp.mean(x * x, axis=-1, keepdims=True)
    return x * lax.rsqrt(ms + NORM_EPS) * g


def _cast_slab(i, n_steps, src_hbm, dst_hbm, inbuf, outbuf, in_sem, out_sem):
    rows = inbuf.shape[1]
    slot = i % 2

    def fetch(step, sl):
        return pltpu.make_async_copy(src_hbm.at[pl.ds(step * rows, rows)], inbuf.at[sl], in_sem.at[sl])

    def write_back(step, sl):
        return pltpu.make_async_copy(outbuf.at[sl], dst_hbm.at[pl.ds(step * rows, rows)], out_sem.at[sl])

    @pl.when(i == 0)
    def _():
        fetch(0, 0).start()

    @pl.when(i + 1 < n_steps)
    def _():
        fetch(i + 1, 1 - slot).start()

    fetch(i, slot).wait()

    @pl.when(i >= 2)
    def _():
        write_back(i - 2, slot).wait()

    outbuf[slot] = inbuf[slot].astype(BF16)
    write_back(i, slot).start()

    @pl.when(i == n_steps - 1)
    def _():
        write_back(i, slot).wait()

        @pl.when(i >= 1)
        def _():
            write_back(i - 1, 1 - slot).wait()


def _with_cast_job(kernel_fn, n_in, n_out):
    def wrapped(*refs):
        ins, src = refs[:n_in], refs[n_in]
        outs, dst = refs[n_in + 1:n_in + 1 + n_out], refs[n_in + 1 + n_out]
        rest = refs[n_in + 2 + n_out:]
        scratch, cast_scratch = rest[:-4], rest[-4:]
        _cast_slab(pl.program_id(0), pl.num_programs(0), src, dst, *cast_scratch)
        kernel_fn(*ins, *outs, *scratch)

    return wrapped


def _cast_job_extras(src, n_steps):
    assert src.shape[0] % n_steps == 0
    rows, cols = src.shape[0] // n_steps, src.shape[1]
    scratch = [pltpu.VMEM((2, rows, cols), F32), pltpu.VMEM((2, rows, cols), BF16),
               pltpu.SemaphoreType.DMA((2,)), pltpu.SemaphoreType.DMA((2,))]
    any_spec = pl.BlockSpec(memory_space=pl.ANY)
    return any_spec, any_spec, jax.ShapeDtypeStruct(src.shape, BF16), scratch


def _norm_matmul_kernel(x_ref, g_ref, w_ref, b_ref, o_ref, xn_ref):
    @pl.when(pl.program_id(1) == 0)
    def _():
        xn_ref[...] = _rms_rows(x_ref[...], g_ref[...]).astype(BF16)

    acc = jnp.dot(xn_ref[...], w_ref[...], preferred_element_type=F32)
    o_ref[...] = (acc + b_ref[...]).astype(o_ref.dtype)


def _norm_matmul(x, g, w, b, *, tm, tn, out_dtype, name):
    m, k = x.shape
    n = w.shape[1]
    return pl.pallas_call(
        _norm_matmul_kernel,
        grid=(m // tm, n // tn),
        in_specs=[
            pl.BlockSpec((tm, k), lambda i, j: (i, 0)),
            pl.BlockSpec((1, k), lambda i, j: (0, 0)),
            pl.BlockSpec((k, tn), lambda i, j: (0, j)),
            pl.BlockSpec((1, tn), lambda i, j: (0, j)),
        ],
        out_specs=pl.BlockSpec((tm, tn), lambda i, j: (i, j)),
        out_shape=jax.ShapeDtypeStruct((m, n), out_dtype),
        scratch_shapes=[pltpu.VMEM((tm, k), BF16)],
        compiler_params=_params(("parallel", "arbitrary")),
        name=name,
    )(x, g, w, b)


def _in_proj_kernel(x_ref, g_ref, w_ref, wdt_ref, cast_src, o_ref, odt_ref, cast_dst,
                    xn_ref, inbuf, outbuf, in_sem, out_sem, *, cast_cols):
    i = pl.program_id(0)
    j = pl.program_id(1)

    @pl.when(j == 0)
    def _():
        xn_ref[...] = _rms_rows(x_ref[...], g_ref[...]).astype(BF16)
        odt_ref[...] = jnp.dot(xn_ref[...], wdt_ref[...], preferred_element_type=F32)

    @pl.when(j < cast_cols)
    def _():
        _cast_slab(i * cast_cols + j, pl.num_programs(0) * cast_cols, cast_src, cast_dst,
                   inbuf, outbuf, in_sem, out_sem)

    o_ref[...] = jnp.dot(xn_ref[...], w_ref[...], preferred_element_type=F32).astype(o_ref.dtype)


def _in_proj(x, g, w, wdt, cast_src, *, tm, tn, n):
    m, k = x.shape
    assert n % tn == 0 and n <= w.shape[1]
    ndt = wdt.shape[1]
    n_rows, n_cols = m // tm, n // tn
    cast_cols = max(s for s in range(1, n_cols + 1)
                    if cast_src.shape[0] % (n_rows * s) == 0 and (cast_src.shape[0] // (n_rows * s)) % 16 == 0)
    cast_in, cast_out, cast_shape, cast_scratch = _cast_job_extras(cast_src, n_rows * cast_cols)
    return pl.pallas_call(
        functools.partial(_in_proj_kernel, cast_cols=cast_cols),
        grid=(n_rows, n_cols),
        in_specs=[
            pl.BlockSpec((tm, k), lambda i, j: (i, 0)),
            pl.BlockSpec((1, k), lambda i, j: (0, 0)),
            pl.BlockSpec((k, tn), lambda i, j: (0, j)),
            pl.BlockSpec((k, ndt), lambda i, j: (0, 0)),
            cast_in,
        ],
        out_specs=[pl.BlockSpec((tm, tn), lambda i, j: (i, j)), pl.BlockSpec((tm, ndt), lambda i, j: (i, 0)), cast_out],
        out_shape=[jax.ShapeDtypeStruct((m, n), BF16), jax.ShapeDtypeStruct((m, ndt), F32), cast_shape],
        scratch_shapes=[pltpu.VMEM((tm, k), BF16)] + cast_scratch,
        compiler_params=_params(("arbitrary", "arbitrary")),
        name="in_proj",
    )(x, g, w, wdt, cast_src)


def _matmul_residual_kernel(*refs, n_in):
    a_refs, w_refs = refs[:n_in], refs[n_in:2 * n_in]
    res_ref, b_ref, o_ref = refs[2 * n_in:]
    acc = res_ref[...] + b_ref[...]
    for a_ref, w_ref in zip(a_refs, w_refs):
        acc = acc + jnp.dot(a_ref[...], w_ref[...], preferred_element_type=F32)
    o_ref[...] = acc


def _matmul_residual(a_list, w_list, res, b, *, tm, tn, name):
    m, n = res.shape
    n_in = len(a_list)
    in_specs = [pl.BlockSpec((tm, a.shape[1]), lambda j, i: (i, 0)) for a in a_list]
    in_specs += [pl.BlockSpec((w.shape[0], tn), lambda j, i: (0, j)) for w in w_list]
    in_specs += [pl.BlockSpec((tm, tn), lambda j, i: (i, j)), pl.BlockSpec((1, tn), lambda j, i: (0, j))]
    return pl.pallas_call(
        functools.partial(_matmul_residual_kernel, n_in=n_in),
        grid=(n // tn, m // tm),
        in_specs=in_specs,
        out_specs=pl.BlockSpec((tm, tn), lambda j, i: (i, j)),
        out_shape=jax.ShapeDtypeStruct((m, n), F32),
        compiler_params=_params(("parallel", "parallel")),
        name=name,
    )(*a_list, *w_list, res, b)


def _swiglu_kernel(x_ref, g_ref, wg_ref, wu_ref, wd_ref, o_ref, xn_ref):
    @pl.when(pl.program_id(1) == 0)
    def _():
        x = x_ref[...]
        xn_ref[...] = _rms_rows(x, g_ref[...]).astype(BF16)
        o_ref[...] = x

    xn = xn_ref[...]
    gate = jnp.dot(xn, wg_ref[...], preferred_element_type=F32)
    up = jnp.dot(xn, wu_ref[...], preferred_element_type=F32)
    hid = (_silu(gate) * up).astype(BF16)
    o_ref[...] += jnp.dot(hid, wd_ref[...], preferred_element_type=F32)


def _swiglu(x, g, wg, wu, wd, *, tm, tf):
    m, d = x.shape
    f = wg.shape[1]
    return pl.pallas_call(
        _swiglu_kernel,
        grid=(m // tm, f // tf),
        in_specs=[
            pl.BlockSpec((tm, d), lambda i, j: (i, 0)),
            pl.BlockSpec((1, d), lambda i, j: (0, 0)),
            pl.BlockSpec((d, tf), lambda i, j: (0, j)),
            pl.BlockSpec((d, tf), lambda i, j: (0, j)),
            pl.BlockSpec((tf, d), lambda i, j: (j, 0)),
        ],
        out_specs=pl.BlockSpec((tm, d), lambda i, j: (i, 0)),
        out_shape=jax.ShapeDtypeStruct((m, d), F32),
        scratch_shapes=[pltpu.VMEM((tm, d), BF16)],
        compiler_params=_params(("parallel", "arbitrary")),
        name="swiglu",
    )(x, g, wg, wu, wd)


CONV_HALO = 32
CONV_ROWS = 64


def _conv_module_kernel(val_ref, gate_ref, pval_ref, pgate_ref, w_ref, b_ref, lg_ref, lb_ref, o_ref,
                        ubuf, cbuf, *, tt):
    i = pl.program_id(0)

    def glu(v, g):
        return v.astype(F32) * _sigmoid(g.astype(F32))

    ubuf[0:CONV_HALO, :] = jnp.where(i > 0, glu(pval_ref[...], pgate_ref[...]), 0.0)
    ubuf[CONV_HALO:CONV_HALO + tt, :] = glu(val_ref[...], gate_ref[...])

    off = CONV_HALO - (CONV_WIDTH - 1)
    taps = [[] for _ in range(SUBLANES)]
    for j in range(CONV_WIDTH):
        a, r = divmod(off + j, SUBLANES)
        taps[r].append((a, j))
    n_ch = ubuf.shape[1]
    for cb in range(n_ch // LANES):
        ls = slice(cb * LANES, (cb + 1) * LANES)
        for r0 in range(0, tt, CONV_ROWS):
            big = ubuf[r0:r0 + CONV_ROWS + CONV_HALO, ls]
            out = None
            for r in range(SUBLANES):
                rows = CONV_ROWS if r == 0 else CONV_ROWS + SUBLANES
                q = None
                for a, j in taps[r]:
                    term = w_ref[j:j + 1, ls] * big[a * SUBLANES:a * SUBLANES + rows, :]
                    q = term if q is None else q + term
                if r:
                    q = pltpu.roll(q, rows - r, axis=0)[:CONV_ROWS, :]
                out = q if out is None else out + q
            cbuf[r0:r0 + CONV_ROWS, ls] = out

    c = cbuf[...] + b_ref[...]
    mu = jnp.mean(c, axis=-1, keepdims=True)
    d = c - mu
    var = jnp.mean(d * d, axis=-1, keepdims=True)
    y = d * lax.rsqrt(var + NORM_EPS) * lg_ref[...] + lb_ref[...]
    o_ref[...] = _silu(y).astype(o_ref.dtype)


def _conv_module(proj, conv_w, conv_b, ln_g, ln_b, cast_src, *, tt):
    m = proj.shape[0]
    c = D_MODEL
    hb = tt // CONV_HALO
    cast_in, cast_out, cast_shape, cast_scratch = _cast_job_extras(cast_src, m // tt)
    return pl.pallas_call(
        _with_cast_job(functools.partial(_conv_module_kernel, tt=tt), 8, 1),
        grid=(m // tt,),
        in_specs=[
            pl.BlockSpec((tt, c), lambda i: (i, 0)),
            pl.BlockSpec((tt, c), lambda i: (i, 1)),
            pl.BlockSpec((CONV_HALO, c), lambda i: (jnp.maximum(i * hb - 1, 0), 0)),
            pl.BlockSpec((CONV_HALO, c), lambda i: (jnp.maximum(i * hb - 1, 0), 1)),
            pl.BlockSpec((CONV_WIDTH, c), lambda i: (0, 0)),
            pl.BlockSpec((1, c), lambda i: (0, 0)),
            pl.BlockSpec((1, c), lambda i: (0, 0)),
            pl.BlockSpec((1, c), lambda i: (0, 0)),
            cast_in,
        ],
        out_specs=[pl.BlockSpec((tt, c), lambda i: (i, 0)), cast_out],
        out_shape=[jax.ShapeDtypeStruct((m, c), BF16), cast_shape],
        scratch_shapes=[pltpu.VMEM((CONV_HALO + tt, c), F32), pltpu.VMEM((tt, c), F32)] + cast_scratch,
        compiler_params=_params(("arbitrary",)),
        name="conv_module",
    )(proj, proj, proj, proj, conv_w, conv_b, ln_g, ln_b, cast_src)


SSM_HALO = 16


def _conv4_silu(cur_ref, prev_ref, shift_ref, w_ref, b_ref, cols, first):
    q = cur_ref.shape[0]
    cur = cur_ref[:, cols]
    prev = prev_ref[:, cols]
    xb = jnp.concatenate([jnp.where(first, jnp.zeros_like(prev), prev), cur], axis=0)
    back = jnp.dot(shift_ref[...], xb, preferred_element_type=F32)
    last = SSM_CONV_WIDTH - 1
    acc = b_ref[:, cols] + w_ref[last:last + 1, cols] * cur.astype(F32)
    for k in range(1, SSM_CONV_WIDTH):
        acc = acc + w_ref[last - k:last - k + 1, cols] * back[(k - 1) * q:k * q, :]
    return _silu(acc)


def _ssd_kernel(z_ref, x_ref, b_ref, c_ref, px_ref, pb_ref, pc_ref, dtr_ref,
                wx_ref, wb_ref, wc_ref, bx_ref, bb_ref, bc_ref,
                dtb_ref, a_ref, dskip_ref, ng_ref, e_ref, sh_ref, o_ref,
                state, ybuf):
    i = pl.program_id(0)
    q = SSM_CHUNK
    first = i == 0

    @pl.when(first)
    def _():
        state[...] = jnp.zeros_like(state)

    everything = slice(None)
    xs = _conv4_silu(x_ref, px_ref, sh_ref, wx_ref, bx_ref, everything, first)
    bm = _conv4_silu(b_ref, pb_ref, sh_ref, wb_ref, bb_ref, everything, first)
    cm = _conv4_silu(c_ref, pc_ref, sh_ref, wc_ref, bc_ref, everything, first)

    pre = dtr_ref[...] + dtb_ref[...]
    dt = jnp.maximum(pre, 0.0) + jnp.log(1.0 + jnp.exp(-jnp.abs(pre)))
    dta = dt * a_ref[...]
    row = lax.broadcasted_iota(jnp.int32, (q, q), 0)
    col = lax.broadcasted_iota(jnp.int32, (q, q), 1)
    causal = row >= col
    acs = jnp.dot(causal.astype(F32), dta, precision=lax.Precision.HIGHEST, preferred_element_type=F32)
    acs_t = acs.T
    dt_t = dt.T
    eacs = jnp.exp(acs)
    wdec = dt * jnp.exp(acs[q - 1:q, :] - acs)

    def split(v):
        hi = v.astype(BF16)
        return hi, (v - hi.astype(F32)).astype(BF16)

    parts = jnp.concatenate(split(eacs) + split(wdec), axis=0)
    wide = jnp.dot(parts, e_ref[...], preferred_element_type=F32)
    eacs_full = wide[0:q] + wide[q:2 * q]
    xd = (xs * (wide[2 * q:3 * q] + wide[3 * q:4 * q])).astype(BF16)

    xs_b = xs.astype(BF16)
    lane = lax.broadcasted_iota(jnp.int32, (1, xs.shape[1]), 1)
    low_head = (lane % LANES) < SSM_HEAD_DIM
    xs_lo = jnp.where(low_head, xs_b, jnp.zeros_like(xs_b))
    xs_hi = jnp.where(low_head, jnp.zeros_like(xs_b), xs_b)

    gw = SSM_HEAD_DIM * (SSM_HEADS // SSM_GROUPS)
    for g in range(SSM_GROUPS):
        ns = slice(g * SSM_STATE, (g + 1) * SSM_STATE)
        gs = slice(g * gw, (g + 1) * gw)
        bg_f = bm[:, ns]
        bg = bg_f.astype(BF16)
        cg = cm[:, ns].astype(BF16)
        cb = lax.dot_general(cg, bg, (((1,), (1,)), ((), ())), preferred_element_type=F32)
        prev = state[g]
        y_off = jnp.dot(cg, prev.astype(BF16), preferred_element_type=F32)
        for c in range(gw // LANES):
            cs = slice(g * gw + c * LANES, g * gw + (c + 1) * LANES)
            acc = y_off[:, c * LANES:(c + 1) * LANES] * eacs_full[:, cs]
            for par, xpart in ((0, xs_lo), (1, xs_hi)):
                h = g * (SSM_HEADS // SSM_GROUPS) + 2 * c + par
                seg = acs[:, h:h + 1] - acs_t[h:h + 1, :]
                mmat = cb * jnp.exp(jnp.where(causal, seg, NEG_INF)) * dt_t[h:h + 1, :]
                acc = acc + jnp.dot(mmat.astype(BF16), xpart[:, cs], preferred_element_type=F32)
            ybuf[:, cs] = acc
        new_states = jnp.dot(bg_f.T.astype(BF16), xd[:, gs], preferred_element_type=F32)
        state[g] = prev * eacs_full[q - 1:q, gs] + new_states

    y = ybuf[...] + xs * dskip_ref[...]
    z = z_ref[...].astype(F32)
    y = y * _silu(z)
    for g in range(SSM_GROUPS):
        gs = slice(g * gw, (g + 1) * gw)
        yg = y[:, gs]
        ms = jnp.mean(yg * yg, axis=-1, keepdims=True)
        o_ref[:, gs] = (yg * lax.rsqrt(ms + NORM_EPS) * ng_ref[:, gs]).astype(o_ref.dtype)


def _ssd(proj, dt_raw, conv_w, conv_b, dt_bias, a_neg, d_skip, norm_g, cast_src):
    m = proj.shape[0]
    q = SSM_CHUNK
    inner = D_MODEL
    bc = SSM_GROUPS * SSM_STATE
    hb = q // SSM_HALO
    wx, wb, wc = conv_w[:, :inner], conv_w[:, inner:inner + bc], conv_w[:, inner + bc:]
    bx, bb, bcc = conv_b[:, :inner], conv_b[:, inner:inner + bc], conv_b[:, inner + bc:]
    expand = jnp.asarray(np.arange(LANES)[:, None] == (np.arange(inner) // SSM_HEAD_DIM)[None, :], BF16)
    t_idx = np.arange(q)
    shift_np = np.zeros((SSM_CONV_WIDTH - 1, q, SSM_HALO + q), np.float32)
    for k in range(1, SSM_CONV_WIDTH):
        shift_np[k - 1, t_idx, SSM_HALO + t_idx - k] = 1.0
    shift = jnp.asarray(shift_np.reshape(-1, SSM_HALO + q), BF16)
    prev = lambda i: jnp.maximum(i * hb - 1, 0)
    full = lambda shape: pl.BlockSpec(shape, lambda i: (0,) * len(shape))
    cast_in, cast_out, cast_shape, cast_scratch = _cast_job_extras(cast_src, m // q)
    return pl.pallas_call(
        _with_cast_job(_ssd_kernel, 20, 1),
        grid=(m // q,),
        in_specs=[
            pl.BlockSpec((q, inner), lambda i: (i, 2)),
            pl.BlockSpec((q, inner), lambda i: (i, 3)),
            pl.BlockSpec((q, bc), lambda i: (i, 8)),
            pl.BlockSpec((q, bc), lambda i: (i, 9)),
            pl.BlockSpec((SSM_HALO, inner), lambda i: (prev(i), 3)),
            pl.BlockSpec((SSM_HALO, bc), lambda i: (prev(i), 8)),
            pl.BlockSpec((SSM_HALO, bc), lambda i: (prev(i), 9)),
            pl.BlockSpec((q, LANES), lambda i: (i, 0)),
            full(wx.shape), full(wb.shape), full(wc.shape),
            full(bx.shape), full(bb.shape), full(bcc.shape),
            full((1, LANES)), full((1, LANES)), full((1, inner)), full((1, inner)),
            full((LANES, inner)),
            full(shift.shape),
            cast_in,
        ],
        out_specs=[pl.BlockSpec((q, inner), lambda i: (i, 0)), cast_out],
        out_shape=[jax.ShapeDtypeStruct((m, inner), BF16), cast_shape],
        scratch_shapes=[
            pltpu.VMEM((SSM_GROUPS, SSM_STATE, inner // SSM_GROUPS), F32),
            pltpu.VMEM((q, inner), F32),
        ] + cast_scratch,
        compiler_params=_params(("arbitrary",)),
        name="ssd",
    )(proj, proj, proj, proj, proj, proj, proj, dt_raw, wx, wb, wc, bx, bb, bcc,
      dt_bias, a_neg, d_skip, norm_g, expand, shift, cast_src)


def _attn_kernel(sinks_ref, q_ref, kvc_ref, kvp_ref, bias_ref, o_ref):
    blk = ATTN_BLOCK
    n = pl.program_id(0)
    for sub in range(ATTN_SUBBLOCKS):
        rows = slice(sub * blk, (sub + 1) * blk)
        prev = kvp_ref[...] if sub == 0 else kvc_ref[(sub - 1) * blk:sub * blk, :]
        table = jnp.where(n == 0, 0, 1) if sub == 0 else 1
        _attend_block(sinks_ref, q_ref, prev, kvc_ref[rows, :], bias_ref, table, o_ref, rows)


def _attend_block(sinks_ref, q_ref, kv_prev, kv_cur, bias_ref, table, o_ref, rows):
    blk = ATTN_BLOCK
    kvw = ATTN_KV_HEADS * ATTN_HEAD_DIM
    rep = ATTN_Q_HEADS // ATTN_KV_HEADS
    kv = jnp.concatenate([kv_prev, kv_cur], axis=0).astype(F32)
    lane = lax.broadcasted_iota(jnp.int32, (1, LANES), 1)
    low = lane < ATTN_HEAD_DIM
    scale = ATTN_HEAD_DIM ** -0.5

    for g in range(ATTN_KV_HEADS):
        pc = g // 2
        kcol = kv[:, pc * LANES:(pc + 1) * LANES] * scale
        vcol = kv[:, kvw + pc * LANES:kvw + (pc + 1) * LANES]
        kroll = pltpu.roll(kcol, ATTN_HEAD_DIM, axis=1)
        vroll = pltpu.roll(vcol, ATTN_HEAD_DIM, axis=1)
        if g % 2 == 0:
            k_lo, k_hi = jnp.where(low, kcol, 0.0), jnp.where(low, 0.0, kroll)
            v_lo, v_hi = jnp.where(low, vcol, 0.0), jnp.where(low, 0.0, vroll)
        else:
            k_lo, k_hi = jnp.where(low, kroll, 0.0), jnp.where(low, 0.0, kcol)
            v_lo, v_hi = jnp.where(low, vroll, 0.0), jnp.where(low, 0.0, vcol)
        k_lo, k_hi, v_lo, v_hi = (t.astype(BF16) for t in (k_lo, k_hi, v_lo, v_hi))

        ncol = rep // 2
        qs = jnp.concatenate([q_ref[rows, (g * ncol + c) * LANES:(g * ncol + c + 1) * LANES] for c in range(ncol)], axis=0)
        nt = (((1,), (1,)), ((), ()))
        s_even = lax.dot_general(qs, k_lo, nt, preferred_element_type=F32)
        s_odd = lax.dot_general(qs, k_hi, nt, preferred_element_type=F32)
        for c in range(ncol):
            out = None
            for par, s_all, vv in ((0, s_even, v_lo), (1, s_odd, v_hi)):
                h = g * rep + 2 * c + par
                logits = s_all[c * blk:(c + 1) * blk, :] + bias_ref[table, h]
                sink = sinks_ref[h]
                mx = jnp.maximum(jnp.max(logits, axis=-1, keepdims=True), sink)
                p = jnp.exp(logits - mx)
                denom = jnp.sum(p, axis=-1, keepdims=True) + jnp.exp(sink - mx)
                o = jnp.dot(p.astype(BF16), vv, preferred_element_type=F32) * (1.0 / denom)
                out = o if out is None else out + o
            col = g * ncol + c
            o_ref[rows, col * LANES:(col + 1) * LANES] = out.astype(o_ref.dtype)


def _t5_bucket_table():
    dist = np.arange(ATTN_BLOCK)[:, None] + ATTN_BLOCK - np.arange(2 * ATTN_BLOCK)[None, :]
    n = np.maximum(dist, 0)
    max_exact = REL_BUCKETS // 2
    nf = np.maximum(n, 1).astype(np.float32)
    large = max_exact + (np.log(nf / max_exact) / math.log(REL_MAX_DIST / max_exact)
                         * (REL_BUCKETS - max_exact)).astype(np.int32)
    large = np.minimum(large, REL_BUCKETS - 1)
    return dist, np.where(n < max_exact, n, large)


def _attention(qkv, sinks, rel_bias):
    m = qkv.shape[0]
    blk = ATTN_BLOCK
    qd = ATTN_Q_HEADS * ATTN_HEAD_DIM
    kvd = 2 * ATTN_KV_HEADS * ATTN_HEAD_DIM
    step_rows = blk * ATTN_SUBBLOCKS
    dist, bucket = _t5_bucket_table()
    visible = (dist >= 0) & (dist < blk)
    onehot = jnp.asarray(np.arange(REL_BUCKETS)[:, None] == bucket.reshape(1, -1), F32)
    bias = jnp.dot(rel_bias.astype(F32).T, onehot, precision=lax.Precision.HIGHEST).reshape(-1, blk, 2 * blk)
    general = jnp.where(visible[None], bias, NEG_INF)
    first = jnp.where((visible & (np.arange(2 * blk) >= blk)[None, :])[None], bias, NEG_INF)
    table = jnp.stack([first, general])
    return pl.pallas_call(
        _attn_kernel,
        grid=(m // step_rows,),
        in_specs=[
            pl.BlockSpec(memory_space=pltpu.SMEM),
            pl.BlockSpec((step_rows, qd), lambda n: (n, 0)),
            pl.BlockSpec((step_rows, kvd), lambda n: (n, qd // kvd)),
            pl.BlockSpec((blk, kvd), lambda n: (jnp.maximum(n * ATTN_SUBBLOCKS - 1, 0), qd // kvd)),
            pl.BlockSpec(table.shape, lambda n: (0, 0, 0, 0)),
        ],
        out_specs=pl.BlockSpec((step_rows, qd), lambda n: (n, 0)),
        out_shape=jax.ShapeDtypeStruct((m, qd), BF16),
        compiler_params=_params(("arbitrary",)),
        name="swa_attention",
    )(sinks.astype(F32), qkv, qkv, qkv, table)


def _router_kernel(x_ref, g_ref, rw_ref, rb_ref, o_ref):
    xn = _rms_rows(x_ref[...], g_ref[...])
    x_hi = xn.astype(BF16)
    x_lo = (xn - x_hi.astype(F32)).astype(BF16)
    both = jnp.dot(x_hi, rw_ref[...], preferred_element_type=F32)
    logits = (both[:, :LANES] + both[:, LANES:]
              + jnp.dot(x_lo, rw_ref[:, :LANES], preferred_element_type=F32) + rb_ref[...])
    lane = lax.broadcasted_iota(jnp.int32, logits.shape, 1)
    m1 = jnp.max(logits, axis=-1, keepdims=True)
    i1 = jnp.min(jnp.where(logits == m1, lane, LANES), axis=-1, keepdims=True)
    rest = jnp.where(lane == i1, NEG_INF, logits)
    m2 = jnp.max(rest, axis=-1, keepdims=True)
    i2 = jnp.min(jnp.where(rest == m2, lane, LANES), axis=-1, keepdims=True)
    e = jnp.exp(m2 - m1)
    g1 = 1.0 / (1.0 + e)
    g2 = e * g1
    out = jnp.where(lane == 0, i1.astype(F32),
                    jnp.where(lane == 1, i2.astype(F32),
                              jnp.where(lane == 2, g1, jnp.where(lane == 3, g2, 0.0))))
    o_ref[...] = out


def _router(h, g, rw, rb, *, tm):
    m, d = h.shape
    rw_pad = jnp.zeros((d, LANES), F32).at[:, :N_EXPERTS].set(rw.astype(F32))
    rw_hi = rw_pad.astype(BF16)
    rw_pad = jnp.concatenate([rw_hi, (rw_pad - rw_hi.astype(F32)).astype(BF16)], axis=1)
    rb_pad = jnp.full((1, LANES), NEG_INF, F32).at[0, :N_EXPERTS].set(rb.astype(F32))
    return pl.pallas_call(
        _router_kernel,
        grid=(m // tm,),
        in_specs=[
            pl.BlockSpec((tm, d), lambda i: (i, 0)),
            pl.BlockSpec((1, d), lambda i: (0, 0)),
            pl.BlockSpec((d, 2 * LANES), lambda i: (0, 0)),
            pl.BlockSpec((1, LANES), lambda i: (0, 0)),
        ],
        out_specs=pl.BlockSpec((tm, LANES), lambda i: (i, 0)),
        out_shape=jax.ShapeDtypeStruct((m, LANES), F32),
        compiler_params=_params(("parallel",)),
        name="router",
    )(h, g, rw_pad, rb_pad)


def _moe_drain(h_hbm, xbuf, sem):
    for s in range(MOE_SUBS_PER_GROUP):
        rs = slice(s * MOE_SUB, (s + 1) * MOE_SUB)
        pltpu.make_async_copy(h_hbm.at[pl.ds(0, MOE_SUB)], xbuf.at[rs], sem).wait()
    slack = xbuf.shape[0] - MOE_GROUP
    pltpu.make_async_copy(h_hbm.at[pl.ds(0, slack)], xbuf.at[MOE_GROUP:, :], sem).wait()


def _moe_kernel(ge_ref, ns_ref, tok_ref, h_hbm, g_ref, wg_ref, wu_ref, wd_ref, o_ref,
                xbuf, xn_ref, sem, *, rps):
    grp = pl.program_id(0)
    j = pl.program_id(1)
    ns = ns_ref[grp]

    def row_copy(gi, r):
        tok = tok_ref[gi * MOE_GROUP + jnp.minimum(r, MOE_GROUP - 1)]
        return pltpu.make_async_copy(h_hbm.at[pl.ds(tok, 1)], xbuf.at[pl.ds(r, 1)], sem)

    def issue_step_rows(gi):
        for u in range(rps):
            row_copy(gi, j * rps + u).start()

    @pl.when(j == 0)
    def _():
        @pl.when(grp == 0)
        def _():
            def issue(r, carry):
                for u in range(MOE_UNROLL):
                    row_copy(0, r * MOE_UNROLL + u).start()
                return carry

            lax.fori_loop(0, xbuf.shape[0] // MOE_UNROLL, issue, 0)

        _moe_drain(h_hbm, xbuf, sem)
        for s in range(MOE_SUBS_PER_GROUP):
            rs = slice(s * MOE_SUB, (s + 1) * MOE_SUB)
            o_ref[rs, :] = jnp.zeros((MOE_SUB, o_ref.shape[1]), F32)

            @pl.when(s < ns)
            def _():
                xn_ref[rs, :] = _rms_rows(xbuf[rs, :], g_ref[...]).astype(BF16)

    def ffn(rs):
        xn = xn_ref[rs, :]
        gate = jnp.dot(xn, wg_ref[...].astype(BF16), preferred_element_type=F32)
        up = jnp.dot(xn, wu_ref[...].astype(BF16), preferred_element_type=F32)
        hid = (_silu(gate) * up).astype(BF16)
        o_ref[rs, :] += jnp.dot(hid, wd_ref[...].astype(BF16), preferred_element_type=F32)

    for k in range(MOE_SUBS_PER_GROUP + 1):
        @pl.when(ns == k)
        def _():
            issue_step_rows(grp + 1)
            if k:
                ffn(slice(0, k * MOE_SUB))

    @pl.when((grp == pl.num_programs(0) - 1) & (j == pl.num_programs(1) - 1))
    def _():
        _moe_drain(h_hbm, xbuf, sem)


def _moe_ffn(h, g, wg, wu, wd, group_e, group_ns, src_tok, *, tf):
    t, d = h.shape
    n_groups = group_e.shape[0]
    f = wg.shape[2]
    nj = f // tf
    rps = -(-MOE_GROUP // (nj * SUBLANES)) * SUBLANES
    assert src_tok.shape[0] == (n_groups + 1) * MOE_GROUP

    def jj(grp, j, ns):
        return jnp.where(ns[grp] > 0, j, nj - 1)

    return pl.pallas_call(
        functools.partial(_moe_kernel, rps=rps),
        grid_spec=pltpu.PrefetchScalarGridSpec(
            num_scalar_prefetch=3,
            grid=(n_groups, nj),
            in_specs=[
                pl.BlockSpec(memory_space=pl.ANY),
                pl.BlockSpec((1, d), lambda grp, j, ge, ns, tok: (0, 0)),
                pl.BlockSpec((None, d, tf), lambda grp, j, ge, ns, tok: (ge[grp], 0, jj(grp, j, ns))),
                pl.BlockSpec((None, d, tf), lambda grp, j, ge, ns, tok: (ge[grp], 0, jj(grp, j, ns))),
                pl.BlockSpec((None, tf, d), lambda grp, j, ge, ns, tok: (ge[grp], jj(grp, j, ns), 0)),
            ],
            out_specs=pl.BlockSpec((MOE_GROUP, d), lambda grp, j, ge, ns, tok: (grp, 0)),
            scratch_shapes=[
                pltpu.VMEM((rps * nj, d), F32),
                pltpu.VMEM((MOE_GROUP, d), BF16),
                pltpu.SemaphoreType.DMA(()),
            ],
        ),
        out_shape=jax.ShapeDtypeStruct((n_groups * MOE_GROUP, d), F32),
        compiler_params=_params(("arbitrary", "arbitrary")),
        name="moe_ffn",
    )(group_e, group_ns, src_tok, h, g, wg, wu, wd)


def _combine_kernel(pos_ref, h_ref, route_ref, fg_ref, rows_hbm, o_ref, buf_a, buf_b, sem, *, tc):
    i = pl.program_id(0)
    slot = i % 2

    def copies(blk, sl, r):
        pa = pos_ref[TOP_K * (blk * tc + r)]
        pb = pos_ref[TOP_K * (blk * tc + r) + 1]
        return (pltpu.make_async_copy(rows_hbm.at[pl.ds(pa, 1)], buf_a.at[sl, pl.ds(r, 1)], sem.at[sl]),
                pltpu.make_async_copy(rows_hbm.at[pl.ds(pb, 1)], buf_b.at[sl, pl.ds(r, 1)], sem.at[sl]))

    def issue_block(blk, sl):
        def issue(r, carry):
            ca, cb = copies(blk, sl, r)
            ca.start()
            cb.start()
            return carry

        lax.fori_loop(0, tc, issue, 0, unroll=8)

    def wait_block(blk, sl):
        def wait(r, carry):
            ca, cb = copies(blk, sl, r)
            ca.wait()
            cb.wait()
            return carry

        lax.fori_loop(0, tc, wait, 0, unroll=8)

    @pl.when(i == 0)
    def _():
        issue_block(0, 0)

    wait_block(i, slot)

    last = pl.num_programs(0) - 1
    nxt = jnp.minimum(i + 1, last)
    for r in range(tc):
        ca, cb = copies(nxt, 1 - slot, r)
        ca.start()
        cb.start()
    route = route_ref[...]
    moe = route[:, 2:3] * buf_a[slot] + route[:, 3:4] * buf_b[slot]
    o_ref[...] = _rms_rows(h_ref[...] + moe, fg_ref[...])

    @pl.when(i == last)
    def _():
        wait_block(nxt, 1 - slot)


def _combine(h, route, final_g, rows, pos, *, tc):
    t, d = h.shape
    return pl.pallas_call(
        functools.partial(_combine_kernel, tc=tc),
        grid_spec=pltpu.PrefetchScalarGridSpec(
            num_scalar_prefetch=1,
            grid=(t // tc,),
            in_specs=[
                pl.BlockSpec((tc, d), lambda i, pos: (i, 0)),
                pl.BlockSpec((tc, LANES), lambda i, pos: (i, 0)),
                pl.BlockSpec((1, d), lambda i, pos: (0, 0)),
                pl.BlockSpec(memory_space=pl.ANY),
            ],
            out_specs=pl.BlockSpec((tc, d), lambda i, pos: (i, 0)),
            scratch_shapes=[pltpu.VMEM((2, tc, d), F32), pltpu.VMEM((2, tc, d), F32), pltpu.SemaphoreType.DMA((2,))],
        ),
        out_shape=jax.ShapeDtypeStruct((t, d), F32),
        compiler_params=_params(("arbitrary",)),
        name="moe_combine",
    )(pos, h, route, final_g, rows)


def _moe_plan(route, n_tok):
    flat_e = route[:, :TOP_K].astype(jnp.int32).reshape(-1)
    onehot = (flat_e[:, None] == jnp.arange(N_EXPERTS, dtype=jnp.int32)[None, :]).astype(jnp.int32)
    csum = jnp.cumsum(onehot, axis=0)
    rank = jnp.sum(onehot * (csum - onehot), axis=1)
    counts = csum[-1]
    n_groups = (n_tok * TOP_K) // MOE_GROUP + N_EXPERTS
    groups_e = (counts + MOE_GROUP - 1) // MOE_GROUP
    group_end = jnp.cumsum(groups_e)
    group_start = group_end - groups_e
    pos = jnp.sum(onehot * (group_start * MOE_GROUP)[None, :], axis=1) + rank
    flat_tok = jnp.arange(n_tok * TOP_K, dtype=jnp.int32) // TOP_K
    src_tok = jnp.zeros(((n_groups + 1) * MOE_GROUP,), jnp.int32).at[pos].set(
        flat_tok, unique_indices=True, mode="promise_in_bounds")
    experts = jnp.arange(N_EXPERTS, dtype=jnp.int32)
    gidx = jnp.arange(n_groups, dtype=jnp.int32)
    used = gidx < group_end[-1]
    last_e = jnp.max(jnp.where(groups_e > 0, experts, 0))
    ge = jnp.sum((group_end[None, :] <= gidx[:, None]).astype(jnp.int32), axis=1)
    ge = jnp.where(used, ge, last_e).astype(jnp.int32)
    mine = (ge[:, None] == experts[None, :]).astype(jnp.int32)
    subs_e = (counts + MOE_SUB - 1) // MOE_SUB
    subs_g = jnp.sum(mine * subs_e[None, :], axis=1)
    start_g = jnp.sum(mine * group_start[None, :], axis=1)
    ns = jnp.clip(subs_g - MOE_SUBS_PER_GROUP * (gidx - start_g), 0, MOE_SUBS_PER_GROUP)
    ns = jnp.where(used, ns, 0).astype(jnp.int32)
    return ge, ns, src_tok, pos.astype(jnp.int32)


def kernel(x, mix_norm_g, ffn_norm_g, final_norm_g, w_in, conv_w, conv_b, conv_ln_g, conv_ln_b, ssm_conv_w, ssm_conv_b, dt_bias, a_log, d_skip, ssm_norm_g, w_out, ffn_w_gate, ffn_w_up, ffn_w_down, w_qkv, b_qkv, w_o, b_o, sinks, rel_bias, router_w, router_b, moe_w_gate, moe_w_up, moe_w_down):
    bsz, seq, d = x.shape
    assert bsz == 1 and d == D_MODEL and seq % 512 == 0
    m = seq
    h = x.reshape(m, d)
    row = lambda v: v.reshape(1, -1).astype(F32)
    tm = min(1024, m)

    main_w = 2 * D_MODEL + D_MODEL + (D_MODEL + 2 * SSM_GROUPS * SSM_STATE)
    w_main = w_in[0].astype(BF16)
    w_dt = jnp.zeros((d, LANES), F32).at[:, :SSM_HEADS].set(w_in[0][:, main_w:]).astype(BF16)
    g0 = row(mix_norm_g[0])
    n_exp, _, expert_dim = moe_w_gate[0].shape
    wg_f32 = moe_w_gate[0].reshape(n_exp * d, expert_dim)
    wu_f32 = moe_w_up[0].reshape(n_exp * d, expert_dim)
    wd_f32 = moe_w_down[0].reshape(n_exp * expert_dim, d)

    proj, dt_raw, wd_b = _in_proj(h, g0, w_main, w_dt, wd_f32, tm=tm, tn=1024, n=main_w)
    conv_out, wg_b = _conv_module(proj, conv_w[0].astype(F32), row(conv_b[0]), row(conv_ln_g[0]), row(conv_ln_b[0]),
                                  wg_f32, tt=256)

    pad_heads = lambda v: jnp.zeros((1, LANES), F32).at[0, :SSM_HEADS].set(v.astype(F32))
    ssm_out, wu_b = _ssd(proj, dt_raw, ssm_conv_w[0].astype(F32), row(ssm_conv_b[0]),
                         pad_heads(dt_bias[0]), pad_heads(-jnp.exp(a_log[0].astype(F32))),
                         row(jnp.repeat(d_skip[0].astype(F32), SSM_HEAD_DIM)), row(ssm_norm_g[0]), wu_f32)

    wo = w_out[0].astype(BF16)
    h = _matmul_residual([conv_out, ssm_out], [wo[:D_MODEL], wo[D_MODEL:]], h, jnp.zeros((1, d), F32),
                         tm=256, tn=d, name="out_proj")
    h = _swiglu(h, row(ffn_norm_g[0]), ffn_w_gate[0].astype(BF16), ffn_w_up[0].astype(BF16),
                ffn_w_down[0].astype(BF16), tm=tm, tf=512)

    qkv = _norm_matmul(h, row(mix_norm_g[1]), w_qkv[0].astype(BF16), row(b_qkv[0]), tm=min(512, m), tn=w_qkv.shape[2],
                       out_dtype=BF16, name="qkv_proj")
    attn = _attention(qkv, sinks[0], rel_bias)
    h = _matmul_residual([attn], [w_o[0].astype(BF16)], h, row(b_o[0]), tm=min(512, m), tn=d, name="attn_out_proj")

    g1 = row(ffn_norm_g[1])
    route = _router(h, g1, router_w[0], router_b[0], tm=512)
    ge, ns, src_tok, pos = _moe_plan(route, m)
    rows = _moe_ffn(h, g1, wg_b.reshape(n_exp, d, expert_dim), wu_b.reshape(n_exp, d, expert_dim),
                    wd_b.reshape(n_exp, expert_dim, d), ge, ns, src_tok, tf=512)
    out = _combine(h, route, row(final_norm_g), rows, pos, tc=256)
    return out.reshape(bsz, seq, d)
```

```python
import functools
import math

import numpy as np
import jax
import jax.numpy as jnp
from jax import lax
from jax.experimental import pallas as pl
from jax.experimental.pallas import tpu as pltpu

F32 = jnp.float32
BF16 = jnp.bfloat16
NORM_EPS = 1e-5
NEG_INF = float("-inf")

LANES = 128
SUBLANES = 8
VMEM_LIMIT = 56 << 20

D_MODEL = 2048
CONV_WIDTH = 31
SSM_HEADS = 32
SSM_HEAD_DIM = 64
SSM_GROUPS = 8
SSM_STATE = 128
SSM_CONV_WIDTH = 4
SSM_CHUNK = 128
ATTN_Q_HEADS = 32
ATTN_KV_HEADS = 4
ATTN_HEAD_DIM = 64
ATTN_BLOCK = 128
ATTN_SUBBLOCKS = 2
REL_BUCKETS = 32
REL_MAX_DIST = 128
N_EXPERTS = 8
TOP_K = 2
MOE_SUB = 256
MOE_SUBS_PER_GROUP = 4
MOE_GROUP = MOE_SUB * MOE_SUBS_PER_GROUP
MOE_UNROLL = 8


def _params(sem):
    return pltpu.CompilerParams(dimension_semantics=sem, vmem_limit_bytes=VMEM_LIMIT)


def _sigmoid(x):
    return 0.5 + 0.5 * jnp.tanh(0.5 * x)


def _silu(x):
    half = 0.5 * x
    return half + half * jnp.tanh(half)


def _rms_rows(x, g):
    ms = jnp.mean(x * x, axis=-1, keepdims=True)
    return x * lax.rsqrt(ms + NORM_EPS) * g


def _cast_slab(i, n_steps, src_hbm, dst_hbm, inbuf, outbuf, in_sem, out_sem):
    rows = inbuf.shape[1]
    slot = i % 2

    def fetch(step, sl):
        return pltpu.make_async_copy(src_hbm.at[pl.ds(step * rows, rows)], inbuf.at[sl], in_sem.at[sl])

    def write_back(step, sl):
        return pltpu.make_async_copy(outbuf.at[sl], dst_hbm.at[pl.ds(step * rows, rows)], out_sem.at[sl])

    @pl.when(i == 0)
    def _():
        fetch(0, 0).start()

    @pl.when(i + 1 < n_steps)
    def _():
        fetch(i + 1, 1 - slot).start()

    fetch(i, slot).wait()

    @pl.when(i >= 2)
    def _():
        write_back(i - 2, slot).wait()

    outbuf[slot] = inbuf[slot].astype(BF16)
    write_back(i, slot).start()

    @pl.when(i == n_steps - 1)
    def _():
        write_back(i, slot).wait()

        @pl.when(i >= 1)
        def _():
            write_back(i - 1, 1 - slot).wait()


def _with_cast_job(kernel_fn, n_in, n_out):
    def wrapped(*refs):
        ins, src = refs[:n_in], refs[n_in]
        outs, dst = refs[n_in + 1:n_in + 1 + n_out], refs[n_in + 1 + n_out]
        rest = refs[n_in + 2 + n_out:]
        scratch, cast_scratch = rest[:-4], rest[-4:]
        _cast_slab(pl.program_id(0), pl.num_programs(0), src, dst, *cast_scratch)
        kernel_fn(*ins, *outs, *scratch)

    return wrapped


def _cast_job_extras(src, n_steps):
    assert src.shape[0] % n_steps == 0
    rows, cols = src.shape[0] // n_steps, src.shape[1]
    scratch = [pltpu.VMEM((2, rows, cols), F32), pltpu.VMEM((2, rows, cols), BF16),
               pltpu.SemaphoreType.DMA((2,)), pltpu.SemaphoreType.DMA((2,))]
    any_spec = pl.BlockSpec(memory_space=pl.ANY)
    return any_spec, any_spec, jax.ShapeDtypeStruct(src.shape, BF16), scratch


def _norm_matmul_kernel(x_ref, g_ref, w_ref, b_ref, o_ref, xn_ref):
    @pl.when(pl.program_id(1) == 0)
    def _():
        xn_ref[...] = _rms_rows(x_ref[...], g_ref[...]).astype(BF16)

    acc = jnp.dot(xn_ref[...], w_ref[...], preferred_element_type=F32)
    o_ref[...] = (acc + b_ref[...]).astype(o_ref.dtype)


def _norm_matmul(x, g, w, b, *, tm, tn, out_dtype, name):
    m, k = x.shape
    n = w.shape[1]
    return pl.pallas_call(
        _norm_matmul_kernel,
        grid=(m // tm, n // tn),
        in_specs=[
            pl.BlockSpec((tm, k), lambda i, j: (i, 0)),
            pl.BlockSpec((1, k), lambda i, j: (0, 0)),
            pl.BlockSpec((k, tn), lambda i, j: (0, j)),
            pl.BlockSpec((1, tn), lambda i, j: (0, j)),
        ],
        out_specs=pl.BlockSpec((tm, tn), lambda i, j: (i, j)),
        out_shape=jax.ShapeDtypeStruct((m, n), out_dtype),
        scratch_shapes=[pltpu.VMEM((tm, k), BF16)],
        compiler_params=_params(("parallel", "arbitrary")),
        name=name,
    )(x, g, w, b)


def _in_proj_kernel(x_ref, g_ref, w_ref, wdt_ref, cast_src, o_ref, odt_ref, cast_dst,
                    xn_ref, inbuf, outbuf, in_sem, out_sem, *, cast_cols):
    i = pl.program_id(0)
    j = pl.program_id(1)

    @pl.when(j == 0)
    def _():
        xn_ref[...] = _rms_rows(x_ref[...], g_ref[...]).astype(BF16)
        odt_ref[...] = jnp.dot(xn_ref[...], wdt_ref[...], preferred_element_type=F32)

    @pl.when(j < cast_cols)
    def _():
        _cast_slab(i * cast_cols + j, pl.num_programs(0) * cast_cols, cast_src, cast_dst,
                   inbuf, outbuf, in_sem, out_sem)

    o_ref[...] = jnp.dot(xn_ref[...], w_ref[...], preferred_element_type=F32).astype(o_ref.dtype)


def _in_proj(x, g, w, wdt, cast_src, *, tm, tn, n):
    m, k = x.shape
    assert n % tn == 0 and n <= w.shape[1]
    ndt = wdt.shape[1]
    n_rows, n_cols = m // tm, n // tn
    cast_cols = max(s for s in range(1, n_cols + 1)
                    if cast_src.shape[0] % (n_rows * s) == 0 and (cast_src.shape[0] // (n_rows * s)) % 16 == 0)
    cast_in, cast_out, cast_shape, cast_scratch = _cast_job_extras(cast_src, n_rows * cast_cols)
    return pl.pallas_call(
        functools.partial(_in_proj_kernel, cast_cols=cast_cols),
        grid=(n_rows, n_cols),
        in_specs=[
            pl.BlockSpec((tm, k), lambda i, j: (i, 0)),
            pl.BlockSpec((1, k), lambda i, j: (0, 0)),
            pl.BlockSpec((k, tn), lambda i, j: (0, j)),
            pl.BlockSpec((k, ndt), lambda i, j: (0, 0)),
            cast_in,
        ],
        out_specs=[pl.BlockSpec((tm, tn), lambda i, j: (i, j)), pl.BlockSpec((tm, ndt), lambda i, j: (i, 0)), cast_out],
        out_shape=[jax.ShapeDtypeStruct((m, n), BF16), jax.ShapeDtypeStruct((m, ndt), F32), cast_shape],
        scratch_shapes=[pltpu.VMEM((tm, k), BF16)] + cast_scratch,
        compiler_params=_params(("arbitrary", "arbitrary")),
        name="in_proj",
    )(x, g, w, wdt, cast_src)


def _matmul_residual_kernel(*refs, n_in):
    a_refs, w_refs = refs[:n_in], refs[n_in:2 * n_in]
    res_ref, b_ref, o_ref = refs[2 * n_in:]
    acc = res_ref[...] + b_ref[...]
    for a_ref, w_ref in zip(a_refs, w_refs):
        acc = acc + jnp.dot(a_ref[...], w_ref[...], preferred_element_type=F32)
    o_ref[...] = acc


def _matmul_residual(a_list, w_list, res, b, *, tm, tn, name):
    m, n = res.shape
    n_in = len(a_list)
    in_specs = [pl.BlockSpec((tm, a.shape[1]), lambda j, i: (i, 0)) for a in a_list]
    in_specs += [pl.BlockSpec((w.shape[0], tn), lambda j, i: (0, j)) for w in w_list]
    in_specs += [pl.BlockSpec((tm, tn), lambda j, i: (i, j)), pl.BlockSpec((1, tn), lambda j, i: (0, j))]
    return pl.pallas_call(
        functools.partial(_matmul_residual_kernel, n_in=n_in),
        grid=(n // tn, m // tm),
        in_specs=in_specs,
        out_specs=pl.BlockSpec((tm, tn), lambda j, i: (i, j)),
        out_shape=jax.ShapeDtypeStruct((m, n), F32),
        compiler_params=_params(("parallel", "parallel")),
        name=name,
    )(*a_list, *w_list, res, b)


def _swiglu_kernel(x_ref, g_ref, wg_ref, wu_ref, wd_ref, o_ref, xn_ref):
    @pl.when(pl.program_id(1) == 0)
    def _():
        x = x_ref[...]
        xn_ref[...] = _rms_rows(x, g_ref[...]).astype(BF16)
        o_ref[...] = x

    xn = xn_ref[...]
    gate = jnp.dot(xn, wg_ref[...], preferred_element_type=F32)
    up = jnp.dot(xn, wu_ref[...], preferred_element_type=F32)
    hid = (_silu(gate) * up).astype(BF16)
    o_ref[...] += jnp.dot(hid, wd_ref[...], preferred_element_type=F32)


def _swiglu(x, g, wg, wu, wd, *, tm, tf):
    m, d = x.shape
    f = wg.shape[1]
    return pl.pallas_call(
        _swiglu_kernel,
        grid=(m // tm, f // tf),
        in_specs=[
            pl.BlockSpec((tm, d), lambda i, j: (i, 0)),
            pl.BlockSpec((1, d), lambda i, j: (0, 0)),
            pl.BlockSpec((d, tf), lambda i, j: (0, j)),
            pl.BlockSpec((d, tf), lambda i, j: (0, j)),
            pl.BlockSpec((tf, d), lambda i, j: (j, 0)),
        ],
        out_specs=pl.BlockSpec((tm, d), lambda i, j: (i, 0)),
        out_shape=jax.ShapeDtypeStruct((m, d), F32),
        scratch_shapes=[pltpu.VMEM((tm, d), BF16)],
        compiler_params=_params(("parallel", "arbitrary")),
        name="swiglu",
    )(x, g, wg, wu, wd)


CONV_HALO = 32
CONV_ROWS = 64


def _conv_module_kernel(val_ref, gate_ref, pval_ref, pgate_ref, w_ref, b_ref, lg_ref, lb_ref, o_ref,
                        ubuf, cbuf, *, tt):
    i = pl.program_id(0)

    def glu(v, g):
        return v.astype(F32) * _sigmoid(g.astype(F32))

    ubuf[0:CONV_HALO, :] = jnp.where(i > 0, glu(pval_ref[...], pgate_ref[...]), 0.0)
    ubuf[CONV_HALO:CONV_HALO + tt, :] = glu(val_ref[...], gate_ref[...])

    off = CONV_HALO - (CONV_WIDTH - 1)
    taps = [[] for _ in range(SUBLANES)]
    for j in range(CONV_WIDTH):
        a, r = divmod(off + j, SUBLANES)
        taps[r].append((a, j))
    n_ch = ubuf.shape[1]
    for cb in range(n_ch // LANES):
        ls = slice(cb * LANES, (cb + 1) * LANES)
        for r0 in range(0, tt, CONV_ROWS):
            big = ubuf[r0:r0 + CONV_ROWS + CONV_HALO, ls]
            out = None
            for r in range(SUBLANES):
                rows = CONV_ROWS if r == 0 else CONV_ROWS + SUBLANES
                q = None
                for a, j in taps[r]:
                    term = w_ref[j:j + 1, ls] * big[a * SUBLANES:a * SUBLANES + rows, :]
                    q = term if q is None else q + term
                if r:
                    q = pltpu.roll(q, rows - r, axis=0)[:CONV_ROWS, :]
                out = q if out is None else out + q
            cbuf[r0:r0 + CONV_ROWS, ls] = out

    c = cbuf[...] + b_ref[...]
    mu = jnp.mean(c, axis=-1, keepdims=True)
    d = c - mu
    var = jnp.mean(d * d, axis=-1, keepdims=True)
    y = d * lax.rsqrt(var + NORM_EPS) * lg_ref[...] + lb_ref[...]
    o_ref[...] = _silu(y).astype(o_ref.dtype)


def _conv_module(proj, conv_w, conv_b, ln_g, ln_b, cast_src, *, tt):
    m = proj.shape[0]
    c = D_MODEL
    hb = tt // CONV_HALO
    cast_in, cast_out, cast_shape, cast_scratch = _cast_job_extras(cast_src, m // tt)
    return pl.pallas_call(
        _with_cast_job(functools.partial(_conv_module_kernel, tt=tt), 8, 1),
        grid=(m // tt,),
        in_specs=[
            pl.BlockSpec((tt, c), lambda i: (i, 0)),
            pl.BlockSpec((tt, c), lambda i: (i, 1)),
            pl.BlockSpec((CONV_HALO, c), lambda i: (jnp.maximum(i * hb - 1, 0), 0)),
            pl.BlockSpec((CONV_HALO, c), lambda i: (jnp.maximum(i * hb - 1, 0), 1)),
            pl.BlockSpec((CONV_WIDTH, c), lambda i: (0, 0)),
            pl.BlockSpec((1, c), lambda i: (0, 0)),
            pl.BlockSpec((1, c), lambda i: (0, 0)),
            pl.BlockSpec((1, c), lambda i: (0, 0)),
            cast_in,
        ],
        out_specs=[pl.BlockSpec((tt, c), lambda i: (i, 0)), cast_out],
        out_shape=[jax.ShapeDtypeStruct((m, c), BF16), cast_shape],
        scratch_shapes=[pltpu.VMEM((CONV_HALO + tt, c), F32), pltpu.VMEM((tt, c), F32)] + cast_scratch,
        compiler_params=_params(("arbitrary",)),
        name="conv_module",
    )(proj, proj, proj, proj, conv_w, conv_b, ln_g, ln_b, cast_src)


SSM_HALO = 16


def _conv4_silu(cur_ref, prev_ref, shift_ref, w_ref, b_ref, cols, first):
    q = cur_ref.shape[0]
    cur = cur_ref[:, cols]
    prev = prev_ref[:, cols]
    xb = jnp.concatenate([jnp.where(first, jnp.zeros_like(prev), prev), cur], axis=0)
    back = jnp.dot(shift_ref[...], xb, preferred_element_type=F32)
    last = SSM_CONV_WIDTH - 1
    acc = b_ref[:, cols] + w_ref[last:last + 1, cols] * cur.astype(F32)
    for k in range(1, SSM_CONV_WIDTH):
        acc = acc + w_ref[last - k:last - k + 1, cols] * back[(k - 1) * q:k * q, :]
    return _silu(acc)


def _ssd_kernel(z_ref, x_ref, b_ref, c_ref, px_ref, pb_ref, pc_ref, dtr_ref,
                wx_ref, wb_ref, wc_ref, bx_ref, bb_ref, bc_ref,
                dtb_ref, a_ref, dskip_ref, ng_ref, e_ref, sh_ref, o_ref,
                state, ybuf):
    i = pl.program_id(0)
    q = SSM_CHUNK
    first = i == 0

    @pl.when(first)
    def _():
        state[...] = jnp.zeros_like(state)

    everything = slice(None)
    xs = _conv4_silu(x_ref, px_ref, sh_ref, wx_ref, bx_ref, everything, first)
    bm = _conv4_silu(b_ref, pb_ref, sh_ref, wb_ref, bb_ref, everything, first)
    cm = _conv4_silu(c_ref, pc_ref, sh_ref, wc_ref, bc_ref, everything, first)

    pre = dtr_ref[...] + dtb_ref[...]
    dt = jnp.maximum(pre, 0.0) + jnp.log(1.0 + jnp.exp(-jnp.abs(pre)))
    dta = dt * a_ref[...]
    row = lax.broadcasted_iota(jnp.int32, (q, q), 0)
    col = lax.broadcasted_iota(jnp.int32, (q, q), 1)
    causal = row >= col
    acs = jnp.dot(causal.astype(F32), dta, precision=lax.Precision.HIGHEST, preferred_element_type=F32)
    acs_t = acs.T
    dt_t = dt.T
    eacs = jnp.exp(acs)
    wdec = dt * jnp.exp(acs[q - 1:q, :] - acs)

    def split(v):
        hi = v.astype(BF16)
        return hi, (v - hi.astype(F32)).astype(BF16)

    parts = jnp.concatenate(split(eacs) + split(wdec), axis=0)
    wide = jnp.dot(parts, e_ref[...], preferred_element_type=F32)
    eacs_full = wide[0:q] + wide[q:2 * q]
    xd = (xs * (wide[2 * q:3 * q] + wide[3 * q:4 * q])).astype(BF16)

    xs_b = xs.astype(BF16)
    lane = lax.broadcasted_iota(jnp.int32, (1, xs.shape[1]), 1)
    low_head = (lane % LANES) < SSM_HEAD_DIM
    xs_lo = jnp.where(low_head, xs_b, jnp.zeros_like(xs_b))
    xs_hi = jnp.where(low_head, jnp.zeros_like(xs_b), xs_b)

    gw = SSM_HEAD_DIM * (SSM_HEADS // SSM_GROUPS)
    for g in range(SSM_GROUPS):
        ns = slice(g * SSM_STATE, (g + 1) * SSM_STATE)
        gs = slice(g * gw, (g + 1) * gw)
        bg_f = bm[:, ns]
        bg = bg_f.astype(BF16)
        cg = cm[:, ns].astype(BF16)
        cb = lax.dot_general(cg, bg, (((1,), (1,)), ((), ())), preferred_element_type=F32)
        prev = state[g]
        y_off = jnp.dot(cg, prev.astype(BF16), preferred_element_type=F32)
        for c in range(gw // LANES):
            cs = slice(g * gw + c * LANES, g * gw + (c + 1) * LANES)
            acc = y_off[:, c * LANES:(c + 1) * LANES] * eacs_full[:, cs]
            for par, xpart in ((0, xs_lo), (1, xs_hi)):
                h = g * (SSM_HEADS // SSM_GROUPS) + 2 * c + par
                seg = acs[:, h:h + 1] - acs_t[h:h + 1, :]
                mmat = cb * jnp.exp(jnp.where(causal, seg, NEG_INF)) * dt_t[h:h + 1, :]
                acc = acc + jnp.dot(mmat.astype(BF16), xpart[:, cs], preferred_element_type=F32)
            ybuf[:, cs] = acc
        new_states = jnp.dot(bg_f.T.astype(BF16), xd[:, gs], preferred_element_type=F32)
        state[g] = prev * eacs_full[q - 1:q, gs] + new_states

    y = ybuf[...] + xs * dskip_ref[...]
    z = z_ref[...].astype(F32)
    y = y * _silu(z)
    for g in range(SSM_GROUPS):
        gs = slice(g * gw, (g + 1) * gw)
        yg = y[:, gs]
        ms = jnp.mean(yg * yg, axis=-1, keepdims=True)
        o_ref[:, gs] = (yg * lax.rsqrt(ms + NORM_EPS) * ng_ref[:, gs]).astype(o_ref.dtype)


def _ssd(proj, dt_raw, conv_w, conv_b, dt_bias, a_neg, d_skip, norm_g, cast_src):
    m = proj.shape[0]
    q = SSM_CHUNK
    inner = D_MODEL
    bc = SSM_GROUPS * SSM_STATE
    hb = q // SSM_HALO
    wx, wb, wc = conv_w[:, :inner], conv_w[:, inner:inner + bc], conv_w[:, inner + bc:]
    bx, bb, bcc = conv_b[:, :inner], conv_b[:, inner:inner + bc], conv_b[:, inner + bc:]
    expand = jnp.asarray(np.arange(LANES)[:, None] == (np.arange(inner) // SSM_HEAD_DIM)[None, :], BF16)
    t_idx = np.arange(q)
    shift_np = np.zeros((SSM_CONV_WIDTH - 1, q, SSM_HALO + q), np.float32)
    for k in range(1, SSM_CONV_WIDTH):
        shift_np[k - 1, t_idx, SSM_HALO + t_idx - k] = 1.0
    shift = jnp.asarray(shift_np.reshape(-1, SSM_HALO + q), BF16)
    prev = lambda i: jnp.maximum(i * hb - 1, 0)
    full = lambda shape: pl.BlockSpec(shape, lambda i: (0,) * len(shape))
    cast_in, cast_out, cast_shape, cast_scratch = _cast_job_extras(cast_src, m // q)
    return pl.pallas_call(
        _with_cast_job(_ssd_kernel, 20, 1),
        grid=(m // q,),
        in_specs=[
            pl.BlockSpec((q, inner), lambda i: (i, 2)),
            pl.BlockSpec((q, inner), lambda i: (i, 3)),
            pl.BlockSpec((q, bc), lambda i: (i, 8)),
            pl.BlockSpec((q, bc), lambda i: (i, 9)),
            pl.BlockSpec((SSM_HALO, inner), lambda i: (prev(i), 3)),
            pl.BlockSpec((SSM_HALO, bc), lambda i: (prev(i), 8)),
            pl.BlockSpec((SSM_HALO, bc), lambda i: (prev(i), 9)),
            pl.BlockSpec((q, LANES), lambda i: (i, 0)),
            full(wx.shape), full(wb.shape), full(wc.shape),
            full(bx.shape), full(bb.shape), full(bcc.shape),
            full((1, LANES)), full((1, LANES)), full((1, inner)), full((1, inner)),
            full((LANES, inner)),
            full(shift.shape),
            cast_in,
        ],
        out_specs=[pl.BlockSpec((q, inner), lambda i: (i, 0)), cast_out],
        out_shape=[jax.ShapeDtypeStruct((m, inner), BF16), cast_shape],
        scratch_shapes=[
            pltpu.VMEM((SSM_GROUPS, SSM_STATE, inner // SSM_GROUPS), F32),
            pltpu.VMEM((q, inner), F32),
        ] + cast_scratch,
        compiler_params=_params(("arbitrary",)),
        name="ssd",
    )(proj, proj, proj, proj, proj, proj, proj, dt_raw, wx, wb, wc, bx, bb, bcc,
      dt_bias, a_neg, d_skip, norm_g, expand, shift, cast_src)


def _attn_kernel(sinks_ref, q_ref, kvc_ref, kvp_ref, bias_ref, o_ref):
    blk = ATTN_BLOCK
    n = pl.program_id(0)
    for sub in range(ATTN_SUBBLOCKS):
        rows = slice(sub * blk, (sub + 1) * blk)
        prev = kvp_ref[...] if sub == 0 else kvc_ref[(sub - 1) * blk:sub * blk, :]
        table = jnp.where(n == 0, 0, 1) if sub == 0 else 1
        _attend_block(sinks_ref, q_ref, prev, kvc_ref[rows, :], bias_ref, table, o_ref, rows)


def _attend_block(sinks_ref, q_ref, kv_prev, kv_cur, bias_ref, table, o_ref, rows):
    blk = ATTN_BLOCK
    kvw = ATTN_KV_HEADS * ATTN_HEAD_DIM
    rep = ATTN_Q_HEADS // ATTN_KV_HEADS
    kv = jnp.concatenate([kv_prev, kv_cur], axis=0).astype(F32)
    lane = lax.broadcasted_iota(jnp.int32, (1, LANES), 1)
    low = lane < ATTN_HEAD_DIM
    scale = ATTN_HEAD_DIM ** -0.5

    for g in range(ATTN_KV_HEADS):
        pc = g // 2
        kcol = kv[:, pc * LANES:(pc + 1) * LANES] * scale
        vcol = kv[:, kvw + pc * LANES:kvw + (pc + 1) * LANES]
        kroll = pltpu.roll(kcol, ATTN_HEAD_DIM, axis=1)
        vroll = pltpu.roll(vcol, ATTN_HEAD_DIM, axis=1)
        if g % 2 == 0:
            k_lo, k_hi = jnp.where(low, kcol, 0.0), jnp.where(low, 0.0, kroll)
            v_lo, v_hi = jnp.where(low, vcol, 0.0), jnp.where(low, 0.0, vroll)
        else:
            k_lo, k_hi = jnp.where(low, kroll, 0.0), jnp.where(low, 0.0, kcol)
            v_lo, v_hi = jnp.where(low, vroll, 0.0), jnp.where(low, 0.0, vcol)
        k_lo, k_hi, v_lo, v_hi = (t.astype(BF16) for t in (k_lo, k_hi, v_lo, v_hi))

        ncol = rep // 2
        qs = jnp.concatenate([q_ref[rows, (g * ncol + c) * LANES:(g * ncol + c + 1) * LANES] for c in range(ncol)], axis=0)
        nt = (((1,), (1,)), ((), ()))
        s_even = lax.dot_general(qs, k_lo, nt, preferred_element_type=F32)
        s_odd = lax.dot_general(qs, k_hi, nt, preferred_element_type=F32)
        for c in range(ncol):
            out = None
            for par, s_all, vv in ((0, s_even, v_lo), (1, s_odd, v_hi)):
                h = g * rep + 2 * c + par
                logits = s_all[c * blk:(c + 1) * blk, :] + bias_ref[table, h]
                sink = sinks_ref[h]
                mx = jnp.maximum(jnp.max(logits, axis=-1, keepdims=True), sink)
                p = jnp.exp(logits - mx)
                denom = jnp.sum(p, axis=-1, keepdims=True) + jnp.exp(sink - mx)
                o = jnp.dot(p.astype(BF16), vv, preferred_element_type=F32) * (1.0 / denom)
                out = o if out is None else out + o
            col = g * ncol + c
            o_ref[rows, col * LANES:(col + 1) * LANES] = out.astype(o_ref.dtype)


def _t5_bucket_table():
    dist = np.arange(ATTN_BLOCK)[:, None] + ATTN_BLOCK - np.arange(2 * ATTN_BLOCK)[None, :]
    n = np.maximum(dist, 0)
    max_exact = REL_BUCKETS // 2
    nf = np.maximum(n, 1).astype(np.float32)
    large = max_exact + (np.log(nf / max_exact) / math.log(REL_MAX_DIST / max_exact)
                         * (REL_BUCKETS - max_exact)).astype(np.int32)
    large = np.minimum(large, REL_BUCKETS - 1)
    return dist, np.where(n < max_exact, n, large)


def _attention(qkv, sinks, rel_bias):
    m = qkv.shape[0]
    blk = ATTN_BLOCK
    qd = ATTN_Q_HEADS * ATTN_HEAD_DIM
    kvd = 2 * ATTN_KV_HEADS * ATTN_HEAD_DIM
    step_rows = blk * ATTN_SUBBLOCKS
    dist, bucket = _t5_bucket_table()
    visible = (dist >= 0) & (dist < blk)
    onehot = jnp.asarray(np.arange(REL_BUCKETS)[:, None] == bucket.reshape(1, -1), F32)
    bias = jnp.dot(rel_bias.astype(F32).T, onehot, precision=lax.Precision.HIGHEST).reshape(-1, blk, 2 * blk)
    general = jnp.where(visible[None], bias, NEG_INF)
    first = jnp.where((visible & (np.arange(2 * blk) >= blk)[None, :])[None], bias, NEG_INF)
    table = jnp.stack([first, general])
    return pl.pallas_call(
        _attn_kernel,
        grid=(m // step_rows,),
        in_specs=[
            pl.BlockSpec(memory_space=pltpu.SMEM),
            pl.BlockSpec((step_rows, qd), lambda n: (n, 0)),
            pl.BlockSpec((step_rows, kvd), lambda n: (n, qd // kvd)),
            pl.BlockSpec((blk, kvd), lambda n: (jnp.maximum(n * ATTN_SUBBLOCKS - 1, 0), qd // kvd)),
            pl.BlockSpec(table.shape, lambda n: (0, 0, 0, 0)),
        ],
        out_specs=pl.BlockSpec((step_rows, qd), lambda n: (n, 0)),
        out_shape=jax.ShapeDtypeStruct((m, qd), BF16),
        compiler_params=_params(("arbitrary",)),
        name="swa_attention",
    )(sinks.astype(F32), qkv, qkv, qkv, table)


def _router_kernel(x_ref, g_ref, rw_ref, rb_ref, o_ref):
    xn = _rms_rows(x_ref[...], g_ref[...])
    x_hi = xn.astype(BF16)
    x_lo = (xn - x_hi.astype(F32)).astype(BF16)
    both = jnp.dot(x_hi, rw_ref[...], preferred_element_type=F32)
    logits = (both[:, :LANES] + both[:, LANES:]
              + jnp.dot(x_lo, rw_ref[:, :LANES], preferred_element_type=F32) + rb_ref[...])
    lane = lax.broadcasted_iota(jnp.int32, logits.shape, 1)
    m1 = jnp.max(logits, axis=-1, keepdims=True)
    i1 = jnp.min(jnp.where(logits == m1, lane, LANES), axis=-1, keepdims=True)
    rest = jnp.where(lane == i1, NEG_INF, logits)
    m2 = jnp.max(rest, axis=-1, keepdims=True)
    i2 = jnp.min(jnp.where(rest == m2, lane, LANES), axis=-1, keepdims=True)
    e = jnp.exp(m2 - m1)
    g1 = 1.0 / (1.0 + e)
    g2 = e * g1
    out = jnp.where(lane == 0, i1.astype(F32),
                    jnp.where(lane == 1, i2.astype(F32),
                              jnp.where(lane == 2, g1, jnp.where(lane == 3, g2, 0.0))))
    o_ref[...] = out


def _router(h, g, rw, rb, *, tm):
    m, d = h.shape
    rw_pad = jnp.zeros((d, LANES), F32).at[:, :N_EXPERTS].set(rw.astype(F32))
    rw_hi = rw_pad.astype(BF16)
    rw_pad = jnp.concatenate([rw_hi, (rw_pad - rw_hi.astype(F32)).astype(BF16)], axis=1)
    rb_pad = jnp.full((1, LANES), NEG_INF, F32).at[0, :N_EXPERTS].set(rb.astype(F32))
    return pl.pallas_call(
        _router_kernel,
        grid=(m // tm,),
        in_specs=[
            pl.BlockSpec((tm, d), lambda i: (i, 0)),
            pl.BlockSpec((1, d), lambda i: (0, 0)),
            pl.BlockSpec((d, 2 * LANES), lambda i: (0, 0)),
            pl.BlockSpec((1, LANES), lambda i: (0, 0)),
        ],
        out_specs=pl.BlockSpec((tm, LANES), lambda i: (i, 0)),
        out_shape=jax.ShapeDtypeStruct((m, LANES), F32),
        compiler_params=_params(("parallel",)),
        name="router",
    )(h, g, rw_pad, rb_pad)


def _moe_drain(h_hbm, xbuf, sem):
    for s in range(MOE_SUBS_PER_GROUP):
        rs = slice(s * MOE_SUB, (s + 1) * MOE_SUB)
        pltpu.make_async_copy(h_hbm.at[pl.ds(0, MOE_SUB)], xbuf.at[rs], sem).wait()
    slack = xbuf.shape[0] - MOE_GROUP
    pltpu.make_async_copy(h_hbm.at[pl.ds(0, slack)], xbuf.at[MOE_GROUP:, :], sem).wait()


def _moe_kernel(ge_ref, ns_ref, tok_ref, h_hbm, g_ref, wg_ref, wu_ref, wd_ref, o_ref,
                xbuf, xn_ref, sem, *, rps):
    grp = pl.program_id(0)
    j = pl.program_id(1)
    ns = ns_ref[grp]

    def row_copy(gi, r):
        tok = tok_ref[gi * MOE_GROUP + jnp.minimum(r, MOE_GROUP - 1)]
        return pltpu.make_async_copy(h_hbm.at[pl.ds(tok, 1)], xbuf.at[pl.ds(r, 1)], sem)

    def issue_step_rows(gi):
        for u in range(rps):
            row_copy(gi, j * rps + u).start()

    @pl.when(j == 0)
    def _():
        @pl.when(grp == 0)
        def _():
            def issue(r, carry):
                for u in range(MOE_UNROLL):
                    row_copy(0, r * MOE_UNROLL + u).start()
                return carry

            lax.fori_loop(0, xbuf.shape[0] // MOE_UNROLL, issue, 0)

        _moe_drain(h_hbm, xbuf, sem)
        for s in range(MOE_SUBS_PER_GROUP):
            rs = slice(s * MOE_SUB, (s + 1) * MOE_SUB)
            o_ref[rs, :] = jnp.zeros((MOE_SUB, o_ref.shape[1]), F32)

            @pl.when(s < ns)
            def _():
                xn_ref[rs, :] = _rms_rows(xbuf[rs, :], g_ref[...]).astype(BF16)

    def ffn(rs):
        xn = xn_ref[rs, :]
        gate = jnp.dot(xn, wg_ref[...].astype(BF16), preferred_element_type=F32)
        up = jnp.dot(xn, wu_ref[...].astype(BF16), preferred_element_type=F32)
        hid = (_silu(gate) * up).astype(BF16)
        o_ref[rs, :] += jnp.dot(hid, wd_ref[...].astype(BF16), preferred_element_type=F32)

    @pl.when(ns == MOE_SUBS_PER_GROUP)
    def _():
        issue_step_rows(grp + 1)
        ffn(slice(0, MOE_GROUP))

    @pl.when(ns < MOE_SUBS_PER_GROUP)
    def _():
        issue_step_rows(grp + 1)

        def sub_block(s, carry):
            ffn(pl.ds(pl.multiple_of(s * MOE_SUB, MOE_SUB), MOE_SUB))
            return carry

        lax.fori_loop(0, ns, sub_block, 0)

    @pl.when((grp == pl.num_programs(0) - 1) & (j == pl.num_programs(1) - 1))
    def _():
        _moe_drain(h_hbm, xbuf, sem)


def _moe_ffn(h, g, wg, wu, wd, group_e, group_ns, src_tok, *, tf):
    t, d = h.shape
    n_groups = group_e.shape[0]
    f = wg.shape[2]
    nj = f // tf
    rps = -(-MOE_GROUP // (nj * SUBLANES)) * SUBLANES
    assert src_tok.shape[0] == (n_groups + 1) * MOE_GROUP

    def jj(grp, j, ns):
        return jnp.where(ns[grp] > 0, j, nj - 1)

    return pl.pallas_call(
        functools.partial(_moe_kernel, rps=rps),
        grid_spec=pltpu.PrefetchScalarGridSpec(
            num_scalar_prefetch=3,
            grid=(n_groups, nj),
            in_specs=[
                pl.BlockSpec(memory_space=pl.ANY),
                pl.BlockSpec((1, d), lambda grp, j, ge, ns, tok: (0, 0)),
                pl.BlockSpec((None, d, tf), lambda grp, j, ge, ns, tok: (ge[grp], 0, jj(grp, j, ns))),
                pl.BlockSpec((None, d, tf), lambda grp, j, ge, ns, tok: (ge[grp], 0, jj(grp, j, ns))),
                pl.BlockSpec((None, tf, d), lambda grp, j, ge, ns, tok: (ge[grp], jj(grp, j, ns), 0)),
            ],
            out_specs=pl.BlockSpec((MOE_GROUP, d), lambda grp, j, ge, ns, tok: (grp, 0)),
            scratch_shapes=[
                pltpu.VMEM((rps * nj, d), F32),
                pltpu.VMEM((MOE_GROUP, d), BF16),
                pltpu.SemaphoreType.DMA(()),
            ],
        ),
        out_shape=jax.ShapeDtypeStruct((n_groups * MOE_GROUP, d), F32),
        compiler_params=_params(("arbitrary", "arbitrary")),
        name="moe_ffn",
    )(group_e, group_ns, src_tok, h, g, wg, wu, wd)


def _combine_kernel(pos_ref, h_ref, route_ref, fg_ref, rows_hbm, o_ref, buf_a, buf_b, sem, *, tc):
    i = pl.program_id(0)
    slot = i % 2

    def copies(blk, sl, r):
        pa = pos_ref[TOP_K * (blk * tc + r)]
        pb = pos_ref[TOP_K * (blk * tc + r) + 1]
        return (pltpu.make_async_copy(rows_hbm.at[pl.ds(pa, 1)], buf_a.at[sl, pl.ds(r, 1)], sem.at[sl]),
                pltpu.make_async_copy(rows_hbm.at[pl.ds(pb, 1)], buf_b.at[sl, pl.ds(r, 1)], sem.at[sl]))

    def issue_block(blk, sl):
        def issue(r, carry):
            ca, cb = copies(blk, sl, r)
            ca.start()
            cb.start()
            return carry

        lax.fori_loop(0, tc, issue, 0, unroll=8)

    def wait_block(blk, sl):
        def wait(r, carry):
            ca, cb = copies(blk, sl, r)
            ca.wait()
            cb.wait()
            return carry

        lax.fori_loop(0, tc, wait, 0, unroll=8)

    @pl.when(i == 0)
    def _():
        issue_block(0, 0)

    wait_block(i, slot)

    last = pl.num_programs(0) - 1
    nxt = jnp.minimum(i + 1, last)
    for r in range(tc):
        ca, cb = copies(nxt, 1 - slot, r)
        ca.start()
        cb.start()
    route = route_ref[...]
    moe = route[:, 2:3] * buf_a[slot] + route[:, 3:4] * buf_b[slot]
    o_ref[...] = _rms_rows(h_ref[...] + moe, fg_ref[...])

    @pl.when(i == last)
    def _():
        wait_block(nxt, 1 - slot)


def _combine(h, route, final_g, rows, pos, *, tc):
    t, d = h.shape
    return pl.pallas_call(
        functools.partial(_combine_kernel, tc=tc),
        grid_spec=pltpu.PrefetchScalarGridSpec(
            num_scalar_prefetch=1,
            grid=(t // tc,),
            in_specs=[
                pl.BlockSpec((tc, d), lambda i, pos: (i, 0)),
                pl.BlockSpec((tc, LANES), lambda i, pos: (i, 0)),
                pl.BlockSpec((1, d), lambda i, pos: (0, 0)),
                pl.BlockSpec(memory_space=pl.ANY),
            ],
            out_specs=pl.BlockSpec((tc, d), lambda i, pos: (i, 0)),
            scratch_shapes=[pltpu.VMEM((2, tc, d), F32), pltpu.VMEM((2, tc, d), F32), pltpu.SemaphoreType.DMA((2,))],
        ),
        out_shape=jax.ShapeDtypeStruct((t, d), F32),
        compiler_params=_params(("arbitrary",)),
        name="moe_combine",
    )(pos, h, route, final_g, rows)


def _moe_plan(route, n_tok):
    flat_e = route[:, :TOP_K].astype(jnp.int32).reshape(-1)
    onehot = (flat_e[:, None] == jnp.arange(N_EXPERTS, dtype=jnp.int32)[None, :]).astype(jnp.int32)
    csum = jnp.cumsum(onehot, axis=0)
    rank = jnp.sum(onehot * (csum - onehot), axis=1)
    counts = csum[-1]
    n_groups = (n_tok * TOP_K) // MOE_GROUP + N_EXPERTS
    groups_e = (counts + MOE_GROUP - 1) // MOE_GROUP
    group_end = jnp.cumsum(groups_e)
    group_start = group_end - groups_e
    pos = jnp.sum(onehot * (group_start * MOE_GROUP)[None, :], axis=1) + rank
    flat_tok = jnp.arange(n_tok * TOP_K, dtype=jnp.int32) // TOP_K
    src_tok = jnp.zeros(((n_groups + 1) * MOE_GROUP,), jnp.int32).at[pos].set(
        flat_tok, unique_indices=True, mode="promise_in_bounds")
    experts = jnp.arange(N_EXPERTS, dtype=jnp.int32)
    gidx = jnp.arange(n_groups, dtype=jnp.int32)
    used = gidx < group_end[-1]
    last_e = jnp.max(jnp.where(groups_e > 0, experts, 0))
    ge = jnp.sum((group_end[None, :] <= gidx[:, None]).astype(jnp.int32), axis=1)
    ge = jnp.where(used, ge, last_e).astype(jnp.int32)
    mine = (ge[:, None] == experts[None, :]).astype(jnp.int32)
    subs_e = (counts + MOE_SUB - 1) // MOE_SUB
    subs_g = jnp.sum(mine * subs_e[None, :], axis=1)
    start_g = jnp.sum(mine * group_start[None, :], axis=1)
    ns = jnp.clip(subs_g - MOE_SUBS_PER_GROUP * (gidx - start_g), 0, MOE_SUBS_PER_GROUP)
    ns = jnp.where(used, ns, 0).astype(jnp.int32)
    return ge, ns, src_tok, pos.astype(jnp.int32)


def kernel(x, mix_norm_g, ffn_norm_g, final_norm_g, w_in, conv_w, conv_b, conv_ln_g, conv_ln_b, ssm_conv_w, ssm_conv_b, dt_bias, a_log, d_skip, ssm_norm_g, w_out, ffn_w_gate, ffn_w_up, ffn_w_down, w_qkv, b_qkv, w_o, b_o, sinks, rel_bias, router_w, router_b, moe_w_gate, moe_w_up, moe_w_down):
    bsz, seq, d = x.shape
    assert bsz == 1 and d == D_MODEL and seq % 512 == 0
    m = seq
    h = x.reshape(m, d)
    row = lambda v: v.reshape(1, -1).astype(F32)
    tm = min(1024, m)

    main_w = 2 * D_MODEL + D_MODEL + (D_MODEL + 2 * SSM_GROUPS * SSM_STATE)
    w_main = w_in[0].astype(BF16)
    w_dt = jnp.zeros((d, LANES), F32).at[:, :SSM_HEADS].set(w_in[0][:, main_w:]).astype(BF16)
    g0 = row(mix_norm_g[0])
    n_exp, _, expert_dim = moe_w_gate[0].shape
    wg_f32 = moe_w_gate[0].reshape(n_exp * d, expert_dim)
    wu_f32 = moe_w_up[0].reshape(n_exp * d, expert_dim)
    wd_f32 = moe_w_down[0].reshape(n_exp * expert_dim, d)

    proj, dt_raw, wd_b = _in_proj(h, g0, w_main, w_dt, wd_f32, tm=tm, tn=1024, n=main_w)
    conv_out, wg_b = _conv_module(proj, conv_w[0].astype(F32), row(conv_b[0]), row(conv_ln_g[0]), row(conv_ln_b[0]),
                                  wg_f32, tt=256)

    pad_heads = lambda v: jnp.zeros((1, LANES), F32).at[0, :SSM_HEADS].set(v.astype(F32))
    ssm_out, wu_b = _ssd(proj, dt_raw, ssm_conv_w[0].astype(F32), row(ssm_conv_b[0]),
                         pad_heads(dt_bias[0]), pad_heads(-jnp.exp(a_log[0].astype(F32))),
                         row(jnp.repeat(d_skip[0].astype(F32), SSM_HEAD_DIM)), row(ssm_norm_g[0]), wu_f32)

    wo = w_out[0].astype(BF16)
    h = _matmul_residual([conv_out, ssm_out], [wo[:D_MODEL], wo[D_MODEL:]], h, jnp.zeros((1, d), F32),
                         tm=256, tn=d, name="out_proj")
    h = _swiglu(h, row(ffn_norm_g[0]), ffn_w_gate[0].astype(BF16), ffn_w_up[0].astype(BF16),
                ffn_w_down[0].astype(BF16), tm=tm, tf=512)

    qkv = _norm_matmul(h, row(mix_norm_g[1]), w_qkv[0].astype(BF16), row(b_qkv[0]), tm=min(512, m), tn=w_qkv.shape[2],
                       out_dtype=BF16, name="qkv_proj")
    attn = _attention(qkv, sinks[0], rel_bias)
    h = _matmul_residual([attn], [w_o[0].astype(BF16)], h, row(b_o[0]), tm=min(512, m), tn=d, name="attn_out_proj")

    g1 = row(ffn_norm_g[1])
    route = _router(h, g1, router_w[0], router_b[0], tm=512)
    ge, ns, src_tok, pos = _moe_plan(route, m)
    rows = _moe_ffn(h, g1, wg_b.reshape(n_exp, d, expert_dim), wu_b.reshape(n_exp, d, expert_dim),
                    wd_b.reshape(n_exp, expert_dim, d), ge, ns, src_tok, tf=512)
    out = _combine(h, route, row(final_norm_g), rows, pos, tc=256)
    return out.reshape(bsz, seq, d)
```

```python
import functools
import math

import numpy as np
import jax
import jax.numpy as jnp
from jax import lax
from jax.experimental import pallas as pl
from jax.experimental.pallas import tpu as pltpu

F32 = jnp.float32
BF16 = jnp.bfloat16
NORM_EPS = 1e-5
NEG_INF = float("-inf")

LANES = 128
SUBLANES = 8
VMEM_LIMIT = 56 << 20

D_MODEL = 2048
CONV_WIDTH = 31
SSM_HEADS = 32
SSM_HEAD_DIM = 64
SSM_GROUPS = 8
SSM_STATE = 128
SSM_CONV_WIDTH = 4
SSM_CHUNK = 128
ATTN_Q_HEADS = 32
ATTN_KV_HEADS = 4
ATTN_HEAD_DIM = 64
ATTN_BLOCK = 128
ATTN_SUBBLOCKS = 4
REL_BUCKETS = 32
REL_MAX_DIST = 128
N_EXPERTS = 8
TOP_K = 2
MOE_SUB = 256
MOE_SUBS_PER_GROUP = 4
MOE_GROUP = MOE_SUB * MOE_SUBS_PER_GROUP
MOE_UNROLL = 8


def _params(sem):
    return pltpu.CompilerParams(dimension_semantics=sem, vmem_limit_bytes=VMEM_LIMIT)


def _sigmoid(x):
    return 0.5 + 0.5 * jnp.tanh(0.5 * x)


def _silu(x):
    half = 0.5 * x
    return half + half * jnp.tanh(half)


def _rms_rows(x, g):
    ms = jnp.mean(x * x, axis=-1, keepdims=True)
    return x * lax.rsqrt(ms + NORM_EPS) * g


def _cast_slab(i, n_steps, src_hbm, dst_hbm, inbuf, outbuf, in_sem, out_sem):
    rows = inbuf.shape[1]
    slot = i % 2

    def fetch(step, sl):
        return pltpu.make_async_copy(src_hbm.at[pl.ds(step * rows, rows)], inbuf.at[sl], in_sem.at[sl])

    def write_back(step, sl):
        return pltpu.make_async_copy(outbuf.at[sl], dst_hbm.at[pl.ds(step * rows, rows)], out_sem.at[sl])

    @pl.when(i == 0)
    def _():
        fetch(0, 0).start()

    @pl.when(i + 1 < n_steps)
    def _():
        fetch(i + 1, 1 - slot).start()

    fetch(i, slot).wait()

    @pl.when(i >= 2)
    def _():
        write_back(i - 2, slot).wait()

    outbuf[slot] = inbuf[slot].astype(BF16)
    write_back(i, slot).start()

    @pl.when(i == n_steps - 1)
    def _():
        write_back(i, slot).wait()

        @pl.when(i >= 1)
        def _():
            write_back(i - 1, 1 - slot).wait()


def _with_cast_job(kernel_fn, n_in, n_out):
    def wrapped(*refs):
        ins, src = refs[:n_in], refs[n_in]
        outs, dst = refs[n_in + 1:n_in + 1 + n_out], refs[n_in + 1 + n_out]
        rest = refs[n_in + 2 + n_out:]
        scratch, cast_scratch = rest[:-4], rest[-4:]
        _cast_slab(pl.program_id(0), pl.num_programs(0), src, dst, *cast_scratch)
        kernel_fn(*ins, *outs, *scratch)

    return wrapped


def _cast_job_extras(src, n_steps):
    assert src.shape[0] % n_steps == 0
    rows, cols = src.shape[0] // n_steps, src.shape[1]
    scratch = [pltpu.VMEM((2, rows, cols), F32), pltpu.VMEM((2, rows, cols), BF16),
               pltpu.SemaphoreType.DMA((2,)), pltpu.SemaphoreType.DMA((2,))]
    any_spec = pl.BlockSpec(memory_space=pl.ANY)
    return any_spec, any_spec, jax.ShapeDtypeStruct(src.shape, BF16), scratch


def _norm_matmul_kernel(x_ref, g_ref, w_ref, b_ref, o_ref, xn_ref):
    @pl.when(pl.program_id(1) == 0)
    def _():
        xn_ref[...] = _rms_rows(x_ref[...], g_ref[...]).astype(BF16)

    acc = jnp.dot(xn_ref[...], w_ref[...], preferred_element_type=F32)
    o_ref[...] = (acc + b_ref[...]).astype(o_ref.dtype)


def _norm_matmul(x, g, w, b, *, tm, tn, out_dtype, name):
    m, k = x.shape
    n = w.shape[1]
    return pl.pallas_call(
        _norm_matmul_kernel,
        grid=(m // tm, n // tn),
        in_specs=[
            pl.BlockSpec((tm, k), lambda i, j: (i, 0)),
            pl.BlockSpec((1, k), lambda i, j: (0, 0)),
            pl.BlockSpec((k, tn), lambda i, j: (0, j)),
            pl.BlockSpec((1, tn), lambda i, j: (0, j)),
        ],
        out_specs=pl.BlockSpec((tm, tn), lambda i, j: (i, j)),
        out_shape=jax.ShapeDtypeStruct((m, n), out_dtype),
        scratch_shapes=[pltpu.VMEM((tm, k), BF16)],
        compiler_params=_params(("parallel", "arbitrary")),
        name=name,
    )(x, g, w, b)


def _in_proj_kernel(x_ref, g_ref, w_ref, wdt_ref, cast_src, o_ref, odt_ref, cast_dst,
                    xn_ref, inbuf, outbuf, in_sem, out_sem, *, cast_cols):
    i = pl.program_id(0)
    j = pl.program_id(1)

    @pl.when(j == 0)
    def _():
        xn_ref[...] = _rms_rows(x_ref[...], g_ref[...]).astype(BF16)
        odt_ref[...] = jnp.dot(xn_ref[...], wdt_ref[...], preferred_element_type=F32)

    @pl.when(j < cast_cols)
    def _():
        _cast_slab(i * cast_cols + j, pl.num_programs(0) * cast_cols, cast_src, cast_dst,
                   inbuf, outbuf, in_sem, out_sem)

    o_ref[...] = jnp.dot(xn_ref[...], w_ref[...], preferred_element_type=F32).astype(o_ref.dtype)


def _in_proj(x, g, w, wdt, cast_src, *, tm, tn, n):
    m, k = x.shape
    assert n % tn == 0 and n <= w.shape[1]
    ndt = wdt.shape[1]
    n_rows, n_cols = m // tm, n // tn
    cast_cols = max(s for s in range(1, n_cols + 1)
                    if cast_src.shape[0] % (n_rows * s) == 0 and (cast_src.shape[0] // (n_rows * s)) % 16 == 0)
    cast_in, cast_out, cast_shape, cast_scratch = _cast_job_extras(cast_src, n_rows * cast_cols)
    return pl.pallas_call(
        functools.partial(_in_proj_kernel, cast_cols=cast_cols),
        grid=(n_rows, n_cols),
        in_specs=[
            pl.BlockSpec((tm, k), lambda i, j: (i, 0)),
            pl.BlockSpec((1, k), lambda i, j: (0, 0)),
            pl.BlockSpec((k, tn), lambda i, j: (0, j)),
            pl.BlockSpec((k, ndt), lambda i, j: (0, 0)),
            cast_in,
        ],
        out_specs=[pl.BlockSpec((tm, tn), lambda i, j: (i, j)), pl.BlockSpec((tm, ndt), lambda i, j: (i, 0)), cast_out],
        out_shape=[jax.ShapeDtypeStruct((m, n), BF16), jax.ShapeDtypeStruct((m, ndt), F32), cast_shape],
        scratch_shapes=[pltpu.VMEM((tm, k), BF16)] + cast_scratch,
        compiler_params=_params(("arbitrary", "arbitrary")),
        name="in_proj",
    )(x, g, w, wdt, cast_src)


def _matmul_residual_kernel(*refs, n_in):
    a_refs, w_refs = refs[:n_in], refs[n_in:2 * n_in]
    res_ref, b_ref, o_ref = refs[2 * n_in:]
    acc = res_ref[...] + b_ref[...]
    for a_ref, w_ref in zip(a_refs, w_refs):
        acc = acc + jnp.dot(a_ref[...], w_ref[...], preferred_element_type=F32)
    o_ref[...] = acc


def _matmul_residual(a_list, w_list, res, b, *, tm, tn, name):
    m, n = res.shape
    n_in = len(a_list)
    in_specs = [pl.BlockSpec((tm, a.shape[1]), lambda j, i: (i, 0)) for a in a_list]
    in_specs += [pl.BlockSpec((w.shape[0], tn), lambda j, i: (0, j)) for w in w_list]
    in_specs += [pl.BlockSpec((tm, tn), lambda j, i: (i, j)), pl.BlockSpec((1, tn), lambda j, i: (0, j))]
    return pl.pallas_call(
        functools.partial(_matmul_residual_kernel, n_in=n_in),
        grid=(n // tn, m // tm),
        in_specs=in_specs,
        out_specs=pl.BlockSpec((tm, tn), lambda j, i: (i, j)),
        out_shape=jax.ShapeDtypeStruct((m, n), F32),
        compiler_params=_params(("parallel", "parallel")),
        name=name,
    )(*a_list, *w_list, res, b)


def _swiglu_kernel(x_ref, g_ref, wg_ref, wu_ref, wd_ref, o_ref, xn_ref):
    @pl.when(pl.program_id(1) == 0)
    def _():
        x = x_ref[...]
        xn_ref[...] = _rms_rows(x, g_ref[...]).astype(BF16)
        o_ref[...] = x

    xn = xn_ref[...]
    gate = jnp.dot(xn, wg_ref[...], preferred_element_type=F32)
    up = jnp.dot(xn, wu_ref[...], preferred_element_type=F32)
    hid = (_silu(gate) * up).astype(BF16)
    o_ref[...] += jnp.dot(hid, wd_ref[...], preferred_element_type=F32)


def _swiglu(x, g, wg, wu, wd, *, tm, tf):
    m, d = x.shape
    f = wg.shape[1]
    return pl.pallas_call(
        _swiglu_kernel,
        grid=(m // tm, f // tf),
        in_specs=[
            pl.BlockSpec((tm, d), lambda i, j: (i, 0)),
            pl.BlockSpec((1, d), lambda i, j: (0, 0)),
            pl.BlockSpec((d, tf), lambda i, j: (0, j)),
            pl.BlockSpec((d, tf), lambda i, j: (0, j)),
            pl.BlockSpec((tf, d), lambda i, j: (j, 0)),
        ],
        out_specs=pl.BlockSpec((tm, d), lambda i, j: (i, 0)),
        out_shape=jax.ShapeDtypeStruct((m, d), F32),
        scratch_shapes=[pltpu.VMEM((tm, d), BF16)],
        compiler_params=_params(("parallel", "arbitrary")),
        name="swiglu",
    )(x, g, wg, wu, wd)


CONV_HALO = 32
CONV_ROWS = 64


def _conv_module_kernel(val_ref, gate_ref, pval_ref, pgate_ref, w_ref, b_ref, lg_ref, lb_ref, o_ref,
                        ubuf, cbuf, *, tt):
    i = pl.program_id(0)

    def glu(v, g):
        return v.astype(F32) * _sigmoid(g.astype(F32))

    ubuf[0:CONV_HALO, :] = jnp.where(i > 0, glu(pval_ref[...], pgate_ref[...]), 0.0)
    ubuf[CONV_HALO:CONV_HALO + tt, :] = glu(val_ref[...], gate_ref[...])

    off = CONV_HALO - (CONV_WIDTH - 1)
    taps = [[] for _ in range(SUBLANES)]
    for j in range(CONV_WIDTH):
        a, r = divmod(off + j, SUBLANES)
        taps[r].append((a, j))
    n_ch = ubuf.shape[1]
    for cb in range(n_ch // LANES):
        ls = slice(cb * LANES, (cb + 1) * LANES)
        for r0 in range(0, tt, CONV_ROWS):
            big = ubuf[r0:r0 + CONV_ROWS + CONV_HALO, ls]
            out = None
            for r in range(SUBLANES):
                rows = CONV_ROWS if r == 0 else CONV_ROWS + SUBLANES
                q = None
                for a, j in taps[r]:
                    term = w_ref[j:j + 1, ls] * big[a * SUBLANES:a * SUBLANES + rows, :]
                    q = term if q is None else q + term
                if r:
                    q = pltpu.roll(q, rows - r, axis=0)[:CONV_ROWS, :]
                out = q if out is None else out + q
            cbuf[r0:r0 + CONV_ROWS, ls] = out

    c = cbuf[...] + b_ref[...]
    mu = jnp.mean(c, axis=-1, keepdims=True)
    d = c - mu
    var = jnp.mean(d * d, axis=-1, keepdims=True)
    y = d * lax.rsqrt(var + NORM_EPS) * lg_ref[...] + lb_ref[...]
    o_ref[...] = _silu(y).astype(o_ref.dtype)


def _conv_module(proj, conv_w, conv_b, ln_g, ln_b, cast_src, *, tt):
    m = proj.shape[0]
    c = D_MODEL
    hb = tt // CONV_HALO
    cast_in, cast_out, cast_shape, cast_scratch = _cast_job_extras(cast_src, m // tt)
    return pl.pallas_call(
        _with_cast_job(functools.partial(_conv_module_kernel, tt=tt), 8, 1),
        grid=(m // tt,),
        in_specs=[
            pl.BlockSpec((tt, c), lambda i: (i, 0)),
            pl.BlockSpec((tt, c), lambda i: (i, 1)),
            pl.BlockSpec((CONV_HALO, c), lambda i: (jnp.maximum(i * hb - 1, 0), 0)),
            pl.BlockSpec((CONV_HALO, c), lambda i: (jnp.maximum(i * hb - 1, 0), 1)),
            pl.BlockSpec((CONV_WIDTH, c), lambda i: (0, 0)),
            pl.BlockSpec((1, c), lambda i: (0, 0)),
            pl.BlockSpec((1, c), lambda i: (0, 0)),
            pl.BlockSpec((1, c), lambda i: (0, 0)),
            cast_in,
        ],
        out_specs=[pl.BlockSpec((tt, c), lambda i: (i, 0)), cast_out],
        out_shape=[jax.ShapeDtypeStruct((m, c), BF16), cast_shape],
        scratch_shapes=[pltpu.VMEM((CONV_HALO + tt, c), F32), pltpu.VMEM((tt, c), F32)] + cast_scratch,
        compiler_params=_params(("arbitrary",)),
        name="conv_module",
    )(proj, proj, proj, proj, conv_w, conv_b, ln_g, ln_b, cast_src)


SSM_HALO = 16


def _conv4_silu(cur_ref, prev_ref, shift_ref, w_ref, b_ref, cols, first):
    q = cur_ref.shape[0]
    cur = cur_ref[:, cols]
    prev = prev_ref[:, cols]
    xb = jnp.concatenate([jnp.where(first, jnp.zeros_like(prev), prev), cur], axis=0)
    back = jnp.dot(shift_ref[...], xb, preferred_element_type=F32)
    last = SSM_CONV_WIDTH - 1
    acc = b_ref[:, cols] + w_ref[last:last + 1, cols] * cur.astype(F32)
    for k in range(1, SSM_CONV_WIDTH):
        acc = acc + w_ref[last - k:last - k + 1, cols] * back[(k - 1) * q:k * q, :]
    return _silu(acc)


def _ssd_kernel(z_ref, x_ref, b_ref, c_ref, px_ref, pb_ref, pc_ref, dtr_ref,
                wx_ref, wb_ref, wc_ref, bx_ref, bb_ref, bc_ref,
                dtb_ref, a_ref, dskip_ref, ng_ref, e_ref, sh_ref, o_ref,
                state, ybuf):
    i = pl.program_id(0)
    q = SSM_CHUNK
    first = i == 0

    @pl.when(first)
    def _():
        state[...] = jnp.zeros_like(state)

    everything = slice(None)
    xs = _conv4_silu(x_ref, px_ref, sh_ref, wx_ref, bx_ref, everything, first)
    bm = _conv4_silu(b_ref, pb_ref, sh_ref, wb_ref, bb_ref, everything, first)
    cm = _conv4_silu(c_ref, pc_ref, sh_ref, wc_ref, bc_ref, everything, first)

    pre = dtr_ref[...] + dtb_ref[...]
    dt = jnp.maximum(pre, 0.0) + jnp.log(1.0 + jnp.exp(-jnp.abs(pre)))
    dta = dt * a_ref[...]
    row = lax.broadcasted_iota(jnp.int32, (q, q), 0)
    col = lax.broadcasted_iota(jnp.int32, (q, q), 1)
    causal = row >= col
    acs = jnp.dot(causal.astype(F32), dta, precision=lax.Precision.HIGHEST, preferred_element_type=F32)
    acs_t = acs.T
    dt_t = dt.T
    eacs = jnp.exp(acs)
    wdec = dt * jnp.exp(acs[q - 1:q, :] - acs)

    def split(v):
        hi = v.astype(BF16)
        return hi, (v - hi.astype(F32)).astype(BF16)

    parts = jnp.concatenate(split(eacs) + split(wdec), axis=0)
    wide = jnp.dot(parts, e_ref[...], preferred_element_type=F32)
    eacs_full = wide[0:q] + wide[q:2 * q]
    xd = (xs * (wide[2 * q:3 * q] + wide[3 * q:4 * q])).astype(BF16)

    xs_b = xs.astype(BF16)
    lane = lax.broadcasted_iota(jnp.int32, (1, xs.shape[1]), 1)
    low_head = (lane % LANES) < SSM_HEAD_DIM
    xs_lo = jnp.where(low_head, xs_b, jnp.zeros_like(xs_b))
    xs_hi = jnp.where(low_head, jnp.zeros_like(xs_b), xs_b)

    gw = SSM_HEAD_DIM * (SSM_HEADS // SSM_GROUPS)
    for g in range(SSM_GROUPS):
        ns = slice(g * SSM_STATE, (g + 1) * SSM_STATE)
        gs = slice(g * gw, (g + 1) * gw)
        bg_f = bm[:, ns]
        bg = bg_f.astype(BF16)
        cg = cm[:, ns].astype(BF16)
        cb = lax.dot_general(cg, bg, (((1,), (1,)), ((), ())), preferred_element_type=F32)
        prev = state[g]
        y_off = jnp.dot(cg, prev.astype(BF16), preferred_element_type=F32)
        for c in range(gw // LANES):
            cs = slice(g * gw + c * LANES, g * gw + (c + 1) * LANES)
            acc = y_off[:, c * LANES:(c + 1) * LANES] * eacs_full[:, cs]
            for par, xpart in ((0, xs_lo), (1, xs_hi)):
                h = g * (SSM_HEADS // SSM_GROUPS) + 2 * c + par
                seg = acs[:, h:h + 1] - acs_t[h:h + 1, :]
                mmat = cb * jnp.exp(jnp.where(causal, seg, NEG_INF)) * dt_t[h:h + 1, :]
                acc = acc + jnp.dot(mmat.astype(BF16), xpart[:, cs], preferred_element_type=F32)
            ybuf[:, cs] = acc
        new_states = jnp.dot(bg_f.T.astype(BF16), xd[:, gs], preferred_element_type=F32)
        state[g] = prev * eacs_full[q - 1:q, gs] + new_states

    y = ybuf[...] + xs * dskip_ref[...]
    z = z_ref[...].astype(F32)
    y = y * _silu(z)
    for g in range(SSM_GROUPS):
        gs = slice(g * gw, (g + 1) * gw)
        yg = y[:, gs]
        ms = jnp.mean(yg * yg, axis=-1, keepdims=True)
        o_ref[:, gs] = (yg * lax.rsqrt(ms + NORM_EPS) * ng_ref[:, gs]).astype(o_ref.dtype)


def _ssd(proj, dt_raw, conv_w, conv_b, dt_bias, a_neg, d_skip, norm_g, cast_src):
    m = proj.shape[0]
    q = SSM_CHUNK
    inner = D_MODEL
    bc = SSM_GROUPS * SSM_STATE
    hb = q // SSM_HALO
    wx, wb, wc = conv_w[:, :inner], conv_w[:, inner:inner + bc], conv_w[:, inner + bc:]
    bx, bb, bcc = conv_b[:, :inner], conv_b[:, inner:inner + bc], conv_b[:, inner + bc:]
    expand = jnp.asarray(np.arange(LANES)[:, None] == (np.arange(inner) // SSM_HEAD_DIM)[None, :], BF16)
    t_idx = np.arange(q)
    shift_np = np.zeros((SSM_CONV_WIDTH - 1, q, SSM_HALO + q), np.float32)
    for k in range(1, SSM_CONV_WIDTH):
        shift_np[k - 1, t_idx, SSM_HALO + t_idx - k] = 1.0
    shift = jnp.asarray(shift_np.reshape(-1, SSM_HALO + q), BF16)
    prev = lambda i: jnp.maximum(i * hb - 1, 0)
    full = lambda shape: pl.BlockSpec(shape, lambda i: (0,) * len(shape))
    cast_in, cast_out, cast_shape, cast_scratch = _cast_job_extras(cast_src, m // q)
    return pl.pallas_call(
        _with_cast_job(_ssd_kernel, 20, 1),
        grid=(m // q,),
        in_specs=[
            pl.BlockSpec((q, inner), lambda i: (i, 2)),
            pl.BlockSpec((q, inner), lambda i: (i, 3)),
            pl.BlockSpec((q, bc), lambda i: (i, 8)),
            pl.BlockSpec((q, bc), lambda i: (i, 9)),
            pl.BlockSpec((SSM_HALO, inner), lambda i: (prev(i), 3)),
            pl.BlockSpec((SSM_HALO, bc), lambda i: (prev(i), 8)),
            pl.BlockSpec((SSM_HALO, bc), lambda i: (prev(i), 9)),
            pl.BlockSpec((q, LANES), lambda i: (i, 0)),
            full(wx.shape), full(wb.shape), full(wc.shape),
            full(bx.shape), full(bb.shape), full(bcc.shape),
            full((1, LANES)), full((1, LANES)), full((1, inner)), full((1, inner)),
            full((LANES, inner)),
            full(shift.shape),
            cast_in,
        ],
        out_specs=[pl.BlockSpec((q, inner), lambda i: (i, 0)), cast_out],
        out_shape=[jax.ShapeDtypeStruct((m, inner), BF16), cast_shape],
        scratch_shapes=[
            pltpu.VMEM((SSM_GROUPS, SSM_STATE, inner // SSM_GROUPS), F32),
            pltpu.VMEM((q, inner), F32),
        ] + cast_scratch,
        compiler_params=_params(("arbitrary",)),
        name="ssd",
    )(proj, proj, proj, proj, proj, proj, proj, dt_raw, wx, wb, wc, bx, bb, bcc,
      dt_bias, a_neg, d_skip, norm_g, expand, shift, cast_src)


def _attn_kernel(sinks_ref, q_ref, kvc_ref, kvp_ref, bias_ref, o_ref):
    blk = ATTN_BLOCK
    n = pl.program_id(0)
    for sub in range(ATTN_SUBBLOCKS):
        rows = slice(sub * blk, (sub + 1) * blk)
        prev = kvp_ref[...] if sub == 0 else kvc_ref[(sub - 1) * blk:sub * blk, :]
        table = jnp.where(n == 0, 0, 1) if sub == 0 else 1
        _attend_block(sinks_ref, q_ref, prev, kvc_ref[rows, :], bias_ref, table, o_ref, rows)


def _attend_block(sinks_ref, q_ref, kv_prev, kv_cur, bias_ref, table, o_ref, rows):
    blk = ATTN_BLOCK
    kvw = ATTN_KV_HEADS * ATTN_HEAD_DIM
    rep = ATTN_Q_HEADS // ATTN_KV_HEADS
    kv = jnp.concatenate([kv_prev, kv_cur], axis=0).astype(F32)
    lane = lax.broadcasted_iota(jnp.int32, (1, LANES), 1)
    low = lane < ATTN_HEAD_DIM
    scale = ATTN_HEAD_DIM ** -0.5

    for g in range(ATTN_KV_HEADS):
        pc = g // 2
        kcol = kv[:, pc * LANES:(pc + 1) * LANES] * scale
        vcol = kv[:, kvw + pc * LANES:kvw + (pc + 1) * LANES]
        kroll = pltpu.roll(kcol, ATTN_HEAD_DIM, axis=1)
        vroll = pltpu.roll(vcol, ATTN_HEAD_DIM, axis=1)
        if g % 2 == 0:
            k_lo, k_hi = jnp.where(low, kcol, 0.0), jnp.where(low, 0.0, kroll)
            v_lo, v_hi = jnp.where(low, vcol, 0.0), jnp.where(low, 0.0, vroll)
        else:
            k_lo, k_hi = jnp.where(low, kroll, 0.0), jnp.where(low, 0.0, kcol)
            v_lo, v_hi = jnp.where(low, vroll, 0.0), jnp.where(low, 0.0, vcol)
        k_lo, k_hi, v_lo, v_hi = (t.astype(BF16) for t in (k_lo, k_hi, v_lo, v_hi))

        ncol = rep // 2
        qs = jnp.concatenate([q_ref[rows, (g * ncol + c) * LANES:(g * ncol + c + 1) * LANES] for c in range(ncol)], axis=0)
        nt = (((1,), (1,)), ((), ()))
        s_even = lax.dot_general(qs, k_lo, nt, preferred_element_type=F32)
        s_odd = lax.dot_general(qs, k_hi, nt, preferred_element_type=F32)
        for c in range(ncol):
            out = None
            for par, s_all, vv in ((0, s_even, v_lo), (1, s_odd, v_hi)):
                h = g * rep + 2 * c + par
                logits = s_all[c * blk:(c + 1) * blk, :] + bias_ref[table, h]
                sink = sinks_ref[h]
                mx = jnp.maximum(jnp.max(logits, axis=-1, keepdims=True), sink)
                p = jnp.exp(logits - mx)
                denom = jnp.sum(p, axis=-1, keepdims=True) + jnp.exp(sink - mx)
                o = jnp.dot(p.astype(BF16), vv, preferred_element_type=F32) * (1.0 / denom)
                out = o if out is None else out + o
            col = g * ncol + c
            o_ref[rows, col * LANES:(col + 1) * LANES] = out.astype(o_ref.dtype)


def _t5_bucket_table():
    dist = np.arange(ATTN_BLOCK)[:, None] + ATTN_BLOCK - np.arange(2 * ATTN_BLOCK)[None, :]
    n = np.maximum(dist, 0)
    max_exact = REL_BUCKETS // 2
    nf = np.maximum(n, 1).astype(np.float32)
    large = max_exact + (np.log(nf / max_exact) / math.log(REL_MAX_DIST / max_exact)
                         * (REL_BUCKETS - max_exact)).astype(np.int32)
    large = np.minimum(large, REL_BUCKETS - 1)
    return dist, np.where(n < max_exact, n, large)


def _attention(qkv, sinks, rel_bias):
    m = qkv.shape[0]
    blk = ATTN_BLOCK
    qd = ATTN_Q_HEADS * ATTN_HEAD_DIM
    kvd = 2 * ATTN_KV_HEADS * ATTN_HEAD_DIM
    step_rows = blk * ATTN_SUBBLOCKS
    dist, bucket = _t5_bucket_table()
    visible = (dist >= 0) & (dist < blk)
    onehot = jnp.asarray(np.arange(REL_BUCKETS)[:, None] == bucket.reshape(1, -1), F32)
    bias = jnp.dot(rel_bias.astype(F32).T, onehot, precision=lax.Precision.HIGHEST).reshape(-1, blk, 2 * blk)
    general = jnp.where(visible[None], bias, NEG_INF)
    first = jnp.where((visible & (np.arange(2 * blk) >= blk)[None, :])[None], bias, NEG_INF)
    table = jnp.stack([first, general])
    return pl.pallas_call(
        _attn_kernel,
        grid=(m // step_rows,),
        in_specs=[
            pl.BlockSpec(memory_space=pltpu.SMEM),
            pl.BlockSpec((step_rows, qd), lambda n: (n, 0)),
            pl.BlockSpec((step_rows, kvd), lambda n: (n, qd // kvd)),
            pl.BlockSpec((blk, kvd), lambda n: (jnp.maximum(n * ATTN_SUBBLOCKS - 1, 0), qd // kvd)),
            pl.BlockSpec(table.shape, lambda n: (0, 0, 0, 0)),
        ],
        out_specs=pl.BlockSpec((step_rows, qd), lambda n: (n, 0)),
        out_shape=jax.ShapeDtypeStruct((m, qd), BF16),
        compiler_params=_params(("arbitrary",)),
        name="swa_attention",
    )(sinks.astype(F32), qkv, qkv, qkv, table)


def _router_kernel(x_ref, g_ref, rw_ref, rb_ref, o_ref):
    xn = _rms_rows(x_ref[...], g_ref[...])
    x_hi = xn.astype(BF16)
    x_lo = (xn - x_hi.astype(F32)).astype(BF16)
    both = jnp.dot(x_hi, rw_ref[...], preferred_element_type=F32)
    logits = (both[:, :LANES] + both[:, LANES:]
              + jnp.dot(x_lo, rw_ref[:, :LANES], preferred_element_type=F32) + rb_ref[...])
    lane = lax.broadcasted_iota(jnp.int32, logits.shape, 1)
    m1 = jnp.max(logits, axis=-1, keepdims=True)
    i1 = jnp.min(jnp.where(logits == m1, lane, LANES), axis=-1, keepdims=True)
    rest = jnp.where(lane == i1, NEG_INF, logits)
    m2 = jnp.max(rest, axis=-1, keepdims=True)
    i2 = jnp.min(jnp.where(rest == m2, lane, LANES), axis=-1, keepdims=True)
    e = jnp.exp(m2 - m1)
    g1 = 1.0 / (1.0 + e)
    g2 = e * g1
    out = jnp.where(lane == 0, i1.astype(F32),
                    jnp.where(lane == 1, i2.astype(F32),
                              jnp.where(lane == 2, g1, jnp.where(lane == 3, g2, 0.0))))
    o_ref[...] = out


def _router(h, g, rw, rb, *, tm):
    m, d = h.shape
    rw_pad = jnp.zeros((d, LANES), F32).at[:, :N_EXPERTS].set(rw.astype(F32))
    rw_hi = rw_pad.astype(BF16)
    rw_pad = jnp.concatenate([rw_hi, (rw_pad - rw_hi.astype(F32)).astype(BF16)], axis=1)
    rb_pad = jnp.full((1, LANES), NEG_INF, F32).at[0, :N_EXPERTS].set(rb.astype(F32))
    return pl.pallas_call(
        _router_kernel,
        grid=(m // tm,),
        in_specs=[
            pl.BlockSpec((tm, d), lambda i: (i, 0)),
            pl.BlockSpec((1, d), lambda i: (0, 0)),
            pl.BlockSpec((d, 2 * LANES), lambda i: (0, 0)),
            pl.BlockSpec((1, LANES), lambda i: (0, 0)),
        ],
        out_specs=pl.BlockSpec((tm, LANES), lambda i: (i, 0)),
        out_shape=jax.ShapeDtypeStruct((m, LANES), F32),
        compiler_params=_params(("parallel",)),
        name="router",
    )(h, g, rw_pad, rb_pad)


def _moe_drain(h_hbm, xbuf, sem):
    for s in range(MOE_SUBS_PER_GROUP):
        rs = slice(s * MOE_SUB, (s + 1) * MOE_SUB)
        pltpu.make_async_copy(h_hbm.at[pl.ds(0, MOE_SUB)], xbuf.at[rs], sem).wait()
    slack = xbuf.shape[0] - MOE_GROUP
    pltpu.make_async_copy(h_hbm.at[pl.ds(0, slack)], xbuf.at[MOE_GROUP:, :], sem).wait()


def _moe_kernel(ge_ref, ns_ref, tok_ref, h_hbm, g_ref, wg_ref, wu_ref, wd_ref, o_ref,
                xbuf, xn_ref, sem, *, rps):
    grp = pl.program_id(0)
    j = pl.program_id(1)
    ns = ns_ref[grp]

    def row_copy(gi, r):
        tok = tok_ref[gi * MOE_GROUP + jnp.minimum(r, MOE_GROUP - 1)]
        return pltpu.make_async_copy(h_hbm.at[pl.ds(tok, 1)], xbuf.at[pl.ds(r, 1)], sem)

    def issue_step_rows(gi):
        for u in range(rps):
            row_copy(gi, j * rps + u).start()

    @pl.when(j == 0)
    def _():
        @pl.when(grp == 0)
        def _():
            def issue(r, carry):
                for u in range(MOE_UNROLL):
                    row_copy(0, r * MOE_UNROLL + u).start()
                return carry

            lax.fori_loop(0, xbuf.shape[0] // MOE_UNROLL, issue, 0)

        _moe_drain(h_hbm, xbuf, sem)
        for s in range(MOE_SUBS_PER_GROUP):
            rs = slice(s * MOE_SUB, (s + 1) * MOE_SUB)
            o_ref[rs, :] = jnp.zeros((MOE_SUB, o_ref.shape[1]), F32)

            @pl.when(s < ns)
            def _():
                xn_ref[rs, :] = _rms_rows(xbuf[rs, :], g_ref[...]).astype(BF16)

    def ffn(rs):
        xn = xn_ref[rs, :]
        gate = jnp.dot(xn, wg_ref[...], preferred_element_type=F32)
        up = jnp.dot(xn, wu_ref[...], preferred_element_type=F32)
        hid = (_silu(gate) * up).astype(BF16)
        o_ref[rs, :] += jnp.dot(hid, wd_ref[...], preferred_element_type=F32)

    @pl.when(ns == MOE_SUBS_PER_GROUP)
    def _():
        issue_step_rows(grp + 1)
        ffn(slice(0, MOE_GROUP))

    @pl.when(ns < MOE_SUBS_PER_GROUP)
    def _():
        issue_step_rows(grp + 1)

        def sub_block(s, carry):
            ffn(pl.ds(pl.multiple_of(s * MOE_SUB, MOE_SUB), MOE_SUB))
            return carry

        lax.fori_loop(0, ns, sub_block, 0)

    @pl.when((grp == pl.num_programs(0) - 1) & (j == pl.num_programs(1) - 1))
    def _():
        _moe_drain(h_hbm, xbuf, sem)


def _moe_ffn(h, g, wg, wu, wd, group_e, group_ns, src_tok, *, tf):
    t, d = h.shape
    n_groups = group_e.shape[0]
    f = wg.shape[2]
    nj = f // tf
    rps = -(-MOE_GROUP // (nj * SUBLANES)) * SUBLANES
    assert src_tok.shape[0] == (n_groups + 1) * MOE_GROUP

    def jj(grp, j, ns):
        return jnp.where(ns[grp] > 0, j, nj - 1)

    return pl.pallas_call(
        functools.partial(_moe_kernel, rps=rps),
        grid_spec=pltpu.PrefetchScalarGridSpec(
            num_scalar_prefetch=3,
            grid=(n_groups, nj),
            in_specs=[
                pl.BlockSpec(memory_space=pl.ANY),
                pl.BlockSpec((1, d), lambda grp, j, ge, ns, tok: (0, 0)),
                pl.BlockSpec((None, d, tf), lambda grp, j, ge, ns, tok: (ge[grp], 0, jj(grp, j, ns))),
                pl.BlockSpec((None, d, tf), lambda grp, j, ge, ns, tok: (ge[grp], 0, jj(grp, j, ns))),
                pl.BlockSpec((None, tf, d), lambda grp, j, ge, ns, tok: (ge[grp], jj(grp, j, ns), 0)),
            ],
            out_specs=pl.BlockSpec((MOE_GROUP, d), lambda grp, j, ge, ns, tok: (grp, 0)),
            scratch_shapes=[
                pltpu.VMEM((rps * nj, d), F32),
                pltpu.VMEM((MOE_GROUP, d), BF16),
                pltpu.SemaphoreType.DMA(()),
            ],
        ),
        out_shape=jax.ShapeDtypeStruct((n_groups * MOE_GROUP, d), F32),
        compiler_params=_params(("arbitrary", "arbitrary")),
        name="moe_ffn",
    )(group_e, group_ns, src_tok, h, g, wg, wu, wd)


def _combine_kernel(pos_ref, h_ref, route_ref, fg_ref, rows_hbm, o_ref, buf_a, buf_b, sem, *, tc):
    i = pl.program_id(0)
    slot = i % 2

    def copies(blk, sl, r):
        pa = pos_ref[TOP_K * (blk * tc + r)]
        pb = pos_ref[TOP_K * (blk * tc + r) + 1]
        return (pltpu.make_async_copy(rows_hbm.at[pl.ds(pa, 1)], buf_a.at[sl, pl.ds(r, 1)], sem.at[sl]),
                pltpu.make_async_copy(rows_hbm.at[pl.ds(pb, 1)], buf_b.at[sl, pl.ds(r, 1)], sem.at[sl]))

    def issue_block(blk, sl):
        def issue(r, carry):
            ca, cb = copies(blk, sl, r)
            ca.start()
            cb.start()
            return carry

        lax.fori_loop(0, tc, issue, 0, unroll=8)

    def wait_block(blk, sl):
        def wait(r, carry):
            ca, cb = copies(blk, sl, r)
            ca.wait()
            cb.wait()
            return carry

        lax.fori_loop(0, tc, wait, 0, unroll=8)

    @pl.when(i == 0)
    def _():
        issue_block(0, 0)

    wait_block(i, slot)

    last = pl.num_programs(0) - 1
    nxt = jnp.minimum(i + 1, last)
    for r in range(tc):
        ca, cb = copies(nxt, 1 - slot, r)
        ca.start()
        cb.start()
    route = route_ref[...]
    moe = route[:, 2:3] * buf_a[slot] + route[:, 3:4] * buf_b[slot]
    o_ref[...] = _rms_rows(h_ref[...] + moe, fg_ref[...])

    @pl.when(i == last)
    def _():
        wait_block(nxt, 1 - slot)


def _combine(h, route, final_g, rows, pos, *, tc):
    t, d = h.shape
    return pl.pallas_call(
        functools.partial(_combine_kernel, tc=tc),
        grid_spec=pltpu.PrefetchScalarGridSpec(
            num_scalar_prefetch=1,
            grid=(t // tc,),
            in_specs=[
                pl.BlockSpec((tc, d), lambda i, pos: (i, 0)),
                pl.BlockSpec((tc, LANES), lambda i, pos: (i, 0)),
                pl.BlockSpec((1, d), lambda i, pos: (0, 0)),
                pl.BlockSpec(memory_space=pl.ANY),
            ],
            out_specs=pl.BlockSpec((tc, d), lambda i, pos: (i, 0)),
            scratch_shapes=[pltpu.VMEM((2, tc, d), F32), pltpu.VMEM((2, tc, d), F32), pltpu.SemaphoreType.DMA((2,))],
        ),
        out_shape=jax.ShapeDtypeStruct((t, d), F32),
        compiler_params=_params(("arbitrary",)),
        name="moe_combine",
    )(pos, h, route, final_g, rows)


def _moe_plan(route, n_tok):
    flat_e = route[:, :TOP_K].astype(jnp.int32).reshape(-1)
    onehot = (flat_e[:, None] == jnp.arange(N_EXPERTS, dtype=jnp.int32)[None, :]).astype(jnp.int32)
    csum = jnp.cumsum(onehot, axis=0)
    rank = jnp.sum(onehot * (csum - onehot), axis=1)
    counts = csum[-1]
    n_groups = (n_tok * TOP_K) // MOE_GROUP + N_EXPERTS
    groups_e = (counts + MOE_GROUP - 1) // MOE_GROUP
    group_end = jnp.cumsum(groups_e)
    group_start = group_end - groups_e
    pos = jnp.sum(onehot * (group_start * MOE_GROUP)[None, :], axis=1) + rank
    flat_tok = jnp.arange(n_tok * TOP_K, dtype=jnp.int32) // TOP_K
    src_tok = jnp.zeros(((n_groups + 1) * MOE_GROUP,), jnp.int32).at[pos].set(
        flat_tok, unique_indices=True, mode="promise_in_bounds")
    experts = jnp.arange(N_EXPERTS, dtype=jnp.int32)
    gidx = jnp.arange(n_groups, dtype=jnp.int32)
    used = gidx < group_end[-1]
    last_e = jnp.max(jnp.where(groups_e > 0, experts, 0))
    ge = jnp.sum((group_end[None, :] <= gidx[:, None]).astype(jnp.int32), axis=1)
    ge = jnp.where(used, ge, last_e).astype(jnp.int32)
    mine = (ge[:, None] == experts[None, :]).astype(jnp.int32)
    subs_e = (counts + MOE_SUB - 1) // MOE_SUB
    subs_g = jnp.sum(mine * subs_e[None, :], axis=1)
    start_g = jnp.sum(mine * group_start[None, :], axis=1)
    ns = jnp.clip(subs_g - MOE_SUBS_PER_GROUP * (gidx - start_g), 0, MOE_SUBS_PER_GROUP)
    ns = jnp.where(used, ns, 0).astype(jnp.int32)
    return ge, ns, src_tok, pos.astype(jnp.int32)


def kernel(x, mix_norm_g, ffn_norm_g, final_norm_g, w_in, conv_w, conv_b, conv_ln_g, conv_ln_b, ssm_conv_w, ssm_conv_b, dt_bias, a_log, d_skip, ssm_norm_g, w_out, ffn_w_gate, ffn_w_up, ffn_w_down, w_qkv, b_qkv, w_o, b_o, sinks, rel_bias, router_w, router_b, moe_w_gate, moe_w_up, moe_w_down):
    bsz, seq, d = x.shape
    assert bsz == 1 and d == D_MODEL and seq % 512 == 0
    m = seq
    h = x.reshape(m, d)
    row = lambda v: v.reshape(1, -1).astype(F32)
    tm = min(1024, m)

    main_w = 2 * D_MODEL + D_MODEL + (D_MODEL + 2 * SSM_GROUPS * SSM_STATE)
    w_main = w_in[0].astype(BF16)
    w_dt = jnp.zeros((d, LANES), F32).at[:, :SSM_HEADS].set(w_in[0][:, main_w:]).astype(BF16)
    g0 = row(mix_norm_g[0])
    n_exp, _, expert_dim = moe_w_gate[0].shape
    wg_f32 = moe_w_gate[0].reshape(n_exp * d, expert_dim)
    wu_f32 = moe_w_up[0].reshape(n_exp * d, expert_dim)
    wd_f32 = moe_w_down[0].reshape(n_exp * expert_dim, d)

    proj, dt_raw, wd_b = _in_proj(h, g0, w_main, w_dt, wd_f32, tm=tm, tn=1024, n=main_w)
    conv_out, wg_b = _conv_module(proj, conv_w[0].astype(F32), row(conv_b[0]), row(conv_ln_g[0]), row(conv_ln_b[0]),
                                  wg_f32, tt=128)

    pad_heads = lambda v: jnp.zeros((1, LANES), F32).at[0, :SSM_HEADS].set(v.astype(F32))
    ssm_out, wu_b = _ssd(proj, dt_raw, ssm_conv_w[0].astype(F32), row(ssm_conv_b[0]),
                         pad_heads(dt_bias[0]), pad_heads(-jnp.exp(a_log[0].astype(F32))),
                         row(jnp.repeat(d_skip[0].astype(F32), SSM_HEAD_DIM)), row(ssm_norm_g[0]), wu_f32)

    wo = w_out[0].astype(BF16)
    h = _matmul_residual([conv_out, ssm_out], [wo[:D_MODEL], wo[D_MODEL:]], h, jnp.zeros((1, d), F32),
                         tm=256, tn=d, name="out_proj")
    h = _swiglu(h, row(ffn_norm_g[0]), ffn_w_gate[0].astype(BF16), ffn_w_up[0].astype(BF16),
                ffn_w_down[0].astype(BF16), tm=tm, tf=512)

    qkv = _norm_matmul(h, row(mix_norm_g[1]), w_qkv[0].astype(BF16), row(b_qkv[0]), tm=min(512, m), tn=w_qkv.shape[2],
                       out_dtype=BF16, name="qkv_proj")
    attn = _attention(qkv, sinks[0], rel_bias)
    h = _matmul_residual([attn], [w_o[0].astype(BF16)], h, row(b_o[0]), tm=min(512, m), tn=d, name="attn_out_proj")

    g1 = row(ffn_norm_g[1])
    route = _router(h, g1, router_w[0], router_b[0], tm=512)
    ge, ns, src_tok, pos = _moe_plan(route, m)
    rows = _moe_ffn(h, g1, wg_b.reshape(n_exp, d, expert_dim), wu_b.reshape(n_exp, d, expert_dim),
                    wd_b.reshape(n_exp, expert_dim, d), ge, ns, src_tok, tf=512)
    out = _combine(h, route, row(final_norm_g), rows, pos, tc=256)
    return out.reshape(bsz, seq, d)
```

```python
import functools
import math

import numpy as np
import jax
import jax.numpy as jnp
from jax import lax
from jax.experimental import pallas as pl
from jax.experimental.pallas import tpu as pltpu

F32 = jnp.float32
BF16 = jnp.bfloat16
NORM_EPS = 1e-5
NEG_INF = float("-inf")

LANES = 128
SUBLANES = 8
VMEM_LIMIT = 56 << 20

D_MODEL = 2048
CONV_WIDTH = 31
SSM_HEADS = 32
SSM_HEAD_DIM = 64
SSM_GROUPS = 8
SSM_STATE = 128
SSM_CONV_WIDTH = 4
SSM_CHUNK = 128
ATTN_Q_HEADS = 32
ATTN_KV_HEADS = 4
ATTN_HEAD_DIM = 64
ATTN_BLOCK = 128
ATTN_SUBBLOCKS = 2
REL_BUCKETS = 32
REL_MAX_DIST = 128
N_EXPERTS = 8
TOP_K = 2
MOE_SUB = 256
MOE_SUBS_PER_GROUP = 4
MOE_GROUP = MOE_SUB * MOE_SUBS_PER_GROUP
MOE_UNROLL = 8


def _params(sem):
    return pltpu.CompilerParams(dimension_semantics=sem, vmem_limit_bytes=VMEM_LIMIT)


def _sigmoid(x):
    return 0.5 + 0.5 * jnp.tanh(0.5 * x)


def _silu(x):
    half = 0.5 * x
    return half + half * jnp.tanh(half)


def _rms_rows(x, g):
    ms = jnp.mean(x * x, axis=-1, keepdims=True)
    return x * lax.rsqrt(ms + NORM_EPS) * g


def _cast_slab(i, n_steps, src_hbm, dst_hbm, inbuf, outbuf, in_sem, out_sem):
    rows = inbuf.shape[1]
    slot = i % 2

    def fetch(step, sl):
        return pltpu.make_async_copy(src_hbm.at[pl.ds(step * rows, rows)], inbuf.at[sl], in_sem.at[sl])

    def write_back(step, sl):
        return pltpu.make_async_copy(outbuf.at[sl], dst_hbm.at[pl.ds(step * rows, rows)], out_sem.at[sl])

    @pl.when(i == 0)
    def _():
        fetch(0, 0).start()

    @pl.when(i + 1 < n_steps)
    def _():
        fetch(i + 1, 1 - slot).start()

    fetch(i, slot).wait()

    @pl.when(i >= 2)
    def _():
        write_back(i - 2, slot).wait()

    outbuf[slot] = inbuf[slot].astype(BF16)
    write_back(i, slot).start()

    @pl.when(i == n_steps - 1)
    def _():
        write_back(i, slot).wait()

        @pl.when(i >= 1)
        def _():
            write_back(i - 1, 1 - slot).wait()


def _with_cast_job(kernel_fn, n_in, n_out):
    def wrapped(*refs):
        ins, src = refs[:n_in], refs[n_in]
        outs, dst = refs[n_in + 1:n_in + 1 + n_out], refs[n_in + 1 + n_out]
        rest = refs[n_in + 2 + n_out:]
        scratch, cast_scratch = rest[:-4], rest[-4:]
        _cast_slab(pl.program_id(0), pl.num_programs(0), src, dst, *cast_scratch)
        kernel_fn(*ins, *outs, *scratch)

    return wrapped


def _cast_job_extras(src, n_steps):
    assert src.shape[0] % n_steps == 0
    rows, cols = src.shape[0] // n_steps, src.shape[1]
    scratch = [pltpu.VMEM((2, rows, cols), F32), pltpu.VMEM((2, rows, cols), BF16),
               pltpu.SemaphoreType.DMA((2,)), pltpu.SemaphoreType.DMA((2,))]
    any_spec = pl.BlockSpec(memory_space=pl.ANY)
    return any_spec, any_spec, jax.ShapeDtypeStruct(src.shape, BF16), scratch


def _norm_matmul_kernel(x_ref, g_ref, w_ref, b_ref, o_ref, xn_ref):
    @pl.when(pl.program_id(1) == 0)
    def _():
        xn_ref[...] = _rms_rows(x_ref[...], g_ref[...]).astype(BF16)

    acc = jnp.dot(xn_ref[...], w_ref[...], preferred_element_type=F32)
    o_ref[...] = (acc + b_ref[...]).astype(o_ref.dtype)


def _norm_matmul(x, g, w, b, *, tm, tn, out_dtype, name):
    m, k = x.shape
    n = w.shape[1]
    return pl.pallas_call(
        _norm_matmul_kernel,
        grid=(m // tm, n // tn),
        in_specs=[
            pl.BlockSpec((tm, k), lambda i, j: (i, 0)),
            pl.BlockSpec((1, k), lambda i, j: (0, 0)),
            pl.BlockSpec((k, tn), lambda i, j: (0, j)),
            pl.BlockSpec((1, tn), lambda i, j: (0, j)),
        ],
        out_specs=pl.BlockSpec((tm, tn), lambda i, j: (i, j)),
        out_shape=jax.ShapeDtypeStruct((m, n), out_dtype),
        scratch_shapes=[pltpu.VMEM((tm, k), BF16)],
        compiler_params=_params(("parallel", "arbitrary")),
        name=name,
    )(x, g, w, b)


def _in_proj_kernel(x_ref, g_ref, w_ref, wdt_ref, cast_src, o_ref, odt_ref, cast_dst,
                    xn_ref, inbuf, outbuf, in_sem, out_sem, *, cast_cols):
    i = pl.program_id(0)
    j = pl.program_id(1)

    @pl.when(j == 0)
    def _():
        xn_ref[...] = _rms_rows(x_ref[...], g_ref[...]).astype(BF16)
        odt_ref[...] = jnp.dot(xn_ref[...], wdt_ref[...], preferred_element_type=F32)

    @pl.when(j < cast_cols)
    def _():
        _cast_slab(i * cast_cols + j, pl.num_programs(0) * cast_cols, cast_src, cast_dst,
                   inbuf, outbuf, in_sem, out_sem)

    o_ref[...] = jnp.dot(xn_ref[...], w_ref[...], preferred_element_type=F32).astype(o_ref.dtype)


def _in_proj(x, g, w, wdt, cast_src, *, tm, tn, n):
    m, k = x.shape
    assert n % tn == 0 and n <= w.shape[1]
    ndt = wdt.shape[1]
    n_rows, n_cols = m // tm, n // tn
    cast_cols = max(s for s in range(1, n_cols + 1)
                    if cast_src.shape[0] % (n_rows * s) == 0 and (cast_src.shape[0] // (n_rows * s)) % 16 == 0)
    cast_in, cast_out, cast_shape, cast_scratch = _cast_job_extras(cast_src, n_rows * cast_cols)
    return pl.pallas_call(
        functools.partial(_in_proj_kernel, cast_cols=cast_cols),
        grid=(n_rows, n_cols),
        in_specs=[
            pl.BlockSpec((tm, k), lambda i, j: (i, 0)),
            pl.BlockSpec((1, k), lambda i, j: (0, 0)),
            pl.BlockSpec((k, tn), lambda i, j: (0, j)),
            pl.BlockSpec((k, ndt), lambda i, j: (0, 0)),
            cast_in,
        ],
        out_specs=[pl.BlockSpec((tm, tn), lambda i, j: (i, j)), pl.BlockSpec((tm, ndt), lambda i, j: (i, 0)), cast_out],
        out_shape=[jax.ShapeDtypeStruct((m, n), BF16), jax.ShapeDtypeStruct((m, ndt), F32), cast_shape],
        scratch_shapes=[pltpu.VMEM((tm, k), BF16)] + cast_scratch,
        compiler_params=_params(("arbitrary", "arbitrary")),
        name="in_proj",
    )(x, g, w, wdt, cast_src)


def _matmul_residual_kernel(*refs, n_in):
    a_refs, w_refs = refs[:n_in], refs[n_in:2 * n_in]
    res_ref, b_ref, o_ref = refs[2 * n_in:]
    acc = res_ref[...] + b_ref[...]
    for a_ref, w_ref in zip(a_refs, w_refs):
        acc = acc + jnp.dot(a_ref[...], w_ref[...], preferred_element_type=F32)
    o_ref[...] = acc


def _matmul_residual(a_list, w_list, res, b, *, tm, tn, name):
    m, n = res.shape
    n_in = len(a_list)
    in_specs = [pl.BlockSpec((tm, a.shape[1]), lambda j, i: (i, 0)) for a in a_list]
    in_specs += [pl.BlockSpec((w.shape[0], tn), lambda j, i: (0, j)) for w in w_list]
    in_specs += [pl.BlockSpec((tm, tn), lambda j, i: (i, j)), pl.BlockSpec((1, tn), lambda j, i: (0, j))]
    return pl.pallas_call(
        functools.partial(_matmul_residual_kernel, n_in=n_in),
        grid=(n // tn, m // tm),
        in_specs=in_specs,
        out_specs=pl.BlockSpec((tm, tn), lambda j, i: (i, j)),
        out_shape=jax.ShapeDtypeStruct((m, n), F32),
        compiler_params=_params(("parallel", "parallel")),
        name=name,
    )(*a_list, *w_list, res, b)


def _swiglu_kernel(x_ref, g_ref, wg_ref, wu_ref, wd_ref, o_ref, xn_ref):
    @pl.when(pl.program_id(1) == 0)
    def _():
        x = x_ref[...]
        xn_ref[...] = _rms_rows(x, g_ref[...]).astype(BF16)
        o_ref[...] = x

    xn = xn_ref[...]
    gate = jnp.dot(xn, wg_ref[...], preferred_element_type=F32)
    up = jnp.dot(xn, wu_ref[...], preferred_element_type=F32)
    hid = (_silu(gate) * up).astype(BF16)
    o_ref[...] += jnp.dot(hid, wd_ref[...], preferred_element_type=F32)


def _swiglu(x, g, wg, wu, wd, *, tm, tf):
    m, d = x.shape
    f = wg.shape[1]
    return pl.pallas_call(
        _swiglu_kernel,
        grid=(m // tm, f // tf),
        in_specs=[
            pl.BlockSpec((tm, d), lambda i, j: (i, 0)),
            pl.BlockSpec((1, d), lambda i, j: (0, 0)),
            pl.BlockSpec((d, tf), lambda i, j: (0, j)),
            pl.BlockSpec((d, tf), lambda i, j: (0, j)),
            pl.BlockSpec((tf, d), lambda i, j: (j, 0)),
        ],
        out_specs=pl.BlockSpec((tm, d), lambda i, j: (i, 0)),
        out_shape=jax.ShapeDtypeStruct((m, d), F32),
        scratch_shapes=[pltpu.VMEM((tm, d), BF16)],
        compiler_params=_params(("parallel", "arbitrary")),
        name="swiglu",
    )(x, g, wg, wu, wd)


CONV_HALO = 32
CONV_ROWS = 64


def _conv_module_kernel(val_ref, gate_ref, pval_ref, pgate_ref, w_ref, b_ref, lg_ref, lb_ref, o_ref,
                        ubuf, cbuf, *, tt):
    i = pl.program_id(0)

    def glu(v, g):
        return v.astype(F32) * _sigmoid(g.astype(F32))

    ubuf[0:CONV_HALO, :] = jnp.where(i > 0, glu(pval_ref[...], pgate_ref[...]), 0.0)
    ubuf[CONV_HALO:CONV_HALO + tt, :] = glu(val_ref[...], gate_ref[...])

    off = CONV_HALO - (CONV_WIDTH - 1)
    taps = [[] for _ in range(SUBLANES)]
    for j in range(CONV_WIDTH):
        a, r = divmod(off + j, SUBLANES)
        taps[r].append((a, j))
    n_ch = ubuf.shape[1]
    for cb in range(n_ch // LANES):
        ls = slice(cb * LANES, (cb + 1) * LANES)
        for r0 in range(0, tt, CONV_ROWS):
            big = ubuf[r0:r0 + CONV_ROWS + CONV_HALO, ls]
            out = None
            for r in range(SUBLANES):
                rows = CONV_ROWS if r == 0 else CONV_ROWS + SUBLANES
                q = None
                for a, j in taps[r]:
                    term = w_ref[j:j + 1, ls] * big[a * SUBLANES:a * SUBLANES + rows, :]
                    q = term if q is None else q + term
                if r:
                    q = pltpu.roll(q, rows - r, axis=0)[:CONV_ROWS, :]
                out = q if out is None else out + q
            cbuf[r0:r0 + CONV_ROWS, ls] = out

    c = cbuf[...] + b_ref[...]
    mu = jnp.mean(c, axis=-1, keepdims=True)
    d = c - mu
    var = jnp.mean(d * d, axis=-1, keepdims=True)
    y = d * lax.rsqrt(var + NORM_EPS) * lg_ref[...] + lb_ref[...]
    o_ref[...] = _silu(y).astype(o_ref.dtype)


def _conv_module(proj, conv_w, conv_b, ln_g, ln_b, cast_src, *, tt):
    m = proj.shape[0]
    c = D_MODEL
    hb = tt // CONV_HALO
    cast_in, cast_out, cast_shape, cast_scratch = _cast_job_extras(cast_src, m // tt)
    return pl.pallas_call(
        _with_cast_job(functools.partial(_conv_module_kernel, tt=tt), 8, 1),
        grid=(m // tt,),
        in_specs=[
            pl.BlockSpec((tt, c), lambda i: (i, 0)),
            pl.BlockSpec((tt, c), lambda i: (i, 1)),
            pl.BlockSpec((CONV_HALO, c), lambda i: (jnp.maximum(i * hb - 1, 0), 0)),
            pl.BlockSpec((CONV_HALO, c), lambda i: (jnp.maximum(i * hb - 1, 0), 1)),
            pl.BlockSpec((CONV_WIDTH, c), lambda i: (0, 0)),
            pl.BlockSpec((1, c), lambda i: (0, 0)),
            pl.BlockSpec((1, c), lambda i: (0, 0)),
            pl.BlockSpec((1, c), lambda i: (0, 0)),
            cast_in,
        ],
        out_specs=[pl.BlockSpec((tt, c), lambda i: (i, 0)), cast_out],
        out_shape=[jax.ShapeDtypeStruct((m, c), BF16), cast_shape],
        scratch_shapes=[pltpu.VMEM((CONV_HALO + tt, c), F32), pltpu.VMEM((tt, c), F32)] + cast_scratch,
        compiler_params=_params(("arbitrary",)),
        name="conv_module",
    )(proj, proj, proj, proj, conv_w, conv_b, ln_g, ln_b, cast_src)


SSM_HALO = 16


def _conv4_silu(cur_ref, prev_ref, shift_ref, w_ref, b_ref, cols, first):
    q = cur_ref.shape[0]
    cur = cur_ref[:, cols]
    prev = prev_ref[:, cols]
    xb = jnp.concatenate([jnp.where(first, jnp.zeros_like(prev), prev), cur], axis=0)
    back = jnp.dot(shift_ref[...], xb, preferred_element_type=F32)
    last = SSM_CONV_WIDTH - 1
    acc = b_ref[:, cols] + w_ref[last:last + 1, cols] * cur.astype(F32)
    for k in range(1, SSM_CONV_WIDTH):
        acc = acc + w_ref[last - k:last - k + 1, cols] * back[(k - 1) * q:k * q, :]
    return _silu(acc)


def _ssd_kernel(z_ref, x_ref, b_ref, c_ref, px_ref, pb_ref, pc_ref, dtr_ref,
                wx_ref, wb_ref, wc_ref, bx_ref, bb_ref, bc_ref,
                dtb_ref, a_ref, dskip_ref, ng_ref, e_ref, sh_ref, o_ref,
                state, ybuf):
    i = pl.program_id(0)
    q = SSM_CHUNK
    first = i == 0

    @pl.when(first)
    def _():
        state[...] = jnp.zeros_like(state)

    everything = slice(None)
    xs = _conv4_silu(x_ref, px_ref, sh_ref, wx_ref, bx_ref, everything, first)
    bm = _conv4_silu(b_ref, pb_ref, sh_ref, wb_ref, bb_ref, everything, first)
    cm = _conv4_silu(c_ref, pc_ref, sh_ref, wc_ref, bc_ref, everything, first)

    pre = dtr_ref[...] + dtb_ref[...]
    dt = jnp.maximum(pre, 0.0) + jnp.log(1.0 + jnp.exp(-jnp.abs(pre)))
    dta = dt * a_ref[...]
    row = lax.broadcasted_iota(jnp.int32, (q, q), 0)
    col = lax.broadcasted_iota(jnp.int32, (q, q), 1)
    causal = row >= col
    acs = jnp.dot(causal.astype(F32), dta, precision=lax.Precision.HIGHEST, preferred_element_type=F32)
    acs_t = acs.T
    dt_t = dt.T
    eacs = jnp.exp(acs)
    wdec = dt * jnp.exp(acs[q - 1:q, :] - acs)

    def split(v):
        hi = v.astype(BF16)
        return hi, (v - hi.astype(F32)).astype(BF16)

    parts = jnp.concatenate(split(eacs) + split(wdec), axis=0)
    wide = jnp.dot(parts, e_ref[...], preferred_element_type=F32)
    eacs_full = wide[0:q] + wide[q:2 * q]
    xd = (xs * (wide[2 * q:3 * q] + wide[3 * q:4 * q])).astype(BF16)

    xs_b = xs.astype(BF16)
    lane = lax.broadcasted_iota(jnp.int32, (1, xs.shape[1]), 1)
    low_head = (lane % LANES) < SSM_HEAD_DIM
    xs_lo = jnp.where(low_head, xs_b, jnp.zeros_like(xs_b))
    xs_hi = jnp.where(low_head, jnp.zeros_like(xs_b), xs_b)

    gw = SSM_HEAD_DIM * (SSM_HEADS // SSM_GROUPS)
    for g in range(SSM_GROUPS):
        ns = slice(g * SSM_STATE, (g + 1) * SSM_STATE)
        gs = slice(g * gw, (g + 1) * gw)
        bg_f = bm[:, ns]
        bg = bg_f.astype(BF16)
        cg = cm[:, ns].astype(BF16)
        cb = lax.dot_general(cg, bg, (((1,), (1,)), ((), ())), preferred_element_type=F32)
        prev = state[g]
        y_off = jnp.dot(cg, prev.astype(BF16), preferred_element_type=F32)
        for c in range(gw // LANES):
            cs = slice(g * gw + c * LANES, g * gw + (c + 1) * LANES)
            acc = y_off[:, c * LANES:(c + 1) * LANES] * eacs_full[:, cs]
            for par, xpart in ((0, xs_lo), (1, xs_hi)):
                h = g * (SSM_HEADS // SSM_GROUPS) + 2 * c + par
                seg = acs[:, h:h + 1] - acs_t[h:h + 1, :]
                mmat = cb * jnp.exp(jnp.where(causal, seg, NEG_INF)) * dt_t[h:h + 1, :]
                acc = acc + jnp.dot(mmat.astype(BF16), xpart[:, cs], preferred_element_type=F32)
            ybuf[:, cs] = acc
        new_states = jnp.dot(bg_f.T.astype(BF16), xd[:, gs], preferred_element_type=F32)
        state[g] = prev * eacs_full[q - 1:q, gs] + new_states

    y = ybuf[...] + xs * dskip_ref[...]
    z = z_ref[...].astype(F32)
    y = y * _silu(z)
    for g in range(SSM_GROUPS):
        gs = slice(g * gw, (g + 1) * gw)
        yg = y[:, gs]
        ms = jnp.mean(yg * yg, axis=-1, keepdims=True)
        o_ref[:, gs] = (yg * lax.rsqrt(ms + NORM_EPS) * ng_ref[:, gs]).astype(o_ref.dtype)


def _ssd(proj, dt_raw, conv_w, conv_b, dt_bias, a_neg, d_skip, norm_g, cast_src):
    m = proj.shape[0]
    q = SSM_CHUNK
    inner = D_MODEL
    bc = SSM_GROUPS * SSM_STATE
    hb = q // SSM_HALO
    wx, wb, wc = conv_w[:, :inner], conv_w[:, inner:inner + bc], conv_w[:, inner + bc:]
    bx, bb, bcc = conv_b[:, :inner], conv_b[:, inner:inner + bc], conv_b[:, inner + bc:]
    expand = jnp.asarray(np.arange(LANES)[:, None] == (np.arange(inner) // SSM_HEAD_DIM)[None, :], BF16)
    t_idx = np.arange(q)
    shift_np = np.zeros((SSM_CONV_WIDTH - 1, q, SSM_HALO + q), np.float32)
    for k in range(1, SSM_CONV_WIDTH):
        shift_np[k - 1, t_idx, SSM_HALO + t_idx - k] = 1.0
    shift = jnp.asarray(shift_np.reshape(-1, SSM_HALO + q), BF16)
    prev = lambda i: jnp.maximum(i * hb - 1, 0)
    full = lambda shape: pl.BlockSpec(shape, lambda i: (0,) * len(shape))
    cast_in, cast_out, cast_shape, cast_scratch = _cast_job_extras(cast_src, m // q)
    return pl.pallas_call(
        _with_cast_job(_ssd_kernel, 20, 1),
        grid=(m // q,),
        in_specs=[
            pl.BlockSpec((q, inner), lambda i: (i, 2)),
            pl.BlockSpec((q, inner), lambda i: (i, 3)),
            pl.BlockSpec((q, bc), lambda i: (i, 8)),
            pl.BlockSpec((q, bc), lambda i: (i, 9)),
            pl.BlockSpec((SSM_HALO, inner), lambda i: (prev(i), 3)),
            pl.BlockSpec((SSM_HALO, bc), lambda i: (prev(i), 8)),
            pl.BlockSpec((SSM_HALO, bc), lambda i: (prev(i), 9)),
            pl.BlockSpec((q, LANES), lambda i: (i, 0)),
            full(wx.shape), full(wb.shape), full(wc.shape),
            full(bx.shape), full(bb.shape), full(bcc.shape),
            full((1, LANES)), full((1, LANES)), full((1, inner)), full((1, inner)),
            full((LANES, inner)),
            full(shift.shape),
            cast_in,
        ],
        out_specs=[pl.BlockSpec((q, inner), lambda i: (i, 0)), cast_out],
        out_shape=[jax.ShapeDtypeStruct((m, inner), BF16), cast_shape],
        scratch_shapes=[
            pltpu.VMEM((SSM_GROUPS, SSM_STATE, inner // SSM_GROUPS), F32),
            pltpu.VMEM((q, inner), F32),
        ] + cast_scratch,
        compiler_params=_params(("arbitrary",)),
        name="ssd",
    )(proj, proj, proj, proj, proj, proj, proj, dt_raw, wx, wb, wc, bx, bb, bcc,
      dt_bias, a_neg, d_skip, norm_g, expand, shift, cast_src)


def _attn_kernel(sinks_ref, q_ref, kvc_ref, kvp_ref, bias_ref, o_ref):
    blk = ATTN_BLOCK
    n = pl.program_id(0)
    for sub in range(ATTN_SUBBLOCKS):
        rows = slice(sub * blk, (sub + 1) * blk)
        prev = kvp_ref[...] if sub == 0 else kvc_ref[(sub - 1) * blk:sub * blk, :]
        table = jnp.where(n == 0, 0, 1) if sub == 0 else 1
        _attend_block(sinks_ref, q_ref, prev, kvc_ref[rows, :], bias_ref, table, o_ref, rows)


def _attend_block(sinks_ref, q_ref, kv_prev, kv_cur, bias_ref, table, o_ref, rows):
    blk = ATTN_BLOCK
    kvw = ATTN_KV_HEADS * ATTN_HEAD_DIM
    rep = ATTN_Q_HEADS // ATTN_KV_HEADS
    kv = jnp.concatenate([kv_prev, kv_cur], axis=0).astype(F32)
    lane = lax.broadcasted_iota(jnp.int32, (1, LANES), 1)
    low = lane < ATTN_HEAD_DIM
    scale = ATTN_HEAD_DIM ** -0.5

    for g in range(ATTN_KV_HEADS):
        pc = g // 2
        kcol = kv[:, pc * LANES:(pc + 1) * LANES] * scale
        vcol = kv[:, kvw + pc * LANES:kvw + (pc + 1) * LANES]
        kroll = pltpu.roll(kcol, ATTN_HEAD_DIM, axis=1)
        vroll = pltpu.roll(vcol, ATTN_HEAD_DIM, axis=1)
        if g % 2 == 0:
            k_lo, k_hi = jnp.where(low, kcol, 0.0), jnp.where(low, 0.0, kroll)
            v_lo, v_hi = jnp.where(low, vcol, 0.0), jnp.where(low, 0.0, vroll)
        else:
            k_lo, k_hi = jnp.where(low, kroll, 0.0), jnp.where(low, 0.0, kcol)
            v_lo, v_hi = jnp.where(low, vroll, 0.0), jnp.where(low, 0.0, vcol)
        k_lo, k_hi, v_lo, v_hi = (t.astype(BF16) for t in (k_lo, k_hi, v_lo, v_hi))

        ncol = rep // 2
        qs = jnp.concatenate([q_ref[rows, (g * ncol + c) * LANES:(g * ncol + c + 1) * LANES] for c in range(ncol)], axis=0)
        nt = (((1,), (1,)), ((), ()))
        s_even = lax.dot_general(qs, k_lo, nt, preferred_element_type=F32)
        s_odd = lax.dot_general(qs, k_hi, nt, preferred_element_type=F32)
        for c in range(ncol):
            out = None
            for par, s_all, vv in ((0, s_even, v_lo), (1, s_odd, v_hi)):
                h = g * rep + 2 * c + par
                logits = s_all[c * blk:(c + 1) * blk, :] + bias_ref[table, h]
                sink = sinks_ref[h]
                mx = jnp.maximum(jnp.max(logits, axis=-1, keepdims=True), sink)
                p = jnp.exp(logits - mx)
                denom = jnp.sum(p, axis=-1, keepdims=True) + jnp.exp(sink - mx)
                o = jnp.dot(p.astype(BF16), vv, preferred_element_type=F32) * (1.0 / denom)
                out = o if out is None else out + o
            col = g * ncol + c
            o_ref[rows, col * LANES:(col + 1) * LANES] = out.astype(o_ref.dtype)


def _t5_bucket_table():
    dist = np.arange(ATTN_BLOCK)[:, None] + ATTN_BLOCK - np.arange(2 * ATTN_BLOCK)[None, :]
    n = np.maximum(dist, 0)
    max_exact = REL_BUCKETS // 2
    nf = np.maximum(n, 1).astype(np.float32)
    large = max_exact + (np.log(nf / max_exact) / math.log(REL_MAX_DIST / max_exact)
                         * (REL_BUCKETS - max_exact)).astype(np.int32)
    large = np.minimum(large, REL_BUCKETS - 1)
    return dist, np.where(n < max_exact, n, large)


def _attention(qkv, sinks, rel_bias):
    m = qkv.shape[0]
    blk = ATTN_BLOCK
    qd = ATTN_Q_HEADS * ATTN_HEAD_DIM
    kvd = 2 * ATTN_KV_HEADS * ATTN_HEAD_DIM
    step_rows = blk * ATTN_SUBBLOCKS
    dist, bucket = _t5_bucket_table()
    visible = (dist >= 0) & (dist < blk)
    onehot = jnp.asarray(np.arange(REL_BUCKETS)[:, None] == bucket.reshape(1, -1), F32)
    bias = jnp.dot(rel_bias.astype(F32).T, onehot, precision=lax.Precision.HIGHEST).reshape(-1, blk, 2 * blk)
    general = jnp.where(visible[None], bias, NEG_INF)
    first = jnp.where((visible & (np.arange(2 * blk) >= blk)[None, :])[None], bias, NEG_INF)
    table = jnp.stack([first, general])
    return pl.pallas_call(
        _attn_kernel,
        grid=(m // step_rows,),
        in_specs=[
            pl.BlockSpec(memory_space=pltpu.SMEM),
            pl.BlockSpec((step_rows, qd), lambda n: (n, 0)),
            pl.BlockSpec((step_rows, kvd), lambda n: (n, qd // kvd)),
            pl.BlockSpec((blk, kvd), lambda n: (jnp.maximum(n * ATTN_SUBBLOCKS - 1, 0), qd // kvd)),
            pl.BlockSpec(table.shape, lambda n: (0, 0, 0, 0)),
        ],
        out_specs=pl.BlockSpec((step_rows, qd), lambda n: (n, 0)),
        out_shape=jax.ShapeDtypeStruct((m, qd), BF16),
        compiler_params=_params(("arbitrary",)),
        name="swa_attention",
    )(sinks.astype(F32), qkv, qkv, qkv, table)


def _router_kernel(x_ref, g_ref, rw_ref, rb_ref, o_ref):
    xn = _rms_rows(x_ref[...], g_ref[...])
    x_hi = xn.astype(BF16)
    x_lo = (xn - x_hi.astype(F32)).astype(BF16)
    both = jnp.dot(x_hi, rw_ref[...], preferred_element_type=F32)
    logits = (both[:, :LANES] + both[:, LANES:]
              + jnp.dot(x_lo, rw_ref[:, :LANES], preferred_element_type=F32) + rb_ref[...])
    lane = lax.broadcasted_iota(jnp.int32, logits.shape, 1)
    m1 = jnp.max(logits, axis=-1, keepdims=True)
    i1 = jnp.min(jnp.where(logits == m1, lane, LANES), axis=-1, keepdims=True)
    rest = jnp.where(lane == i1, NEG_INF, logits)
    m2 = jnp.max(rest, axis=-1, keepdims=True)
    i2 = jnp.min(jnp.where(rest == m2, lane, LANES), axis=-1, keepdims=True)
    e = jnp.exp(m2 - m1)
    g1 = 1.0 / (1.0 + e)
    g2 = e * g1
    out = jnp.where(lane == 0, i1.astype(F32),
                    jnp.where(lane == 1, i2.astype(F32),
                              jnp.where(lane == 2, g1, jnp.where(lane == 3, g2, 0.0))))
    o_ref[...] = out


def _router(h, g, rw, rb, *, tm):
    m, d = h.shape
    rw_pad = jnp.zeros((d, LANES), F32).at[:, :N_EXPERTS].set(rw.astype(F32))
    rw_hi = rw_pad.astype(BF16)
    rw_pad = jnp.concatenate([rw_hi, (rw_pad - rw_hi.astype(F32)).astype(BF16)], axis=1)
    rb_pad = jnp.full((1, LANES), NEG_INF, F32).at[0, :N_EXPERTS].set(rb.astype(F32))
    return pl.pallas_call(
        _router_kernel,
        grid=(m // tm,),
        in_specs=[
            pl.BlockSpec((tm, d), lambda i: (i, 0)),
            pl.BlockSpec((1, d), lambda i: (0, 0)),
            pl.BlockSpec((d, 2 * LANES), lambda i: (0, 0)),
            pl.BlockSpec((1, LANES), lambda i: (0, 0)),
        ],
        out_specs=pl.BlockSpec((tm, LANES), lambda i: (i, 0)),
        out_shape=jax.ShapeDtypeStruct((m, LANES), F32),
        compiler_params=_params(("parallel",)),
        name="router",
    )(h, g, rw_pad, rb_pad)


def _moe_drain(h_hbm, xbuf, sem):
    for s in range(MOE_SUBS_PER_GROUP):
        rs = slice(s * MOE_SUB, (s + 1) * MOE_SUB)
        pltpu.make_async_copy(h_hbm.at[pl.ds(0, MOE_SUB)], xbuf.at[rs], sem).wait()
    slack = xbuf.shape[0] - MOE_GROUP
    pltpu.make_async_copy(h_hbm.at[pl.ds(0, slack)], xbuf.at[MOE_GROUP:, :], sem).wait()


def _moe_kernel(ge_ref, ns_ref, tok_ref, h_hbm, g_ref, wg_ref, wu_ref, wd_ref, o_ref,
                xbuf, xn_ref, sem, *, rps):
    grp = pl.program_id(0)
    j = pl.program_id(1)
    ns = ns_ref[grp]

    def row_copy(gi, r):
        tok = tok_ref[gi * MOE_GROUP + jnp.minimum(r, MOE_GROUP - 1)]
        return pltpu.make_async_copy(h_hbm.at[pl.ds(tok, 1)], xbuf.at[pl.ds(r, 1)], sem)

    def issue_step_rows(gi):
        for u in range(rps):
            row_copy(gi, j * rps + u).start()

    @pl.when(j == 0)
    def _():
        @pl.when(grp == 0)
        def _():
            def issue(r, carry):
                for u in range(MOE_UNROLL):
                    row_copy(0, r * MOE_UNROLL + u).start()
                return carry

            lax.fori_loop(0, xbuf.shape[0] // MOE_UNROLL, issue, 0)

        _moe_drain(h_hbm, xbuf, sem)
        for s in range(MOE_SUBS_PER_GROUP):
            rs = slice(s * MOE_SUB, (s + 1) * MOE_SUB)
            o_ref[rs, :] = jnp.zeros((MOE_SUB, o_ref.shape[1]), F32)

            @pl.when(s < ns)
            def _():
                xn_ref[rs, :] = _rms_rows(xbuf[rs, :], g_ref[...]).astype(BF16)

    def ffn(rs):
        xn = xn_ref[rs, :]
        gate = jnp.dot(xn, wg_ref[...].astype(BF16), preferred_element_type=F32)
        up = jnp.dot(xn, wu_ref[...].astype(BF16), preferred_element_type=F32)
        hid = (_silu(gate) * up).astype(BF16)
        o_ref[rs, :] += jnp.dot(hid, wd_ref[...].astype(BF16), preferred_element_type=F32)

    @pl.when(ns == MOE_SUBS_PER_GROUP)
    def _():
        issue_step_rows(grp + 1)
        ffn(slice(0, MOE_GROUP))

    @pl.when(ns < MOE_SUBS_PER_GROUP)
    def _():
        issue_step_rows(grp + 1)

        def sub_block(s, carry):
            ffn(pl.ds(pl.multiple_of(s * MOE_SUB, MOE_SUB), MOE_SUB))
            return carry

        lax.fori_loop(0, ns, sub_block, 0)

    @pl.when((grp == pl.num_programs(0) - 1) & (j == pl.num_programs(1) - 1))
    def _():
        _moe_drain(h_hbm, xbuf, sem)


def _moe_ffn(h, g, wg, wu, wd, group_e, group_ns, src_tok, *, tf):
    t, d = h.shape
    n_groups = group_e.shape[0]
    f = wg.shape[2]
    nj = f // tf
    rps = -(-MOE_GROUP // (nj * SUBLANES)) * SUBLANES
    assert src_tok.shape[0] == (n_groups + 1) * MOE_GROUP

    def jj(grp, j, ns):
        return jnp.where(ns[grp] > 0, j, nj - 1)

    return pl.pallas_call(
        functools.partial(_moe_kernel, rps=rps),
        grid_spec=pltpu.PrefetchScalarGridSpec(
            num_scalar_prefetch=3,
            grid=(n_groups, nj),
            in_specs=[
                pl.BlockSpec(memory_space=pl.ANY),
                pl.BlockSpec((1, d), lambda grp, j, ge, ns, tok: (0, 0)),
                pl.BlockSpec((None, d, tf), lambda grp, j, ge, ns, tok: (ge[grp], 0, jj(grp, j, ns))),
                pl.BlockSpec((None, d, tf), lambda grp, j, ge, ns, tok: (ge[grp], 0, jj(grp, j, ns))),
                pl.BlockSpec((None, tf, d), lambda grp, j, ge, ns, tok: (ge[grp], jj(grp, j, ns), 0)),
            ],
            out_specs=pl.BlockSpec((MOE_GROUP, d), lambda grp, j, ge, ns, tok: (grp, 0)),
            scratch_shapes=[
                pltpu.VMEM((rps * nj, d), F32),
                pltpu.VMEM((MOE_GROUP, d), BF16),
                pltpu.SemaphoreType.DMA(()),
            ],
        ),
        out_shape=jax.ShapeDtypeStruct((n_groups * MOE_GROUP, d), F32),
        compiler_params=_params(("arbitrary", "arbitrary")),
        name="moe_ffn",
    )(group_e, group_ns, src_tok, h, g, wg, wu, wd)


def _combine_kernel(pos_ref, h_ref, route_ref, fg_ref, rows_hbm, o_ref, buf_a, buf_b, sem, *, tc):
    i = pl.program_id(0)
    slot = i % 2

    def copies(blk, sl, r):
        pa = pos_ref[TOP_K * (blk * tc + r)]
        pb = pos_ref[TOP_K * (blk * tc + r) + 1]
        return (pltpu.make_async_copy(rows_hbm.at[pl.ds(pa, 1)], buf_a.at[sl, pl.ds(r, 1)], sem.at[sl]),
                pltpu.make_async_copy(rows_hbm.at[pl.ds(pb, 1)], buf_b.at[sl, pl.ds(r, 1)], sem.at[sl]))

    def issue_block(blk, sl):
        def issue(r, carry):
            ca, cb = copies(blk, sl, r)
            ca.start()
            cb.start()
            return carry

        lax.fori_loop(0, tc, issue, 0, unroll=8)

    def wait_block(blk, sl):
        def wait(r, carry):
            ca, cb = copies(blk, sl, r)
            ca.wait()
            cb.wait()
            return carry

        lax.fori_loop(0, tc, wait, 0, unroll=8)

    @pl.when(i == 0)
    def _():
        issue_block(0, 0)

    wait_block(i, slot)

    last = pl.num_programs(0) - 1
    nxt = jnp.minimum(i + 1, last)
    for r in range(tc):
        ca, cb = copies(nxt, 1 - slot, r)
        ca.start(priority=0)
        cb.start(priority=1)
    route = route_ref[...]
    moe = route[:, 2:3] * buf_a[slot] + route[:, 3:4] * buf_b[slot]
    o_ref[...] = _rms_rows(h_ref[...] + moe, fg_ref[...])

    @pl.when(i == last)
    def _():
        wait_block(nxt, 1 - slot)


def _combine(h, route, final_g, rows, pos, *, tc):
    t, d = h.shape
    return pl.pallas_call(
        functools.partial(_combine_kernel, tc=tc),
        grid_spec=pltpu.PrefetchScalarGridSpec(
            num_scalar_prefetch=1,
            grid=(t // tc,),
            in_specs=[
                pl.BlockSpec((tc, d), lambda i, pos: (i, 0)),
                pl.BlockSpec((tc, LANES), lambda i, pos: (i, 0)),
                pl.BlockSpec((1, d), lambda i, pos: (0, 0)),
                pl.BlockSpec(memory_space=pl.ANY),
            ],
            out_specs=pl.BlockSpec((tc, d), lambda i, pos: (i, 0)),
            scratch_shapes=[pltpu.VMEM((2, tc, d), F32), pltpu.VMEM((2, tc, d), F32), pltpu.SemaphoreType.DMA((2,))],
        ),
        out_shape=jax.ShapeDtypeStruct((t, d), F32),
        compiler_params=_params(("arbitrary",)),
        name="moe_combine",
    )(pos, h, route, final_g, rows)


def _moe_plan(route, n_tok):
    flat_e = route[:, :TOP_K].astype(jnp.int32).reshape(-1)
    onehot = (flat_e[:, None] == jnp.arange(N_EXPERTS, dtype=jnp.int32)[None, :]).astype(jnp.int32)
    csum = jnp.cumsum(onehot, axis=0)
    rank = jnp.sum(onehot * (csum - onehot), axis=1)
    counts = csum[-1]
    n_groups = (n_tok * TOP_K) // MOE_GROUP + N_EXPERTS
    groups_e = (counts + MOE_GROUP - 1) // MOE_GROUP
    group_end = jnp.cumsum(groups_e)
    group_start = group_end - groups_e
    pos = jnp.sum(onehot * (group_start * MOE_GROUP)[None, :], axis=1) + rank
    flat_tok = jnp.arange(n_tok * TOP_K, dtype=jnp.int32) // TOP_K
    src_tok = jnp.zeros(((n_groups + 1) * MOE_GROUP,), jnp.int32).at[pos].set(
        flat_tok, unique_indices=True, mode="promise_in_bounds")
    experts = jnp.arange(N_EXPERTS, dtype=jnp.int32)
    gidx = jnp.arange(n_groups, dtype=jnp.int32)
    used = gidx < group_end[-1]
    last_e = jnp.max(jnp.where(groups_e > 0, experts, 0))
    ge = jnp.sum((group_end[None, :] <= gidx[:, None]).astype(jnp.int32), axis=1)
    ge = jnp.where(used, ge, last_e).astype(jnp.int32)
    mine = (ge[:, None] == experts[None, :]).astype(jnp.int32)
    subs_e = (counts + MOE_SUB - 1) // MOE_SUB
    subs_g = jnp.sum(mine * subs_e[None, :], axis=1)
    start_g = jnp.sum(mine * group_start[None, :], axis=1)
    ns = jnp.clip(subs_g - MOE_SUBS_PER_GROUP * (gidx - start_g), 0, MOE_SUBS_PER_GROUP)
    ns = jnp.where(used, ns, 0).astype(jnp.int32)
    return ge, ns, src_tok, pos.astype(jnp.int32)


def kernel(x, mix_norm_g, ffn_norm_g, final_norm_g, w_in, conv_w, conv_b, conv_ln_g, conv_ln_b, ssm_conv_w, ssm_conv_b, dt_bias, a_log, d_skip, ssm_norm_g, w_out, ffn_w_gate, ffn_w_up, ffn_w_down, w_qkv, b_qkv, w_o, b_o, sinks, rel_bias, router_w, router_b, moe_w_gate, moe_w_up, moe_w_down):
    bsz, seq, d = x.shape
    assert bsz == 1 and d == D_MODEL and seq % 512 == 0
    m = seq
    h = x.reshape(m, d)
    row = lambda v: v.reshape(1, -1).astype(F32)
    tm = min(1024, m)

    main_w = 2 * D_MODEL + D_MODEL + (D_MODEL + 2 * SSM_GROUPS * SSM_STATE)
    w_main = w_in[0].astype(BF16)
    w_dt = jnp.zeros((d, LANES), F32).at[:, :SSM_HEADS].set(w_in[0][:, main_w:]).astype(BF16)
    g0 = row(mix_norm_g[0])
    n_exp, _, expert_dim = moe_w_gate[0].shape
    wg_f32 = moe_w_gate[0].reshape(n_exp * d, expert_dim)
    wu_f32 = moe_w_up[0].reshape(n_exp * d, expert_dim)
    wd_f32 = moe_w_down[0].reshape(n_exp * expert_dim, d)

    proj, dt_raw, wd_b = _in_proj(h, g0, w_main, w_dt, wd_f32, tm=tm, tn=1024, n=main_w)
    conv_out, wg_b = _conv_module(proj, conv_w[0].astype(F32), row(conv_b[0]), row(conv_ln_g[0]), row(conv_ln_b[0]),
                                  wg_f32, tt=256)

    pad_heads = lambda v: jnp.zeros((1, LANES), F32).at[0, :SSM_HEADS].set(v.astype(F32))
    ssm_out, wu_b = _ssd(proj, dt_raw, ssm_conv_w[0].astype(F32), row(ssm_conv_b[0]),
                         pad_heads(dt_bias[0]), pad_heads(-jnp.exp(a_log[0].astype(F32))),
                         row(jnp.repeat(d_skip[0].astype(F32), SSM_HEAD_DIM)), row(ssm_norm_g[0]), wu_f32)

    wo = w_out[0].astype(BF16)
    h = _matmul_residual([conv_out, ssm_out], [wo[:D_MODEL], wo[D_MODEL:]], h, jnp.zeros((1, d), F32),
                         tm=256, tn=d, name="out_proj")
    h = _swiglu(h, row(ffn_norm_g[0]), ffn_w_gate[0].astype(BF16), ffn_w_up[0].astype(BF16),
                ffn_w_down[0].astype(BF16), tm=tm, tf=512)

    qkv = _norm_matmul(h, row(mix_norm_g[1]), w_qkv[0].astype(BF16), row(b_qkv[0]), tm=min(512, m), tn=w_qkv.shape[2],
                       out_dtype=BF16, name="qkv_proj")
    attn = _attention(qkv, sinks[0], rel_bias)
    h = _matmul_residual([attn], [w_o[0].astype(BF16)], h, row(b_o[0]), tm=min(512, m), tn=d, name="attn_out_proj")

    g1 = row(ffn_norm_g[1])
    route = _router(h, g1, router_w[0], router_b[0], tm=512)
    ge, ns, src_tok, pos = _moe_plan(route, m)
    rows = _moe_ffn(h, g1, wg_b.reshape(n_exp, d, expert_dim), wu_b.reshape(n_exp, d, expert_dim),
                    wd_b.reshape(n_exp, expert_dim, d), ge, ns, src_tok, tf=512)
    out = _combine(h, route, row(final_norm_g), rows, pos, tc=256)
    return out.reshape(bsz, seq, d)
```
